```python
import math
import jax, jax.numpy as jnp
from jax import lax
import numpy as np

D_MODEL = 1024
BATCH = 32
SEQ = 256
DEPTH = 4
DEC_BATCH = 4
DEC_SEQ = 1024
PAST_LEN = 256

GRID_W = 64
HEAD_DIM = 64
N_Q_HEADS = 8
N_KV_HEADS = 2
Q_PER_KV = N_Q_HEADS // N_KV_HEADS
ATTN_W = N_Q_HEADS * HEAD_DIM
KV_W = N_KV_HEADS * HEAD_DIM
POOL_WINDOWS = (2, 4, 8, 16)
N_POOL_GROUPS = len(POOL_WINDOWS)
POOL_W = D_MODEL // 2
POOL_GW = POOL_W // N_POOL_GROUPS
N_BRANCHES = 2
IN_COLS = ATTN_W + 2 * KV_W + POOL_W + N_BRANCHES * D_MODEL
FFN_HIDDEN = ((8 * D_MODEL + 3 * 256 - 1) // (3 * 256)) * 256
N_MOD = 6
Q_BLOCK = 128
ROPE_THETA = 10000.0
EPS = 1e-6

kernel_name = 'hybrid_gqa_pool_prefix_dit_step'


def rmsnorm(x, g):
    xf = x.astype(jnp.float32)
    y = xf * lax.rsqrt(jnp.mean(xf * xf, axis=-1, keepdims=True) + EPS)
    return (y * g.astype(jnp.float32)).astype(x.dtype)


def grid_rope(n):
    rows = n // GRID_W
    row = jnp.broadcast_to(jnp.arange(rows, dtype=jnp.int32)[:, None], (rows, GRID_W)).reshape(-1)
    col = jnp.broadcast_to(jnp.arange(GRID_W, dtype=jnp.int32)[None, :], (rows, GRID_W)).reshape(-1)
    n_freq = HEAD_DIM // 4
    inv = ROPE_THETA ** (-jnp.arange(n_freq, dtype=jnp.float32) / n_freq)
    ang = jnp.concatenate([row.astype(jnp.float32)[:, None] * inv[None, :],
                           col.astype(jnp.float32)[:, None] * inv[None, :]], axis=-1)
    return jnp.cos(ang), jnp.sin(ang)


def apply_rope(x, cos, sin):
    xf = x.astype(jnp.float32)
    x1, x2 = jnp.split(xf, 2, axis=-1)
    c = cos[None, :, None, :]
    s = sin[None, :, None, :]
    return jnp.concatenate([x1 * c - x2 * s, x2 * c + x1 * s], axis=-1).astype(x.dtype)


def block_attention(q, k, v):
    b, n = q.shape[0], q.shape[1]
    blk = math.gcd(n, Q_BLOCK)
    nb = n // blk
    qb = q.reshape(b, nb, blk, N_KV_HEADS, Q_PER_KV, HEAD_DIM).transpose(1, 0, 2, 3, 4, 5)
    scale = HEAD_DIM ** -0.5

    def one_block(qi):
        s = jnp.einsum('bqhgd,bkhd->bhgqk', qi, k).astype(jnp.float32) * scale
        p = jax.nn.softmax(s, axis=-1).astype(v.dtype)
        return jnp.einsum('bhgqk,bkhd->bqhgd', p, v)

    o = lax.map(one_block, qb)
    return o.transpose(1, 0, 2, 3, 4, 5).reshape(b, n, ATTN_W)


def multiscale_pool(p, w_pool, pool_scale):
    b, n, _ = p.shape
    pf = p.astype(jnp.float32)
    csum = jnp.concatenate([jnp.zeros((b, 1, POOL_W), jnp.float32), jnp.cumsum(pf, axis=1)], axis=1)
    t = jnp.arange(n, dtype=jnp.int32)
    outs = []
    for g, w in enumerate(POOL_WINDOWS):
        left = w // 2
        right = w - 1 - left
        lo = jnp.clip(t - left, 0, n)
        hi = jnp.clip(t + right + 1, 0, n)
        sl = slice(g * POOL_GW, (g + 1) * POOL_GW)
        cs = csum[:, :, sl]
        mean = (cs[:, hi] - cs[:, lo]) / (hi - lo).astype(jnp.float32)[None, :, None]
        d = mean - pf[:, :, sl]
        outs.append(jnp.einsum('bnc,cd->bnd', d, w_pool[g].astype(jnp.float32)))
    y = jnp.concatenate(outs, axis=-1) * pool_scale.astype(jnp.float32)
    return y.astype(p.dtype)


def trunk_layer(x, mod, rope, ctx_k, ctx_v, w_in, q_norm, k_norm, w_attn_up, w_pool, pool_scale,
                w_pool_up, w_out, n_pre_mix, n_post_mix, n_pre_ffn, n_post_ffn,
                w_ffn_gate, w_ffn_up, w_ffn_down):
    b, n, _ = x.shape
    sh1, sc1, g1, sh2, sc2, g2 = jnp.split(mod[:, None, :], N_MOD, axis=-1)
    h = rmsnorm(x, n_pre_mix) * (1 + sc1) + sh1
    proj = h @ w_in
    q, k, v, p, gates = jnp.split(
        proj, [ATTN_W, ATTN_W + KV_W, ATTN_W + 2 * KV_W, ATTN_W + 2 * KV_W + POOL_W], axis=-1)
    q = rmsnorm(q.reshape(b, n, N_Q_HEADS, HEAD_DIM), q_norm)
    k = rmsnorm(k.reshape(b, n, N_KV_HEADS, HEAD_DIM), k_norm)
    v = v.reshape(b, n, N_KV_HEADS, HEAD_DIM)
    if rope is None:
        keys, vals = k, v
    else:
        cos, sin = rope
        q = apply_rope(q, cos, sin)
        k_lat = apply_rope(k, cos, sin)
        keys = jnp.concatenate([ctx_k, k_lat], axis=1)
        vals = jnp.concatenate([ctx_v, v], axis=1)
    attn = block_attention(q, keys, vals) @ w_attn_up
    pool = multiscale_pool(p, w_pool, pool_scale) @ w_pool_up
    g_attn, g_pool = jnp.split(jax.nn.sigmoid(gates), N_BRANCHES, axis=-1)
    mixed = (g_attn * attn + g_pool * pool) @ w_out
    x = x + g1 * rmsnorm(mixed, n_post_mix)
    h2 = rmsnorm(x, n_pre_ffn) * (1 + sc2) + sh2
    f = (jax.nn.silu(h2 @ w_ffn_gate) * (h2 @ w_ffn_up)) @ w_ffn_down
    x = x + g2 * rmsnorm(f, n_post_ffn)
    return x, k, v


def setup_inputs(seed: int = 0) -> dict:
    key = jax.random.key(seed)
    ks = jax.random.split(key, 32)
    nrm = jax.random.normal
    f32 = jnp.float32

    def gain(k, shape):
        return jnp.ones(shape, f32) + 0.05 * nrm(k, shape, f32)

    return {
        'x_prompt': nrm(ks[0], (BATCH, SEQ, D_MODEL), f32),
        'x_sample': nrm(ks[1], (DEC_BATCH, DEC_SEQ, D_MODEL), f32),
        'cache_k': nrm(ks[2], (DEC_BATCH, DEPTH, PAST_LEN, N_KV_HEADS, HEAD_DIM), f32),
        'cache_v': nrm(ks[3], (DEC_BATCH, DEPTH, PAST_LEN, N_KV_HEADS, HEAD_DIM), f32),
        'c': nrm(ks[4], (DEC_BATCH, D_MODEL), f32),
        'c_ctx': nrm(ks[5], (D_MODEL,), f32),
        'w_ada': nrm(ks[6], (DEPTH, D_MODEL, N_MOD * D_MODEL), f32) * (0.5 * D_MODEL ** -0.5),
        'b_ada': 0.01 * nrm(ks[7], (DEPTH, N_MOD * D_MODEL), f32),
        'w_in': nrm(ks[8], (DEPTH, D_MODEL, IN_COLS), f32) * D_MODEL ** -0.5,
        'q_norm': gain(ks[9], (DEPTH, HEAD_DIM)),
        'k_norm': gain(ks[10], (DEPTH, HEAD_DIM)),
        'w_attn_up': nrm(ks[11], (DEPTH, ATTN_W, D_MODEL), f32) * ATTN_W ** -0.5,
        'w_pool': nrm(ks[12], (DEPTH, N_POOL_GROUPS, POOL_GW, POOL_GW), f32) * POOL_GW ** -0.5,
        'pool_scale': gain(ks[13], (DEPTH, POOL_W)),
        'w_pool_up': nrm(ks[14], (DEPTH, POOL_W, D_MODEL), f32) * POOL_W ** -0.5,
        'w_out': nrm(ks[15], (DEPTH, D_MODEL, D_MODEL), f32) * D_MODEL ** -0.5,
        'n_pre_mix': gain(ks[16], (DEPTH, D_MODEL)),
        'n_post_mix': gain(ks[17], (DEPTH, D_MODEL)),
        'n_pre_ffn': gain(ks[18], (DEPTH, D_MODEL)),
        'n_post_ffn': gain(ks[19], (DEPTH, D_MODEL)),
        'w_ffn_gate': nrm(ks[20], (DEPTH, D_MODEL, FFN_HIDDEN), f32) * D_MODEL ** -0.5,
        'w_ffn_up': nrm(ks[21], (DEPTH, D_MODEL, FFN_HIDDEN), f32) * D_MODEL ** -0.5,
        'w_ffn_down': nrm(ks[22], (DEPTH, FFN_HIDDEN, D_MODEL), f32) * FFN_HIDDEN ** -0.5,
    }


def reference(x_prompt, x_sample, cache_k, cache_v, c, c_ctx, w_ada, b_ada, w_in, q_norm, k_norm,
              w_attn_up, w_pool, pool_scale, w_pool_up, w_out, n_pre_mix, n_post_mix, n_pre_ffn,
              n_post_ffn, w_ffn_gate, w_ffn_up, w_ffn_down):
    rope = grid_rope(x_sample.shape[1])
    silu_ctx = jax.nn.silu(c_ctx)[None, :]
    silu_c = jax.nn.silu(c)
    y = x_prompt
    z = x_sample
    new_ks = []
    new_vs = []
    for l in range(DEPTH):
        lw = (w_in[l], q_norm[l], k_norm[l], w_attn_up[l], w_pool[l], pool_scale[l], w_pool_up[l],
              w_out[l], n_pre_mix[l], n_post_mix[l], n_pre_ffn[l], n_post_ffn[l],
              w_ffn_gate[l], w_ffn_up[l], w_ffn_down[l])
        mod_ctx = silu_ctx @ w_ada[l] + b_ada[l]
        mod_lat = silu_c @ w_ada[l] + b_ada[l]
        y, k_ctx, v_ctx = trunk_layer(y, mod_ctx, None, None, None, *lw)
        new_ks.append(k_ctx)
        new_vs.append(v_ctx)
        z, _, _ = trunk_layer(z, mod_lat, rope, cache_k[:, l], cache_v[:, l], *lw)
    new_k = jnp.stack(new_ks, axis=1)
    new_v = jnp.stack(new_vs, axis=1)
    return (y, z, new_k, new_v)
```

```python
import functools

import jax
import jax.numpy as jnp
import numpy as np
from jax import lax
from jax.experimental import pallas as pl
from jax.experimental.pallas import tpu as pltpu

D_MODEL = 1024
DEPTH = 4
GRID_W = 64
HEAD_DIM = 64
N_Q_HEADS = 8
N_KV_HEADS = 2
ATTN_W = N_Q_HEADS * HEAD_DIM
KV_W = N_KV_HEADS * HEAD_DIM
POOL_WINDOWS = (2, 4, 8, 16)
POOL_W = D_MODEL // 2
POOL_GW = POOL_W // len(POOL_WINDOWS)
IN_COLS = ATTN_W + 2 * KV_W + POOL_W + 2 * D_MODEL
GATE_COL0 = ATTN_W + 2 * KV_W + POOL_W
FFN_HIDDEN = 2816
N_MOD = 6
ROPE_THETA = 10000.0
EPS = 1e-6

LANES = 128
MOD_ROWS = 8
CTX_MOD_ROW = 4
VMEM_LIMIT_BYTES = 56 * 1024 * 1024

F32 = jnp.float32
BF16 = jnp.bfloat16


def _dot(a, b):
    return jnp.dot(a, b, preferred_element_type=F32)


def _dot_nt(a, b):
    return lax.dot_general(a, b, (((1,), (1,)), ((), ())), preferred_element_type=F32)


def _rms(x):
    return x * lax.rsqrt(jnp.mean(x * x, axis=-1, keepdims=True) + EPS)


def _ada_kernel(c_ref, w_ref, b_ref, o_ref):
    c = c_ref[...]
    s = (c * jax.nn.sigmoid(c)).astype(BF16)
    o_ref[...] = _dot(s, w_ref[...].astype(BF16)) + b_ref[...]


def _ada_call(c_rows, w_ada, b_ada):
    tn = 1536
    n_cols = N_MOD * D_MODEL
    return pl.pallas_call(
        _ada_kernel,
        grid=(DEPTH, n_cols // tn),
        in_specs=[
            pl.BlockSpec((MOD_ROWS, D_MODEL), lambda l, j: (0, 0)),
            pl.BlockSpec((None, D_MODEL, tn), lambda l, j: (l, 0, j)),
            pl.BlockSpec((None, 1, tn), lambda l, j: (l, 0, j)),
        ],
        out_specs=pl.BlockSpec((None, MOD_ROWS, tn), lambda l, j: (l, 0, j)),
        out_shape=jax.ShapeDtypeStruct((DEPTH, MOD_ROWS, n_cols), F32),
        compiler_params=pltpu.CompilerParams(
            dimension_semantics=("arbitrary", "arbitrary"), vmem_limit_bytes=VMEM_LIMIT_BYTES),
        name="ada_rows",
    )(c_rows, w_ada, b_ada.reshape(DEPTH, 1, n_cols))


def _proj_kernel(l_ref, x_ref, mod_ref, npre_ref, win_ref, qg_ref, kg_ref, cos_ref, sin_ref,
                 q_out, k2_out, v2_out, p_out, ga_out, gp_out, *maybe_kv_out, rope):
    del l_ref
    x = x_ref[...]
    mod = mod_ref[...]
    sh1 = mod[:, 0:D_MODEL]
    sc1 = mod[:, D_MODEL:2 * D_MODEL]
    h = (_rms(x) * npre_ref[...]) * (1.0 + sc1) + sh1
    hb = h.astype(BF16)

    tm = x.shape[0]
    lane = lax.broadcasted_iota(jnp.int32, (tm, LANES), 1)
    first_head = lane < HEAD_DIM
    second_half = (lane & (HEAD_DIM // 2)) != 0

    def head_norm(t, gain):
        t2 = t * t
        s_lo = jnp.sum(jnp.where(first_head, t2, 0.0), axis=-1, keepdims=True)
        s_hi = jnp.sum(jnp.where(first_head, 0.0, t2), axis=-1, keepdims=True)
        ms = jnp.where(first_head, s_lo, s_hi) * (1.0 / HEAD_DIM)
        return (t * lax.rsqrt(ms + EPS)) * gain

    def rotary(t):
        partner = jnp.where(second_half, pltpu.roll(t, HEAD_DIM // 2, 1),
                            pltpu.roll(t, LANES - HEAD_DIM // 2, 1))
        return t * cos_ref[...] + partner * sin_ref[...]

    qkvp = _dot(hb, win_ref[:, 0:GATE_COL0])
    for j in range(ATTN_W // LANES):
        t = head_norm(qkvp[:, j * LANES:(j + 1) * LANES], qg_ref[...])
        if rope:
            t = rotary(t)
        q_out[:, j * LANES:(j + 1) * LANES] = (t * (HEAD_DIM ** -0.5)).astype(BF16)

    kn = head_norm(qkvp[:, ATTN_W:ATTN_W + KV_W], kg_ref[...])
    v = qkvp[:, ATTN_W + KV_W:ATTN_W + 2 * KV_W]
    if maybe_kv_out:
        nk_out, nv_out = maybe_kv_out
        nk_out[...] = kn
        nv_out[...] = v
    if rope:
        kn = rotary(kn)
    k2_out[:, 0:LANES] = kn.astype(BF16)
    k2_out[:, LANES:2 * LANES] = pltpu.roll(kn, HEAD_DIM, 1).astype(BF16)
    v2_out[:, 0:LANES] = v.astype(BF16)
    v2_out[:, LANES:2 * LANES] = pltpu.roll(v, HEAD_DIM, 1).astype(BF16)
    p_out[...] = qkvp[:, ATTN_W + 2 * KV_W:GATE_COL0]

    gates = jax.nn.sigmoid(_dot(hb, win_ref[:, GATE_COL0:IN_COLS]))
    ga_out[...] = gates[:, 0:D_MODEL]
    gp_out[...] = gates[:, D_MODEL:2 * D_MODEL]


def _proj_call(l, x, mod_rows, n_pre, w_in, q_gain, k_gain, cos_t, sin_t, *, seq_len, rope, emit_kv, tm):
    n_tok = x.shape[0]
    tiles_per_seq = seq_len // tm if rope else 1
    if rope:
        mod_idx = lambda i, l_ref: (l_ref[0] * MOD_ROWS + i // tiles_per_seq, 0, 0)
        pos_idx = lambda i, l_ref: (i % tiles_per_seq, 0)
    else:
        mod_idx = lambda i, l_ref: (l_ref[0] * MOD_ROWS + CTX_MOD_ROW, 0, 0)
        pos_idx = lambda i, l_ref: (0, 0)
    lay = lambda i, l_ref: (l_ref[0], 0, 0)
    row = lambda i, l_ref: (i, 0)
    out_shapes = [
        jax.ShapeDtypeStruct((n_tok, ATTN_W), BF16),
        jax.ShapeDtypeStruct((n_tok, 2 * KV_W), BF16),
        jax.ShapeDtypeStruct((n_tok, 2 * KV_W), BF16),
        jax.ShapeDtypeStruct((n_tok, POOL_W), F32),
        jax.ShapeDtypeStruct((n_tok, D_MODEL), F32),
        jax.ShapeDtypeStruct((n_tok, D_MODEL), F32),
    ]
    out_specs = [
        pl.BlockSpec((tm, ATTN_W), row),
        pl.BlockSpec((tm, 2 * KV_W), row),
        pl.BlockSpec((tm, 2 * KV_W), row),
        pl.BlockSpec((tm, POOL_W), row),
        pl.BlockSpec((tm, D_MODEL), row),
        pl.BlockSpec((tm, D_MODEL), row),
    ]
    if emit_kv:
        out_shapes += [jax.ShapeDtypeStruct((n_tok, KV_W), F32)] * 2
        out_specs += [pl.BlockSpec((tm, KV_W), row)] * 2
    grid_spec = pltpu.PrefetchScalarGridSpec(
        num_scalar_prefetch=1,
        grid=(n_tok // tm,),
        in_specs=[
            pl.BlockSpec((tm, D_MODEL), row),
            pl.BlockSpec((None, 1, N_MOD * D_MODEL), mod_idx),
            pl.BlockSpec((None, 1, D_MODEL), lay),
            pl.BlockSpec((None, D_MODEL, IN_COLS), lay),
            pl.BlockSpec((None, 1, LANES), lay),
            pl.BlockSpec((None, 1, LANES), lay),
            pl.BlockSpec((tm, LANES), pos_idx),
            pl.BlockSpec((tm, LANES), pos_idx),
        ],
        out_specs=out_specs,
    )
    return pl.pallas_call(
        functools.partial(_proj_kernel, rope=rope),
        grid_spec=grid_spec,
        out_shape=out_shapes,
        compiler_params=pltpu.CompilerParams(
            dimension_semantics=("arbitrary",), vmem_limit_bytes=VMEM_LIMIT_BYTES),
        name="proj_lat" if rope else "proj_ctx",
    )(l, x, mod_rows, n_pre, w_in, q_gain, k_gain, cos_t, sin_t)


def _pool_group(x, window, seq_len):
    left = window // 2
    right = window - 1 - left
    t = lax.broadcasted_iota(jnp.int32, x.shape, 0)
    total = x
    for k in range(1, left + 1):
        total = total + jnp.where(t >= k, pltpu.roll(x, k, 0), 0.0)
    for k in range(1, right + 1):
        total = total + jnp.where(t < seq_len - k, pltpu.roll(x, seq_len - k, 0), 0.0)
    count = jnp.minimum(t + (right + 1), seq_len) - jnp.maximum(t - left, 0)
    return total / count.astype(F32) - x


def _attn_kernel(l_ref, q_ref, k2_ref, v2_ref, p_ref, wpool_ref, pscale_ref, *rest,
                 seq_len, n_seq, n_cache, tq):
    del l_ref
    if n_cache:
        ck_ref, cv_ref, ao_out, y_out, keys_ref, vals_ref = rest
    else:
        ao_out, y_out = rest
    n_keys = n_cache + seq_len

    lane = lax.broadcasted_iota(jnp.int32, (tq, LANES), 1)
    first_head = lane < HEAD_DIM

    def softmax_parts(s):
        m = jnp.max(s, axis=-1, keepdims=True)
        e = jnp.exp(s - m)
        return e.astype(BF16), 1.0 / jnp.sum(e, axis=-1, keepdims=True)

    for s_idx in range(n_seq):
        r0 = s_idx * seq_len
        if n_cache:
            ck = ck_ref[0, 0]
            cv = cv_ref[0, 0]
            keys_ref[0:n_cache, 0:LANES] = ck.astype(BF16)
            keys_ref[0:n_cache, LANES:2 * LANES] = pltpu.roll(ck, HEAD_DIM, 1).astype(BF16)
            vals_ref[0:n_cache, 0:LANES] = cv.astype(BF16)
            vals_ref[0:n_cache, LANES:2 * LANES] = pltpu.roll(cv, HEAD_DIM, 1).astype(BF16)
            keys_ref[n_cache:n_keys, :] = k2_ref[r0:r0 + seq_len, :]
            vals_ref[n_cache:n_keys, :] = v2_ref[r0:r0 + seq_len, :]
            k_src, v_src, kv0 = keys_ref, vals_ref, 0
        else:
            k_src, v_src, kv0 = k2_ref, v2_ref, r0

        def q_tile(qi, carry):
            q0 = pl.multiple_of(r0 + qi * tq, tq)
            for j in range(ATTN_W // LANES):
                straight = (j // 2) == 0
                c_even = 0 if straight else LANES
                c_odd = LANES if straight else 0
                qp = q_ref[pl.ds(q0, tq), j * LANES:(j + 1) * LANES]
                zero = jnp.zeros_like(qp)
                q_even = jnp.where(first_head, qp, zero)
                q_odd = jnp.where(first_head, zero, qp)
                k_even = k_src[kv0:kv0 + n_keys, c_even:c_even + LANES]
                k_odd = k_src[kv0:kv0 + n_keys, c_odd:c_odd + LANES]
                p_even, r_even = softmax_parts(_dot_nt(q_even, k_even))
                p_odd, r_odd = softmax_parts(_dot_nt(q_odd, k_odd))
                o_even = _dot(p_even, v_src[kv0:kv0 + n_keys, c_even:c_even + LANES])
                o_odd = _dot(p_odd, v_src[kv0:kv0 + n_keys, c_odd:c_odd + LANES])
                o = jnp.where(first_head, o_even * r_even, o_odd * r_odd)
                ao_out[pl.ds(q0, tq), j * LANES:(j + 1) * LANES] = o.astype(BF16)
            return carry

        lax.fori_loop(0, seq_len // tq, q_tile, 0)

        for g, window in enumerate(POOL_WINDOWS):
            cols = slice(g * POOL_GW, (g + 1) * POOL_GW)
            d = _pool_group(p_ref[r0:r0 + seq_len, cols], window, seq_len)
            yg = _dot(d.astype(BF16), wpool_ref[g]) * pscale_ref[:, cols]
            y_out[r0:r0 + seq_len, cols] = yg.astype(BF16)


def _attn_call(l, q, k2, v2, p, w_pool, pool_scale, cache_k, cache_v, *, seq_len, n_seq, tq):
    n_tok = q.shape[0]
    rows = n_seq * seq_len
    row = lambda i, l_ref: (i, 0)
    in_specs = [
        pl.BlockSpec((rows, ATTN_W), row),
        pl.BlockSpec((rows, 2 * KV_W), row),
        pl.BlockSpec((rows, 2 * KV_W), row),
        pl.BlockSpec((rows, POOL_W), row),
        pl.BlockSpec((None, len(POOL_WINDOWS), POOL_GW, POOL_GW), lambda i, l_ref: (l_ref[0], 0, 0, 0)),
        pl.BlockSpec((None, 1, POOL_W), lambda i, l_ref: (l_ref[0], 0, 0)),
    ]
    args = [q, k2, v2, p, w_pool, pool_scale]
    scratch = []
    n_cache = 0
    if cache_k is not None:
        assert n_seq == 1
        n_cache = cache_k.shape[2]
        cache_spec = pl.BlockSpec((1, 1, n_cache, KV_W), lambda i, l_ref: (i, l_ref[0], 0, 0))
        in_specs += [cache_spec, cache_spec]
        args += [cache_k, cache_v]
        scratch = [pltpu.VMEM((n_cache + seq_len, 2 * KV_W), BF16)] * 2
    grid_spec = pltpu.PrefetchScalarGridSpec(
        num_scalar_prefetch=1,
        grid=(n_tok // rows,),
        in_specs=in_specs,
        out_specs=[pl.BlockSpec((rows, ATTN_W), row), pl.BlockSpec((rows, POOL_W), row)],
        scratch_shapes=scratch,
    )
    return pl.pallas_call(
        functools.partial(_attn_kernel, seq_len=seq_len, n_seq=n_seq, n_cache=n_cache, tq=tq),
        grid_spec=grid_spec,
        out_shape=[jax.ShapeDtypeStruct((n_tok, ATTN_W), BF16),
                   jax.ShapeDtypeStruct((n_tok, POOL_W), BF16)],
        compiler_params=pltpu.CompilerParams(
            dimension_semantics=("arbitrary",), vmem_limit_bytes=VMEM_LIMIT_BYTES),
        name="attn_lat" if n_cache else "attn_ctx",
    )(l, *args)


def _post_kernel(l_ref, ao_ref, y_ref, ga_ref, gp_ref, x_ref, mod_ref, npost_ref,
                 wau_ref, wpu_ref, wout_ref, o_ref):
    del l_ref
    attn = _dot(ao_ref[...], wau_ref[...])
    pool = _dot(y_ref[...], wpu_ref[...])
    mixed = ga_ref[...] * attn + gp_ref[...] * pool
    m = _dot(mixed.astype(BF16), wout_ref[...])
    g1 = mod_ref[:, 2 * D_MODEL:3 * D_MODEL]
    o_ref[...] = x_ref[...] + g1 * (_rms(m) * npost_ref[...])


def _post_call(l, ao, y, ga, gp, x, mod_rows, n_post, w_attn_up, w_pool_up, w_out, *, seq_len, per_seq_mod, tm):
    n_tok = x.shape[0]
    if per_seq_mod:
        mod_idx = lambda i, l_ref: (l_ref[0] * MOD_ROWS + i // (seq_len // tm), 0, 0)
    else:
        mod_idx = lambda i, l_ref: (l_ref[0] * MOD_ROWS + CTX_MOD_ROW, 0, 0)
    lay = lambda i, l_ref: (l_ref[0], 0, 0)
    row = lambda i, l_ref: (i, 0)
    grid_spec = pltpu.PrefetchScalarGridSpec(
        num_scalar_prefetch=1,
        grid=(n_tok // tm,),
        in_specs=[
            pl.BlockSpec((tm, ATTN_W), row),
            pl.BlockSpec((tm, POOL_W), row),
            pl.BlockSpec((tm, D_MODEL), row),
            pl.BlockSpec((tm, D_MODEL), row),
            pl.BlockSpec((tm, D_MODEL), row),
            pl.BlockSpec((None, 1, N_MOD * D_MODEL), mod_idx),
            pl.BlockSpec((None, 1, D_MODEL), lay),
            pl.BlockSpec((None, ATTN_W, D_MODEL), lay),
            pl.BlockSpec((None, POOL_W, D_MODEL), lay),
            pl.BlockSpec((None, D_MODEL, D_MODEL), lay),
        ],
        out_specs=pl.BlockSpec((tm, D_MODEL), row),
    )
    return pl.pallas_call(
        _post_kernel,
        grid_spec=grid_spec,
        out_shape=jax.ShapeDtypeStruct((n_tok, D_MODEL), F32),
        compiler_params=pltpu.CompilerParams(
            dimension_semantics=("arbitrary",), vmem_limit_bytes=VMEM_LIMIT_BYTES),
        name="post_mix",
    )(l, ao, y, ga, gp, x, mod_rows, n_post, w_attn_up, w_pool_up, w_out)


def _ffn_kernel(l_ref, x_ref, mod_ref, npre_ref, npost_ref, wg_ref, wu_ref, wd_ref, o_ref, *, chunk):
    del l_ref
    mod = mod_ref[...]
    sh2 = mod[:, 3 * D_MODEL:4 * D_MODEL]
    sc2 = mod[:, 4 * D_MODEL:5 * D_MODEL]
    g2 = mod[:, 5 * D_MODEL:6 * D_MODEL]

    def rows(ci, carry):
        r0 = pl.multiple_of(ci * chunk, chunk)
        x = x_ref[pl.ds(r0, chunk), :]
        hb = ((_rms(x) * npre_ref[...]) * (1.0 + sc2) + sh2).astype(BF16)
        g = _dot(hb, wg_ref[...])
        u = _dot(hb, wu_ref[...])
        a = ((g * jax.nn.sigmoid(g)) * u).astype(BF16)
        f = _dot(a, wd_ref[...])
        o_ref[pl.ds(r0, chunk), :] = x + g2 * (_rms(f) * npost_ref[...])
        return carry

    lax.fori_loop(0, x_ref.shape[0] // chunk, rows, 0)


def _ffn_call(l, x, mod_rows, n_pre, n_post, w_gate, w_up, w_down, *, seq_len, per_seq_mod, tm, chunk):
    n_tok = x.shape[0]
    if per_seq_mod:
        mod_idx = lambda i, l_ref: (l_ref[0] * MOD_ROWS + i // (seq_len // tm), 0, 0)
    else:
        mod_idx = lambda i, l_ref: (l_ref[0] * MOD_ROWS + CTX_MOD_ROW, 0, 0)
    lay = lambda i, l_ref: (l_ref[0], 0, 0)
    row = lambda i, l_ref: (i, 0)
    resident = pl.Buffered(1)
    grid_spec = pltpu.PrefetchScalarGridSpec(
        num_scalar_prefetch=1,
        grid=(n_tok // tm,),
        in_specs=[
            pl.BlockSpec((tm, D_MODEL), row),
            pl.BlockSpec((None, 1, N_MOD * D_MODEL), mod_idx),
            pl.BlockSpec((None, 1, D_MODEL), lay),
            pl.BlockSpec((None, 1, D_MODEL), lay),
            pl.BlockSpec((None, D_MODEL, FFN_HIDDEN), lay, pipeline_mode=resident),
            pl.BlockSpec((None, D_MODEL, FFN_HIDDEN), lay, pipeline_mode=resident),
            pl.BlockSpec((None, FFN_HIDDEN, D_MODEL), lay, pipeline_mode=resident),
        ],
        out_specs=pl.BlockSpec((tm, D_MODEL), row),
    )
    return pl.pallas_call(
        functools.partial(_ffn_kernel, chunk=chunk),
        grid_spec=grid_spec,
        out_shape=jax.ShapeDtypeStruct((n_tok, D_MODEL), F32),
        compiler_params=pltpu.CompilerParams(
            dimension_semantics=("arbitrary",), vmem_limit_bytes=VMEM_LIMIT_BYTES),
        name="ffn",
    )(l, x, mod_rows, n_pre, n_post, w_gate, w_up, w_down)


def _rope_tables(n):
    rows = n // GRID_W
    row = np.repeat(np.arange(rows), GRID_W).astype(np.float32)
    col = np.tile(np.arange(GRID_W), rows).astype(np.float32)
    n_freq = HEAD_DIM // 4
    inv = jnp.asarray(ROPE_THETA, F32) ** (-jnp.arange(n_freq, dtype=F32) / n_freq)
    ang = jnp.concatenate([jnp.asarray(row)[:, None] * inv[None, :],
                           jnp.asarray(col)[:, None] * inv[None, :]], axis=-1)
    cos, sin = jnp.cos(ang), jnp.sin(ang)
    cos_t = jnp.tile(cos, (1, LANES // (HEAD_DIM // 2)))
    sin_t = jnp.tile(jnp.concatenate([-sin, sin], axis=-1), (1, LANES // HEAD_DIM))
    return cos_t, sin_t


def kernel(x_prompt, x_sample, cache_k, cache_v, c, c_ctx, w_ada, b_ada, w_in, q_norm, k_norm,
           w_attn_up, w_pool, pool_scale, w_pool_up, w_out, n_pre_mix, n_post_mix, n_pre_ffn,
           n_post_ffn, w_ffn_gate, w_ffn_up, w_ffn_down):
    batch, seq, _ = x_prompt.shape
    dec_batch, dec_seq, _ = x_sample.shape
    past_len = cache_k.shape[2]
    assert dec_batch <= CTX_MOD_ROW

    c_rows = jnp.concatenate(
        [c, c_ctx[None, :], jnp.zeros((MOD_ROWS - dec_batch - 1, D_MODEL), F32)], axis=0)
    mod_rows = _ada_call(c_rows, w_ada, b_ada).reshape(DEPTH * MOD_ROWS, 1, N_MOD * D_MODEL)

    cos_t, sin_t = _rope_tables(dec_seq)
    vec = lambda a: a.reshape(DEPTH, 1, a.shape[-1])
    q_gain = vec(jnp.tile(q_norm, (1, LANES // HEAD_DIM)))
    k_gain = vec(jnp.tile(k_norm, (1, LANES // HEAD_DIM)))
    w_in_b = w_in.astype(BF16)
    w_pool_b = w_pool.astype(BF16)
    w_au_b = w_attn_up.astype(BF16)
    w_pu_b = w_pool_up.astype(BF16)
    w_out_b = w_out.astype(BF16)
    w_g_b = w_ffn_gate.astype(BF16)
    w_u_b = w_ffn_up.astype(BF16)
    w_d_b = w_ffn_down.astype(BF16)
    n_pre_mix_v, n_post_mix_v = vec(n_pre_mix), vec(n_post_mix)
    n_pre_ffn_v, n_post_ffn_v = vec(n_pre_ffn), vec(n_post_ffn)
    pool_scale_v = vec(pool_scale)
    ck = cache_k.reshape(dec_batch, DEPTH, past_len, KV_W)
    cv = cache_v.reshape(dec_batch, DEPTH, past_len, KV_W)

    tm = 512
    ctx_seqs_per_step = 4

    def trunk(l, x, *, is_lat):
        seq_len = dec_seq if is_lat else seq
        outs = _proj_call(l, x, mod_rows, n_pre_mix_v, w_in_b, q_gain, k_gain, cos_t, sin_t,
                          seq_len=seq_len, rope=is_lat, emit_kv=not is_lat, tm=tm)
        q, k2, v2, p, ga, gp = outs[:6]
        ao, y = _attn_call(l, q, k2, v2, p, w_pool_b, pool_scale_v,
                           ck if is_lat else None, cv if is_lat else None,
                           seq_len=seq_len, n_seq=1 if is_lat else ctx_seqs_per_step, tq=256)
        x1 = _post_call(l, ao, y, ga, gp, x, mod_rows, n_post_mix_v, w_au_b, w_pu_b, w_out_b,
                        seq_len=seq_len, per_seq_mod=is_lat, tm=tm)
        x2 = _ffn_call(l, x1, mod_rows, n_pre_ffn_v, n_post_ffn_v, w_g_b, w_u_b, w_d_b,
                       seq_len=seq_len, per_seq_mod=is_lat, tm=tm, chunk=256)
        return x2, outs[6:]

    def layer(carry, l):
        y, z = carry
        lv = jnp.reshape(l, (1,)).astype(jnp.int32)
        y, (nk, nv) = trunk(lv, y, is_lat=False)
        z, _ = trunk(lv, z, is_lat=True)
        return (y, z), (nk, nv)

    y0 = x_prompt.reshape(batch * seq, D_MODEL)
    z0 = x_sample.reshape(dec_batch * dec_seq, D_MODEL)
    (y, z), (nks, nvs) = lax.scan(layer, (y0, z0), jnp.arange(DEPTH, dtype=jnp.int32))

    def kv_layout(a):
        a = a.reshape(DEPTH, batch, seq, N_KV_HEADS, HEAD_DIM)
        return jnp.transpose(a, (1, 0, 2, 3, 4))

    return (y.reshape(batch, seq, D_MODEL), z.reshape(dec_batch, dec_seq, D_MODEL),
            kv_layout(nks), kv_layout(nvs))
```

```python
import functools

import jax
import jax.numpy as jnp
import numpy as np
from jax import lax
from jax.experimental import pallas as pl
from jax.experimental.pallas import tpu as pltpu

D_MODEL = 1024
DEPTH = 4
GRID_W = 64
HEAD_DIM = 64
N_Q_HEADS = 8
N_KV_HEADS = 2
Q_PER_KV = N_Q_HEADS // N_KV_HEADS
ATTN_W = N_Q_HEADS * HEAD_DIM
KV_W = N_KV_HEADS * HEAD_DIM
POOL_WINDOWS = (2, 4, 8, 16)
POOL_W = D_MODEL // 2
POOL_GW = POOL_W // len(POOL_WINDOWS)
GATE_COL0 = ATTN_W + 2 * KV_W + POOL_W
FFN_HIDDEN = 2816
N_MOD = 6
ROPE_THETA = 10000.0
EPS = 1e-6

LANES = 128
MOD_ROWS = 8
CTX_MOD_ROW = 4
VMEM_LIMIT_BYTES = 56 * 1024 * 1024

F32 = jnp.float32
BF16 = jnp.bfloat16


def _dot(a, b):
    return jnp.dot(a, b, preferred_element_type=F32)


def _dot_nt(a, b):
    return lax.dot_general(a, b, (((1,), (1,)), ((), ())), preferred_element_type=F32)


def _rms(x):
    return x * lax.rsqrt(jnp.mean(x * x, axis=-1, keepdims=True) + EPS)


def _params(n_axes=1):
    return pltpu.CompilerParams(dimension_semantics=("arbitrary",) * n_axes,
                                vmem_limit_bytes=VMEM_LIMIT_BYTES)


def _ada_kernel(c_ref, w_ref, b_ref, o_ref):
    c = c_ref[...]
    s = (c * jax.nn.sigmoid(c)).astype(BF16)
    o_ref[...] = _dot(s, w_ref[...].astype(BF16)) + b_ref[...]


def _ada_call(c_rows, w_ada, b_ada):
    tn = 1536
    n_cols = N_MOD * D_MODEL
    return pl.pallas_call(
        _ada_kernel,
        grid=(DEPTH, n_cols // tn),
        in_specs=[
            pl.BlockSpec((MOD_ROWS, D_MODEL), lambda l, j: (0, 0)),
            pl.BlockSpec((None, D_MODEL, tn), lambda l, j: (l, 0, j)),
            pl.BlockSpec((None, 1, tn), lambda l, j: (l, 0, j)),
        ],
        out_specs=pl.BlockSpec((None, MOD_ROWS, tn), lambda l, j: (l, 0, j)),
        out_shape=jax.ShapeDtypeStruct((DEPTH, MOD_ROWS, n_cols), F32),
        compiler_params=_params(2),
        name="ada_rows",
    )(c_rows, w_ada, b_ada.reshape(DEPTH, 1, n_cols))


def _pool_group(x, window, seq_len):
    left = window // 2
    right = window - 1 - left
    t = lax.broadcasted_iota(jnp.int32, x.shape, 0)
    total = x
    for k in range(1, left + 1):
        total = total + jnp.where(t >= k, pltpu.roll(x, k, 0), 0.0)
    for k in range(1, right + 1):
        total = total + jnp.where(t < seq_len - k, pltpu.roll(x, seq_len - k, 0), 0.0)
    count = jnp.minimum(t + (right + 1), seq_len) - jnp.maximum(t - left, 0)
    return total / count.astype(F32) - x


def _mix_kernel(x_ref, mod_ref, npre_ref, w_ref, qg_ref, kg_ref, cos_ref, sin_ref, wpool_ref,
                pscale_ref, *rest, seq_len, n_seq, n_cache, rope, chunk, tq):
    if n_cache:
        ck_ref, cv_ref, ao_out, y_out, qa_ref, k_ref, vt_ref, p_ref = rest
    else:
        ao_out, y_out, nk_out, nv_out, qa_ref, k_ref, vt_ref, p_ref = rest
    n_keys = n_cache + seq_len
    chunks_per_seq = seq_len // chunk

    mod = mod_ref[...]
    sh1 = mod[:, 0:D_MODEL]
    sc1 = mod[:, D_MODEL:2 * D_MODEL]

    lane = lax.broadcasted_iota(jnp.int32, (chunk, LANES), 1)
    first_head = lane < HEAD_DIM
    second_half = (lane & (HEAD_DIM // 2)) != 0

    def head_norm(t, gain):
        t2 = t * t
        s_lo = jnp.sum(jnp.where(first_head, t2, 0.0), axis=-1, keepdims=True)
        s_hi = jnp.sum(jnp.where(first_head, 0.0, t2), axis=-1, keepdims=True)
        ms = jnp.where(first_head, s_lo, s_hi) * (1.0 / HEAD_DIM)
        return (t * lax.rsqrt(ms + EPS)) * gain

    def rotary(t, pos0):
        partner = jnp.where(second_half, pltpu.roll(t, HEAD_DIM // 2, 1),
                            pltpu.roll(t, LANES - HEAD_DIM // 2, 1))
        return t * cos_ref[pl.ds(pos0, chunk), :] + partner * sin_ref[pl.ds(pos0, chunk), :]

    if n_cache:
        k_ref[0, 0:n_cache, :] = ck_ref[0, 0].astype(BF16)
        vt_ref[0, :, 0:n_cache] = cv_ref[0, 0].T.astype(BF16)

    for s_idx in range(n_seq):
        for c_idx in range(chunks_per_seq):
            pos0 = c_idx * chunk
            r0 = s_idx * seq_len + pos0
            x = x_ref[r0:r0 + chunk, :]
            hb = ((_rms(x) * npre_ref[...]) * (1.0 + sc1) + sh1).astype(BF16)
            qkvp = _dot(hb, w_ref[...])
            for j in range(ATTN_W // LANES):
                t = head_norm(qkvp[:, j * LANES:(j + 1) * LANES], qg_ref[...])
                if rope:
                    t = rotary(t, pos0)
                t = t * (HEAD_DIM ** -0.5)
                swapped = pltpu.roll(t, HEAD_DIM, 1)
                if (2 * j) // Q_PER_KV == 0:
                    even = jnp.where(first_head, t, 0.0)
                    odd = jnp.where(first_head, swapped, 0.0)
                else:
                    even = jnp.where(first_head, 0.0, swapped)
                    odd = jnp.where(first_head, 0.0, t)
                qa_ref[r0:r0 + chunk, (2 * j) * LANES:(2 * j + 1) * LANES] = even.astype(BF16)
                qa_ref[r0:r0 + chunk, (2 * j + 1) * LANES:(2 * j + 2) * LANES] = odd.astype(BF16)
            kn = head_norm(qkvp[:, ATTN_W:ATTN_W + KV_W], kg_ref[...])
            v = qkvp[:, ATTN_W + KV_W:ATTN_W + 2 * KV_W]
            if not n_cache:
                nk_out[r0:r0 + chunk, :] = kn
                nv_out[r0:r0 + chunk, :] = v
            if rope:
                kn = rotary(kn, pos0)
            k_ref[s_idx, n_cache + pos0:n_cache + pos0 + chunk, :] = kn.astype(BF16)
            vt_ref[s_idx, :, n_cache + pos0:n_cache + pos0 + chunk] = v.T.astype(BF16)
            p_ref[r0:r0 + chunk, :] = qkvp[:, ATTN_W + 2 * KV_W:GATE_COL0]

    def one_head(s_idx, q0, head):
        kv = head // Q_PER_KV
        qa = qa_ref[pl.ds(q0, tq), head * LANES:(head + 1) * LANES]
        st = _dot_nt(k_ref[s_idx], qa)
        e = jnp.exp(st - jnp.max(st, axis=0, keepdims=True))
        inv = 1.0 / jnp.sum(e, axis=0, keepdims=True)
        vt = vt_ref[s_idx, kv * HEAD_DIM:(kv + 1) * HEAD_DIM, :]
        return _dot(vt, e.astype(BF16)) * inv

    def q_tile(i, carry):
        s_idx = i // (seq_len // tq)
        q0 = pl.multiple_of(i * tq, tq)
        for j in range(ATTN_W // LANES):
            ot = jnp.concatenate([one_head(s_idx, q0, 2 * j), one_head(s_idx, q0, 2 * j + 1)], axis=0)
            ao_out[pl.ds(q0, tq), j * LANES:(j + 1) * LANES] = ot.T.astype(BF16)
        return carry

    lax.fori_loop(0, n_seq * (seq_len // tq), q_tile, 0)

    for s_idx in range(n_seq):
        r0 = s_idx * seq_len
        for g, window in enumerate(POOL_WINDOWS):
            cols = slice(g * POOL_GW, (g + 1) * POOL_GW)
            d = _pool_group(p_ref[r0:r0 + seq_len, cols], window, seq_len)
            yg = _dot(d.astype(BF16), wpool_ref[g]) * pscale_ref[:, cols]
            y_out[r0:r0 + seq_len, cols] = yg.astype(BF16)


def _mix_call(l, x, mod_rows, n_pre, w_qkvp, q_gain, k_gain, cos_t, sin_t, w_pool, pool_scale,
              cache_k, cache_v, *, seq_len, n_seq, rope):
    n_tok = x.shape[0]
    rows = n_seq * seq_len
    is_lat = cache_k is not None
    n_cache = cache_k.shape[2] if is_lat else 0
    if is_lat:
        assert n_seq == 1
        mod_idx = lambda i: (l * MOD_ROWS + i, 0, 0)
    else:
        mod_idx = lambda i: (l * MOD_ROWS + CTX_MOD_ROW, 0, 0)
    lay = lambda i: (l, 0, 0)
    row = lambda i: (i, 0)
    in_specs = [
        pl.BlockSpec((rows, D_MODEL), row),
        pl.BlockSpec((None, 1, N_MOD * D_MODEL), mod_idx),
        pl.BlockSpec((None, 1, D_MODEL), lay),
        pl.BlockSpec((None, D_MODEL, GATE_COL0), lay),
        pl.BlockSpec((None, 1, LANES), lay),
        pl.BlockSpec((None, 1, LANES), lay),
        pl.BlockSpec((seq_len, LANES), lambda i: (0, 0)),
        pl.BlockSpec((seq_len, LANES), lambda i: (0, 0)),
        pl.BlockSpec((None, len(POOL_WINDOWS), POOL_GW, POOL_GW), lambda i: (l, 0, 0, 0)),
        pl.BlockSpec((None, 1, POOL_W), lay),
    ]
    args = [x, mod_rows, n_pre, w_qkvp, q_gain, k_gain, cos_t, sin_t, w_pool, pool_scale]
    out_shape = [jax.ShapeDtypeStruct((n_tok, ATTN_W), BF16), jax.ShapeDtypeStruct((n_tok, POOL_W), BF16)]
    out_specs = [pl.BlockSpec((rows, ATTN_W), row), pl.BlockSpec((rows, POOL_W), row)]
    if is_lat:
        cache_spec = pl.BlockSpec((1, 1, n_cache, KV_W), lambda i: (i, l, 0, 0))
        in_specs += [cache_spec, cache_spec]
        args += [cache_k, cache_v]
    else:
        out_shape += [jax.ShapeDtypeStruct((n_tok, KV_W), F32)] * 2
        out_specs += [pl.BlockSpec((rows, KV_W), row)] * 2
    scratch = [
        pltpu.VMEM((rows, N_Q_HEADS * LANES), BF16),
        pltpu.VMEM((n_seq, n_cache + seq_len, KV_W), BF16),
        pltpu.VMEM((n_seq, KV_W, n_cache + seq_len), BF16),
        pltpu.VMEM((rows, POOL_W), F32),
    ]
    return pl.pallas_call(
        functools.partial(_mix_kernel, seq_len=seq_len, n_seq=n_seq, n_cache=n_cache, rope=rope,
                          chunk=256, tq=256),
        grid=(n_tok // rows,),
        in_specs=in_specs,
        out_specs=out_specs,
        out_shape=out_shape,
        scratch_shapes=scratch,
        compiler_params=_params(),
        name="mix_lat" if is_lat else "mix_ctx",
    )(*args)


def _post_kernel(x_ref, ao_ref, y_ref, mod_ref, npre_ref, npost_ref, wgate_ref, wau_ref, wpu_ref,
                 wout_ref, o_ref, *, chunk):
    mod = mod_ref[...]
    sh1 = mod[:, 0:D_MODEL]
    sc1 = mod[:, D_MODEL:2 * D_MODEL]
    g1 = mod[:, 2 * D_MODEL:3 * D_MODEL]
    for r0 in range(0, x_ref.shape[0], chunk):
        x = x_ref[r0:r0 + chunk, :]
        hb = ((_rms(x) * npre_ref[...]) * (1.0 + sc1) + sh1).astype(BF16)
        g_attn = jax.nn.sigmoid(_dot(hb, wgate_ref[:, 0:D_MODEL]))
        g_pool = jax.nn.sigmoid(_dot(hb, wgate_ref[:, D_MODEL:2 * D_MODEL]))
        attn = _dot(ao_ref[r0:r0 + chunk, :], wau_ref[...])
        pool = _dot(y_ref[r0:r0 + chunk, :], wpu_ref[...])
        mixed = g_attn * attn + g_pool * pool
        m = _dot(mixed.astype(BF16), wout_ref[...])
        o_ref[r0:r0 + chunk, :] = x + g1 * (_rms(m) * npost_ref[...])


def _mod_index(l, seq_len, per_seq_mod, tm):
    if per_seq_mod:
        return lambda i: (l * MOD_ROWS + i // (seq_len // tm), 0, 0)
    return lambda i: (l * MOD_ROWS + CTX_MOD_ROW, 0, 0)


def _post_call(l, x, ao, y, mod_rows, n_pre, n_post, w_gate, w_attn_up, w_pool_up, w_out,
               *, seq_len, per_seq_mod, tm, chunk):
    n_tok = x.shape[0]
    lay = lambda i: (l, 0, 0)
    row = lambda i: (i, 0)
    resident = pl.Buffered(1)
    return pl.pallas_call(
        functools.partial(_post_kernel, chunk=chunk),
        grid=(n_tok // tm,),
        in_specs=[
            pl.BlockSpec((tm, D_MODEL), row),
            pl.BlockSpec((tm, ATTN_W), row),
            pl.BlockSpec((tm, POOL_W), row),
            pl.BlockSpec((None, 1, N_MOD * D_MODEL), _mod_index(l, seq_len, per_seq_mod, tm)),
            pl.BlockSpec((None, 1, D_MODEL), lay),
            pl.BlockSpec((None, 1, D_MODEL), lay),
            pl.BlockSpec((None, D_MODEL, 2 * D_MODEL), lay, pipeline_mode=resident),
            pl.BlockSpec((None, ATTN_W, D_MODEL), lay, pipeline_mode=resident),
            pl.BlockSpec((None, POOL_W, D_MODEL), lay, pipeline_mode=resident),
            pl.BlockSpec((None, D_MODEL, D_MODEL), lay, pipeline_mode=resident),
        ],
        out_specs=pl.BlockSpec((tm, D_MODEL), row),
        out_shape=jax.ShapeDtypeStruct((n_tok, D_MODEL), F32),
        compiler_params=_params(),
        name="post_mix",
    )(x, ao, y, mod_rows, n_pre, n_post, w_gate, w_attn_up, w_pool_up, w_out)


def _ffn_kernel(x_ref, mod_ref, npre_ref, npost_ref, wg_ref, wu_ref, wd_ref, o_ref, *, chunk):
    mod = mod_ref[...]
    sh2 = mod[:, 3 * D_MODEL:4 * D_MODEL]
    sc2 = mod[:, 4 * D_MODEL:5 * D_MODEL]
    g2 = mod[:, 5 * D_MODEL:6 * D_MODEL]
    for r0 in range(0, x_ref.shape[0], chunk):
        x = x_ref[r0:r0 + chunk, :]
        hb = ((_rms(x) * npre_ref[...]) * (1.0 + sc2) + sh2).astype(BF16)
        g = _dot(hb, wg_ref[...])
        u = _dot(hb, wu_ref[...])
        a = ((g * jax.nn.sigmoid(g)) * u).astype(BF16)
        f = _dot(a, wd_ref[...])
        o_ref[r0:r0 + chunk, :] = x + g2 * (_rms(f) * npost_ref[...])


def _ffn_call(l, x, mod_rows, n_pre, n_post, w_gate, w_up, w_down, *, seq_len, per_seq_mod, tm, chunk):
    n_tok = x.shape[0]
    lay = lambda i: (l, 0, 0)
    row = lambda i: (i, 0)
    resident = pl.Buffered(1)
    return pl.pallas_call(
        functools.partial(_ffn_kernel, chunk=chunk),
        grid=(n_tok // tm,),
        in_specs=[
            pl.BlockSpec((tm, D_MODEL), row),
            pl.BlockSpec((None, 1, N_MOD * D_MODEL), _mod_index(l, seq_len, per_seq_mod, tm)),
            pl.BlockSpec((None, 1, D_MODEL), lay),
            pl.BlockSpec((None, 1, D_MODEL), lay),
            pl.BlockSpec((None, D_MODEL, FFN_HIDDEN), lay, pipeline_mode=resident),
            pl.BlockSpec((None, D_MODEL, FFN_HIDDEN), lay, pipeline_mode=resident),
            pl.BlockSpec((None, FFN_HIDDEN, D_MODEL), lay, pipeline_mode=resident),
        ],
        out_specs=pl.BlockSpec((tm, D_MODEL), row),
        out_shape=jax.ShapeDtypeStruct((n_tok, D_MODEL), F32),
        compiler_params=_params(),
        name="ffn",
    )(x, mod_rows, n_pre, n_post, w_gate, w_up, w_down)


def _rope_tables(n):
    rows = n // GRID_W
    row = np.repeat(np.arange(rows), GRID_W).astype(np.float32)
    col = np.tile(np.arange(GRID_W), rows).astype(np.float32)
    n_freq = HEAD_DIM // 4
    inv = jnp.asarray(ROPE_THETA, F32) ** (-jnp.arange(n_freq, dtype=F32) / n_freq)
    ang = jnp.concatenate([jnp.asarray(row)[:, None] * inv[None, :],
                           jnp.asarray(col)[:, None] * inv[None, :]], axis=-1)
    cos, sin = jnp.cos(ang), jnp.sin(ang)
    cos_t = jnp.tile(cos, (1, LANES // (HEAD_DIM // 2)))
    sin_t = jnp.tile(jnp.concatenate([-sin, sin], axis=-1), (1, LANES // HEAD_DIM))
    return cos_t, sin_t


def kernel(x_prompt, x_sample, cache_k, cache_v, c, c_ctx, w_ada, b_ada, w_in, q_norm, k_norm,
           w_attn_up, w_pool, pool_scale, w_pool_up, w_out, n_pre_mix, n_post_mix, n_pre_ffn,
           n_post_ffn, w_ffn_gate, w_ffn_up, w_ffn_down):
    batch, seq, _ = x_prompt.shape
    dec_batch, dec_seq, _ = x_sample.shape
    past_len = cache_k.shape[2]
    assert dec_batch <= CTX_MOD_ROW

    c_rows = jnp.concatenate(
        [c, c_ctx[None, :], jnp.zeros((MOD_ROWS - dec_batch - 1, D_MODEL), F32)], axis=0)
    mod_rows = _ada_call(c_rows, w_ada, b_ada).reshape(DEPTH * MOD_ROWS, 1, N_MOD * D_MODEL)

    cos_t, sin_t = _rope_tables(dec_seq)
    vec = lambda a: a.reshape(DEPTH, 1, a.shape[-1])
    q_gain = vec(jnp.tile(q_norm, (1, LANES // HEAD_DIM)))
    k_gain = vec(jnp.tile(k_norm, (1, LANES // HEAD_DIM)))
    w_qkvp_b = w_in[:, :, :GATE_COL0].astype(BF16)
    w_gate_b = w_in[:, :, GATE_COL0:].astype(BF16)
    w_pool_b = w_pool.astype(BF16)
    w_au_b = w_attn_up.astype(BF16)
    w_pu_b = w_pool_up.astype(BF16)
    w_out_b = w_out.astype(BF16)
    w_g_b = w_ffn_gate.astype(BF16)
    w_u_b = w_ffn_up.astype(BF16)
    w_d_b = w_ffn_down.astype(BF16)
    n_pre_mix_v, n_post_mix_v = vec(n_pre_mix), vec(n_post_mix)
    n_pre_ffn_v, n_post_ffn_v = vec(n_pre_ffn), vec(n_post_ffn)
    pool_scale_v = vec(pool_scale)
    ck = cache_k.reshape(dec_batch, DEPTH, past_len, KV_W)
    cv = cache_v.reshape(dec_batch, DEPTH, past_len, KV_W)

    tm = 512
    ctx_seqs_per_step = 4

    def trunk(l, x, *, is_lat):
        seq_len = dec_seq if is_lat else seq
        outs = _mix_call(l, x, mod_rows, n_pre_mix_v, w_qkvp_b, q_gain, k_gain, cos_t, sin_t,
                         w_pool_b, pool_scale_v, ck if is_lat else None, cv if is_lat else None,
                         seq_len=seq_len, n_seq=1 if is_lat else ctx_seqs_per_step, rope=is_lat)
        ao, y = outs[:2]
        x1 = _post_call(l, x, ao, y, mod_rows, n_pre_mix_v, n_post_mix_v, w_gate_b, w_au_b, w_pu_b,
                        w_out_b, seq_len=seq_len, per_seq_mod=is_lat, tm=tm, chunk=256)
        x2 = _ffn_call(l, x1, mod_rows, n_pre_ffn_v, n_post_ffn_v, w_g_b, w_u_b, w_d_b,
                       seq_len=seq_len, per_seq_mod=is_lat, tm=tm, chunk=256)
        return x2, outs[2:]

    y = x_prompt.reshape(batch * seq, D_MODEL)
    z = x_sample.reshape(dec_batch * dec_seq, D_MODEL)
    new_ks, new_vs = [], []
    for l in range(DEPTH):
        y, (nk, nv) = trunk(l, y, is_lat=False)
        z, _ = trunk(l, z, is_lat=True)
        new_ks.append(nk.reshape(batch, seq, N_KV_HEADS, HEAD_DIM))
        new_vs.append(nv.reshape(batch, seq, N_KV_HEADS, HEAD_DIM))

    return (y.reshape(batch, seq, D_MODEL), z.reshape(dec_batch, dec_seq, D_MODEL),
            jnp.stack(new_ks, axis=1), jnp.stack(new_vs, axis=1))
```

```python
import functools

import jax
import jax.numpy as jnp
import numpy as np
from jax import lax
from jax.experimental import pallas as pl
from jax.experimental.pallas import tpu as pltpu

D_MODEL = 1024
DEPTH = 4
GRID_W = 64
HEAD_DIM = 64
N_Q_HEADS = 8
N_KV_HEADS = 2
Q_PER_KV = N_Q_HEADS // N_KV_HEADS
ATTN_W = N_Q_HEADS * HEAD_DIM
KV_W = N_KV_HEADS * HEAD_DIM
POOL_WINDOWS = (2, 4, 8, 16)
POOL_W = D_MODEL // 2
POOL_GW = POOL_W // len(POOL_WINDOWS)
GATE_COL0 = ATTN_W + 2 * KV_W + POOL_W
IN_COLS = GATE_COL0 + 2 * D_MODEL
FFN_HIDDEN = 2816
N_MOD = 6
ROPE_THETA = 10000.0
EPS = 1e-6

LANES = 128
Q_TILE = 256
VT_ROWS = 80
MOD_ROWS = 8
CTX_MOD_ROW = 4
VMEM_LIMIT_BYTES = 56 * 1024 * 1024

F32 = jnp.float32
BF16 = jnp.bfloat16


def _dot(a, b):
    return jnp.dot(a, b, preferred_element_type=F32)


def _dot_nt(a, b):
    return lax.dot_general(a, b, (((1,), (1,)), ((), ())), preferred_element_type=F32)


def _rms(x):
    return x * lax.rsqrt(jnp.mean(x * x, axis=-1, keepdims=True) + EPS)


def _params(n_axes=1):
    return pltpu.CompilerParams(dimension_semantics=("arbitrary",) * n_axes,
                                vmem_limit_bytes=VMEM_LIMIT_BYTES)


def _ada_kernel(c_ref, w_ref, b_ref, o_ref):
    c = c_ref[...]
    s = (c * jax.nn.sigmoid(c)).astype(BF16)
    o_ref[...] = _dot(s, w_ref[...].astype(BF16)) + b_ref[...]


def _ada_call(c_rows, w_ada, b_ada):
    tn = 1536
    n_cols = N_MOD * D_MODEL
    return pl.pallas_call(
        _ada_kernel,
        grid=(DEPTH, n_cols // tn),
        in_specs=[
            pl.BlockSpec((MOD_ROWS, D_MODEL), lambda l, j: (0, 0)),
            pl.BlockSpec((None, D_MODEL, tn), lambda l, j: (l, 0, j)),
            pl.BlockSpec((None, 1, tn), lambda l, j: (l, 0, j)),
        ],
        out_specs=pl.BlockSpec((None, MOD_ROWS, tn), lambda l, j: (l, 0, j)),
        out_shape=jax.ShapeDtypeStruct((DEPTH, MOD_ROWS, n_cols), F32),
        compiler_params=_params(2),
        name="ada_rows",
    )(c_rows, w_ada, b_ada.reshape(DEPTH, 1, n_cols))


def _pool_group(x, window, seq_len):
    half = window // 2
    t = lax.broadcasted_iota(jnp.int32, x.shape, 0)
    shift_down = lambda a, k: jnp.where(t >= k, pltpu.roll(a, k, 0), 0.0)
    shift_up = lambda a, k: jnp.where(t < seq_len - k, pltpu.roll(a, seq_len - k, 0), 0.0)
    back, fwd, k = x, x, 1
    while k < half:
        back = back + shift_down(back, k)
        fwd = fwd + shift_up(fwd, k)
        k *= 2
    total = shift_down(back, 1) + fwd
    count = jnp.minimum(t + half, seq_len) - jnp.maximum(t - half, 0)
    return total / count.astype(F32) - x


def _mix_kernel(x_ref, mod_ref, npre_ref, w_ref, qg_ref, kg_ref, cos_ref, sin_ref, wpool_ref,
                pscale_ref, *rest, seq_len, n_seq, n_cache, rope):
    if n_cache:
        ck_ref, cv_ref, ao_out, y_out = rest[:4]
    else:
        ao_out, y_out, nk_out, nv_out = rest[:4]
    qa_ref, k_ref, vt_ref, p_ref, raw_a, raw_b, s_a, s_b, e_a, e_b = rest[4:]
    raw_ref, s_ref, e_ref = (raw_a, raw_b), (s_a, s_b), (e_a, e_b)
    n_keys = n_cache + seq_len
    tiles_per_seq = seq_len // Q_TILE
    n_tiles = n_seq * tiles_per_seq

    mod = mod_ref[...]
    sh1 = mod[:, 0:D_MODEL]
    sc1 = mod[:, D_MODEL:2 * D_MODEL]

    lane = lax.broadcasted_iota(jnp.int32, (Q_TILE, LANES), 1)
    first_head = lane < HEAD_DIM
    second_half = (lane & (HEAD_DIM // 2)) != 0

    def head_norm(t, gain):
        t2 = t * t
        s_lo = jnp.sum(jnp.where(first_head, t2, 0.0), axis=-1, keepdims=True)
        s_hi = jnp.sum(jnp.where(first_head, 0.0, t2), axis=-1, keepdims=True)
        ms = jnp.where(first_head, s_lo, s_hi) * (1.0 / HEAD_DIM)
        return (t * lax.rsqrt(ms + EPS)) * gain

    def rotary(t, pos0):
        partner = jnp.where(second_half, pltpu.roll(t, HEAD_DIM // 2, 1),
                            pltpu.roll(t, LANES - HEAD_DIM // 2, 1))
        return t * cos_ref[pos0:pos0 + Q_TILE, :] + partner * sin_ref[pos0:pos0 + Q_TILE, :]

    ones_rows = jnp.ones((VT_ROWS - HEAD_DIM, n_keys), BF16)
    for s_idx in range(n_seq):
        for kv in range(N_KV_HEADS):
            vt_ref[s_idx, kv * VT_ROWS + HEAD_DIM:(kv + 1) * VT_ROWS, :] = ones_rows
        if n_cache:
            k_ref[s_idx, 0:n_cache, :] = ck_ref[0, 0].astype(BF16)
            cvt = cv_ref[0, 0].T.astype(BF16)
            for kv in range(N_KV_HEADS):
                vt_ref[s_idx, kv * VT_ROWS:kv * VT_ROWS + HEAD_DIM, 0:n_cache] = (
                    cvt[kv * HEAD_DIM:(kv + 1) * HEAD_DIM, :])

    def project(tile, slot):
        r0 = tile * Q_TILE
        x = x_ref[r0:r0 + Q_TILE, :]
        hb = ((_rms(x) * npre_ref[...]) * (1.0 + sc1) + sh1).astype(BF16)
        raw_ref[slot][...] = _dot(hb, w_ref[...])

    def epilogue(tile, slot):
        s_idx, pos0 = tile // tiles_per_seq, (tile % tiles_per_seq) * Q_TILE
        r0 = tile * Q_TILE
        raw = raw_ref[slot]
        for j in range(ATTN_W // LANES):
            t = head_norm(raw[:, j * LANES:(j + 1) * LANES], qg_ref[...])
            if rope:
                t = rotary(t, pos0)
            t = t * (HEAD_DIM ** -0.5)
            swapped = pltpu.roll(t, HEAD_DIM, 1)
            if (2 * j) // Q_PER_KV == 0:
                even = jnp.where(first_head, t, 0.0)
                odd = jnp.where(first_head, swapped, 0.0)
            else:
                even = jnp.where(first_head, 0.0, swapped)
                odd = jnp.where(first_head, 0.0, t)
            qa_ref[tile, (2 * j) * Q_TILE:(2 * j + 1) * Q_TILE, :] = even.astype(BF16)
            qa_ref[tile, (2 * j + 1) * Q_TILE:(2 * j + 2) * Q_TILE, :] = odd.astype(BF16)
        kn = head_norm(raw[:, ATTN_W:ATTN_W + KV_W], kg_ref[...])
        v = raw[:, ATTN_W + KV_W:ATTN_W + 2 * KV_W]
        vt_f32 = v.T
        if not n_cache:
            nk_out[s_idx, :, pos0:pos0 + Q_TILE] = kn.T
            nv_out[s_idx, :, pos0:pos0 + Q_TILE] = vt_f32
        if rope:
            kn = rotary(kn, pos0)
        k0 = n_cache + pos0
        k_ref[s_idx, k0:k0 + Q_TILE, :] = kn.astype(BF16)
        vt = vt_f32.astype(BF16)
        for kv in range(N_KV_HEADS):
            vt_ref[s_idx, kv * VT_ROWS:kv * VT_ROWS + HEAD_DIM, k0:k0 + Q_TILE] = (
                vt[kv * HEAD_DIM:(kv + 1) * HEAD_DIM, :])
        p_ref[r0:r0 + Q_TILE, :] = raw[:, ATTN_W + 2 * KV_W:GATE_COL0]

    project(0, 0)
    for tile in range(n_tiles):
        if tile + 1 < n_tiles:
            project(tile + 1, (tile + 1) % 2)
        epilogue(tile, tile % 2)

    n_pairs = N_Q_HEADS // 2
    n_steps = n_tiles * n_pairs

    def scores(step, slot):
        tile, pair = step // n_pairs, step % n_pairs
        q2 = qa_ref[tile, pair * 2 * Q_TILE:(pair + 1) * 2 * Q_TILE, :]
        s_ref[slot][...] = _dot_nt(k_ref[tile // tiles_per_seq], q2)

    def exponent(step, slot):
        s = s_ref[slot][...]
        e_ref[slot][...] = jnp.exp(s - jnp.max(s, axis=0, keepdims=True)).astype(BF16)

    def values(step, slot):
        tile, pair = step // n_pairs, step % n_pairs
        kv = (2 * pair) // Q_PER_KV
        vt = vt_ref[tile // tiles_per_seq, kv * VT_ROWS:(kv + 1) * VT_ROWS, :]
        ot = _dot(vt, e_ref[slot][...])
        ot = ot[0:HEAD_DIM] * (1.0 / ot[HEAD_DIM:HEAD_DIM + 1])
        both = jnp.concatenate([ot[:, 0:Q_TILE], ot[:, Q_TILE:2 * Q_TILE]], axis=0)
        ao_out[tile * Q_TILE:(tile + 1) * Q_TILE, pair * LANES:(pair + 1) * LANES] = both.T.astype(BF16)

    scores(0, 0)
    for step in range(n_steps + 1):
        if step + 1 < n_steps:
            scores(step + 1, (step + 1) % 2)
        if step < n_steps:
            exponent(step, step % 2)
        if step >= 1:
            values(step - 1, (step - 1) % 2)

    for s_idx in range(n_seq):
        r0 = s_idx * seq_len
        for g, window in enumerate(POOL_WINDOWS):
            cols = slice(g * POOL_GW, (g + 1) * POOL_GW)
            d = _pool_group(p_ref[r0:r0 + seq_len, cols], window, seq_len)
            yg = _dot(d.astype(BF16), wpool_ref[g]) * pscale_ref[:, cols]
            y_out[r0:r0 + seq_len, cols] = yg.astype(BF16)


def _mix_call(l, x, mod_rows, n_pre, w_qkvp, q_gain, k_gain, cos_t, sin_t, w_pool, pool_scale,
              cache_k, cache_v, *, seq_len, n_seq, rope):
    n_tok = x.shape[0]
    rows = n_seq * seq_len
    is_lat = cache_k is not None
    n_cache = cache_k.shape[2] if is_lat else 0
    n_keys = n_cache + seq_len
    if is_lat:
        assert n_seq == 1
        mod_idx = lambda i: (l * MOD_ROWS + i, 0, 0)
    else:
        mod_idx = lambda i: (l * MOD_ROWS + CTX_MOD_ROW, 0, 0)
    lay = lambda i: (l, 0, 0)
    row = lambda i: (i, 0)
    in_specs = [
        pl.BlockSpec((rows, D_MODEL), row),
        pl.BlockSpec((None, 1, N_MOD * D_MODEL), mod_idx),
        pl.BlockSpec((None, 1, D_MODEL), lay),
        pl.BlockSpec((None, D_MODEL, GATE_COL0), lay),
        pl.BlockSpec((None, 1, LANES), lay),
        pl.BlockSpec((None, 1, LANES), lay),
        pl.BlockSpec((seq_len, LANES), lambda i: (0, 0)),
        pl.BlockSpec((seq_len, LANES), lambda i: (0, 0)),
        pl.BlockSpec((None, len(POOL_WINDOWS), POOL_GW, POOL_GW), lambda i: (l, 0, 0, 0)),
        pl.BlockSpec((None, 1, POOL_W), lay),
    ]
    args = [x, mod_rows, n_pre, w_qkvp, q_gain, k_gain, cos_t, sin_t, w_pool, pool_scale]
    out_shape = [jax.ShapeDtypeStruct((n_tok, ATTN_W), BF16), jax.ShapeDtypeStruct((n_tok, POOL_W), BF16)]
    out_specs = [pl.BlockSpec((rows, ATTN_W), row), pl.BlockSpec((rows, POOL_W), row)]
    if is_lat:
        cache_spec = pl.BlockSpec((1, 1, n_cache, KV_W), lambda i: (i, l, 0, 0))
        in_specs += [cache_spec, cache_spec]
        args += [cache_k, cache_v]
    else:
        out_shape += [jax.ShapeDtypeStruct((n_tok // seq_len, KV_W, seq_len), F32)] * 2
        out_specs += [pl.BlockSpec((n_seq, KV_W, seq_len), lambda i: (i, 0, 0))] * 2
    scratch = [
        pltpu.VMEM((rows // Q_TILE, N_Q_HEADS * Q_TILE, LANES), BF16),
        pltpu.VMEM((n_seq, n_keys, KV_W), BF16),
        pltpu.VMEM((n_seq, N_KV_HEADS * VT_ROWS, n_keys), BF16),
        pltpu.VMEM((rows, POOL_W), F32),
    ]
    scratch += [pltpu.VMEM((Q_TILE, GATE_COL0), F32)] * 2
    scratch += [pltpu.VMEM((n_keys, 2 * Q_TILE), F32)] * 2
    scratch += [pltpu.VMEM((n_keys, 2 * Q_TILE), BF16)] * 2
    return pl.pallas_call(
        functools.partial(_mix_kernel, seq_len=seq_len, n_seq=n_seq, n_cache=n_cache, rope=rope),
        grid=(n_tok // rows,),
        in_specs=in_specs,
        out_specs=out_specs,
        out_shape=out_shape,
        scratch_shapes=scratch,
        compiler_params=_params(),
        name="mix_lat" if is_lat else "mix_ctx",
    )(*args)


def _tail_kernel(x_ref, ao_ref, y_ref, mod_ref, npre_mix_ref, npost_mix_ref, npre_ffn_ref,
                 npost_ffn_ref, win_ref, wau_ref, wpu_ref, wout_ref, wg_ref, wu_ref, wd_ref, o_ref,
                 x1_a, x1_b, *, sub):
    mod = mod_ref[...]
    sh1, sc1, g1, sh2, sc2, g2 = [mod[:, i * D_MODEL:(i + 1) * D_MODEL] for i in range(N_MOD)]
    x1_ref = (x1_a, x1_b)
    n_sub = x_ref.shape[0] // sub

    def post(i, slot):
        r0 = i * sub
        x = x_ref[r0:r0 + sub, :]
        hb = ((_rms(x) * npre_mix_ref[...]) * (1.0 + sc1) + sh1).astype(BF16)
        g_attn = jax.nn.sigmoid(_dot(hb, win_ref[:, GATE_COL0:GATE_COL0 + D_MODEL]))
        g_pool = jax.nn.sigmoid(_dot(hb, win_ref[:, GATE_COL0 + D_MODEL:IN_COLS]))
        attn = _dot(ao_ref[r0:r0 + sub, :], wau_ref[...])
        pool = _dot(y_ref[r0:r0 + sub, :], wpu_ref[...])
        mixed = g_attn * attn + g_pool * pool
        m = _dot(mixed.astype(BF16), wout_ref[...])
        x1_ref[slot][...] = x + g1 * (_rms(m) * npost_mix_ref[...])

    def ffn(i, slot):
        r0 = i * sub
        x1 = x1_ref[slot][...]
        hb = ((_rms(x1) * npre_ffn_ref[...]) * (1.0 + sc2) + sh2).astype(BF16)
        g = _dot(hb, wg_ref[...])
        u = _dot(hb, wu_ref[...])
        a = ((g * jax.nn.sigmoid(g)) * u).astype(BF16)
        f = _dot(a, wd_ref[...])
        o_ref[r0:r0 + sub, :] = x1 + g2 * (_rms(f) * npost_ffn_ref[...])

    post(0, 0)
    for i in range(n_sub):
        if i + 1 < n_sub:
            post(i + 1, (i + 1) % 2)
        ffn(i, i % 2)


def _tail_call(l, x, ao, y, mod_rows, n_pre_mix, n_post_mix, n_pre_ffn, n_post_ffn, w_in,
               w_attn_up, w_pool_up, w_out, w_ffn_gate, w_ffn_up, w_ffn_down,
               *, seq_len, per_seq_mod, tm, sub):
    n_tok = x.shape[0]
    if per_seq_mod:
        mod_idx = lambda i: (l * MOD_ROWS + i // (seq_len // tm), 0, 0)
    else:
        mod_idx = lambda i: (l * MOD_ROWS + CTX_MOD_ROW, 0, 0)
    lay = lambda i: (l, 0, 0)
    row = lambda i: (i, 0)
    vec_spec = pl.BlockSpec((None, 1, D_MODEL), lay)
    resident = lambda r, c: pl.BlockSpec((None, r, c), lay, pipeline_mode=pl.Buffered(1))
    return pl.pallas_call(
        functools.partial(_tail_kernel, sub=sub),
        grid=(n_tok // tm,),
        in_specs=[
            pl.BlockSpec((tm, D_MODEL), row),
            pl.BlockSpec((tm, ATTN_W), row),
            pl.BlockSpec((tm, POOL_W), row),
            pl.BlockSpec((None, 1, N_MOD * D_MODEL), mod_idx),
            vec_spec, vec_spec, vec_spec, vec_spec,
            resident(D_MODEL, IN_COLS),
            resident(ATTN_W, D_MODEL),
            resident(POOL_W, D_MODEL),
            resident(D_MODEL, D_MODEL),
            resident(D_MODEL, FFN_HIDDEN),
            resident(D_MODEL, FFN_HIDDEN),
            resident(FFN_HIDDEN, D_MODEL),
        ],
        out_specs=pl.BlockSpec((tm, D_MODEL), row),
        out_shape=jax.ShapeDtypeStruct((n_tok, D_MODEL), F32),
        scratch_shapes=[pltpu.VMEM((sub, D_MODEL), F32)] * 2,
        compiler_params=_params(),
        name="tail",
    )(x, ao, y, mod_rows, n_pre_mix, n_post_mix, n_pre_ffn, n_post_ffn, w_in, w_attn_up,
      w_pool_up, w_out, w_ffn_gate, w_ffn_up, w_ffn_down)


def _rope_tables(n):
    rows = n // GRID_W
    row = np.repeat(np.arange(rows), GRID_W).astype(np.float32)
    col = np.tile(np.arange(GRID_W), rows).astype(np.float32)
    n_freq = HEAD_DIM // 4
    inv = jnp.asarray(ROPE_THETA, F32) ** (-jnp.arange(n_freq, dtype=F32) / n_freq)
    ang = jnp.concatenate([jnp.asarray(row)[:, None] * inv[None, :],
                           jnp.asarray(col)[:, None] * inv[None, :]], axis=-1)
    cos, sin = jnp.cos(ang), jnp.sin(ang)
    cos_t = jnp.tile(cos, (1, LANES // (HEAD_DIM // 2)))
    sin_t = jnp.tile(jnp.concatenate([-sin, sin], axis=-1), (1, LANES // HEAD_DIM))
    return cos_t, sin_t


def kernel(x_prompt, x_sample, cache_k, cache_v, c, c_ctx, w_ada, b_ada, w_in, q_norm, k_norm,
           w_attn_up, w_pool, pool_scale, w_pool_up, w_out, n_pre_mix, n_post_mix, n_pre_ffn,
           n_post_ffn, w_ffn_gate, w_ffn_up, w_ffn_down):
    batch, seq, _ = x_prompt.shape
    dec_batch, dec_seq, _ = x_sample.shape
    past_len = cache_k.shape[2]
    assert dec_batch <= CTX_MOD_ROW

    c_rows = jnp.concatenate(
        [c, c_ctx[None, :], jnp.zeros((MOD_ROWS - dec_batch - 1, D_MODEL), F32)], axis=0)
    mod_rows = _ada_call(c_rows, w_ada, b_ada).reshape(DEPTH * MOD_ROWS, 1, N_MOD * D_MODEL)

    cos_t, sin_t = _rope_tables(dec_seq)
    vec = lambda a: a.reshape(DEPTH, 1, a.shape[-1])
    q_gain = vec(jnp.tile(q_norm, (1, LANES // HEAD_DIM)))
    k_gain = vec(jnp.tile(k_norm, (1, LANES // HEAD_DIM)))
    w_in_b = w_in.astype(BF16)
    w_pool_b = w_pool.astype(BF16)
    w_au_b = w_attn_up.astype(BF16)
    w_pu_b = w_pool_up.astype(BF16)
    w_out_b = w_out.astype(BF16)
    w_g_b = w_ffn_gate.astype(BF16)
    w_u_b = w_ffn_up.astype(BF16)
    w_d_b = w_ffn_down.astype(BF16)
    n_pre_mix_v, n_post_mix_v = vec(n_pre_mix), vec(n_post_mix)
    n_pre_ffn_v, n_post_ffn_v = vec(n_pre_ffn), vec(n_post_ffn)
    pool_scale_v = vec(pool_scale)
    ck = cache_k.reshape(dec_batch, DEPTH, past_len, KV_W)
    cv = cache_v.reshape(dec_batch, DEPTH, past_len, KV_W)

    tm = 512
    ctx_seqs_per_step = 4

    def trunk(l, x, *, is_lat):
        seq_len = dec_seq if is_lat else seq
        outs = _mix_call(l, x, mod_rows, n_pre_mix_v, w_in_b, q_gain, k_gain, cos_t, sin_t,
                         w_pool_b, pool_scale_v, ck if is_lat else None, cv if is_lat else None,
                         seq_len=seq_len, n_seq=1 if is_lat else ctx_seqs_per_step, rope=is_lat)
        ao, y = outs[:2]
        x2 = _tail_call(l, x, ao, y, mod_rows, n_pre_mix_v, n_post_mix_v, n_pre_ffn_v, n_post_ffn_v,
                        w_in_b, w_au_b, w_pu_b, w_out_b, w_g_b, w_u_b, w_d_b,
                        seq_len=seq_len, per_seq_mod=is_lat, tm=tm, sub=128)
        return x2, outs[2:]

    y = x_prompt.reshape(batch * seq, D_MODEL)
    z = x_sample.reshape(dec_batch * dec_seq, D_MODEL)
    new_ks, new_vs = [], []
    for l in range(DEPTH):
        y, (nk, nv) = trunk(l, y, is_lat=False)
        z, _ = trunk(l, z, is_lat=True)
        new_ks.append(nk)
        new_vs.append(nv)

    def kv_layout(parts):
        a = jnp.stack(parts, axis=1).reshape(batch, DEPTH, N_KV_HEADS, HEAD_DIM, seq)
        return jnp.transpose(a, (0, 1, 4, 2, 3))

    return (y.reshape(batch, seq, D_MODEL), z.reshape(dec_batch, dec_seq, D_MODEL),
            kv_layout(new_ks), kv_layout(new_vs))
```

```python
import functools

import jax
import jax.numpy as jnp
import numpy as np
from jax import lax
from jax.experimental import pallas as pl
from jax.experimental.pallas import tpu as pltpu

D_MODEL = 1024
DEPTH = 4
GRID_W = 64
HEAD_DIM = 64
N_Q_HEADS = 8
N_KV_HEADS = 2
Q_PER_KV = N_Q_HEADS // N_KV_HEADS
ATTN_W = N_Q_HEADS * HEAD_DIM
KV_W = N_KV_HEADS * HEAD_DIM
POOL_WINDOWS = (2, 4, 8, 16)
POOL_W = D_MODEL // 2
POOL_GW = POOL_W // len(POOL_WINDOWS)
GATE_COL0 = ATTN_W + 2 * KV_W + POOL_W
IN_COLS = GATE_COL0 + 2 * D_MODEL
FFN_HIDDEN = 2816
N_MOD = 6
ROPE_THETA = 10000.0
EPS = 1e-6

LANES = 128
Q_TILE = 256
VT_ROWS = 80
MOD_ROWS = 8
CTX_MOD_ROW = 4
VMEM_LIMIT_BYTES = 56 * 1024 * 1024

F32 = jnp.float32
BF16 = jnp.bfloat16


def _dot(a, b):
    return jnp.dot(a, b, preferred_element_type=F32)


def _dot_nt(a, b):
    return lax.dot_general(a, b, (((1,), (1,)), ((), ())), preferred_element_type=F32)


def _rms(x):
    return x * lax.rsqrt(jnp.mean(x * x, axis=-1, keepdims=True) + EPS)


def _params(n_axes=1):
    return pltpu.CompilerParams(dimension_semantics=("arbitrary",) * n_axes,
                                vmem_limit_bytes=VMEM_LIMIT_BYTES)


def _ada_kernel(c_ref, w_ref, b_ref, o_ref):
    c = c_ref[...]
    s = (c * jax.nn.sigmoid(c)).astype(BF16)
    o_ref[...] = _dot(s, w_ref[...].astype(BF16)) + b_ref[...]


def _ada_call(c_rows, w_ada, b_ada):
    tn = 1536
    n_cols = N_MOD * D_MODEL
    return pl.pallas_call(
        _ada_kernel,
        grid=(DEPTH, n_cols // tn),
        in_specs=[
            pl.BlockSpec((MOD_ROWS, D_MODEL), lambda l, j: (0, 0)),
            pl.BlockSpec((None, D_MODEL, tn), lambda l, j: (l, 0, j)),
            pl.BlockSpec((None, 1, tn), lambda l, j: (l, 0, j)),
        ],
        out_specs=pl.BlockSpec((None, MOD_ROWS, tn), lambda l, j: (l, 0, j)),
        out_shape=jax.ShapeDtypeStruct((DEPTH, MOD_ROWS, n_cols), F32),
        compiler_params=_params(2),
        name="ada_rows",
    )(c_rows, w_ada, b_ada.reshape(DEPTH, 1, n_cols))


def _pool_group(x, window, seq_len):
    half = window // 2
    t = lax.broadcasted_iota(jnp.int32, x.shape, 0)
    shift_down = lambda a, k: jnp.where(t >= k, pltpu.roll(a, k, 0), 0.0)
    shift_up = lambda a, k: jnp.where(t < seq_len - k, pltpu.roll(a, seq_len - k, 0), 0.0)
    back, fwd, k = x, x, 1
    while k < half:
        back = back + shift_down(back, k)
        fwd = fwd + shift_up(fwd, k)
        k *= 2
    total = shift_down(back, 1) + fwd
    count = jnp.minimum(t + half, seq_len) - jnp.maximum(t - half, 0)
    return total / count.astype(F32) - x


def _mix_kernel(x_ref, mod_ref, npre_ref, w_ref, qg_ref, kg_ref, cos_ref, sin_ref, wpool_ref,
                pscale_ref, *rest, seq_len, n_seq, n_cache, rope):
    if n_cache:
        ck_ref, cv_ref, ao_out, y_out = rest[:4]
    else:
        ao_out, y_out, nk_out, nv_out = rest[:4]
    qa_ref, k_ref, vt_ref, p_ref, raw_a, raw_b, s_a, s_b, e_a, e_b = rest[4:]
    raw_ref, s_ref, e_ref = (raw_a, raw_b), (s_a, s_b), (e_a, e_b)
    n_keys = n_cache + seq_len
    tiles_per_seq = seq_len // Q_TILE
    n_tiles = n_seq * tiles_per_seq

    mod = mod_ref[...]
    sh1 = mod[:, 0:D_MODEL]
    sc1 = mod[:, D_MODEL:2 * D_MODEL]

    lane = lax.broadcasted_iota(jnp.int32, (Q_TILE, LANES), 1)
    first_head = lane < HEAD_DIM
    second_half = (lane & (HEAD_DIM // 2)) != 0

    def head_norm(t, gain):
        t2 = t * t
        s_lo = jnp.sum(jnp.where(first_head, t2, 0.0), axis=-1, keepdims=True)
        s_hi = jnp.sum(jnp.where(first_head, 0.0, t2), axis=-1, keepdims=True)
        ms = jnp.where(first_head, s_lo, s_hi) * (1.0 / HEAD_DIM)
        return (t * lax.rsqrt(ms + EPS)) * gain

    def rotary(t, pos0):
        partner = jnp.where(second_half, pltpu.roll(t, HEAD_DIM // 2, 1),
                            pltpu.roll(t, LANES - HEAD_DIM // 2, 1))
        return t * cos_ref[pos0:pos0 + Q_TILE, :] + partner * sin_ref[pos0:pos0 + Q_TILE, :]

    ones_rows = jnp.ones((VT_ROWS - HEAD_DIM, n_keys), BF16)
    for s_idx in range(n_seq):
        for kv in range(N_KV_HEADS):
            vt_ref[s_idx, kv * VT_ROWS + HEAD_DIM:(kv + 1) * VT_ROWS, :] = ones_rows
        if n_cache:
            k_ref[s_idx, 0:n_cache, :] = ck_ref[0, 0].astype(BF16)
            cvt = cv_ref[0, 0].T.astype(BF16)
            for kv in range(N_KV_HEADS):
                vt_ref[s_idx, kv * VT_ROWS:kv * VT_ROWS + HEAD_DIM, 0:n_cache] = (
                    cvt[kv * HEAD_DIM:(kv + 1) * HEAD_DIM, :])

    def project(tile, slot):
        r0 = tile * Q_TILE
        x = x_ref[r0:r0 + Q_TILE, :]
        hb = ((_rms(x) * npre_ref[...]) * (1.0 + sc1) + sh1).astype(BF16)
        raw_ref[slot][...] = _dot(hb, w_ref[...])

    def epilogue(tile, slot):
        s_idx, pos0 = tile // tiles_per_seq, (tile % tiles_per_seq) * Q_TILE
        r0 = tile * Q_TILE
        raw = raw_ref[slot]
        for j in range(ATTN_W // LANES):
            t = head_norm(raw[:, j * LANES:(j + 1) * LANES], qg_ref[...])
            if rope:
                t = rotary(t, pos0)
            t = t * (HEAD_DIM ** -0.5)
            swapped = pltpu.roll(t, HEAD_DIM, 1)
            if (2 * j) // Q_PER_KV == 0:
                even = jnp.where(first_head, t, 0.0)
                odd = jnp.where(first_head, swapped, 0.0)
            else:
                even = jnp.where(first_head, 0.0, swapped)
                odd = jnp.where(first_head, 0.0, t)
            qa_ref[tile, (2 * j) * Q_TILE:(2 * j + 1) * Q_TILE, :] = even.astype(BF16)
            qa_ref[tile, (2 * j + 1) * Q_TILE:(2 * j + 2) * Q_TILE, :] = odd.astype(BF16)
        kn = head_norm(raw[:, ATTN_W:ATTN_W + KV_W], kg_ref[...])
        v = raw[:, ATTN_W + KV_W:ATTN_W + 2 * KV_W]
        vt_f32 = v.T
        if not n_cache:
            nk_out[s_idx, :, pos0:pos0 + Q_TILE] = kn.T
            nv_out[s_idx, :, pos0:pos0 + Q_TILE] = vt_f32
        if rope:
            kn = rotary(kn, pos0)
        k0 = n_cache + pos0
        k_ref[s_idx, k0:k0 + Q_TILE, :] = kn.astype(BF16)
        vt = vt_f32.astype(BF16)
        for kv in range(N_KV_HEADS):
            vt_ref[s_idx, kv * VT_ROWS:kv * VT_ROWS + HEAD_DIM, k0:k0 + Q_TILE] = (
                vt[kv * HEAD_DIM:(kv + 1) * HEAD_DIM, :])
        p_ref[r0:r0 + Q_TILE, :] = raw[:, ATTN_W + 2 * KV_W:GATE_COL0]

    project(0, 0)
    for tile in range(n_tiles):
        if tile + 1 < n_tiles:
            project(tile + 1, (tile + 1) % 2)
        epilogue(tile, tile % 2)

    n_pairs = N_Q_HEADS // 2
    n_steps = n_tiles * n_pairs

    def scores(step, slot):
        tile, pair = step // n_pairs, step % n_pairs
        q2 = qa_ref[tile, pair * 2 * Q_TILE:(pair + 1) * 2 * Q_TILE, :]
        s_ref[slot][...] = _dot_nt(k_ref[tile // tiles_per_seq], q2)

    def exponent(step, slot):
        s = s_ref[slot][...]
        e_ref[slot][...] = jnp.exp(s - jnp.max(s, axis=0, keepdims=True)).astype(BF16)

    def values(step, slot):
        tile, pair = step // n_pairs, step % n_pairs
        kv = (2 * pair) // Q_PER_KV
        vt = vt_ref[tile // tiles_per_seq, kv * VT_ROWS:(kv + 1) * VT_ROWS, :]
        ot = _dot(vt, e_ref[slot][...])
        ot = ot[0:HEAD_DIM] * (1.0 / ot[HEAD_DIM:HEAD_DIM + 1])
        both = jnp.concatenate([ot[:, 0:Q_TILE], ot[:, Q_TILE:2 * Q_TILE]], axis=0)
        ao_out[tile * Q_TILE:(tile + 1) * Q_TILE, pair * LANES:(pair + 1) * LANES] = both.T.astype(BF16)

    scores(0, 0)
    for step in range(n_steps + 1):
        if step + 1 < n_steps:
            scores(step + 1, (step + 1) % 2)
        if step < n_steps:
            exponent(step, step % 2)
        if step >= 1:
            values(step - 1, (step - 1) % 2)

    for s_idx in range(n_seq):
        r0 = s_idx * seq_len
        for g, window in enumerate(POOL_WINDOWS):
            cols = slice(g * POOL_GW, (g + 1) * POOL_GW)
            d = _pool_group(p_ref[r0:r0 + seq_len, cols], window, seq_len)
            yg = _dot(d.astype(BF16), wpool_ref[g]) * pscale_ref[:, cols]
            y_out[r0:r0 + seq_len, cols] = yg.astype(BF16)


def _mix_call(l, x, mod_rows, n_pre, w_qkvp, q_gain, k_gain, cos_t, sin_t, w_pool, pool_scale,
              cache_k, cache_v, *, seq_len, n_seq, rope):
    n_tok = x.shape[0]
    rows = n_seq * seq_len
    is_lat = cache_k is not None
    n_cache = cache_k.shape[2] if is_lat else 0
    n_keys = n_cache + seq_len
    if is_lat:
        assert n_seq == 1
        mod_idx = lambda i: (l * MOD_ROWS + i, 0, 0)
    else:
        mod_idx = lambda i: (l * MOD_ROWS + CTX_MOD_ROW, 0, 0)
    lay = lambda i: (l, 0, 0)
    row = lambda i: (i, 0)
    in_specs = [
        pl.BlockSpec((rows, D_MODEL), row),
        pl.BlockSpec((None, 1, N_MOD * D_MODEL), mod_idx),
        pl.BlockSpec((None, 1, D_MODEL), lay),
        pl.BlockSpec((None, D_MODEL, GATE_COL0), lay),
        pl.BlockSpec((None, 1, LANES), lay),
        pl.BlockSpec((None, 1, LANES), lay),
        pl.BlockSpec((seq_len, LANES), lambda i: (0, 0)),
        pl.BlockSpec((seq_len, LANES), lambda i: (0, 0)),
        pl.BlockSpec((None, len(POOL_WINDOWS), POOL_GW, POOL_GW), lambda i: (l, 0, 0, 0)),
        pl.BlockSpec((None, 1, POOL_W), lay),
    ]
    args = [x, mod_rows, n_pre, w_qkvp, q_gain, k_gain, cos_t, sin_t, w_pool, pool_scale]
    out_shape = [jax.ShapeDtypeStruct((n_tok, ATTN_W), BF16), jax.ShapeDtypeStruct((n_tok, POOL_W), BF16)]
    out_specs = [pl.BlockSpec((rows, ATTN_W), row), pl.BlockSpec((rows, POOL_W), row)]
    if is_lat:
        cache_spec = pl.BlockSpec((1, 1, n_cache, KV_W), lambda i: (i, l, 0, 0))
        in_specs += [cache_spec, cache_spec]
        args += [cache_k, cache_v]
    else:
        out_shape += [jax.ShapeDtypeStruct((n_tok // seq_len, KV_W, seq_len), F32)] * 2
        out_specs += [pl.BlockSpec((n_seq, KV_W, seq_len), lambda i: (i, 0, 0))] * 2
    scratch = [
        pltpu.VMEM((rows // Q_TILE, N_Q_HEADS * Q_TILE, LANES), BF16),
        pltpu.VMEM((n_seq, n_keys, KV_W), BF16),
        pltpu.VMEM((n_seq, N_KV_HEADS * VT_ROWS, n_keys), BF16),
        pltpu.VMEM((rows, POOL_W), F32),
    ]
    scratch += [pltpu.VMEM((Q_TILE, GATE_COL0), F32)] * 2
    scratch += [pltpu.VMEM((n_keys, 2 * Q_TILE), F32)] * 2
    scratch += [pltpu.VMEM((n_keys, 2 * Q_TILE), BF16)] * 2
    return pl.pallas_call(
        functools.partial(_mix_kernel, seq_len=seq_len, n_seq=n_seq, n_cache=n_cache, rope=rope),
        grid=(n_tok // rows,),
        in_specs=in_specs,
        out_specs=out_specs,
        out_shape=out_shape,
        scratch_shapes=scratch,
        compiler_params=_params(),
        name="mix_lat" if is_lat else "mix_ctx",
    )(*args)


def _tail_kernel(x_ref, ao_ref, y_ref, mod_ref, npre_mix_ref, npost_mix_ref, npre_ffn_ref,
                 npost_ffn_ref, win_ref, wau_ref, wpu_ref, wout_ref, wg_ref, wu_ref, wd_ref, o_ref,
                 x1_a, x1_b, *, sub):
    mod = mod_ref[...]
    sh1, sc1, g1, sh2, sc2, g2 = [mod[:, i * D_MODEL:(i + 1) * D_MODEL] for i in range(N_MOD)]
    x1_ref = (x1_a, x1_b)
    n_sub = x_ref.shape[0] // sub

    def post(i, slot):
        r0 = i * sub
        x = x_ref[r0:r0 + sub, :]
        hb = ((_rms(x) * npre_mix_ref[...]) * (1.0 + sc1) + sh1).astype(BF16)
        g_attn = jax.nn.sigmoid(_dot(hb, win_ref[:, GATE_COL0:GATE_COL0 + D_MODEL]))
        g_pool = jax.nn.sigmoid(_dot(hb, win_ref[:, GATE_COL0 + D_MODEL:IN_COLS]))
        attn = _dot(ao_ref[r0:r0 + sub, :], wau_ref[...])
        pool = _dot(y_ref[r0:r0 + sub, :], wpu_ref[...])
        mixed = g_attn * attn + g_pool * pool
        m = _dot(mixed.astype(BF16), wout_ref[...])
        x1_ref[slot][...] = x + g1 * (_rms(m) * npost_mix_ref[...])

    def ffn(i, slot):
        r0 = i * sub
        x1 = x1_ref[slot][...]
        hb = ((_rms(x1) * npre_ffn_ref[...]) * (1.0 + sc2) + sh2).astype(BF16)
        g = _dot(hb, wg_ref[...])
        u = _dot(hb, wu_ref[...])
        a = ((g * jax.nn.sigmoid(g)) * u).astype(BF16)
        f = _dot(a, wd_ref[...])
        o_ref[r0:r0 + sub, :] = x1 + g2 * (_rms(f) * npost_ffn_ref[...])

    post(0, 0)
    for i in range(n_sub):
        if i + 1 < n_sub:
            post(i + 1, (i + 1) % 2)
        ffn(i, i % 2)


def _tail_call(l, x, ao, y, mod_rows, n_pre_mix, n_post_mix, n_pre_ffn, n_post_ffn, w_in,
               w_attn_up, w_pool_up, w_out, w_ffn_gate, w_ffn_up, w_ffn_down,
               *, seq_len, per_seq_mod, tm, sub):
    n_tok = x.shape[0]
    if per_seq_mod:
        mod_idx = lambda i: (l * MOD_ROWS + i // (seq_len // tm), 0, 0)
    else:
        mod_idx = lambda i: (l * MOD_ROWS + CTX_MOD_ROW, 0, 0)
    lay = lambda i: (l, 0, 0)
    row = lambda i: (i, 0)
    vec_spec = pl.BlockSpec((None, 1, D_MODEL), lay)
    resident = lambda r, c: pl.BlockSpec((None, r, c), lay, pipeline_mode=pl.Buffered(1))
    return pl.pallas_call(
        functools.partial(_tail_kernel, sub=sub),
        grid=(n_tok // tm,),
        in_specs=[
            pl.BlockSpec((tm, D_MODEL), row),
            pl.BlockSpec((tm, ATTN_W), row),
            pl.BlockSpec((tm, POOL_W), row),
            pl.BlockSpec((None, 1, N_MOD * D_MODEL), mod_idx),
            vec_spec, vec_spec, vec_spec, vec_spec,
            resident(D_MODEL, IN_COLS),
            resident(ATTN_W, D_MODEL),
            resident(POOL_W, D_MODEL),
            resident(D_MODEL, D_MODEL),
            resident(D_MODEL, FFN_HIDDEN),
            resident(D_MODEL, FFN_HIDDEN),
            resident(FFN_HIDDEN, D_MODEL),
        ],
        out_specs=pl.BlockSpec((tm, D_MODEL), row),
        out_shape=jax.ShapeDtypeStruct((n_tok, D_MODEL), F32),
        scratch_shapes=[pltpu.VMEM((sub, D_MODEL), F32)] * 2,
        compiler_params=_params(),
        name="tail",
    )(x, ao, y, mod_rows, n_pre_mix, n_post_mix, n_pre_ffn, n_post_ffn, w_in, w_attn_up,
      w_pool_up, w_out, w_ffn_gate, w_ffn_up, w_ffn_down)


def _rope_tables(n):
    rows = n // GRID_W
    row = np.repeat(np.arange(rows), GRID_W).astype(np.float32)
    col = np.tile(np.arange(GRID_W), rows).astype(np.float32)
    n_freq = HEAD_DIM // 4
    inv = jnp.asarray(ROPE_THETA, F32) ** (-jnp.arange(n_freq, dtype=F32) / n_freq)
    ang = jnp.concatenate([jnp.asarray(row)[:, None] * inv[None, :],
                           jnp.asarray(col)[:, None] * inv[None, :]], axis=-1)
    cos, sin = jnp.cos(ang), jnp.sin(ang)
    cos_t = jnp.tile(cos, (1, LANES // (HEAD_DIM // 2)))
    sin_t = jnp.tile(jnp.concatenate([-sin, sin], axis=-1), (1, LANES // HEAD_DIM))
    return cos_t, sin_t


def kernel(x_prompt, x_sample, cache_k, cache_v, c, c_ctx, w_ada, b_ada, w_in, q_norm, k_norm,
           w_attn_up, w_pool, pool_scale, w_pool_up, w_out, n_pre_mix, n_post_mix, n_pre_ffn,
           n_post_ffn, w_ffn_gate, w_ffn_up, w_ffn_down):
    batch, seq, _ = x_prompt.shape
    dec_batch, dec_seq, _ = x_sample.shape
    past_len = cache_k.shape[2]
    assert dec_batch <= CTX_MOD_ROW

    c_rows = jnp.concatenate(
        [c, c_ctx[None, :], jnp.zeros((MOD_ROWS - dec_batch - 1, D_MODEL), F32)], axis=0)
    mod_rows = _ada_call(c_rows, w_ada, b_ada).reshape(DEPTH * MOD_ROWS, 1, N_MOD * D_MODEL)

    cos_t, sin_t = _rope_tables(dec_seq)
    vec = lambda a: a.reshape(DEPTH, 1, a.shape[-1])
    q_gain = vec(jnp.tile(q_norm, (1, LANES // HEAD_DIM)))
    k_gain = vec(jnp.tile(k_norm, (1, LANES // HEAD_DIM)))
    w_in_b = w_in.astype(BF16)
    w_pool_b = w_pool.astype(BF16)
    w_au_b = w_attn_up.astype(BF16)
    w_pu_b = w_pool_up.astype(BF16)
    w_out_b = w_out.astype(BF16)
    w_g_b = w_ffn_gate.astype(BF16)
    w_u_b = w_ffn_up.astype(BF16)
    w_d_b = w_ffn_down.astype(BF16)
    n_pre_mix_v, n_post_mix_v = vec(n_pre_mix), vec(n_post_mix)
    n_pre_ffn_v, n_post_ffn_v = vec(n_pre_ffn), vec(n_post_ffn)
    pool_scale_v = vec(pool_scale)
    ck = cache_k.reshape(dec_batch, DEPTH, past_len, KV_W)
    cv = cache_v.reshape(dec_batch, DEPTH, past_len, KV_W)

    tm = 1024
    ctx_seqs_per_step = 4

    def trunk(l, x, *, is_lat):
        seq_len = dec_seq if is_lat else seq
        outs = _mix_call(l, x, mod_rows, n_pre_mix_v, w_in_b, q_gain, k_gain, cos_t, sin_t,
                         w_pool_b, pool_scale_v, ck if is_lat else None, cv if is_lat else None,
                         seq_len=seq_len, n_seq=1 if is_lat else ctx_seqs_per_step, rope=is_lat)
        ao, y = outs[:2]
        x2 = _tail_call(l, x, ao, y, mod_rows, n_pre_mix_v, n_post_mix_v, n_pre_ffn_v, n_post_ffn_v,
                        w_in_b, w_au_b, w_pu_b, w_out_b, w_g_b, w_u_b, w_d_b,
                        seq_len=seq_len, per_seq_mod=is_lat, tm=tm, sub=128)
        return x2, outs[2:]

    y = x_prompt.reshape(batch * seq, D_MODEL)
    z = x_sample.reshape(dec_batch * dec_seq, D_MODEL)
    new_ks, new_vs = [], []
    for l in range(DEPTH):
        y, (nk, nv) = trunk(l, y, is_lat=False)
        z, _ = trunk(l, z, is_lat=True)
        new_ks.append(nk)
        new_vs.append(nv)

    def kv_layout(parts):
        a = jnp.stack(parts, axis=1).reshape(batch, DEPTH, N_KV_HEADS, HEAD_DIM, seq)
        return jnp.transpose(a, (0, 1, 4, 2, 3))

    return (y.reshape(batch, seq, D_MODEL), z.reshape(dec_batch, dec_seq, D_MODEL),
            kv_layout(new_ks), kv_layout(new_vs))
```

```python
import functools

import jax
import jax.numpy as jnp
import numpy as np
from jax import lax
from jax.experimental import pallas as pl
from jax.experimental.pallas import tpu as pltpu

D_MODEL = 1024
DEPTH = 4
GRID_W = 64
HEAD_DIM = 64
N_Q_HEADS = 8
N_KV_HEADS = 2
Q_PER_KV = N_Q_HEADS // N_KV_HEADS
ATTN_W = N_Q_HEADS * HEAD_DIM
KV_W = N_KV_HEADS * HEAD_DIM
POOL_WINDOWS = (2, 4, 8, 16)
POOL_W = D_MODEL // 2
POOL_GW = POOL_W // len(POOL_WINDOWS)
GATE_COL0 = ATTN_W + 2 * KV_W + POOL_W
IN_COLS = GATE_COL0 + 2 * D_MODEL
FFN_HIDDEN = 2816
N_MOD = 6
ROPE_THETA = 10000.0
EPS = 1e-6

LANES = 128
Q_TILE = 256
VT_ROWS = 80
MOD_ROWS = 8
CTX_MOD_ROW = 4
VMEM_LIMIT_BYTES = 56 * 1024 * 1024

F32 = jnp.float32
BF16 = jnp.bfloat16


def _dot(a, b):
    return jnp.dot(a, b, preferred_element_type=F32)


def _dot_nt(a, b):
    return lax.dot_general(a, b, (((1,), (1,)), ((), ())), preferred_element_type=F32)


def _rms(x):
    return x * lax.rsqrt(jnp.mean(x * x, axis=-1, keepdims=True) + EPS)


def _params(n_axes=1):
    return pltpu.CompilerParams(dimension_semantics=("arbitrary",) * n_axes,
                                vmem_limit_bytes=VMEM_LIMIT_BYTES)


def _ada_kernel(c_ref, w_ref, b_ref, o_ref):
    c = c_ref[...]
    s = (c * jax.nn.sigmoid(c)).astype(BF16)
    o_ref[...] = _dot(s, w_ref[...].astype(BF16)) + b_ref[...]


def _ada_call(c_rows, w_ada, b_ada):
    tn = 1536
    n_cols = N_MOD * D_MODEL
    return pl.pallas_call(
        _ada_kernel,
        grid=(DEPTH, n_cols // tn),
        in_specs=[
            pl.BlockSpec((MOD_ROWS, D_MODEL), lambda l, j: (0, 0)),
            pl.BlockSpec((None, D_MODEL, tn), lambda l, j: (l, 0, j)),
            pl.BlockSpec((None, 1, tn), lambda l, j: (l, 0, j)),
        ],
        out_specs=pl.BlockSpec((None, MOD_ROWS, tn), lambda l, j: (l, 0, j)),
        out_shape=jax.ShapeDtypeStruct((DEPTH, MOD_ROWS, n_cols), F32),
        compiler_params=_params(2),
        name="ada_rows",
    )(c_rows, w_ada, b_ada.reshape(DEPTH, 1, n_cols))


def _pool_group(x, window, seq_len):
    half = window // 2
    t = lax.broadcasted_iota(jnp.int32, x.shape, 0)
    shift_down = lambda a, k: jnp.where(t >= k, pltpu.roll(a, k, 0), 0.0)
    shift_up = lambda a, k: jnp.where(t < seq_len - k, pltpu.roll(a, seq_len - k, 0), 0.0)
    back, fwd, k = x, x, 1
    while k < half:
        back = back + shift_down(back, k)
        fwd = fwd + shift_up(fwd, k)
        k *= 2
    total = shift_down(back, 1) + fwd
    count = jnp.minimum(t + half, seq_len) - jnp.maximum(t - half, 0)
    return total / count.astype(F32) - x


def _mix_kernel(x_ref, mod_ref, npre_ref, w_ref, qg_ref, kg_ref, cos_ref, sin_ref, wpool_ref,
                pscale_ref, *rest, seq_len, n_seq, n_cache, rope):
    if n_cache:
        ck_ref, cv_ref, ao_out, y_out = rest[:4]
    else:
        ao_out, y_out, nk_out, nv_out = rest[:4]
    qa_ref, k_ref, vt_ref, p_ref, raw_a, raw_b, s_a, s_b, e_a, e_b = rest[4:]
    raw_ref, s_ref, e_ref = (raw_a, raw_b), (s_a, s_b), (e_a, e_b)
    n_keys = n_cache + seq_len
    tiles_per_seq = seq_len // Q_TILE
    n_tiles = n_seq * tiles_per_seq

    mod = mod_ref[...]
    sh1 = mod[:, 0:D_MODEL]
    sc1 = mod[:, D_MODEL:2 * D_MODEL]

    lane = lax.broadcasted_iota(jnp.int32, (Q_TILE, LANES), 1)
    first_head = lane < HEAD_DIM
    second_half = (lane & (HEAD_DIM // 2)) != 0

    def head_norm(t, gain):
        t2 = t * t
        s_lo = jnp.sum(jnp.where(first_head, t2, 0.0), axis=-1, keepdims=True)
        s_hi = jnp.sum(jnp.where(first_head, 0.0, t2), axis=-1, keepdims=True)
        ms = jnp.where(first_head, s_lo, s_hi) * (1.0 / HEAD_DIM)
        return (t * lax.rsqrt(ms + EPS)) * gain

    def rotary(t, pos0):
        partner = jnp.where(second_half, pltpu.roll(t, HEAD_DIM // 2, 1),
                            pltpu.roll(t, LANES - HEAD_DIM // 2, 1))
        return t * cos_ref[pos0:pos0 + Q_TILE, :] + partner * sin_ref[pos0:pos0 + Q_TILE, :]

    ones_rows = jnp.ones((VT_ROWS - HEAD_DIM, n_keys), BF16)
    for s_idx in range(n_seq):
        for kv in range(N_KV_HEADS):
            vt_ref[s_idx, kv * VT_ROWS + HEAD_DIM:(kv + 1) * VT_ROWS, :] = ones_rows
        if n_cache:
            k_ref[s_idx, 0:n_cache, :] = ck_ref[0, 0].astype(BF16)
            cvt = cv_ref[0, 0].T.astype(BF16)
            for kv in range(N_KV_HEADS):
                vt_ref[s_idx, kv * VT_ROWS:kv * VT_ROWS + HEAD_DIM, 0:n_cache] = (
                    cvt[kv * HEAD_DIM:(kv + 1) * HEAD_DIM, :])

    def project(tile, slot):
        r0 = tile * Q_TILE
        x = x_ref[r0:r0 + Q_TILE, :]
        hb = ((_rms(x) * npre_ref[...]) * (1.0 + sc1) + sh1).astype(BF16)
        raw_ref[slot][...] = _dot(hb, w_ref[...])

    def epilogue(tile, slot):
        s_idx, pos0 = tile // tiles_per_seq, (tile % tiles_per_seq) * Q_TILE
        r0 = tile * Q_TILE
        raw = raw_ref[slot]
        for j in range(ATTN_W // LANES):
            t = head_norm(raw[:, j * LANES:(j + 1) * LANES], qg_ref[...])
            if rope:
                t = rotary(t, pos0)
            t = t * (HEAD_DIM ** -0.5)
            swapped = pltpu.roll(t, HEAD_DIM, 1)
            if (2 * j) // Q_PER_KV == 0:
                even = jnp.where(first_head, t, 0.0)
                odd = jnp.where(first_head, swapped, 0.0)
            else:
                even = jnp.where(first_head, 0.0, swapped)
                odd = jnp.where(first_head, 0.0, t)
            qa_ref[tile, (2 * j) * Q_TILE:(2 * j + 1) * Q_TILE, :] = even.astype(BF16)
            qa_ref[tile, (2 * j + 1) * Q_TILE:(2 * j + 2) * Q_TILE, :] = odd.astype(BF16)
        kn = head_norm(raw[:, ATTN_W:ATTN_W + KV_W], kg_ref[...])
        v = raw[:, ATTN_W + KV_W:ATTN_W + 2 * KV_W]
        vt_f32 = v.T
        if not n_cache:
            nk_out[s_idx, :, pos0:pos0 + Q_TILE] = kn.T
            nv_out[s_idx, :, pos0:pos0 + Q_TILE] = vt_f32
        if rope:
            kn = rotary(kn, pos0)
        k0 = n_cache + pos0
        k_ref[s_idx, k0:k0 + Q_TILE, :] = kn.astype(BF16)
        vt = vt_f32.astype(BF16)
        for kv in range(N_KV_HEADS):
            vt_ref[s_idx, kv * VT_ROWS:kv * VT_ROWS + HEAD_DIM, k0:k0 + Q_TILE] = (
                vt[kv * HEAD_DIM:(kv + 1) * HEAD_DIM, :])
        p_ref[r0:r0 + Q_TILE, :] = raw[:, ATTN_W + 2 * KV_W:GATE_COL0]

    project(0, 0)
    for tile in range(n_tiles):
        if tile + 1 < n_tiles:
            project(tile + 1, (tile + 1) % 2)
        epilogue(tile, tile % 2)

    n_pairs = N_Q_HEADS // 2
    n_steps = n_tiles * n_pairs

    def scores(step, slot):
        tile, pair = step // n_pairs, step % n_pairs
        q2 = qa_ref[tile, pair * 2 * Q_TILE:(pair + 1) * 2 * Q_TILE, :]
        s_ref[slot][...] = _dot_nt(k_ref[tile // tiles_per_seq], q2)

    def exponent(step, slot):
        s = s_ref[slot][...]
        e_ref[slot][...] = jnp.exp(s - jnp.max(s, axis=0, keepdims=True)).astype(BF16)

    def values(step, slot):
        tile, pair = step // n_pairs, step % n_pairs
        kv = (2 * pair) // Q_PER_KV
        vt = vt_ref[tile // tiles_per_seq, kv * VT_ROWS:(kv + 1) * VT_ROWS, :]
        ot = _dot(vt, e_ref[slot][...])
        ot = ot[0:HEAD_DIM] * (1.0 / ot[HEAD_DIM:HEAD_DIM + 1])
        both = jnp.concatenate([ot[:, 0:Q_TILE], ot[:, Q_TILE:2 * Q_TILE]], axis=0)
        ao_out[tile * Q_TILE:(tile + 1) * Q_TILE, pair * LANES:(pair + 1) * LANES] = both.T.astype(BF16)

    scores(0, 0)
    for step in range(n_steps + 1):
        if step + 1 < n_steps:
            scores(step + 1, (step + 1) % 2)
        if step < n_steps:
            exponent(step, step % 2)
        if step >= 1:
            values(step - 1, (step - 1) % 2)

    for s_idx in range(n_seq):
        r0 = s_idx * seq_len
        for g, window in enumerate(POOL_WINDOWS):
            cols = slice(g * POOL_GW, (g + 1) * POOL_GW)
            d = _pool_group(p_ref[r0:r0 + seq_len, cols], window, seq_len)
            yg = _dot(d.astype(BF16), wpool_ref[g]) * pscale_ref[:, cols]
            y_out[r0:r0 + seq_len, cols] = yg.astype(BF16)


def _mix_call(l, x, mod_rows, n_pre, w_qkvp, q_gain, k_gain, cos_t, sin_t, w_pool, pool_scale,
              cache_k, cache_v, *, seq_len, n_seq, rope):
    n_tok = x.shape[0]
    rows = n_seq * seq_len
    is_lat = cache_k is not None
    n_cache = cache_k.shape[2] if is_lat else 0
    n_keys = n_cache + seq_len
    if is_lat:
        assert n_seq == 1
        mod_idx = lambda i: (l * MOD_ROWS + i, 0, 0)
    else:
        mod_idx = lambda i: (l * MOD_ROWS + CTX_MOD_ROW, 0, 0)
    lay = lambda i: (l, 0, 0)
    row = lambda i: (i, 0)
    in_specs = [
        pl.BlockSpec((rows, D_MODEL), row),
        pl.BlockSpec((None, 1, N_MOD * D_MODEL), mod_idx),
        pl.BlockSpec((None, 1, D_MODEL), lay),
        pl.BlockSpec((None, D_MODEL, GATE_COL0), lay),
        pl.BlockSpec((None, 1, LANES), lay),
        pl.BlockSpec((None, 1, LANES), lay),
        pl.BlockSpec((seq_len, LANES), lambda i: (0, 0)),
        pl.BlockSpec((seq_len, LANES), lambda i: (0, 0)),
        pl.BlockSpec((None, len(POOL_WINDOWS), POOL_GW, POOL_GW), lambda i: (l, 0, 0, 0)),
        pl.BlockSpec((None, 1, POOL_W), lay),
    ]
    args = [x, mod_rows, n_pre, w_qkvp, q_gain, k_gain, cos_t, sin_t, w_pool, pool_scale]
    out_shape = [jax.ShapeDtypeStruct((n_tok, ATTN_W), BF16), jax.ShapeDtypeStruct((n_tok, POOL_W), BF16)]
    out_specs = [pl.BlockSpec((rows, ATTN_W), row), pl.BlockSpec((rows, POOL_W), row)]
    if is_lat:
        cache_spec = pl.BlockSpec((1, 1, n_cache, KV_W), lambda i: (i, l, 0, 0))
        in_specs += [cache_spec, cache_spec]
        args += [cache_k, cache_v]
    else:
        out_shape += [jax.ShapeDtypeStruct((n_tok // seq_len, KV_W, seq_len), F32)] * 2
        out_specs += [pl.BlockSpec((n_seq, KV_W, seq_len), lambda i: (i, 0, 0))] * 2
    scratch = [
        pltpu.VMEM((rows // Q_TILE, N_Q_HEADS * Q_TILE, LANES), BF16),
        pltpu.VMEM((n_seq, n_keys, KV_W), BF16),
        pltpu.VMEM((n_seq, N_KV_HEADS * VT_ROWS, n_keys), BF16),
        pltpu.VMEM((rows, POOL_W), F32),
    ]
    scratch += [pltpu.VMEM((Q_TILE, GATE_COL0), F32)] * 2
    scratch += [pltpu.VMEM((n_keys, 2 * Q_TILE), F32)] * 2
    scratch += [pltpu.VMEM((n_keys, 2 * Q_TILE), BF16)] * 2
    return pl.pallas_call(
        functools.partial(_mix_kernel, seq_len=seq_len, n_seq=n_seq, n_cache=n_cache, rope=rope),
        grid=(n_tok // rows,),
        in_specs=in_specs,
        out_specs=out_specs,
        out_shape=out_shape,
        scratch_shapes=scratch,
        compiler_params=_params(),
        name="mix_lat" if is_lat else "mix_ctx",
    )(*args)


def _tail_kernel(x_ref, ao_ref, y_ref, mod_ref, npre_mix_ref, npost_mix_ref, npre_ffn_ref,
                 npost_ffn_ref, win_ref, wau_ref, wpu_ref, wout_ref, wg_ref, wu_ref, wd_ref, o_ref,
                 x1_a, x1_b, *, sub):
    mod = mod_ref[...]
    sh1, sc1, g1, sh2, sc2, g2 = [mod[:, i * D_MODEL:(i + 1) * D_MODEL] for i in range(N_MOD)]
    x1_ref = (x1_a, x1_b)
    n_sub = x_ref.shape[0] // sub

    def post(i, slot):
        r0 = i * sub
        x = x_ref[r0:r0 + sub, :]
        hb = ((_rms(x) * npre_mix_ref[...]) * (1.0 + sc1) + sh1).astype(BF16)
        g_attn = jax.nn.sigmoid(_dot(hb, win_ref[:, GATE_COL0:GATE_COL0 + D_MODEL]))
        g_pool = jax.nn.sigmoid(_dot(hb, win_ref[:, GATE_COL0 + D_MODEL:IN_COLS]))
        attn = _dot(ao_ref[r0:r0 + sub, :], wau_ref[...])
        pool = _dot(y_ref[r0:r0 + sub, :], wpu_ref[...])
        mixed = g_attn * attn + g_pool * pool
        m = _dot(mixed.astype(BF16), wout_ref[...])
        x1_ref[slot][...] = x + g1 * (_rms(m) * npost_mix_ref[...])

    def ffn(i, slot):
        r0 = i * sub
        x1 = x1_ref[slot][...]
        hb = ((_rms(x1) * npre_ffn_ref[...]) * (1.0 + sc2) + sh2).astype(BF16)
        g = _dot(hb, wg_ref[...])
        u = _dot(hb, wu_ref[...])
        a = ((g * jax.nn.sigmoid(g)) * u).astype(BF16)
        f = _dot(a, wd_ref[...])
        o_ref[r0:r0 + sub, :] = x1 + g2 * (_rms(f) * npost_ffn_ref[...])

    post(0, 0)
    for i in range(n_sub):
        if i + 1 < n_sub:
            post(i + 1, (i + 1) % 2)
        ffn(i, i % 2)


def _tail_call(l, x, ao, y, mod_rows, n_pre_mix, n_post_mix, n_pre_ffn, n_post_ffn, w_in,
               w_attn_up, w_pool_up, w_out, w_ffn_gate, w_ffn_up, w_ffn_down,
               *, seq_len, per_seq_mod, tm, sub):
    n_tok = x.shape[0]
    if per_seq_mod:
        mod_idx = lambda i: (l * MOD_ROWS + i // (seq_len // tm), 0, 0)
    else:
        mod_idx = lambda i: (l * MOD_ROWS + CTX_MOD_ROW, 0, 0)
    lay = lambda i: (l, 0, 0)
    row = lambda i: (i, 0)
    vec_spec = pl.BlockSpec((None, 1, D_MODEL), lay)
    resident = lambda r, c: pl.BlockSpec((None, r, c), lay, pipeline_mode=pl.Buffered(1))
    return pl.pallas_call(
        functools.partial(_tail_kernel, sub=sub),
        grid=(n_tok // tm,),
        in_specs=[
            pl.BlockSpec((tm, D_MODEL), row),
            pl.BlockSpec((tm, ATTN_W), row),
            pl.BlockSpec((tm, POOL_W), row),
            pl.BlockSpec((None, 1, N_MOD * D_MODEL), mod_idx),
            vec_spec, vec_spec, vec_spec, vec_spec,
            resident(D_MODEL, IN_COLS),
            resident(ATTN_W, D_MODEL),
            resident(POOL_W, D_MODEL),
            resident(D_MODEL, D_MODEL),
            resident(D_MODEL, FFN_HIDDEN),
            resident(D_MODEL, FFN_HIDDEN),
            resident(FFN_HIDDEN, D_MODEL),
        ],
        out_specs=pl.BlockSpec((tm, D_MODEL), row),
        out_shape=jax.ShapeDtypeStruct((n_tok, D_MODEL), F32),
        scratch_shapes=[pltpu.VMEM((sub, D_MODEL), F32)] * 2,
        compiler_params=_params(),
        name="tail",
    )(x, ao, y, mod_rows, n_pre_mix, n_post_mix, n_pre_ffn, n_post_ffn, w_in, w_attn_up,
      w_pool_up, w_out, w_ffn_gate, w_ffn_up, w_ffn_down)


def _rope_tables(n):
    rows = n // GRID_W
    row = np.repeat(np.arange(rows), GRID_W).astype(np.float32)
    col = np.tile(np.arange(GRID_W), rows).astype(np.float32)
    n_freq = HEAD_DIM // 4
    inv = jnp.asarray(ROPE_THETA, F32) ** (-jnp.arange(n_freq, dtype=F32) / n_freq)
    ang = jnp.concatenate([jnp.asarray(row)[:, None] * inv[None, :],
                           jnp.asarray(col)[:, None] * inv[None, :]], axis=-1)
    cos, sin = jnp.cos(ang), jnp.sin(ang)
    cos_t = jnp.tile(cos, (1, LANES // (HEAD_DIM // 2)))
    sin_t = jnp.tile(jnp.concatenate([-sin, sin], axis=-1), (1, LANES // HEAD_DIM))
    return cos_t, sin_t


def kernel(x_prompt, x_sample, cache_k, cache_v, c, c_ctx, w_ada, b_ada, w_in, q_norm, k_norm,
           w_attn_up, w_pool, pool_scale, w_pool_up, w_out, n_pre_mix, n_post_mix, n_pre_ffn,
           n_post_ffn, w_ffn_gate, w_ffn_up, w_ffn_down):
    batch, seq, _ = x_prompt.shape
    dec_batch, dec_seq, _ = x_sample.shape
    past_len = cache_k.shape[2]
    assert dec_batch <= CTX_MOD_ROW

    c_rows = jnp.concatenate(
        [c, c_ctx[None, :], jnp.zeros((MOD_ROWS - dec_batch - 1, D_MODEL), F32)], axis=0)
    mod_rows = _ada_call(c_rows, w_ada, b_ada).reshape(DEPTH * MOD_ROWS, 1, N_MOD * D_MODEL)

    cos_t, sin_t = _rope_tables(dec_seq)
    vec = lambda a: a.reshape(DEPTH, 1, a.shape[-1])
    q_gain = vec(jnp.tile(q_norm, (1, LANES // HEAD_DIM)))
    k_gain = vec(jnp.tile(k_norm, (1, LANES // HEAD_DIM)))
    w_in_b = w_in.astype(BF16)
    w_pool_b = w_pool.astype(BF16)
    w_au_b = w_attn_up.astype(BF16)
    w_pu_b = w_pool_up.astype(BF16)
    w_out_b = w_out.astype(BF16)
    w_g_b = w_ffn_gate.astype(BF16)
    w_u_b = w_ffn_up.astype(BF16)
    w_d_b = w_ffn_down.astype(BF16)
    n_pre_mix_v, n_post_mix_v = vec(n_pre_mix), vec(n_post_mix)
    n_pre_ffn_v, n_post_ffn_v = vec(n_pre_ffn), vec(n_post_ffn)
    pool_scale_v = vec(pool_scale)
    ck = cache_k.reshape(dec_batch, DEPTH, past_len, KV_W)
    cv = cache_v.reshape(dec_batch, DEPTH, past_len, KV_W)

    tm = 512
    ctx_seqs_per_step = 4

    def trunk(l, x, *, is_lat):
        seq_len = dec_seq if is_lat else seq
        outs = _mix_call(l, x, mod_rows, n_pre_mix_v, w_in_b, q_gain, k_gain, cos_t, sin_t,
                         w_pool_b, pool_scale_v, ck if is_lat else None, cv if is_lat else None,
                         seq_len=seq_len, n_seq=1 if is_lat else ctx_seqs_per_step, rope=is_lat)
        ao, y = outs[:2]
        x2 = _tail_call(l, x, ao, y, mod_rows, n_pre_mix_v, n_post_mix_v, n_pre_ffn_v, n_post_ffn_v,
                        w_in_b, w_au_b, w_pu_b, w_out_b, w_g_b, w_u_b, w_d_b,
                        seq_len=seq_len, per_seq_mod=is_lat, tm=tm, sub=256)
        return x2, outs[2:]

    y = x_prompt.reshape(batch * seq, D_MODEL)
    z = x_sample.reshape(dec_batch * dec_seq, D_MODEL)
    new_ks, new_vs = [], []
    for l in range(DEPTH):
        y, (nk, nv) = trunk(l, y, is_lat=False)
        z, _ = trunk(l, z, is_lat=True)
        new_ks.append(nk)
        new_vs.append(nv)

    def kv_layout(parts):
        a = jnp.stack(parts, axis=1).reshape(batch, DEPTH, N_KV_HEADS, HEAD_DIM, seq)
        return jnp.transpose(a, (0, 1, 4, 2, 3))

    return (y.reshape(batch, seq, D_MODEL), z.reshape(dec_batch, dec_seq, D_MODEL),
            kv_layout(new_ks), kv_layout(new_vs))
```

```python
import functools

import jax
import jax.numpy as jnp
import numpy as np
from jax import lax
from jax.experimental import pallas as pl
from jax.experimental.pallas import tpu as pltpu

D_MODEL = 1024
DEPTH = 4
GRID_W = 64
HEAD_DIM = 64
N_Q_HEADS = 8
N_KV_HEADS = 2
Q_PER_KV = N_Q_HEADS // N_KV_HEADS
ATTN_W = N_Q_HEADS * HEAD_DIM
KV_W = N_KV_HEADS * HEAD_DIM
POOL_WINDOWS = (2, 4, 8, 16)
POOL_W = D_MODEL // 2
POOL_GW = POOL_W // len(POOL_WINDOWS)
GATE_COL0 = ATTN_W + 2 * KV_W + POOL_W
IN_COLS = GATE_COL0 + 2 * D_MODEL
FFN_HIDDEN = 2816
N_MOD = 6
ROPE_THETA = 10000.0
EPS = 1e-6

LANES = 128
Q_TILE = 256
VT_ROWS = 80
POOL_PAD = 24
LOG2_E = 1.4426950408889634
MOD_ROWS = 8
CTX_MOD_ROW = 4
VMEM_LIMIT_BYTES = 56 * 1024 * 1024

F32 = jnp.float32
BF16 = jnp.bfloat16


def _dot(a, b):
    return jnp.dot(a, b, preferred_element_type=F32)


def _dot_nt(a, b):
    return lax.dot_general(a, b, (((1,), (1,)), ((), ())), preferred_element_type=F32)


def _rms(x):
    return x * lax.rsqrt(jnp.mean(x * x, axis=-1, keepdims=True) + EPS)


def _params(n_axes=1):
    return pltpu.CompilerParams(dimension_semantics=("arbitrary",) * n_axes,
                                vmem_limit_bytes=VMEM_LIMIT_BYTES)


def _ada_kernel(c_ref, w_ref, b_ref, o_ref):
    c = c_ref[...]
    s = (c * jax.nn.sigmoid(c)).astype(BF16)
    o_ref[...] = _dot(s, w_ref[...].astype(BF16)) + b_ref[...]


def _ada_call(c_rows, w_ada, b_ada):
    tn = 1536
    n_cols = N_MOD * D_MODEL
    return pl.pallas_call(
        _ada_kernel,
        grid=(DEPTH, n_cols // tn),
        in_specs=[
            pl.BlockSpec((MOD_ROWS, D_MODEL), lambda l, j: (0, 0)),
            pl.BlockSpec((None, D_MODEL, tn), lambda l, j: (l, 0, j)),
            pl.BlockSpec((None, 1, tn), lambda l, j: (l, 0, j)),
        ],
        out_specs=pl.BlockSpec((None, MOD_ROWS, tn), lambda l, j: (l, 0, j)),
        out_shape=jax.ShapeDtypeStruct((DEPTH, MOD_ROWS, n_cols), F32),
        compiler_params=_params(2),
        name="ada_rows",
    )(c_rows, w_ada, b_ada.reshape(DEPTH, 1, n_cols))


def _pool_group(src, window, seq_len, p2_ref, p4_ref):
    half = window // 2
    rows = lambda ref, off: ref[POOL_PAD + off:POOL_PAD + off + seq_len, :]
    if window <= 4:
        total = rows(src, -half)
        for off in range(-half + 1, half):
            total = total + rows(src, off)
    else:
        n2 = seq_len + 2 * POOL_PAD - 8
        p2_ref[0:n2, :] = src[0:n2, :] + src[1:n2 + 1, :]
        if window == 8:
            terms = [rows(p2_ref, off) for off in (-4, -2, 0, 2)]
        else:
            assert window == 16
            n4 = n2 - 8
            p4_ref[0:n4, :] = p2_ref[0:n4, :] + p2_ref[2:n4 + 2, :]
            terms = [rows(p4_ref, off) for off in (-8, -4, 0, 4)]
        total = (terms[0] + terms[1]) + (terms[2] + terms[3])
    t = lax.broadcasted_iota(jnp.int32, (seq_len, POOL_GW), 0)
    count = jnp.minimum(t + half, seq_len) - jnp.maximum(t - half, 0)
    return total / count.astype(F32) - rows(src, 0)


def _mix_kernel(x_ref, mod_ref, npre_ref, w_ref, qg_ref, kg_ref, cos_ref, sin_ref, wpool_ref,
                pscale_ref, *rest, seq_len, n_seq, n_cache, rope):
    if n_cache:
        ck_ref, cv_ref, ao_out, y_out = rest[:4]
    else:
        ao_out, y_out, nk_out, nv_out = rest[:4]
    qa_ref, k_ref, vt_ref, p_ref, raw_a, raw_b, s_a, s_b, e_a, e_b, p2_a, p2_b, p4_b = rest[4:]
    raw_ref, s_ref, e_ref = (raw_a, raw_b), (s_a, s_b), (e_a, e_b)
    n_keys = n_cache + seq_len
    tiles_per_seq = seq_len // Q_TILE
    n_tiles = n_seq * tiles_per_seq

    mod = mod_ref[...]
    sh1 = mod[:, 0:D_MODEL]
    norm_gain = npre_ref[...] * (1.0 + mod[:, D_MODEL:2 * D_MODEL])

    lane = lax.broadcasted_iota(jnp.int32, (Q_TILE, LANES), 1)
    first_head = lane < HEAD_DIM
    second_half = (lane & (HEAD_DIM // 2)) != 0

    def head_norm(t, gain):
        t2 = t * t
        s_lo = jnp.sum(jnp.where(first_head, t2, 0.0), axis=-1, keepdims=True)
        s_hi = jnp.sum(jnp.where(first_head, 0.0, t2), axis=-1, keepdims=True)
        ms = jnp.where(first_head, s_lo, s_hi) * (1.0 / HEAD_DIM)
        return (t * lax.rsqrt(ms + EPS)) * gain

    def rotary(t, pos0):
        partner = jnp.where(second_half, pltpu.roll(t, HEAD_DIM // 2, 1),
                            pltpu.roll(t, LANES - HEAD_DIM // 2, 1))
        return t * cos_ref[pos0:pos0 + Q_TILE, :] + partner * sin_ref[pos0:pos0 + Q_TILE, :]

    ones_rows = jnp.ones((VT_ROWS - HEAD_DIM, n_keys), BF16)
    pad_rows = jnp.zeros((POOL_PAD, POOL_GW), F32)
    for s_idx in range(n_seq):
        for g in range(len(POOL_WINDOWS)):
            p_ref[s_idx, g, 0:POOL_PAD, :] = pad_rows
            p_ref[s_idx, g, POOL_PAD + seq_len:2 * POOL_PAD + seq_len, :] = pad_rows
        for kv in range(N_KV_HEADS):
            vt_ref[s_idx, kv * VT_ROWS + HEAD_DIM:(kv + 1) * VT_ROWS, :] = ones_rows
        if n_cache:
            k_ref[s_idx, 0:n_cache, :] = ck_ref[0, 0].astype(BF16)
            cvt = cv_ref[0, 0].T.astype(BF16)
            for kv in range(N_KV_HEADS):
                vt_ref[s_idx, kv * VT_ROWS:kv * VT_ROWS + HEAD_DIM, 0:n_cache] = (
                    cvt[kv * HEAD_DIM:(kv + 1) * HEAD_DIM, :])

    def project(tile, slot):
        r0 = tile * Q_TILE
        x = x_ref[r0:r0 + Q_TILE, :]
        hb = (_rms(x) * norm_gain + sh1).astype(BF16)
        raw_ref[slot][...] = _dot(hb, w_ref[...])

    def epilogue(tile, slot):
        s_idx, pos0 = tile // tiles_per_seq, (tile % tiles_per_seq) * Q_TILE
        r0 = tile * Q_TILE
        raw = raw_ref[slot]
        for j in range(ATTN_W // LANES):
            t = head_norm(raw[:, j * LANES:(j + 1) * LANES], qg_ref[...])
            if rope:
                t = rotary(t, pos0)
            t = t * (HEAD_DIM ** -0.5 * LOG2_E)
            swapped = pltpu.roll(t, HEAD_DIM, 1)
            if (2 * j) // Q_PER_KV == 0:
                even = jnp.where(first_head, t, 0.0)
                odd = jnp.where(first_head, swapped, 0.0)
            else:
                even = jnp.where(first_head, 0.0, swapped)
                odd = jnp.where(first_head, 0.0, t)
            qa_ref[tile, (2 * j) * Q_TILE:(2 * j + 1) * Q_TILE, :] = even.astype(BF16)
            qa_ref[tile, (2 * j + 1) * Q_TILE:(2 * j + 2) * Q_TILE, :] = odd.astype(BF16)
        kn = head_norm(raw[:, ATTN_W:ATTN_W + KV_W], kg_ref[...])
        v = raw[:, ATTN_W + KV_W:ATTN_W + 2 * KV_W]
        vt_f32 = v.T
        if not n_cache:
            nk_out[s_idx, :, pos0:pos0 + Q_TILE] = kn.T
            nv_out[s_idx, :, pos0:pos0 + Q_TILE] = vt_f32
        if rope:
            kn = rotary(kn, pos0)
        k0 = n_cache + pos0
        k_ref[s_idx, k0:k0 + Q_TILE, :] = kn.astype(BF16)
        vt = vt_f32.astype(BF16)
        for kv in range(N_KV_HEADS):
            vt_ref[s_idx, kv * VT_ROWS:kv * VT_ROWS + HEAD_DIM, k0:k0 + Q_TILE] = (
                vt[kv * HEAD_DIM:(kv + 1) * HEAD_DIM, :])
        for g in range(len(POOL_WINDOWS)):
            c0 = ATTN_W + 2 * KV_W + g * POOL_GW
            p_ref[s_idx, g, POOL_PAD + pos0:POOL_PAD + pos0 + Q_TILE, :] = raw[:, c0:c0 + POOL_GW]

    project(0, 0)
    for tile in range(n_tiles):
        if tile + 1 < n_tiles:
            project(tile + 1, (tile + 1) % 2)
        epilogue(tile, tile % 2)

    n_pairs = N_Q_HEADS // 2
    n_steps = n_tiles * n_pairs

    def scores(step, slot):
        tile, pair = step // n_pairs, step % n_pairs
        q2 = qa_ref[tile, pair * 2 * Q_TILE:(pair + 1) * 2 * Q_TILE, :]
        s_ref[slot][...] = _dot_nt(k_ref[tile // tiles_per_seq], q2)

    def exponent(step, slot):
        s = s_ref[slot][...]
        e_ref[slot][...] = jnp.exp2(s - jnp.max(s, axis=0, keepdims=True)).astype(BF16)

    def values(step, slot):
        tile, pair = step // n_pairs, step % n_pairs
        kv = (2 * pair) // Q_PER_KV
        vt = vt_ref[tile // tiles_per_seq, kv * VT_ROWS:(kv + 1) * VT_ROWS, :]
        ot = _dot(vt, e_ref[slot][...])
        ot = ot[0:HEAD_DIM] * (1.0 / ot[HEAD_DIM:HEAD_DIM + 1])
        both = jnp.concatenate([ot[:, 0:Q_TILE], ot[:, Q_TILE:2 * Q_TILE]], axis=0)
        ao_out[tile * Q_TILE:(tile + 1) * Q_TILE, pair * LANES:(pair + 1) * LANES] = both.T.astype(BF16)

    scores(0, 0)
    for step in range(n_steps + 1):
        if step + 1 < n_steps:
            scores(step + 1, (step + 1) % 2)
        if step < n_steps:
            exponent(step, step % 2)
        if step >= 1:
            values(step - 1, (step - 1) % 2)

    for s_idx in range(n_seq):
        r0 = s_idx * seq_len
        for g, window in enumerate(POOL_WINDOWS):
            cols = slice(g * POOL_GW, (g + 1) * POOL_GW)
            p2_ref = p2_a if window == 8 else p2_b
            d = _pool_group(p_ref.at[s_idx, g], window, seq_len, p2_ref, p4_b)
            yg = _dot(d.astype(BF16), wpool_ref[g]) * pscale_ref[:, cols]
            y_out[r0:r0 + seq_len, cols] = yg.astype(BF16)


def _mix_call(l, x, mod_rows, n_pre, w_qkvp, q_gain, k_gain, cos_t, sin_t, w_pool, pool_scale,
              cache_k, cache_v, *, seq_len, n_seq, rope):
    n_tok = x.shape[0]
    rows = n_seq * seq_len
    is_lat = cache_k is not None
    n_cache = cache_k.shape[2] if is_lat else 0
    n_keys = n_cache + seq_len
    if is_lat:
        assert n_seq == 1
        mod_idx = lambda i: (l * MOD_ROWS + i, 0, 0)
    else:
        mod_idx = lambda i: (l * MOD_ROWS + CTX_MOD_ROW, 0, 0)
    lay = lambda i: (l, 0, 0)
    row = lambda i: (i, 0)
    in_specs = [
        pl.BlockSpec((rows, D_MODEL), row),
        pl.BlockSpec((None, 1, N_MOD * D_MODEL), mod_idx),
        pl.BlockSpec((None, 1, D_MODEL), lay),
        pl.BlockSpec((None, D_MODEL, GATE_COL0), lay),
        pl.BlockSpec((None, 1, LANES), lay),
        pl.BlockSpec((None, 1, LANES), lay),
        pl.BlockSpec((seq_len, LANES), lambda i: (0, 0)),
        pl.BlockSpec((seq_len, LANES), lambda i: (0, 0)),
        pl.BlockSpec((None, len(POOL_WINDOWS), POOL_GW, POOL_GW), lambda i: (l, 0, 0, 0)),
        pl.BlockSpec((None, 1, POOL_W), lay),
    ]
    args = [x, mod_rows, n_pre, w_qkvp, q_gain, k_gain, cos_t, sin_t, w_pool, pool_scale]
    out_shape = [jax.ShapeDtypeStruct((n_tok, ATTN_W), BF16), jax.ShapeDtypeStruct((n_tok, POOL_W), BF16)]
    out_specs = [pl.BlockSpec((rows, ATTN_W), row), pl.BlockSpec((rows, POOL_W), row)]
    if is_lat:
        cache_spec = pl.BlockSpec((1, 1, n_cache, KV_W), lambda i: (i, l, 0, 0))
        in_specs += [cache_spec, cache_spec]
        args += [cache_k, cache_v]
    else:
        out_shape += [jax.ShapeDtypeStruct((n_tok // seq_len, KV_W, seq_len), F32)] * 2
        out_specs += [pl.BlockSpec((n_seq, KV_W, seq_len), lambda i: (i, 0, 0))] * 2
    scratch = [
        pltpu.VMEM((rows // Q_TILE, N_Q_HEADS * Q_TILE, LANES), BF16),
        pltpu.VMEM((n_seq, n_keys, KV_W), BF16),
        pltpu.VMEM((n_seq, N_KV_HEADS * VT_ROWS, n_keys), BF16),
        pltpu.VMEM((n_seq, len(POOL_WINDOWS), seq_len + 2 * POOL_PAD, POOL_GW), F32),
    ]
    scratch += [pltpu.VMEM((Q_TILE, GATE_COL0), F32)] * 2
    scratch += [pltpu.VMEM((n_keys, 2 * Q_TILE), F32)] * 2
    scratch += [pltpu.VMEM((n_keys, 2 * Q_TILE), BF16)] * 2
    scratch += [pltpu.VMEM((seq_len + 2 * POOL_PAD - 8, POOL_GW), F32)] * 2
    scratch += [pltpu.VMEM((seq_len + 2 * POOL_PAD - 16, POOL_GW), F32)]
    return pl.pallas_call(
        functools.partial(_mix_kernel, seq_len=seq_len, n_seq=n_seq, n_cache=n_cache, rope=rope),
        grid=(n_tok // rows,),
        in_specs=in_specs,
        out_specs=out_specs,
        out_shape=out_shape,
        scratch_shapes=scratch,
        compiler_params=_params(),
        name="mix_lat" if is_lat else "mix_ctx",
    )(*args)


def _tail_kernel(x_ref, ao_ref, y_ref, mod_ref, npre_mix_ref, npost_mix_ref, npre_ffn_ref,
                 npost_ffn_ref, win_ref, wau_ref, wpu_ref, wout_ref, wg_ref, wu_ref, wd_ref, o_ref,
                 x1_a, x1_b, *, sub):
    mod = mod_ref[...]
    sh1, sc1, g1, sh2, sc2, g2 = [mod[:, i * D_MODEL:(i + 1) * D_MODEL] for i in range(N_MOD)]
    x1_ref = (x1_a, x1_b)
    n_sub = x_ref.shape[0] // sub

    def post(i, slot):
        r0 = i * sub
        x = x_ref[r0:r0 + sub, :]
        hb = ((_rms(x) * npre_mix_ref[...]) * (1.0 + sc1) + sh1).astype(BF16)
        g_attn = jax.nn.sigmoid(_dot(hb, win_ref[:, GATE_COL0:GATE_COL0 + D_MODEL]))
        g_pool = jax.nn.sigmoid(_dot(hb, win_ref[:, GATE_COL0 + D_MODEL:IN_COLS]))
        attn = _dot(ao_ref[r0:r0 + sub, :], wau_ref[...])
        pool = _dot(y_ref[r0:r0 + sub, :], wpu_ref[...])
        mixed = g_attn * attn + g_pool * pool
        m = _dot(mixed.astype(BF16), wout_ref[...])
        x1_ref[slot][...] = x + g1 * (_rms(m) * npost_mix_ref[...])

    def ffn(i, slot):
        r0 = i * sub
        x1 = x1_ref[slot][...]
        hb = ((_rms(x1) * npre_ffn_ref[...]) * (1.0 + sc2) + sh2).astype(BF16)
        g = _dot(hb, wg_ref[...])
        u = _dot(hb, wu_ref[...])
        a = ((g * jax.nn.sigmoid(g)) * u).astype(BF16)
        f = _dot(a, wd_ref[...])
        o_ref[r0:r0 + sub, :] = x1 + g2 * (_rms(f) * npost_ffn_ref[...])

    post(0, 0)
    for i in range(n_sub):
        if i + 1 < n_sub:
            post(i + 1, (i + 1) % 2)
        ffn(i, i % 2)


def _tail_call(l, x, ao, y, mod_rows, n_pre_mix, n_post_mix, n_pre_ffn, n_post_ffn, w_in,
               w_attn_up, w_pool_up, w_out, w_ffn_gate, w_ffn_up, w_ffn_down,
               *, seq_len, per_seq_mod, tm, sub):
    n_tok = x.shape[0]
    if per_seq_mod:
        mod_idx = lambda i: (l * MOD_ROWS + i // (seq_len // tm), 0, 0)
    else:
        mod_idx = lambda i: (l * MOD_ROWS + CTX_MOD_ROW, 0, 0)
    lay = lambda i: (l, 0, 0)
    row = lambda i: (i, 0)
    vec_spec = pl.BlockSpec((None, 1, D_MODEL), lay)
    resident = lambda r, c: pl.BlockSpec((None, r, c), lay, pipeline_mode=pl.Buffered(1))
    return pl.pallas_call(
        functools.partial(_tail_kernel, sub=sub),
        grid=(n_tok // tm,),
        in_specs=[
            pl.BlockSpec((tm, D_MODEL), row),
            pl.BlockSpec((tm, ATTN_W), row),
            pl.BlockSpec((tm, POOL_W), row),
            pl.BlockSpec((None, 1, N_MOD * D_MODEL), mod_idx),
            vec_spec, vec_spec, vec_spec, vec_spec,
            resident(D_MODEL, IN_COLS),
            resident(ATTN_W, D_MODEL),
            resident(POOL_W, D_MODEL),
            resident(D_MODEL, D_MODEL),
            resident(D_MODEL, FFN_HIDDEN),
            resident(D_MODEL, FFN_HIDDEN),
            resident(FFN_HIDDEN, D_MODEL),
        ],
        out_specs=pl.BlockSpec((tm, D_MODEL), row),
        out_shape=jax.ShapeDtypeStruct((n_tok, D_MODEL), F32),
        scratch_shapes=[pltpu.VMEM((sub, D_MODEL), F32)] * 2,
        compiler_params=_params(),
        name="tail",
    )(x, ao, y, mod_rows, n_pre_mix, n_post_mix, n_pre_ffn, n_post_ffn, w_in, w_attn_up,
      w_pool_up, w_out, w_ffn_gate, w_ffn_up, w_ffn_down)


def _rope_tables(n):
    rows = n // GRID_W
    row = np.repeat(np.arange(rows), GRID_W).astype(np.float32)
    col = np.tile(np.arange(GRID_W), rows).astype(np.float32)
    n_freq = HEAD_DIM // 4
    inv = jnp.asarray(ROPE_THETA, F32) ** (-jnp.arange(n_freq, dtype=F32) / n_freq)
    ang = jnp.concatenate([jnp.asarray(row)[:, None] * inv[None, :],
                           jnp.asarray(col)[:, None] * inv[None, :]], axis=-1)
    cos, sin = jnp.cos(ang), jnp.sin(ang)
    cos_t = jnp.tile(cos, (1, LANES // (HEAD_DIM // 2)))
    sin_t = jnp.tile(jnp.concatenate([-sin, sin], axis=-1), (1, LANES // HEAD_DIM))
    return cos_t, sin_t


def kernel(x_prompt, x_sample, cache_k, cache_v, c, c_ctx, w_ada, b_ada, w_in, q_norm, k_norm,
           w_attn_up, w_pool, pool_scale, w_pool_up, w_out, n_pre_mix, n_post_mix, n_pre_ffn,
           n_post_ffn, w_ffn_gate, w_ffn_up, w_ffn_down):
    batch, seq, _ = x_prompt.shape
    dec_batch, dec_seq, _ = x_sample.shape
    past_len = cache_k.shape[2]
    assert dec_batch <= CTX_MOD_ROW

    c_rows = jnp.concatenate(
        [c, c_ctx[None, :], jnp.zeros((MOD_ROWS - dec_batch - 1, D_MODEL), F32)], axis=0)
    mod_rows = _ada_call(c_rows, w_ada, b_ada).reshape(DEPTH * MOD_ROWS, 1, N_MOD * D_MODEL)

    cos_t, sin_t = _rope_tables(dec_seq)
    vec = lambda a: a.reshape(DEPTH, 1, a.shape[-1])
    q_gain = vec(jnp.tile(q_norm, (1, LANES // HEAD_DIM)))
    k_gain = vec(jnp.tile(k_norm, (1, LANES // HEAD_DIM)))
    w_in_b = w_in.astype(BF16)
    w_pool_b = w_pool.astype(BF16)
    w_au_b = w_attn_up.astype(BF16)
    w_pu_b = w_pool_up.astype(BF16)
    w_out_b = w_out.astype(BF16)
    w_g_b = w_ffn_gate.astype(BF16)
    w_u_b = w_ffn_up.astype(BF16)
    w_d_b = w_ffn_down.astype(BF16)
    n_pre_mix_v, n_post_mix_v = vec(n_pre_mix), vec(n_post_mix)
    n_pre_ffn_v, n_post_ffn_v = vec(n_pre_ffn), vec(n_post_ffn)
    pool_scale_v = vec(pool_scale)
    ck = cache_k.reshape(dec_batch, DEPTH, past_len, KV_W)
    cv = cache_v.reshape(dec_batch, DEPTH, past_len, KV_W)

    tm = 512
    ctx_seqs_per_step = 4

    def trunk(l, x, *, is_lat):
        seq_len = dec_seq if is_lat else seq
        outs = _mix_call(l, x, mod_rows, n_pre_mix_v, w_in_b, q_gain, k_gain, cos_t, sin_t,
                         w_pool_b, pool_scale_v, ck if is_lat else None, cv if is_lat else None,
                         seq_len=seq_len, n_seq=1 if is_lat else ctx_seqs_per_step, rope=is_lat)
        ao, y = outs[:2]
        x2 = _tail_call(l, x, ao, y, mod_rows, n_pre_mix_v, n_post_mix_v, n_pre_ffn_v, n_post_ffn_v,
                        w_in_b, w_au_b, w_pu_b, w_out_b, w_g_b, w_u_b, w_d_b,
                        seq_len=seq_len, per_seq_mod=is_lat, tm=tm, sub=256)
        return x2, outs[2:]

    y = x_prompt.reshape(batch * seq, D_MODEL)
    z = x_sample.reshape(dec_batch * dec_seq, D_MODEL)
    new_ks, new_vs = [], []
    for l in range(DEPTH):
        y, (nk, nv) = trunk(l, y, is_lat=False)
        z, _ = trunk(l, z, is_lat=True)
        new_ks.append(nk)
        new_vs.append(nv)

    def kv_layout(parts):
        a = jnp.stack(parts, axis=1).reshape(batch, DEPTH, N_KV_HEADS, HEAD_DIM, seq)
        return jnp.transpose(a, (0, 1, 4, 2, 3))

    return (y.reshape(batch, seq, D_MODEL), z.reshape(dec_batch, dec_seq, D_MODEL),
            kv_layout(new_ks), kv_layout(new_vs))
```

```python
import functools

import jax
import jax.numpy as jnp
import numpy as np
from jax import lax
from jax.experimental import pallas as pl
from jax.experimental.pallas import tpu as pltpu

D_MODEL = 1024
DEPTH = 4
GRID_W = 64
HEAD_DIM = 64
N_Q_HEADS = 8
N_KV_HEADS = 2
Q_PER_KV = N_Q_HEADS // N_KV_HEADS
ATTN_W = N_Q_HEADS * HEAD_DIM
KV_W = N_KV_HEADS * HEAD_DIM
POOL_WINDOWS = (2, 4, 8, 16)
POOL_W = D_MODEL // 2
POOL_GW = POOL_W // len(POOL_WINDOWS)
GATE_COL0 = ATTN_W + 2 * KV_W + POOL_W
IN_COLS = GATE_COL0 + 2 * D_MODEL
FFN_HIDDEN = 2816
N_MOD = 6
ROPE_THETA = 10000.0
EPS = 1e-6

LANES = 128
Q_TILE = 256
VT_ROWS = 80
POOL_PAD = 24
LOG2_E = 1.4426950408889634
MOD_ROWS = 8
CTX_MOD_ROW = 4
VMEM_LIMIT_BYTES = 56 * 1024 * 1024

F32 = jnp.float32
BF16 = jnp.bfloat16


def _dot(a, b):
    return jnp.dot(a, b, preferred_element_type=F32)


def _dot_nt(a, b):
    return lax.dot_general(a, b, (((1,), (1,)), ((), ())), preferred_element_type=F32)


def _rms(x):
    return x * lax.rsqrt(jnp.mean(x * x, axis=-1, keepdims=True) + EPS)


def _params(n_axes=1):
    return pltpu.CompilerParams(dimension_semantics=("arbitrary",) * n_axes,
                                vmem_limit_bytes=VMEM_LIMIT_BYTES)


def _ada_kernel(c_ref, w_ref, b_ref, o_ref):
    c = c_ref[...]
    s = (c * jax.nn.sigmoid(c)).astype(BF16)
    o_ref[...] = _dot(s, w_ref[...].astype(BF16)) + b_ref[...]


def _ada_call(c_rows, w_ada, b_ada):
    tn = 1536
    n_cols = N_MOD * D_MODEL
    return pl.pallas_call(
        _ada_kernel,
        grid=(DEPTH, n_cols // tn),
        in_specs=[
            pl.BlockSpec((MOD_ROWS, D_MODEL), lambda l, j: (0, 0)),
            pl.BlockSpec((None, D_MODEL, tn), lambda l, j: (l, 0, j)),
            pl.BlockSpec((None, 1, tn), lambda l, j: (l, 0, j)),
        ],
        out_specs=pl.BlockSpec((None, MOD_ROWS, tn), lambda l, j: (l, 0, j)),
        out_shape=jax.ShapeDtypeStruct((DEPTH, MOD_ROWS, n_cols), F32),
        compiler_params=_params(2),
        name="ada_rows",
    )(c_rows, w_ada, b_ada.reshape(DEPTH, 1, n_cols))


def _pool_group(src, window, seq_len, p2_ref, p4_ref):
    half = window // 2
    rows = lambda ref, off: ref[POOL_PAD + off:POOL_PAD + off + seq_len, :]
    if window <= 4:
        total = rows(src, -half)
        for off in range(-half + 1, half):
            total = total + rows(src, off)
    else:
        n2 = seq_len + 2 * POOL_PAD - 8
        p2_ref[0:n2, :] = src[0:n2, :] + src[1:n2 + 1, :]
        if window == 8:
            terms = [rows(p2_ref, off) for off in (-4, -2, 0, 2)]
        else:
            assert window == 16
            n4 = n2 - 8
            p4_ref[0:n4, :] = p2_ref[0:n4, :] + p2_ref[2:n4 + 2, :]
            terms = [rows(p4_ref, off) for off in (-8, -4, 0, 4)]
        total = (terms[0] + terms[1]) + (terms[2] + terms[3])
    t = lax.broadcasted_iota(jnp.int32, (seq_len, POOL_GW), 0)
    count = jnp.minimum(t + half, seq_len) - jnp.maximum(t - half, 0)
    return total / count.astype(F32) - rows(src, 0)


def _mix_kernel(x_ref, mod_ref, npre_ref, w_ref, qg_ref, kg_ref, cos_ref, sin_ref, wpool_ref,
                pscale_ref, *rest, layer, seq_len, n_seq, n_cache, rope):
    if n_cache:
        ck_ref, cv_ref, ao_out, y_out = rest[:4]
        rest = rest[4:]
    else:
        w32_refs, rest = rest[:3], rest[3:]
        ao_out, y_out, nk_out, nv_out = rest[:4]
        wb_outs, rest = rest[4:7], rest[7:]
    (qa_ref, k_ref, vt_ref, p_ref, raw_a, raw_b, s_a, s_b, e_a, e_b, p2_a, p2_b, p4_b), rest = (
        rest[:13], rest[13:])
    raw_ref, s_ref, e_ref = (raw_a, raw_b), (s_a, s_b), (e_a, e_b)

    if not n_cache:
        stage_in, stage_out, sems = rest[0:3], rest[3:6], rest[6]
        grid_step = pl.program_id(0)
        last_grid_step = pl.num_programs(0) - 1

        def copy_in(j, at_step):
            n = stage_in[j].shape[0]
            row0 = layer * (w32_refs[j].shape[0] // DEPTH) + at_step * n
            return pltpu.make_async_copy(w32_refs[j].at[pl.ds(row0, n), :], stage_in[j], sems.at[j])

        def copy_out(j, at_step):
            n = stage_out[j].shape[0]
            return pltpu.make_async_copy(stage_out[j], wb_outs[j].at[pl.ds(at_step * n, n), :],
                                         sems.at[3 + j])

        @pl.when(grid_step > 0)
        def _():
            for j in range(3):
                copy_out(j, grid_step - 1).wait()

        for j in range(3):
            copy_in(j, grid_step).start()
    n_keys = n_cache + seq_len
    tiles_per_seq = seq_len // Q_TILE
    n_tiles = n_seq * tiles_per_seq

    mod = mod_ref[...]
    sh1 = mod[:, 0:D_MODEL]
    norm_gain = npre_ref[...] * (1.0 + mod[:, D_MODEL:2 * D_MODEL])

    lane = lax.broadcasted_iota(jnp.int32, (Q_TILE, LANES), 1)
    first_head = lane < HEAD_DIM
    second_half = (lane & (HEAD_DIM // 2)) != 0

    def head_norm(t, gain):
        t2 = t * t
        s_lo = jnp.sum(jnp.where(first_head, t2, 0.0), axis=-1, keepdims=True)
        s_hi = jnp.sum(jnp.where(first_head, 0.0, t2), axis=-1, keepdims=True)
        ms = jnp.where(first_head, s_lo, s_hi) * (1.0 / HEAD_DIM)
        return (t * lax.rsqrt(ms + EPS)) * gain

    def rotary(t, pos0):
        partner = jnp.where(second_half, pltpu.roll(t, HEAD_DIM // 2, 1),
                            pltpu.roll(t, LANES - HEAD_DIM // 2, 1))
        return t * cos_ref[pos0:pos0 + Q_TILE, :] + partner * sin_ref[pos0:pos0 + Q_TILE, :]

    ones_rows = jnp.ones((VT_ROWS - HEAD_DIM, n_keys), BF16)
    pad_rows = jnp.zeros((POOL_PAD, POOL_GW), F32)
    for s_idx in range(n_seq):
        for g in range(len(POOL_WINDOWS)):
            p_ref[s_idx, g, 0:POOL_PAD, :] = pad_rows
            p_ref[s_idx, g, POOL_PAD + seq_len:2 * POOL_PAD + seq_len, :] = pad_rows
        for kv in range(N_KV_HEADS):
            vt_ref[s_idx, kv * VT_ROWS + HEAD_DIM:(kv + 1) * VT_ROWS, :] = ones_rows
        if n_cache:
            k_ref[s_idx, 0:n_cache, :] = ck_ref[0, 0].astype(BF16)
            cvt = cv_ref[0, 0].T.astype(BF16)
            for kv in range(N_KV_HEADS):
                vt_ref[s_idx, kv * VT_ROWS:kv * VT_ROWS + HEAD_DIM, 0:n_cache] = (
                    cvt[kv * HEAD_DIM:(kv + 1) * HEAD_DIM, :])

    def project(tile, slot):
        r0 = tile * Q_TILE
        x = x_ref[r0:r0 + Q_TILE, :]
        hb = (_rms(x) * norm_gain + sh1).astype(BF16)
        raw_ref[slot][...] = _dot(hb, w_ref[...])

    def epilogue(tile, slot):
        s_idx, pos0 = tile // tiles_per_seq, (tile % tiles_per_seq) * Q_TILE
        r0 = tile * Q_TILE
        raw = raw_ref[slot]
        for j in range(ATTN_W // LANES):
            t = head_norm(raw[:, j * LANES:(j + 1) * LANES], qg_ref[...])
            if rope:
                t = rotary(t, pos0)
            t = t * (HEAD_DIM ** -0.5 * LOG2_E)
            swapped = pltpu.roll(t, HEAD_DIM, 1)
            if (2 * j) // Q_PER_KV == 0:
                even = jnp.where(first_head, t, 0.0)
                odd = jnp.where(first_head, swapped, 0.0)
            else:
                even = jnp.where(first_head, 0.0, swapped)
                odd = jnp.where(first_head, 0.0, t)
            qa_ref[tile, (2 * j) * Q_TILE:(2 * j + 1) * Q_TILE, :] = even.astype(BF16)
            qa_ref[tile, (2 * j + 1) * Q_TILE:(2 * j + 2) * Q_TILE, :] = odd.astype(BF16)
        kn = head_norm(raw[:, ATTN_W:ATTN_W + KV_W], kg_ref[...])
        v = raw[:, ATTN_W + KV_W:ATTN_W + 2 * KV_W]
        vt_f32 = v.T
        if not n_cache:
            nk_out[s_idx, :, pos0:pos0 + Q_TILE] = kn.T
            nv_out[s_idx, :, pos0:pos0 + Q_TILE] = vt_f32
        if rope:
            kn = rotary(kn, pos0)
        k0 = n_cache + pos0
        k_ref[s_idx, k0:k0 + Q_TILE, :] = kn.astype(BF16)
        vt = vt_f32.astype(BF16)
        for kv in range(N_KV_HEADS):
            vt_ref[s_idx, kv * VT_ROWS:kv * VT_ROWS + HEAD_DIM, k0:k0 + Q_TILE] = (
                vt[kv * HEAD_DIM:(kv + 1) * HEAD_DIM, :])
        for g in range(len(POOL_WINDOWS)):
            c0 = ATTN_W + 2 * KV_W + g * POOL_GW
            p_ref[s_idx, g, POOL_PAD + pos0:POOL_PAD + pos0 + Q_TILE, :] = raw[:, c0:c0 + POOL_GW]

    project(0, 0)
    for tile in range(n_tiles):
        if tile + 1 < n_tiles:
            project(tile + 1, (tile + 1) % 2)
        epilogue(tile, tile % 2)

    n_pairs = N_Q_HEADS // 2
    n_steps = n_tiles * n_pairs

    def scores(step, slot):
        tile, pair = step // n_pairs, step % n_pairs
        q2 = qa_ref[tile, pair * 2 * Q_TILE:(pair + 1) * 2 * Q_TILE, :]
        s_ref[slot][...] = _dot_nt(k_ref[tile // tiles_per_seq], q2)

    def exponent(step, slot):
        s = s_ref[slot][...]
        e_ref[slot][...] = jnp.exp2(s - jnp.max(s, axis=0, keepdims=True)).astype(BF16)

    def values(step, slot):
        tile, pair = step // n_pairs, step % n_pairs
        kv = (2 * pair) // Q_PER_KV
        vt = vt_ref[tile // tiles_per_seq, kv * VT_ROWS:(kv + 1) * VT_ROWS, :]
        ot = _dot(vt, e_ref[slot][...])
        ot = ot[0:HEAD_DIM] * (1.0 / ot[HEAD_DIM:HEAD_DIM + 1])
        both = jnp.concatenate([ot[:, 0:Q_TILE], ot[:, Q_TILE:2 * Q_TILE]], axis=0)
        ao_out[tile * Q_TILE:(tile + 1) * Q_TILE, pair * LANES:(pair + 1) * LANES] = both.T.astype(BF16)

    scores(0, 0)
    for step in range(n_steps + 1):
        if step + 1 < n_steps:
            scores(step + 1, (step + 1) % 2)
        if step < n_steps:
            exponent(step, step % 2)
        if step >= 1:
            values(step - 1, (step - 1) % 2)

    for s_idx in range(n_seq):
        r0 = s_idx * seq_len
        for g, window in enumerate(POOL_WINDOWS):
            cols = slice(g * POOL_GW, (g + 1) * POOL_GW)
            p2_ref = p2_a if window == 8 else p2_b
            d = _pool_group(p_ref.at[s_idx, g], window, seq_len, p2_ref, p4_b)
            yg = _dot(d.astype(BF16), wpool_ref[g]) * pscale_ref[:, cols]
            y_out[r0:r0 + seq_len, cols] = yg.astype(BF16)

    if not n_cache:
        for j in range(3):
            copy_in(j, grid_step).wait()
            stage_out[j][...] = stage_in[j][...].astype(BF16)
            copy_out(j, grid_step).start()

        @pl.when(grid_step == last_grid_step)
        def _():
            for j in range(3):
                copy_out(j, grid_step).wait()


def _mix_call(l, x, mod_rows, n_pre, w_qkvp, q_gain, k_gain, cos_t, sin_t, w_pool, pool_scale,
              cache_k, cache_v, ffn_weights_f32, *, seq_len, n_seq, rope):
    n_tok = x.shape[0]
    rows = n_seq * seq_len
    is_lat = cache_k is not None
    n_cache = cache_k.shape[2] if is_lat else 0
    n_keys = n_cache + seq_len
    if is_lat:
        assert n_seq == 1
        mod_idx = lambda i: (l * MOD_ROWS + i, 0, 0)
    else:
        mod_idx = lambda i: (l * MOD_ROWS + CTX_MOD_ROW, 0, 0)
    lay = lambda i: (l, 0, 0)
    row = lambda i: (i, 0)
    in_specs = [
        pl.BlockSpec((rows, D_MODEL), row),
        pl.BlockSpec((None, 1, N_MOD * D_MODEL), mod_idx),
        pl.BlockSpec((None, 1, D_MODEL), lay),
        pl.BlockSpec((None, D_MODEL, GATE_COL0), lay),
        pl.BlockSpec((None, 1, LANES), lay),
        pl.BlockSpec((None, 1, LANES), lay),
        pl.BlockSpec((seq_len, LANES), lambda i: (0, 0)),
        pl.BlockSpec((seq_len, LANES), lambda i: (0, 0)),
        pl.BlockSpec((None, len(POOL_WINDOWS), POOL_GW, POOL_GW), lambda i: (l, 0, 0, 0)),
        pl.BlockSpec((None, 1, POOL_W), lay),
    ]
    args = [x, mod_rows, n_pre, w_qkvp, q_gain, k_gain, cos_t, sin_t, w_pool, pool_scale]
    out_shape = [jax.ShapeDtypeStruct((n_tok, ATTN_W), BF16), jax.ShapeDtypeStruct((n_tok, POOL_W), BF16)]
    out_specs = [pl.BlockSpec((rows, ATTN_W), row), pl.BlockSpec((rows, POOL_W), row)]
    if is_lat:
        cache_spec = pl.BlockSpec((1, 1, n_cache, KV_W), lambda i: (i, l, 0, 0))
        in_specs += [cache_spec, cache_spec]
        args += [cache_k, cache_v]
    else:
        n_steps = n_tok // rows
        any_spec = pl.BlockSpec(memory_space=pl.ANY)
        in_specs += [any_spec] * 3
        args += [w.reshape(-1, w.shape[2]) for w in ffn_weights_f32]
        out_shape += [jax.ShapeDtypeStruct((n_tok // seq_len, KV_W, seq_len), F32)] * 2
        out_specs += [pl.BlockSpec((n_seq, KV_W, seq_len), lambda i: (i, 0, 0))] * 2
        out_shape += [jax.ShapeDtypeStruct(w.shape[1:], BF16) for w in ffn_weights_f32]
        out_specs += [any_spec] * 3
        slices = [(w.shape[1] // n_steps, w.shape[2]) for w in ffn_weights_f32]
        assert all(w.shape[1] % n_steps == 0 and r % 16 == 0 for w, (r, _) in zip(ffn_weights_f32, slices))
    scratch = [
        pltpu.VMEM((rows // Q_TILE, N_Q_HEADS * Q_TILE, LANES), BF16),
        pltpu.VMEM((n_seq, n_keys, KV_W), BF16),
        pltpu.VMEM((n_seq, N_KV_HEADS * VT_ROWS, n_keys), BF16),
        pltpu.VMEM((n_seq, len(POOL_WINDOWS), seq_len + 2 * POOL_PAD, POOL_GW), F32),
    ]
    scratch += [pltpu.VMEM((Q_TILE, GATE_COL0), F32)] * 2
    scratch += [pltpu.VMEM((n_keys, 2 * Q_TILE), F32)] * 2
    scratch += [pltpu.VMEM((n_keys, 2 * Q_TILE), BF16)] * 2
    scratch += [pltpu.VMEM((seq_len + 2 * POOL_PAD - 8, POOL_GW), F32)] * 2
    scratch += [pltpu.VMEM((seq_len + 2 * POOL_PAD - 16, POOL_GW), F32)]
    if not is_lat:
        scratch += [pltpu.VMEM(s, F32) for s in slices] + [pltpu.VMEM(s, BF16) for s in slices]
        scratch += [pltpu.SemaphoreType.DMA((6,))]
    return pl.pallas_call(
        functools.partial(_mix_kernel, layer=l, seq_len=seq_len, n_seq=n_seq, n_cache=n_cache, rope=rope),
        grid=(n_tok // rows,),
        in_specs=in_specs,
        out_specs=out_specs,
        out_shape=out_shape,
        scratch_shapes=scratch,
        compiler_params=_params(),
        name="mix_lat" if is_lat else "mix_ctx",
    )(*args)


def _tail_kernel(x_ref, ao_ref, y_ref, mod_ref, npre_mix_ref, npost_mix_ref, npre_ffn_ref,
                 npost_ffn_ref, win_ref, wau_ref, wpu_ref, wout_ref, wg_ref, wu_ref, wd_ref, o_ref,
                 x1_a, x1_b, *, sub):
    mod = mod_ref[...]
    sh1, sc1, g1, sh2, sc2, g2 = [mod[:, i * D_MODEL:(i + 1) * D_MODEL] for i in range(N_MOD)]
    x1_ref = (x1_a, x1_b)
    n_sub = x_ref.shape[0] // sub

    def post(i, slot):
        r0 = i * sub
        x = x_ref[r0:r0 + sub, :]
        hb = ((_rms(x) * npre_mix_ref[...]) * (1.0 + sc1) + sh1).astype(BF16)
        g_attn = jax.nn.sigmoid(_dot(hb, win_ref[:, GATE_COL0:GATE_COL0 + D_MODEL]))
        g_pool = jax.nn.sigmoid(_dot(hb, win_ref[:, GATE_COL0 + D_MODEL:IN_COLS]))
        attn = _dot(ao_ref[r0:r0 + sub, :], wau_ref[...])
        pool = _dot(y_ref[r0:r0 + sub, :], wpu_ref[...])
        mixed = g_attn * attn + g_pool * pool
        m = _dot(mixed.astype(BF16), wout_ref[...])
        x1_ref[slot][...] = x + g1 * (_rms(m) * npost_mix_ref[...])

    def ffn(i, slot):
        r0 = i * sub
        x1 = x1_ref[slot][...]
        hb = ((_rms(x1) * npre_ffn_ref[...]) * (1.0 + sc2) + sh2).astype(BF16)
        g = _dot(hb, wg_ref[...])
        u = _dot(hb, wu_ref[...])
        a = ((g * jax.nn.sigmoid(g)) * u).astype(BF16)
        f = _dot(a, wd_ref[...])
        o_ref[r0:r0 + sub, :] = x1 + g2 * (_rms(f) * npost_ffn_ref[...])

    post(0, 0)
    for i in range(n_sub):
        if i + 1 < n_sub:
            post(i + 1, (i + 1) % 2)
        ffn(i, i % 2)


def _tail_call(l, x, ao, y, mod_rows, n_pre_mix, n_post_mix, n_pre_ffn, n_post_ffn, w_in,
               w_attn_up, w_pool_up, w_out, w_ffn_gate, w_ffn_up, w_ffn_down,
               *, seq_len, per_seq_mod, tm, sub):
    n_tok = x.shape[0]
    if per_seq_mod:
        mod_idx = lambda i: (l * MOD_ROWS + i // (seq_len // tm), 0, 0)
    else:
        mod_idx = lambda i: (l * MOD_ROWS + CTX_MOD_ROW, 0, 0)
    lay = lambda i: (l, 0, 0)
    row = lambda i: (i, 0)
    vec_spec = pl.BlockSpec((None, 1, D_MODEL), lay)
    resident = lambda r, c: pl.BlockSpec((None, r, c), lay, pipeline_mode=pl.Buffered(1))
    layer_resident = lambda r, c: pl.BlockSpec((r, c), lambda i: (0, 0), pipeline_mode=pl.Buffered(1))
    return pl.pallas_call(
        functools.partial(_tail_kernel, sub=sub),
        grid=(n_tok // tm,),
        in_specs=[
            pl.BlockSpec((tm, D_MODEL), row),
            pl.BlockSpec((tm, ATTN_W), row),
            pl.BlockSpec((tm, POOL_W), row),
            pl.BlockSpec((None, 1, N_MOD * D_MODEL), mod_idx),
            vec_spec, vec_spec, vec_spec, vec_spec,
            resident(D_MODEL, IN_COLS),
            resident(ATTN_W, D_MODEL),
            resident(POOL_W, D_MODEL),
            resident(D_MODEL, D_MODEL),
            layer_resident(D_MODEL, FFN_HIDDEN),
            layer_resident(D_MODEL, FFN_HIDDEN),
            layer_resident(FFN_HIDDEN, D_MODEL),
        ],
        out_specs=pl.BlockSpec((tm, D_MODEL), row),
        out_shape=jax.ShapeDtypeStruct((n_tok, D_MODEL), F32),
        scratch_shapes=[pltpu.VMEM((sub, D_MODEL), F32)] * 2,
        compiler_params=_params(),
        name="tail",
    )(x, ao, y, mod_rows, n_pre_mix, n_post_mix, n_pre_ffn, n_post_ffn, w_in, w_attn_up,
      w_pool_up, w_out, w_ffn_gate, w_ffn_up, w_ffn_down)


def _rope_tables(n):
    rows = n // GRID_W
    row = np.repeat(np.arange(rows), GRID_W).astype(np.float32)
    col = np.tile(np.arange(GRID_W), rows).astype(np.float32)
    n_freq = HEAD_DIM // 4
    inv = jnp.asarray(ROPE_THETA, F32) ** (-jnp.arange(n_freq, dtype=F32) / n_freq)
    ang = jnp.concatenate([jnp.asarray(row)[:, None] * inv[None, :],
                           jnp.asarray(col)[:, None] * inv[None, :]], axis=-1)
    cos, sin = jnp.cos(ang), jnp.sin(ang)
    cos_t = jnp.tile(cos, (1, LANES // (HEAD_DIM // 2)))
    sin_t = jnp.tile(jnp.concatenate([-sin, sin], axis=-1), (1, LANES // HEAD_DIM))
    return cos_t, sin_t


def kernel(x_prompt, x_sample, cache_k, cache_v, c, c_ctx, w_ada, b_ada, w_in, q_norm, k_norm,
           w_attn_up, w_pool, pool_scale, w_pool_up, w_out, n_pre_mix, n_post_mix, n_pre_ffn,
           n_post_ffn, w_ffn_gate, w_ffn_up, w_ffn_down):
    batch, seq, _ = x_prompt.shape
    dec_batch, dec_seq, _ = x_sample.shape
    past_len = cache_k.shape[2]
    assert dec_batch <= CTX_MOD_ROW

    c_rows = jnp.concatenate(
        [c, c_ctx[None, :], jnp.zeros((MOD_ROWS - dec_batch - 1, D_MODEL), F32)], axis=0)
    mod_rows = _ada_call(c_rows, w_ada, b_ada).reshape(DEPTH * MOD_ROWS, 1, N_MOD * D_MODEL)

    cos_t, sin_t = _rope_tables(dec_seq)
    vec = lambda a: a.reshape(DEPTH, 1, a.shape[-1])
    q_gain = vec(jnp.tile(q_norm, (1, LANES // HEAD_DIM)))
    k_gain = vec(jnp.tile(k_norm, (1, LANES // HEAD_DIM)))
    w_in_b = w_in.astype(BF16)
    w_pool_b = w_pool.astype(BF16)
    w_au_b = w_attn_up.astype(BF16)
    w_pu_b = w_pool_up.astype(BF16)
    w_out_b = w_out.astype(BF16)
    ffn_weights_f32 = (w_ffn_gate, w_ffn_up, w_ffn_down)
    n_pre_mix_v, n_post_mix_v = vec(n_pre_mix), vec(n_post_mix)
    n_pre_ffn_v, n_post_ffn_v = vec(n_pre_ffn), vec(n_post_ffn)
    pool_scale_v = vec(pool_scale)
    ck = cache_k.reshape(dec_batch, DEPTH, past_len, KV_W)
    cv = cache_v.reshape(dec_batch, DEPTH, past_len, KV_W)

    tm = 512
    ctx_seqs_per_step = 4

    def trunk(l, x, ffn_b, *, is_lat):
        seq_len = dec_seq if is_lat else seq
        outs = _mix_call(l, x, mod_rows, n_pre_mix_v, w_in_b, q_gain, k_gain, cos_t, sin_t,
                         w_pool_b, pool_scale_v, ck if is_lat else None, cv if is_lat else None,
                         None if is_lat else ffn_weights_f32,
                         seq_len=seq_len, n_seq=1 if is_lat else ctx_seqs_per_step, rope=is_lat)
        ao, y = outs[:2]
        if not is_lat:
            ffn_b = outs[4:7]
        x2 = _tail_call(l, x, ao, y, mod_rows, n_pre_mix_v, n_post_mix_v, n_pre_ffn_v, n_post_ffn_v,
                        w_in_b, w_au_b, w_pu_b, w_out_b, *ffn_b,
                        seq_len=seq_len, per_seq_mod=is_lat, tm=tm, sub=256)
        return x2, outs[2:4], ffn_b

    y = x_prompt.reshape(batch * seq, D_MODEL)
    z = x_sample.reshape(dec_batch * dec_seq, D_MODEL)
    new_ks, new_vs = [], []
    for l in range(DEPTH):
        y, (nk, nv), ffn_b = trunk(l, y, None, is_lat=False)
        z, _, _ = trunk(l, z, ffn_b, is_lat=True)
        new_ks.append(nk)
        new_vs.append(nv)

    def kv_layout(parts):
        a = jnp.stack(parts, axis=1).reshape(batch, DEPTH, N_KV_HEADS, HEAD_DIM, seq)
        return jnp.transpose(a, (0, 1, 4, 2, 3))

    return (y.reshape(batch, seq, D_MODEL), z.reshape(dec_batch, dec_seq, D_MODEL),
            kv_layout(new_ks), kv_layout(new_vs))
```

```python
import functools

import jax
import jax.numpy as jnp
import numpy as np
from jax import lax
from jax.experimental import pallas as pl
from jax.experimental.pallas import tpu as pltpu

D_MODEL = 1024
DEPTH = 4
GRID_W = 64
HEAD_DIM = 64
N_Q_HEADS = 8
N_KV_HEADS = 2
Q_PER_KV = N_Q_HEADS // N_KV_HEADS
ATTN_W = N_Q_HEADS * HEAD_DIM
KV_W = N_KV_HEADS * HEAD_DIM
POOL_WINDOWS = (2, 4, 8, 16)
POOL_W = D_MODEL // 2
POOL_GW = POOL_W // len(POOL_WINDOWS)
GATE_COL0 = ATTN_W + 2 * KV_W + POOL_W
IN_COLS = GATE_COL0 + 2 * D_MODEL
FFN_HIDDEN = 2816
N_MOD = 6
ROPE_THETA = 10000.0
EPS = 1e-6

LANES = 128
Q_TILE = 256
VT_ROWS = 80
POOL_PAD = 24
LOG2_E = 1.4426950408889634
MOD_ROWS = 8
CTX_MOD_ROW = 4
VMEM_LIMIT_BYTES = 56 * 1024 * 1024

F32 = jnp.float32
BF16 = jnp.bfloat16


def _dot(a, b):
    return jnp.dot(a, b, preferred_element_type=F32)


def _dot_nt(a, b):
    return lax.dot_general(a, b, (((1,), (1,)), ((), ())), preferred_element_type=F32)


def _rms(x):
    return x * lax.rsqrt(jnp.mean(x * x, axis=-1, keepdims=True) + EPS)


def _params(n_axes=1):
    return pltpu.CompilerParams(dimension_semantics=("arbitrary",) * n_axes,
                                vmem_limit_bytes=VMEM_LIMIT_BYTES)


def _ada_kernel(c_ref, w_ref, b_ref, o_ref):
    c = c_ref[...]
    s = (c * jax.nn.sigmoid(c)).astype(BF16)
    o_ref[...] = _dot(s, w_ref[...].astype(BF16)) + b_ref[...]


def _ada_call(c_rows, w_ada, b_ada):
    tn = 1536
    n_cols = N_MOD * D_MODEL
    return pl.pallas_call(
        _ada_kernel,
        grid=(DEPTH, n_cols // tn),
        in_specs=[
            pl.BlockSpec((MOD_ROWS, D_MODEL), lambda l, j: (0, 0)),
            pl.BlockSpec((None, D_MODEL, tn), lambda l, j: (l, 0, j)),
            pl.BlockSpec((None, 1, tn), lambda l, j: (l, 0, j)),
        ],
        out_specs=pl.BlockSpec((None, MOD_ROWS, tn), lambda l, j: (l, 0, j)),
        out_shape=jax.ShapeDtypeStruct((DEPTH, MOD_ROWS, n_cols), F32),
        compiler_params=_params(2),
        name="ada_rows",
    )(c_rows, w_ada, b_ada.reshape(DEPTH, 1, n_cols))


def _pool_group(src, window, seq_len, p2_ref, p4_ref):
    half = window // 2
    rows = lambda ref, off: ref[POOL_PAD + off:POOL_PAD + off + seq_len, :]
    if window <= 4:
        total = rows(src, -half)
        for off in range(-half + 1, half):
            total = total + rows(src, off)
    else:
        n2 = seq_len + 2 * POOL_PAD - 8
        p2_ref[0:n2, :] = src[0:n2, :] + src[1:n2 + 1, :]
        if window == 8:
            terms = [rows(p2_ref, off) for off in (-4, -2, 0, 2)]
        else:
            assert window == 16
            n4 = n2 - 8
            p4_ref[0:n4, :] = p2_ref[0:n4, :] + p2_ref[2:n4 + 2, :]
            terms = [rows(p4_ref, off) for off in (-8, -4, 0, 4)]
        total = (terms[0] + terms[1]) + (terms[2] + terms[3])
    t = lax.broadcasted_iota(jnp.int32, (seq_len, POOL_GW), 0)
    count = jnp.minimum(t + half, seq_len) - jnp.maximum(t - half, 0)
    return total / count.astype(F32) - rows(src, 0)


def _mix_kernel(x_ref, mod_ref, npre_ref, w_ref, qg_ref, kg_ref, cos_ref, sin_ref, wpool_ref,
                pscale_ref, *rest, layer, seq_len, n_seq, n_cache, rope):
    if n_cache:
        ck_ref, cv_ref, ao_out, y_out = rest[:4]
        rest = rest[4:]
    else:
        w32_refs, rest = rest[:3], rest[3:]
        ao_out, y_out, nk_out, nv_out = rest[:4]
        wb_outs, rest = rest[4:7], rest[7:]
    (qa_ref, k_ref, vt_ref, p_ref, raw_a, raw_b, s_a, s_b, e_a, e_b, p2_a, p2_b, p4_b), rest = (
        rest[:13], rest[13:])
    raw_ref, s_ref, e_ref = (raw_a, raw_b), (s_a, s_b), (e_a, e_b)

    if not n_cache:
        stage_in, stage_out, sems = rest[0:3], rest[3:6], rest[6]
        grid_step = pl.program_id(0)
        last_grid_step = pl.num_programs(0) - 1

        def copy_in(j, at_step):
            n = stage_in[j].shape[0]
            row0 = layer * (w32_refs[j].shape[0] // DEPTH) + at_step * n
            return pltpu.make_async_copy(w32_refs[j].at[pl.ds(row0, n), :], stage_in[j], sems.at[j])

        def copy_out(j, at_step):
            n = stage_out[j].shape[0]
            return pltpu.make_async_copy(stage_out[j], wb_outs[j].at[pl.ds(at_step * n, n), :],
                                         sems.at[3 + j])

        for j in range(3):
            copy_in(j, grid_step).start()
    n_keys = n_cache + seq_len
    tiles_per_seq = seq_len // Q_TILE
    n_tiles = n_seq * tiles_per_seq

    mod = mod_ref[...]
    sh1 = mod[:, 0:D_MODEL]
    norm_gain = npre_ref[...] * (1.0 + mod[:, D_MODEL:2 * D_MODEL])

    lane = lax.broadcasted_iota(jnp.int32, (Q_TILE, LANES), 1)
    first_head = lane < HEAD_DIM
    second_half = (lane & (HEAD_DIM // 2)) != 0

    def head_norm(t, gain):
        t2 = t * t
        s_lo = jnp.sum(jnp.where(first_head, t2, 0.0), axis=-1, keepdims=True)
        s_hi = jnp.sum(jnp.where(first_head, 0.0, t2), axis=-1, keepdims=True)
        ms = jnp.where(first_head, s_lo, s_hi) * (1.0 / HEAD_DIM)
        return (t * lax.rsqrt(ms + EPS)) * gain

    def rotary(t, pos0):
        partner = jnp.where(second_half, pltpu.roll(t, HEAD_DIM // 2, 1),
                            pltpu.roll(t, LANES - HEAD_DIM // 2, 1))
        return t * cos_ref[pos0:pos0 + Q_TILE, :] + partner * sin_ref[pos0:pos0 + Q_TILE, :]

    ones_rows = jnp.ones((VT_ROWS - HEAD_DIM, n_keys), BF16)
    pad_rows = jnp.zeros((POOL_PAD, POOL_GW), F32)
    for s_idx in range(n_seq):
        for g in range(len(POOL_WINDOWS)):
            p_ref[s_idx, g, 0:POOL_PAD, :] = pad_rows
            p_ref[s_idx, g, POOL_PAD + seq_len:2 * POOL_PAD + seq_len, :] = pad_rows
        for kv in range(N_KV_HEADS):
            vt_ref[s_idx, kv * VT_ROWS + HEAD_DIM:(kv + 1) * VT_ROWS, :] = ones_rows
        if n_cache:
            k_ref[s_idx, 0:n_cache, :] = ck_ref[0, 0].astype(BF16)
            cvt = cv_ref[0, 0].T.astype(BF16)
            for kv in range(N_KV_HEADS):
                vt_ref[s_idx, kv * VT_ROWS:kv * VT_ROWS + HEAD_DIM, 0:n_cache] = (
                    cvt[kv * HEAD_DIM:(kv + 1) * HEAD_DIM, :])

    def project(tile, slot):
        r0 = tile * Q_TILE
        x = x_ref[r0:r0 + Q_TILE, :]
        hb = (_rms(x) * norm_gain + sh1).astype(BF16)
        raw_ref[slot][...] = _dot(hb, w_ref[...])

    def epilogue(tile, slot):
        s_idx, pos0 = tile // tiles_per_seq, (tile % tiles_per_seq) * Q_TILE
        r0 = tile * Q_TILE
        raw = raw_ref[slot]
        for j in range(ATTN_W // LANES):
            t = head_norm(raw[:, j * LANES:(j + 1) * LANES], qg_ref[...])
            if rope:
                t = rotary(t, pos0)
            t = t * (HEAD_DIM ** -0.5 * LOG2_E)
            swapped = pltpu.roll(t, HEAD_DIM, 1)
            if (2 * j) // Q_PER_KV == 0:
                even = jnp.where(first_head, t, 0.0)
                odd = jnp.where(first_head, swapped, 0.0)
            else:
                even = jnp.where(first_head, 0.0, swapped)
                odd = jnp.where(first_head, 0.0, t)
            qa_ref[tile, (2 * j) * Q_TILE:(2 * j + 1) * Q_TILE, :] = even.astype(BF16)
            qa_ref[tile, (2 * j + 1) * Q_TILE:(2 * j + 2) * Q_TILE, :] = odd.astype(BF16)
        kn = head_norm(raw[:, ATTN_W:ATTN_W + KV_W], kg_ref[...])
        v = raw[:, ATTN_W + KV_W:ATTN_W + 2 * KV_W]
        vt_f32 = v.T
        if not n_cache:
            nk_out[s_idx, :, pos0:pos0 + Q_TILE] = kn.T
            nv_out[s_idx, :, pos0:pos0 + Q_TILE] = vt_f32
        if rope:
            kn = rotary(kn, pos0)
        k0 = n_cache + pos0
        k_ref[s_idx, k0:k0 + Q_TILE, :] = kn.astype(BF16)
        vt = vt_f32.astype(BF16)
        for kv in range(N_KV_HEADS):
            vt_ref[s_idx, kv * VT_ROWS:kv * VT_ROWS + HEAD_DIM, k0:k0 + Q_TILE] = (
                vt[kv * HEAD_DIM:(kv + 1) * HEAD_DIM, :])
        for g in range(len(POOL_WINDOWS)):
            c0 = ATTN_W + 2 * KV_W + g * POOL_GW
            p_ref[s_idx, g, POOL_PAD + pos0:POOL_PAD + pos0 + Q_TILE, :] = raw[:, c0:c0 + POOL_GW]

    project(0, 0)
    for tile in range(n_tiles):
        if tile + 1 < n_tiles:
            project(tile + 1, (tile + 1) % 2)
        epilogue(tile, tile % 2)

    n_pairs = N_Q_HEADS // 2
    n_steps = n_tiles * n_pairs

    def scores(step, slot):
        tile, pair = step // n_pairs, step % n_pairs
        q2 = qa_ref[tile, pair * 2 * Q_TILE:(pair + 1) * 2 * Q_TILE, :]
        s_ref[slot][...] = _dot_nt(k_ref[tile // tiles_per_seq], q2)

    def exponent(step, slot):
        s = s_ref[slot][...]
        e_ref[slot][...] = jnp.exp2(s - jnp.max(s, axis=0, keepdims=True)).astype(BF16)

    def values(step, slot):
        tile, pair = step // n_pairs, step % n_pairs
        kv = (2 * pair) // Q_PER_KV
        vt = vt_ref[tile // tiles_per_seq, kv * VT_ROWS:(kv + 1) * VT_ROWS, :]
        ot = _dot(vt, e_ref[slot][...])
        ot = ot[0:HEAD_DIM] * (1.0 / ot[HEAD_DIM:HEAD_DIM + 1])
        both = jnp.concatenate([ot[:, 0:Q_TILE], ot[:, Q_TILE:2 * Q_TILE]], axis=0)
        ao_out[tile * Q_TILE:(tile + 1) * Q_TILE, pair * LANES:(pair + 1) * LANES] = both.T.astype(BF16)

    scores(0, 0)
    for step in range(n_steps + 1):
        if step + 1 < n_steps:
            scores(step + 1, (step + 1) % 2)
        if step < n_steps:
            exponent(step, step % 2)
        if step >= 1:
            values(step - 1, (step - 1) % 2)

    for s_idx in range(n_seq):
        r0 = s_idx * seq_len
        for g, window in enumerate(POOL_WINDOWS):
            cols = slice(g * POOL_GW, (g + 1) * POOL_GW)
            p2_ref = p2_a if window == 8 else p2_b
            d = _pool_group(p_ref.at[s_idx, g], window, seq_len, p2_ref, p4_b)
            yg = _dot(d.astype(BF16), wpool_ref[g]) * pscale_ref[:, cols]
            y_out[r0:r0 + seq_len, cols] = yg.astype(BF16)

    if not n_cache:
        @pl.when(grid_step > 0)
        def _():
            for j in range(3):
                copy_out(j, grid_step - 1).wait()

        for j in range(3):
            copy_in(j, grid_step).wait()
            stage_out[j][...] = stage_in[j][...].astype(BF16)
            copy_out(j, grid_step).start()

        @pl.when(grid_step == last_grid_step)
        def _():
            for j in range(3):
                copy_out(j, grid_step).wait()


def _mix_call(l, x, mod_rows, n_pre, w_qkvp, q_gain, k_gain, cos_t, sin_t, w_pool, pool_scale,
              cache_k, cache_v, ffn_weights_f32, *, seq_len, n_seq, rope):
    n_tok = x.shape[0]
    rows = n_seq * seq_len
    is_lat = cache_k is not None
    n_cache = cache_k.shape[2] if is_lat else 0
    n_keys = n_cache + seq_len
    if is_lat:
        assert n_seq == 1
        mod_idx = lambda i: (l * MOD_ROWS + i, 0, 0)
    else:
        mod_idx = lambda i: (l * MOD_ROWS + CTX_MOD_ROW, 0, 0)
    lay = lambda i: (l, 0, 0)
    row = lambda i: (i, 0)
    in_specs = [
        pl.BlockSpec((rows, D_MODEL), row),
        pl.BlockSpec((None, 1, N_MOD * D_MODEL), mod_idx),
        pl.BlockSpec((None, 1, D_MODEL), lay),
        pl.BlockSpec((None, D_MODEL, GATE_COL0), lay),
        pl.BlockSpec((None, 1, LANES), lay),
        pl.BlockSpec((None, 1, LANES), lay),
        pl.BlockSpec((seq_len, LANES), lambda i: (0, 0)),
        pl.BlockSpec((seq_len, LANES), lambda i: (0, 0)),
        pl.BlockSpec((None, len(POOL_WINDOWS), POOL_GW, POOL_GW), lambda i: (l, 0, 0, 0)),
        pl.BlockSpec((None, 1, POOL_W), lay),
    ]
    args = [x, mod_rows, n_pre, w_qkvp, q_gain, k_gain, cos_t, sin_t, w_pool, pool_scale]
    out_shape = [jax.ShapeDtypeStruct((n_tok, ATTN_W), BF16), jax.ShapeDtypeStruct((n_tok, POOL_W), BF16)]
    out_specs = [pl.BlockSpec((rows, ATTN_W), row), pl.BlockSpec((rows, POOL_W), row)]
    if is_lat:
        cache_spec = pl.BlockSpec((1, 1, n_cache, KV_W), lambda i: (i, l, 0, 0))
        in_specs += [cache_spec, cache_spec]
        args += [cache_k, cache_v]
    else:
        n_steps = n_tok // rows
        any_spec = pl.BlockSpec(memory_space=pl.ANY)
        in_specs += [any_spec] * 3
        args += [w.reshape(-1, w.shape[2]) for w in ffn_weights_f32]
        out_shape += [jax.ShapeDtypeStruct((n_tok // seq_len, KV_W, seq_len), F32)] * 2
        out_specs += [pl.BlockSpec((n_seq, KV_W, seq_len), lambda i: (i, 0, 0))] * 2
        out_shape += [jax.ShapeDtypeStruct(w.shape[1:], BF16) for w in ffn_weights_f32]
        out_specs += [any_spec] * 3
        slices = [(w.shape[1] // n_steps, w.shape[2]) for w in ffn_weights_f32]
        assert all(w.shape[1] % n_steps == 0 and r % 16 == 0 for w, (r, _) in zip(ffn_weights_f32, slices))
    scratch = [
        pltpu.VMEM((rows // Q_TILE, N_Q_HEADS * Q_TILE, LANES), BF16),
        pltpu.VMEM((n_seq, n_keys, KV_W), BF16),
        pltpu.VMEM((n_seq, N_KV_HEADS * VT_ROWS, n_keys), BF16),
        pltpu.VMEM((n_seq, len(POOL_WINDOWS), seq_len + 2 * POOL_PAD, POOL_GW), F32),
    ]
    scratch += [pltpu.VMEM((Q_TILE, GATE_COL0), F32)] * 2
    scratch += [pltpu.VMEM((n_keys, 2 * Q_TILE), F32)] * 2
    scratch += [pltpu.VMEM((n_keys, 2 * Q_TILE), BF16)] * 2
    scratch += [pltpu.VMEM((seq_len + 2 * POOL_PAD - 8, POOL_GW), F32)] * 2
    scratch += [pltpu.VMEM((seq_len + 2 * POOL_PAD - 16, POOL_GW), F32)]
    if not is_lat:
        scratch += [pltpu.VMEM(s, F32) for s in slices] + [pltpu.VMEM(s, BF16) for s in slices]
        scratch += [pltpu.SemaphoreType.DMA((6,))]
    return pl.pallas_call(
        functools.partial(_mix_kernel, layer=l, seq_len=seq_len, n_seq=n_seq, n_cache=n_cache, rope=rope),
        grid=(n_tok // rows,),
        in_specs=in_specs,
        out_specs=out_specs,
        out_shape=out_shape,
        scratch_shapes=scratch,
        compiler_params=_params(),
        name="mix_lat" if is_lat else "mix_ctx",
    )(*args)


def _tail_kernel(x_ref, ao_ref, y_ref, mod_ref, npre_mix_ref, npost_mix_ref, npre_ffn_ref,
                 npost_ffn_ref, win_ref, wau_ref, wpu_ref, wout_ref, wg_ref, wu_ref, wd_ref, o_ref,
                 x1_a, x1_b, *, sub):
    mod = mod_ref[...]
    sh1, sc1, g1, sh2, sc2, g2 = [mod[:, i * D_MODEL:(i + 1) * D_MODEL] for i in range(N_MOD)]
    x1_ref = (x1_a, x1_b)
    n_sub = x_ref.shape[0] // sub

    def post(i, slot):
        r0 = i * sub
        x = x_ref[r0:r0 + sub, :]
        hb = ((_rms(x) * npre_mix_ref[...]) * (1.0 + sc1) + sh1).astype(BF16)
        g_attn = jax.nn.sigmoid(_dot(hb, win_ref[:, GATE_COL0:GATE_COL0 + D_MODEL]))
        g_pool = jax.nn.sigmoid(_dot(hb, win_ref[:, GATE_COL0 + D_MODEL:IN_COLS]))
        attn = _dot(ao_ref[r0:r0 + sub, :], wau_ref[...])
        pool = _dot(y_ref[r0:r0 + sub, :], wpu_ref[...])
        mixed = g_attn * attn + g_pool * pool
        m = _dot(mixed.astype(BF16), wout_ref[...])
        x1_ref[slot][...] = x + g1 * (_rms(m) * npost_mix_ref[...])

    def ffn(i, slot):
        r0 = i * sub
        x1 = x1_ref[slot][...]
        hb = ((_rms(x1) * npre_ffn_ref[...]) * (1.0 + sc2) + sh2).astype(BF16)
        g = _dot(hb, wg_ref[...])
        u = _dot(hb, wu_ref[...])
        a = ((g * jax.nn.sigmoid(g)) * u).astype(BF16)
        f = _dot(a, wd_ref[...])
        o_ref[r0:r0 + sub, :] = x1 + g2 * (_rms(f) * npost_ffn_ref[...])

    post(0, 0)
    for i in range(n_sub):
        if i + 1 < n_sub:
            post(i + 1, (i + 1) % 2)
        ffn(i, i % 2)


def _tail_call(l, x, ao, y, mod_rows, n_pre_mix, n_post_mix, n_pre_ffn, n_post_ffn, w_in,
               w_attn_up, w_pool_up, w_out, w_ffn_gate, w_ffn_up, w_ffn_down,
               *, seq_len, per_seq_mod, tm, sub):
    n_tok = x.shape[0]
    if per_seq_mod:
        mod_idx = lambda i: (l * MOD_ROWS + i // (seq_len // tm), 0, 0)
    else:
        mod_idx = lambda i: (l * MOD_ROWS + CTX_MOD_ROW, 0, 0)
    lay = lambda i: (l, 0, 0)
    row = lambda i: (i, 0)
    vec_spec = pl.BlockSpec((None, 1, D_MODEL), lay)
    resident = lambda r, c: pl.BlockSpec((None, r, c), lay, pipeline_mode=pl.Buffered(1))
    layer_resident = lambda r, c: pl.BlockSpec((r, c), lambda i: (0, 0), pipeline_mode=pl.Buffered(1))
    return pl.pallas_call(
        functools.partial(_tail_kernel, sub=sub),
        grid=(n_tok // tm,),
        in_specs=[
            pl.BlockSpec((tm, D_MODEL), row),
            pl.BlockSpec((tm, ATTN_W), row),
            pl.BlockSpec((tm, POOL_W), row),
            pl.BlockSpec((None, 1, N_MOD * D_MODEL), mod_idx),
            vec_spec, vec_spec, vec_spec, vec_spec,
            resident(D_MODEL, IN_COLS),
            resident(ATTN_W, D_MODEL),
            resident(POOL_W, D_MODEL),
            resident(D_MODEL, D_MODEL),
            layer_resident(D_MODEL, FFN_HIDDEN),
            layer_resident(D_MODEL, FFN_HIDDEN),
            layer_resident(FFN_HIDDEN, D_MODEL),
        ],
        out_specs=pl.BlockSpec((tm, D_MODEL), row),
        out_shape=jax.ShapeDtypeStruct((n_tok, D_MODEL), F32),
        scratch_shapes=[pltpu.VMEM((sub, D_MODEL), F32)] * 2,
        compiler_params=_params(),
        name="tail",
    )(x, ao, y, mod_rows, n_pre_mix, n_post_mix, n_pre_ffn, n_post_ffn, w_in, w_attn_up,
      w_pool_up, w_out, w_ffn_gate, w_ffn_up, w_ffn_down)


def _rope_tables(n):
    rows = n // GRID_W
    row = np.repeat(np.arange(rows), GRID_W).astype(np.float32)
    col = np.tile(np.arange(GRID_W), rows).astype(np.float32)
    n_freq = HEAD_DIM // 4
    inv = jnp.asarray(ROPE_THETA, F32) ** (-jnp.arange(n_freq, dtype=F32) / n_freq)
    ang = jnp.concatenate([jnp.asarray(row)[:, None] * inv[None, :],
                           jnp.asarray(col)[:, None] * inv[None, :]], axis=-1)
    cos, sin = jnp.cos(ang), jnp.sin(ang)
    cos_t = jnp.tile(cos, (1, LANES // (HEAD_DIM // 2)))
    sin_t = jnp.tile(jnp.concatenate([-sin, sin], axis=-1), (1, LANES // HEAD_DIM))
    return cos_t, sin_t


def kernel(x_prompt, x_sample, cache_k, cache_v, c, c_ctx, w_ada, b_ada, w_in, q_norm, k_norm,
           w_attn_up, w_pool, pool_scale, w_pool_up, w_out, n_pre_mix, n_post_mix, n_pre_ffn,
           n_post_ffn, w_ffn_gate, w_ffn_up, w_ffn_down):
    batch, seq, _ = x_prompt.shape
    dec_batch, dec_seq, _ = x_sample.shape
    past_len = cache_k.shape[2]
    assert dec_batch <= CTX_MOD_ROW

    c_rows = jnp.concatenate(
        [c, c_ctx[None, :], jnp.zeros((MOD_ROWS - dec_batch - 1, D_MODEL), F32)], axis=0)
    mod_rows = _ada_call(c_rows, w_ada, b_ada).reshape(DEPTH * MOD_ROWS, 1, N_MOD * D_MODEL)

    cos_t, sin_t = _rope_tables(dec_seq)
    vec = lambda a: a.reshape(DEPTH, 1, a.shape[-1])
    q_gain = vec(jnp.tile(q_norm, (1, LANES // HEAD_DIM)))
    k_gain = vec(jnp.tile(k_norm, (1, LANES // HEAD_DIM)))
    w_in_b = w_in.astype(BF16)
    w_pool_b = w_pool.astype(BF16)
    w_au_b = w_attn_up.astype(BF16)
    w_pu_b = w_pool_up.astype(BF16)
    w_out_b = w_out.astype(BF16)
    ffn_weights_f32 = (w_ffn_gate, w_ffn_up, w_ffn_down)
    n_pre_mix_v, n_post_mix_v = vec(n_pre_mix), vec(n_post_mix)
    n_pre_ffn_v, n_post_ffn_v = vec(n_pre_ffn), vec(n_post_ffn)
    pool_scale_v = vec(pool_scale)
    ck = cache_k.reshape(dec_batch, DEPTH, past_len, KV_W)
    cv = cache_v.reshape(dec_batch, DEPTH, past_len, KV_W)

    tm = 512
    ctx_seqs_per_step = 4

    def trunk(l, x, ffn_b, *, is_lat):
        seq_len = dec_seq if is_lat else seq
        outs = _mix_call(l, x, mod_rows, n_pre_mix_v, w_in_b, q_gain, k_gain, cos_t, sin_t,
                         w_pool_b, pool_scale_v, ck if is_lat else None, cv if is_lat else None,
                         None if is_lat else ffn_weights_f32,
                         seq_len=seq_len, n_seq=1 if is_lat else ctx_seqs_per_step, rope=is_lat)
        ao, y = outs[:2]
        if not is_lat:
            ffn_b = outs[4:7]
        x2 = _tail_call(l, x, ao, y, mod_rows, n_pre_mix_v, n_post_mix_v, n_pre_ffn_v, n_post_ffn_v,
                        w_in_b, w_au_b, w_pu_b, w_out_b, *ffn_b,
                        seq_len=seq_len, per_seq_mod=is_lat, tm=tm, sub=256)
        return x2, outs[2:4], ffn_b

    y = x_prompt.reshape(batch * seq, D_MODEL)
    z = x_sample.reshape(dec_batch * dec_seq, D_MODEL)
    new_ks, new_vs = [], []
    for l in range(DEPTH):
        y, (nk, nv), ffn_b = trunk(l, y, None, is_lat=False)
        z, _, _ = trunk(l, z, ffn_b, is_lat=True)
        new_ks.append(nk)
        new_vs.append(nv)

    def kv_layout(parts):
        a = jnp.stack(parts, axis=1).reshape(batch, DEPTH, N_KV_HEADS, HEAD_DIM, seq)
        return jnp.transpose(a, (0, 1, 4, 2, 3))

    return (y.reshape(batch, seq, D_MODEL), z.reshape(dec_batch, dec_seq, D_MODEL),
            kv_layout(new_ks), kv_layout(new_vs))
```

```python
import functools

import jax
import jax.numpy as jnp
import numpy as np
from jax import lax
from jax.experimental import pallas as pl
from jax.experimental.pallas import tpu as pltpu

D_MODEL = 1024
DEPTH = 4
GRID_W = 64
HEAD_DIM = 64
N_Q_HEADS = 8
N_KV_HEADS = 2
Q_PER_KV = N_Q_HEADS // N_KV_HEADS
ATTN_W = N_Q_HEADS * HEAD_DIM
KV_W = N_KV_HEADS * HEAD_DIM
POOL_WINDOWS = (2, 4, 8, 16)
POOL_W = D_MODEL // 2
POOL_GW = POOL_W // len(POOL_WINDOWS)
GATE_COL0 = ATTN_W + 2 * KV_W + POOL_W
IN_COLS = GATE_COL0 + 2 * D_MODEL
FFN_HIDDEN = 2816
N_MOD = 6
ROPE_THETA = 10000.0
EPS = 1e-6

LANES = 128
Q_TILE = 256
VT_ROWS = 80
POOL_PAD = 24
LOG2_E = 1.4426950408889634
MOD_ROWS = 8
CTX_MOD_ROW = 4
VMEM_LIMIT_BYTES = 56 * 1024 * 1024

F32 = jnp.float32
BF16 = jnp.bfloat16


def _dot(a, b):
    return jnp.dot(a, b, preferred_element_type=F32)


def _dot_nt(a, b):
    return lax.dot_general(a, b, (((1,), (1,)), ((), ())), preferred_element_type=F32)


def _rms(x):
    return x * lax.rsqrt(jnp.mean(x * x, axis=-1, keepdims=True) + EPS)


def _params(n_axes=1):
    return pltpu.CompilerParams(dimension_semantics=("arbitrary",) * n_axes,
                                vmem_limit_bytes=VMEM_LIMIT_BYTES)


def _ada_kernel(c_ref, w_ref, b_ref, o_ref):
    c = c_ref[...]
    s = (c * jax.nn.sigmoid(c)).astype(BF16)
    o_ref[...] = _dot(s, w_ref[...].astype(BF16)) + b_ref[...]


def _ada_call(c_rows, w_ada, b_ada):
    tn = 1536
    n_cols = N_MOD * D_MODEL
    return pl.pallas_call(
        _ada_kernel,
        grid=(DEPTH, n_cols // tn),
        in_specs=[
            pl.BlockSpec((MOD_ROWS, D_MODEL), lambda l, j: (0, 0)),
            pl.BlockSpec((None, D_MODEL, tn), lambda l, j: (l, 0, j)),
            pl.BlockSpec((None, 1, tn), lambda l, j: (l, 0, j)),
        ],
        out_specs=pl.BlockSpec((None, MOD_ROWS, tn), lambda l, j: (l, 0, j)),
        out_shape=jax.ShapeDtypeStruct((DEPTH, MOD_ROWS, n_cols), F32),
        compiler_params=_params(2),
        name="ada_rows",
    )(c_rows, w_ada, b_ada.reshape(DEPTH, 1, n_cols))


def _pool_group(src, window, seq_len, p2_ref, p4_ref):
    half = window // 2
    rows = lambda ref, off: ref[POOL_PAD + off:POOL_PAD + off + seq_len, :]
    if window <= 4:
        total = rows(src, -half)
        for off in range(-half + 1, half):
            total = total + rows(src, off)
    else:
        n2 = seq_len + 2 * POOL_PAD - 8
        p2_ref[0:n2, :] = src[0:n2, :] + src[1:n2 + 1, :]
        if window == 8:
            terms = [rows(p2_ref, off) for off in (-4, -2, 0, 2)]
        else:
            assert window == 16
            n4 = n2 - 8
            p4_ref[0:n4, :] = p2_ref[0:n4, :] + p2_ref[2:n4 + 2, :]
            terms = [rows(p4_ref, off) for off in (-8, -4, 0, 4)]
        total = (terms[0] + terms[1]) + (terms[2] + terms[3])
    t = lax.broadcasted_iota(jnp.int32, (seq_len, POOL_GW), 0)
    count = jnp.minimum(t + half, seq_len) - jnp.maximum(t - half, 0)
    return total / count.astype(F32) - rows(src, 0)


def _mix_kernel(x_ref, mod_ref, npre_ref, w_ref, qg_ref, kg_ref, cos_ref, sin_ref, wpool_ref,
                pscale_ref, *rest, seq_len, n_seq, n_cache, rope, n_cast, cast_layer):
    if n_cache:
        (ck_ref, cv_ref), rest = rest[:2], rest[2:]
    w32_refs, rest = rest[:n_cast], rest[n_cast:]
    if n_cache:
        (ao_out, y_out), rest = rest[:2], rest[2:]
    else:
        (ao_out, y_out, nk_out, nv_out), rest = rest[:4], rest[4:]
    wb_outs, rest = rest[:n_cast], rest[n_cast:]
    (qa_ref, k_ref, vt_ref, p_ref, raw_a, raw_b, s_a, s_b, e_a, e_b, p2_a, p2_b, p4_b), rest = (
        rest[:13], rest[13:])
    raw_ref, s_ref, e_ref = (raw_a, raw_b), (s_a, s_b), (e_a, e_b)

    if n_cast:
        stage_in, stage_out, sems = rest[0:n_cast], rest[n_cast:2 * n_cast], rest[2 * n_cast]
        grid_step = pl.program_id(0)
        last_grid_step = pl.num_programs(0) - 1

        def copy_in(j, at_step):
            n = stage_in[j].shape[0]
            row0 = cast_layer * (w32_refs[j].shape[0] // DEPTH) + at_step * n
            return pltpu.make_async_copy(w32_refs[j].at[pl.ds(row0, n), :], stage_in[j], sems.at[j])

        def copy_out(j, at_step):
            n = stage_out[j].shape[0]
            return pltpu.make_async_copy(stage_out[j], wb_outs[j].at[pl.ds(at_step * n, n), :],
                                         sems.at[n_cast + j])

        for j in range(n_cast):
            copy_in(j, grid_step).start()
    n_keys = n_cache + seq_len
    tiles_per_seq = seq_len // Q_TILE
    n_tiles = n_seq * tiles_per_seq

    mod = mod_ref[...]
    sh1 = mod[:, 0:D_MODEL]
    norm_gain = npre_ref[...] * (1.0 + mod[:, D_MODEL:2 * D_MODEL])

    lane = lax.broadcasted_iota(jnp.int32, (Q_TILE, LANES), 1)
    first_head = lane < HEAD_DIM
    second_half = (lane & (HEAD_DIM // 2)) != 0

    def head_norm(t, gain):
        t2 = t * t
        s_lo = jnp.sum(jnp.where(first_head, t2, 0.0), axis=-1, keepdims=True)
        s_hi = jnp.sum(jnp.where(first_head, 0.0, t2), axis=-1, keepdims=True)
        ms = jnp.where(first_head, s_lo, s_hi) * (1.0 / HEAD_DIM)
        return (t * lax.rsqrt(ms + EPS)) * gain

    def rotary(t, pos0):
        partner = jnp.where(second_half, pltpu.roll(t, HEAD_DIM // 2, 1),
                            pltpu.roll(t, LANES - HEAD_DIM // 2, 1))
        return t * cos_ref[pos0:pos0 + Q_TILE, :] + partner * sin_ref[pos0:pos0 + Q_TILE, :]

    ones_rows = jnp.ones((VT_ROWS - HEAD_DIM, n_keys), BF16)
    pad_rows = jnp.zeros((POOL_PAD, POOL_GW), F32)
    for s_idx in range(n_seq):
        for g in range(len(POOL_WINDOWS)):
            p_ref[s_idx, g, 0:POOL_PAD, :] = pad_rows
            p_ref[s_idx, g, POOL_PAD + seq_len:2 * POOL_PAD + seq_len, :] = pad_rows
        for kv in range(N_KV_HEADS):
            vt_ref[s_idx, kv * VT_ROWS + HEAD_DIM:(kv + 1) * VT_ROWS, :] = ones_rows
        if n_cache:
            k_ref[s_idx, 0:n_cache, :] = ck_ref[0, 0].astype(BF16)
            cvt = cv_ref[0, 0].T.astype(BF16)
            for kv in range(N_KV_HEADS):
                vt_ref[s_idx, kv * VT_ROWS:kv * VT_ROWS + HEAD_DIM, 0:n_cache] = (
                    cvt[kv * HEAD_DIM:(kv + 1) * HEAD_DIM, :])

    def project(tile, slot):
        r0 = tile * Q_TILE
        x = x_ref[r0:r0 + Q_TILE, :]
        hb = (_rms(x) * norm_gain + sh1).astype(BF16)
        raw_ref[slot][...] = _dot(hb, w_ref[...])

    def epilogue(tile, slot):
        s_idx, pos0 = tile // tiles_per_seq, (tile % tiles_per_seq) * Q_TILE
        r0 = tile * Q_TILE
        raw = raw_ref[slot]
        for j in range(ATTN_W // LANES):
            t = head_norm(raw[:, j * LANES:(j + 1) * LANES], qg_ref[...])
            if rope:
                t = rotary(t, pos0)
            t = t * (HEAD_DIM ** -0.5 * LOG2_E)
            swapped = pltpu.roll(t, HEAD_DIM, 1)
            if (2 * j) // Q_PER_KV == 0:
                even = jnp.where(first_head, t, 0.0)
                odd = jnp.where(first_head, swapped, 0.0)
            else:
                even = jnp.where(first_head, 0.0, swapped)
                odd = jnp.where(first_head, 0.0, t)
            qa_ref[tile, (2 * j) * Q_TILE:(2 * j + 1) * Q_TILE, :] = even.astype(BF16)
            qa_ref[tile, (2 * j + 1) * Q_TILE:(2 * j + 2) * Q_TILE, :] = odd.astype(BF16)
        kn = head_norm(raw[:, ATTN_W:ATTN_W + KV_W], kg_ref[...])
        v = raw[:, ATTN_W + KV_W:ATTN_W + 2 * KV_W]
        vt_f32 = v.T
        if not n_cache:
            nk_out[s_idx, :, pos0:pos0 + Q_TILE] = kn.T
            nv_out[s_idx, :, pos0:pos0 + Q_TILE] = vt_f32
        if rope:
            kn = rotary(kn, pos0)
        k0 = n_cache + pos0
        k_ref[s_idx, k0:k0 + Q_TILE, :] = kn.astype(BF16)
        vt = vt_f32.astype(BF16)
        for kv in range(N_KV_HEADS):
            vt_ref[s_idx, kv * VT_ROWS:kv * VT_ROWS + HEAD_DIM, k0:k0 + Q_TILE] = (
                vt[kv * HEAD_DIM:(kv + 1) * HEAD_DIM, :])
        for g in range(len(POOL_WINDOWS)):
            c0 = ATTN_W + 2 * KV_W + g * POOL_GW
            p_ref[s_idx, g, POOL_PAD + pos0:POOL_PAD + pos0 + Q_TILE, :] = raw[:, c0:c0 + POOL_GW]

    project(0, 0)
    for tile in range(n_tiles):
        if tile + 1 < n_tiles:
            project(tile + 1, (tile + 1) % 2)
        epilogue(tile, tile % 2)

    n_pairs = N_Q_HEADS // 2
    n_steps = n_tiles * n_pairs

    def scores(step, slot):
        tile, pair = step // n_pairs, step % n_pairs
        q2 = qa_ref[tile, pair * 2 * Q_TILE:(pair + 1) * 2 * Q_TILE, :]
        s_ref[slot][...] = _dot_nt(k_ref[tile // tiles_per_seq], q2)

    def exponent(step, slot):
        s = s_ref[slot][...]
        e_ref[slot][...] = jnp.exp2(s - jnp.max(s, axis=0, keepdims=True)).astype(BF16)

    def values(step, slot):
        tile, pair = step // n_pairs, step % n_pairs
        kv = (2 * pair) // Q_PER_KV
        vt = vt_ref[tile // tiles_per_seq, kv * VT_ROWS:(kv + 1) * VT_ROWS, :]
        ot = _dot(vt, e_ref[slot][...])
        ot = ot[0:HEAD_DIM] * (1.0 / ot[HEAD_DIM:HEAD_DIM + 1])
        both = jnp.concatenate([ot[:, 0:Q_TILE], ot[:, Q_TILE:2 * Q_TILE]], axis=0)
        ao_out[tile * Q_TILE:(tile + 1) * Q_TILE, pair * LANES:(pair + 1) * LANES] = both.T.astype(BF16)

    scores(0, 0)
    for step in range(n_steps + 1):
        if step + 1 < n_steps:
            scores(step + 1, (step + 1) % 2)
        if step < n_steps:
            exponent(step, step % 2)
        if step >= 1:
            values(step - 1, (step - 1) % 2)

    for s_idx in range(n_seq):
        r0 = s_idx * seq_len
        for g, window in enumerate(POOL_WINDOWS):
            cols = slice(g * POOL_GW, (g + 1) * POOL_GW)
            p2_ref = p2_a if window == 8 else p2_b
            d = _pool_group(p_ref.at[s_idx, g], window, seq_len, p2_ref, p4_b)
            yg = _dot(d.astype(BF16), wpool_ref[g]) * pscale_ref[:, cols]
            y_out[r0:r0 + seq_len, cols] = yg.astype(BF16)

    if n_cast:
        @pl.when(grid_step > 0)
        def _():
            for j in range(n_cast):
                copy_out(j, grid_step - 1).wait()

        for j in range(n_cast):
            copy_in(j, grid_step).wait()
            stage_out[j][...] = stage_in[j][...].astype(BF16)
            copy_out(j, grid_step).start()

        @pl.when(grid_step == last_grid_step)
        def _():
            for j in range(n_cast):
                copy_out(j, grid_step).wait()


def _mix_call(l, x, mod_rows, n_pre, w_in_l, q_gain, k_gain, cos_t, sin_t, w_pool, pool_scale,
              cache_k, cache_v, cast_weights_f32, cast_layer, *, seq_len, n_seq, rope):
    n_tok = x.shape[0]
    rows = n_seq * seq_len
    is_lat = cache_k is not None
    n_cache = cache_k.shape[2] if is_lat else 0
    n_keys = n_cache + seq_len
    if is_lat:
        assert n_seq == 1
        mod_idx = lambda i: (l * MOD_ROWS + i, 0, 0)
    else:
        mod_idx = lambda i: (l * MOD_ROWS + CTX_MOD_ROW, 0, 0)
    lay = lambda i: (l, 0, 0)
    row = lambda i: (i, 0)
    in_specs = [
        pl.BlockSpec((rows, D_MODEL), row),
        pl.BlockSpec((None, 1, N_MOD * D_MODEL), mod_idx),
        pl.BlockSpec((None, 1, D_MODEL), lay),
        pl.BlockSpec((D_MODEL, GATE_COL0), lambda i: (0, 0)),
        pl.BlockSpec((None, 1, LANES), lay),
        pl.BlockSpec((None, 1, LANES), lay),
        pl.BlockSpec((seq_len, LANES), lambda i: (0, 0)),
        pl.BlockSpec((seq_len, LANES), lambda i: (0, 0)),
        pl.BlockSpec((None, len(POOL_WINDOWS), POOL_GW, POOL_GW), lambda i: (l, 0, 0, 0)),
        pl.BlockSpec((None, 1, POOL_W), lay),
    ]
    args = [x, mod_rows, n_pre, w_in_l, q_gain, k_gain, cos_t, sin_t, w_pool, pool_scale]
    out_shape = [jax.ShapeDtypeStruct((n_tok, ATTN_W), BF16), jax.ShapeDtypeStruct((n_tok, POOL_W), BF16)]
    out_specs = [pl.BlockSpec((rows, ATTN_W), row), pl.BlockSpec((rows, POOL_W), row)]
    if is_lat:
        cache_spec = pl.BlockSpec((1, 1, n_cache, KV_W), lambda i: (i, l, 0, 0))
        in_specs += [cache_spec, cache_spec]
        args += [cache_k, cache_v]
    else:
        out_shape += [jax.ShapeDtypeStruct((n_tok // seq_len, KV_W, seq_len), F32)] * 2
        out_specs += [pl.BlockSpec((n_seq, KV_W, seq_len), lambda i: (i, 0, 0))] * 2
    n_cast = len(cast_weights_f32)
    n_steps = n_tok // rows
    any_spec = pl.BlockSpec(memory_space=pl.ANY)
    in_specs += [any_spec] * n_cast
    args += [w.reshape(-1, w.shape[2]) for w in cast_weights_f32]
    out_shape += [jax.ShapeDtypeStruct(w.shape[1:], BF16) for w in cast_weights_f32]
    out_specs += [any_spec] * n_cast
    slices = [(w.shape[1] // n_steps, w.shape[2]) for w in cast_weights_f32]
    assert all(w.shape[1] % n_steps == 0 and r % 16 == 0 for w, (r, _) in zip(cast_weights_f32, slices))
    scratch = [
        pltpu.VMEM((rows // Q_TILE, N_Q_HEADS * Q_TILE, LANES), BF16),
        pltpu.VMEM((n_seq, n_keys, KV_W), BF16),
        pltpu.VMEM((n_seq, N_KV_HEADS * VT_ROWS, n_keys), BF16),
        pltpu.VMEM((n_seq, len(POOL_WINDOWS), seq_len + 2 * POOL_PAD, POOL_GW), F32),
    ]
    scratch += [pltpu.VMEM((Q_TILE, GATE_COL0), F32)] * 2
    scratch += [pltpu.VMEM((n_keys, 2 * Q_TILE), F32)] * 2
    scratch += [pltpu.VMEM((n_keys, 2 * Q_TILE), BF16)] * 2
    scratch += [pltpu.VMEM((seq_len + 2 * POOL_PAD - 8, POOL_GW), F32)] * 2
    scratch += [pltpu.VMEM((seq_len + 2 * POOL_PAD - 16, POOL_GW), F32)]
    if n_cast:
        scratch += [pltpu.VMEM(s, F32) for s in slices] + [pltpu.VMEM(s, BF16) for s in slices]
        scratch += [pltpu.SemaphoreType.DMA((2 * n_cast,))]
    return pl.pallas_call(
        functools.partial(_mix_kernel, seq_len=seq_len, n_seq=n_seq, n_cache=n_cache, rope=rope,
                          n_cast=n_cast, cast_layer=cast_layer),
        grid=(n_tok // rows,),
        in_specs=in_specs,
        out_specs=out_specs,
        out_shape=out_shape,
        scratch_shapes=scratch,
        compiler_params=_params(),
        name="mix_lat" if is_lat else "mix_ctx",
    )(*args)


def _tail_kernel(x_ref, ao_ref, y_ref, mod_ref, npre_mix_ref, npost_mix_ref, npre_ffn_ref,
                 npost_ffn_ref, win_ref, wau_ref, wpu_ref, wout_ref, wg_ref, wu_ref, wd_ref, o_ref,
                 x1_a, x1_b, *, sub):
    mod = mod_ref[...]
    sh1, sc1, g1, sh2, sc2, g2 = [mod[:, i * D_MODEL:(i + 1) * D_MODEL] for i in range(N_MOD)]
    x1_ref = (x1_a, x1_b)
    n_sub = x_ref.shape[0] // sub

    def post(i, slot):
        r0 = i * sub
        x = x_ref[r0:r0 + sub, :]
        hb = ((_rms(x) * npre_mix_ref[...]) * (1.0 + sc1) + sh1).astype(BF16)
        g_attn = jax.nn.sigmoid(_dot(hb, win_ref[:, GATE_COL0:GATE_COL0 + D_MODEL]))
        g_pool = jax.nn.sigmoid(_dot(hb, win_ref[:, GATE_COL0 + D_MODEL:IN_COLS]))
        attn = _dot(ao_ref[r0:r0 + sub, :], wau_ref[...])
        pool = _dot(y_ref[r0:r0 + sub, :], wpu_ref[...])
        mixed = g_attn * attn + g_pool * pool
        m = _dot(mixed.astype(BF16), wout_ref[...])
        x1_ref[slot][...] = x + g1 * (_rms(m) * npost_mix_ref[...])

    def ffn(i, slot):
        r0 = i * sub
        x1 = x1_ref[slot][...]
        hb = ((_rms(x1) * npre_ffn_ref[...]) * (1.0 + sc2) + sh2).astype(BF16)
        g = _dot(hb, wg_ref[...])
        u = _dot(hb, wu_ref[...])
        a = ((g * jax.nn.sigmoid(g)) * u).astype(BF16)
        f = _dot(a, wd_ref[...])
        o_ref[r0:r0 + sub, :] = x1 + g2 * (_rms(f) * npost_ffn_ref[...])

    post(0, 0)
    for i in range(n_sub):
        if i + 1 < n_sub:
            post(i + 1, (i + 1) % 2)
        ffn(i, i % 2)


def _tail_call(l, x, ao, y, mod_rows, n_pre_mix, n_post_mix, n_pre_ffn, n_post_ffn, w_in,
               w_attn_up, w_pool_up, w_out, w_ffn_gate, w_ffn_up, w_ffn_down,
               *, seq_len, per_seq_mod, tm, sub):
    n_tok = x.shape[0]
    if per_seq_mod:
        mod_idx = lambda i: (l * MOD_ROWS + i // (seq_len // tm), 0, 0)
    else:
        mod_idx = lambda i: (l * MOD_ROWS + CTX_MOD_ROW, 0, 0)
    lay = lambda i: (l, 0, 0)
    row = lambda i: (i, 0)
    vec_spec = pl.BlockSpec((None, 1, D_MODEL), lay)
    layer_resident = lambda r, c: pl.BlockSpec((r, c), lambda i: (0, 0), pipeline_mode=pl.Buffered(1))
    return pl.pallas_call(
        functools.partial(_tail_kernel, sub=sub),
        grid=(n_tok // tm,),
        in_specs=[
            pl.BlockSpec((tm, D_MODEL), row),
            pl.BlockSpec((tm, ATTN_W), row),
            pl.BlockSpec((tm, POOL_W), row),
            pl.BlockSpec((None, 1, N_MOD * D_MODEL), mod_idx),
            vec_spec, vec_spec, vec_spec, vec_spec,
            layer_resident(D_MODEL, IN_COLS),
            layer_resident(ATTN_W, D_MODEL),
            layer_resident(POOL_W, D_MODEL),
            layer_resident(D_MODEL, D_MODEL),
            layer_resident(D_MODEL, FFN_HIDDEN),
            layer_resident(D_MODEL, FFN_HIDDEN),
            layer_resident(FFN_HIDDEN, D_MODEL),
        ],
        out_specs=pl.BlockSpec((tm, D_MODEL), row),
        out_shape=jax.ShapeDtypeStruct((n_tok, D_MODEL), F32),
        scratch_shapes=[pltpu.VMEM((sub, D_MODEL), F32)] * 2,
        compiler_params=_params(),
        name="tail",
    )(x, ao, y, mod_rows, n_pre_mix, n_post_mix, n_pre_ffn, n_post_ffn, w_in, w_attn_up,
      w_pool_up, w_out, w_ffn_gate, w_ffn_up, w_ffn_down)


def _rope_tables(n):
    rows = n // GRID_W
    row = np.repeat(np.arange(rows), GRID_W).astype(np.float32)
    col = np.tile(np.arange(GRID_W), rows).astype(np.float32)
    n_freq = HEAD_DIM // 4
    inv = jnp.asarray(ROPE_THETA, F32) ** (-jnp.arange(n_freq, dtype=F32) / n_freq)
    ang = jnp.concatenate([jnp.asarray(row)[:, None] * inv[None, :],
                           jnp.asarray(col)[:, None] * inv[None, :]], axis=-1)
    cos, sin = jnp.cos(ang), jnp.sin(ang)
    cos_t = jnp.tile(cos, (1, LANES // (HEAD_DIM // 2)))
    sin_t = jnp.tile(jnp.concatenate([-sin, sin], axis=-1), (1, LANES // HEAD_DIM))
    return cos_t, sin_t


def kernel(x_prompt, x_sample, cache_k, cache_v, c, c_ctx, w_ada, b_ada, w_in, q_norm, k_norm,
           w_attn_up, w_pool, pool_scale, w_pool_up, w_out, n_pre_mix, n_post_mix, n_pre_ffn,
           n_post_ffn, w_ffn_gate, w_ffn_up, w_ffn_down):
    batch, seq, _ = x_prompt.shape
    dec_batch, dec_seq, _ = x_sample.shape
    past_len = cache_k.shape[2]
    assert dec_batch <= CTX_MOD_ROW

    c_rows = jnp.concatenate(
        [c, c_ctx[None, :], jnp.zeros((MOD_ROWS - dec_batch - 1, D_MODEL), F32)], axis=0)
    mod_rows = _ada_call(c_rows, w_ada, b_ada).reshape(DEPTH * MOD_ROWS, 1, N_MOD * D_MODEL)

    cos_t, sin_t = _rope_tables(dec_seq)
    vec = lambda a: a.reshape(DEPTH, 1, a.shape[-1])
    q_gain = vec(jnp.tile(q_norm, (1, LANES // HEAD_DIM)))
    k_gain = vec(jnp.tile(k_norm, (1, LANES // HEAD_DIM)))
    w_in_l = w_in[0].astype(BF16)
    w_pool_b = w_pool.astype(BF16)
    tail_weights_f32 = (w_attn_up, w_pool_up, w_out, w_ffn_gate, w_ffn_up, w_ffn_down)
    n_pre_mix_v, n_post_mix_v = vec(n_pre_mix), vec(n_post_mix)
    n_pre_ffn_v, n_post_ffn_v = vec(n_pre_ffn), vec(n_post_ffn)
    pool_scale_v = vec(pool_scale)
    ck = cache_k.reshape(dec_batch, DEPTH, past_len, KV_W)
    cv = cache_v.reshape(dec_batch, DEPTH, past_len, KV_W)

    tm = 512
    ctx_seqs_per_step = 4

    y = x_prompt.reshape(batch * seq, D_MODEL)
    z = x_sample.reshape(dec_batch * dec_seq, D_MODEL)
    new_ks, new_vs = [], []
    for l in range(DEPTH):
        outs = _mix_call(l, y, mod_rows, n_pre_mix_v, w_in_l, q_gain, k_gain, cos_t, sin_t, w_pool_b,
                         pool_scale_v, None, None, tail_weights_f32, l,
                         seq_len=seq, n_seq=ctx_seqs_per_step, rope=False)
        ao, yp, nk, nv = outs[:4]
        tail_weights_b = outs[4:]
        new_ks.append(nk)
        new_vs.append(nv)
        y = _tail_call(l, y, ao, yp, mod_rows, n_pre_mix_v, n_post_mix_v, n_pre_ffn_v, n_post_ffn_v,
                       w_in_l, *tail_weights_b, seq_len=seq, per_seq_mod=False, tm=tm, sub=256)
        has_next = l + 1 < DEPTH
        outs = _mix_call(l, z, mod_rows, n_pre_mix_v, w_in_l, q_gain, k_gain, cos_t, sin_t, w_pool_b,
                         pool_scale_v, ck, cv, (w_in,) if has_next else (), l + 1,
                         seq_len=dec_seq, n_seq=1, rope=True)
        ao, yp = outs[:2]
        z = _tail_call(l, z, ao, yp, mod_rows, n_pre_mix_v, n_post_mix_v, n_pre_ffn_v, n_post_ffn_v,
                       w_in_l, *tail_weights_b, seq_len=dec_seq, per_seq_mod=True, tm=tm, sub=256)
        if has_next:
            w_in_l = outs[2]

    def kv_layout(parts):
        a = jnp.stack(parts, axis=1).reshape(batch, DEPTH, N_KV_HEADS, HEAD_DIM, seq)
        return jnp.transpose(a, (0, 1, 4, 2, 3))

    return (y.reshape(batch, seq, D_MODEL), z.reshape(dec_batch, dec_seq, D_MODEL),
            kv_layout(new_ks), kv_layout(new_vs))
```

```python
import functools

import jax
import jax.numpy as jnp
import numpy as np
from jax import lax
from jax.experimental import pallas as pl
from jax.experimental.pallas import tpu as pltpu

D_MODEL = 1024
DEPTH = 4
GRID_W = 64
HEAD_DIM = 64
N_Q_HEADS = 8
N_KV_HEADS = 2
Q_PER_KV = N_Q_HEADS // N_KV_HEADS
ATTN_W = N_Q_HEADS * HEAD_DIM
KV_W = N_KV_HEADS * HEAD_DIM
POOL_WINDOWS = (2, 4, 8, 16)
POOL_W = D_MODEL // 2
POOL_GW = POOL_W // len(POOL_WINDOWS)
GATE_COL0 = ATTN_W + 2 * KV_W + POOL_W
IN_COLS = GATE_COL0 + 2 * D_MODEL
FFN_HIDDEN = 2816
N_MOD = 6
ROPE_THETA = 10000.0
EPS = 1e-6

LANES = 128
Q_TILE = 256
VT_ROWS = 80
POOL_PAD = 24
LOG2_E = 1.4426950408889634
MOD_ROWS = 8
CTX_MOD_ROW = 4
VMEM_LIMIT_BYTES = 56 * 1024 * 1024

F32 = jnp.float32
BF16 = jnp.bfloat16


def _dot(a, b):
    return jnp.dot(a, b, preferred_element_type=F32)


def _dot_nt(a, b):
    return lax.dot_general(a, b, (((1,), (1,)), ((), ())), preferred_element_type=F32)


def _rms(x):
    return x * lax.rsqrt(jnp.mean(x * x, axis=-1, keepdims=True) + EPS)


def _params(n_axes=1):
    return pltpu.CompilerParams(dimension_semantics=("arbitrary",) * n_axes,
                                vmem_limit_bytes=VMEM_LIMIT_BYTES)


def _ada_kernel(c_ref, w_ref, b_ref, o_ref):
    c = c_ref[...]
    s = (c * jax.nn.sigmoid(c)).astype(BF16)
    o_ref[...] = _dot(s, w_ref[...].astype(BF16)) + b_ref[...]


def _ada_call(c_rows, w_ada, b_ada):
    tn = 1536
    n_cols = N_MOD * D_MODEL
    return pl.pallas_call(
        _ada_kernel,
        grid=(DEPTH, n_cols // tn),
        in_specs=[
            pl.BlockSpec((MOD_ROWS, D_MODEL), lambda l, j: (0, 0)),
            pl.BlockSpec((None, D_MODEL, tn), lambda l, j: (l, 0, j)),
            pl.BlockSpec((None, 1, tn), lambda l, j: (l, 0, j)),
        ],
        out_specs=pl.BlockSpec((None, MOD_ROWS, tn), lambda l, j: (l, 0, j)),
        out_shape=jax.ShapeDtypeStruct((DEPTH, MOD_ROWS, n_cols), F32),
        compiler_params=_params(2),
        name="ada_rows",
    )(c_rows, w_ada, b_ada.reshape(DEPTH, 1, n_cols))


def _pool_group(src, window, seq_len, p2_ref, p4_ref):
    half = window // 2
    rows = lambda ref, off: ref[POOL_PAD + off:POOL_PAD + off + seq_len, :]
    if window <= 4:
        total = rows(src, -half)
        for off in range(-half + 1, half):
            total = total + rows(src, off)
    else:
        n2 = seq_len + 2 * POOL_PAD - 8
        p2_ref[0:n2, :] = src[0:n2, :] + src[1:n2 + 1, :]
        if window == 8:
            terms = [rows(p2_ref, off) for off in (-4, -2, 0, 2)]
        else:
            assert window == 16
            n4 = n2 - 8
            p4_ref[0:n4, :] = p2_ref[0:n4, :] + p2_ref[2:n4 + 2, :]
            terms = [rows(p4_ref, off) for off in (-8, -4, 0, 4)]
        total = (terms[0] + terms[1]) + (terms[2] + terms[3])
    t = lax.broadcasted_iota(jnp.int32, (seq_len, POOL_GW), 0)
    count = jnp.minimum(t + half, seq_len) - jnp.maximum(t - half, 0)
    return total / count.astype(F32) - rows(src, 0)


def _mix_kernel(x_ref, mod_ref, npre_ref, w_ref, qg_ref, kg_ref, cos_ref, sin_ref, wpool_ref,
                pscale_ref, *rest, seq_len, n_seq, n_cache, rope, n_cast, cast_layer):
    if n_cache:
        (ck_ref, cv_ref), rest = rest[:2], rest[2:]
    w32_refs, rest = rest[:n_cast], rest[n_cast:]
    if n_cache:
        (ao_out, y_out), rest = rest[:2], rest[2:]
    else:
        (ao_out, y_out, nk_out, nv_out), rest = rest[:4], rest[4:]
    wb_outs, rest = rest[:n_cast], rest[n_cast:]
    (qa_ref, k_ref, vt_ref, p_ref, raw_a, raw_b, s_a, s_b, e_a, e_b, p2_a, p2_b, p4_b), rest = (
        rest[:13], rest[13:])
    raw_ref, s_ref, e_ref = (raw_a, raw_b), (s_a, s_b), (e_a, e_b)

    if n_cast:
        stage_in, stage_out, sems = rest[0:n_cast], rest[n_cast:2 * n_cast], rest[2 * n_cast]
        grid_step = pl.program_id(0)
        last_grid_step = pl.num_programs(0) - 1

        def copy_in(j, at_step):
            n = stage_in[j].shape[0]
            row0 = cast_layer * (w32_refs[j].shape[0] // DEPTH) + at_step * n
            return pltpu.make_async_copy(w32_refs[j].at[pl.ds(row0, n), :], stage_in[j], sems.at[j])

        def copy_out(j, at_step):
            n = stage_out[j].shape[0]
            return pltpu.make_async_copy(stage_out[j], wb_outs[j].at[pl.ds(at_step * n, n), :],
                                         sems.at[n_cast + j])

        for j in range(n_cast):
            copy_in(j, grid_step).start()
    n_keys = n_cache + seq_len
    tiles_per_seq = seq_len // Q_TILE
    n_tiles = n_seq * tiles_per_seq

    mod = mod_ref[...]
    sh1 = mod[:, 0:D_MODEL]
    norm_gain = npre_ref[...] * (1.0 + mod[:, D_MODEL:2 * D_MODEL])

    lane = lax.broadcasted_iota(jnp.int32, (Q_TILE, LANES), 1)
    first_head = lane < HEAD_DIM
    second_half = (lane & (HEAD_DIM // 2)) != 0

    def head_norm(t, gain):
        t2 = t * t
        s_lo = jnp.sum(jnp.where(first_head, t2, 0.0), axis=-1, keepdims=True)
        s_hi = jnp.sum(jnp.where(first_head, 0.0, t2), axis=-1, keepdims=True)
        ms = jnp.where(first_head, s_lo, s_hi) * (1.0 / HEAD_DIM)
        return (t * lax.rsqrt(ms + EPS)) * gain

    def rotary(t, pos0):
        partner = jnp.where(second_half, pltpu.roll(t, HEAD_DIM // 2, 1),
                            pltpu.roll(t, LANES - HEAD_DIM // 2, 1))
        return t * cos_ref[pos0:pos0 + Q_TILE, :] + partner * sin_ref[pos0:pos0 + Q_TILE, :]

    ones_rows = jnp.ones((VT_ROWS - HEAD_DIM, n_keys), BF16)
    pad_rows = jnp.zeros((POOL_PAD, POOL_GW), F32)
    for s_idx in range(n_seq):
        for g in range(len(POOL_WINDOWS)):
            p_ref[s_idx, g, 0:POOL_PAD, :] = pad_rows
            p_ref[s_idx, g, POOL_PAD + seq_len:2 * POOL_PAD + seq_len, :] = pad_rows
        for kv in range(N_KV_HEADS):
            vt_ref[s_idx, kv * VT_ROWS + HEAD_DIM:(kv + 1) * VT_ROWS, :] = ones_rows
        if n_cache:
            k_ref[s_idx, 0:n_cache, :] = ck_ref[0, 0].astype(BF16)
            cvt = cv_ref[0, 0].T.astype(BF16)
            for kv in range(N_KV_HEADS):
                vt_ref[s_idx, kv * VT_ROWS:kv * VT_ROWS + HEAD_DIM, 0:n_cache] = (
                    cvt[kv * HEAD_DIM:(kv + 1) * HEAD_DIM, :])

    def project(tile, slot):
        r0 = tile * Q_TILE
        x = x_ref[r0:r0 + Q_TILE, :]
        hb = (_rms(x) * norm_gain + sh1).astype(BF16)
        raw_ref[slot][...] = _dot(hb, w_ref[...])

    def epilogue(tile, slot):
        s_idx, pos0 = tile // tiles_per_seq, (tile % tiles_per_seq) * Q_TILE
        r0 = tile * Q_TILE
        raw = raw_ref[slot]
        for j in range(ATTN_W // LANES):
            t = head_norm(raw[:, j * LANES:(j + 1) * LANES], qg_ref[...])
            if rope:
                t = rotary(t, pos0)
            t = t * (HEAD_DIM ** -0.5 * LOG2_E)
            swapped = pltpu.roll(t, HEAD_DIM, 1)
            if (2 * j) // Q_PER_KV == 0:
                even = jnp.where(first_head, t, 0.0)
                odd = jnp.where(first_head, swapped, 0.0)
            else:
                even = jnp.where(first_head, 0.0, swapped)
                odd = jnp.where(first_head, 0.0, t)
            qa_ref[tile, (2 * j) * Q_TILE:(2 * j + 1) * Q_TILE, :] = even.astype(BF16)
            qa_ref[tile, (2 * j + 1) * Q_TILE:(2 * j + 2) * Q_TILE, :] = odd.astype(BF16)
        kn = head_norm(raw[:, ATTN_W:ATTN_W + KV_W], kg_ref[...])
        v = raw[:, ATTN_W + KV_W:ATTN_W + 2 * KV_W]
        vt_f32 = v.T
        if not n_cache:
            nk_out[s_idx, :, pos0:pos0 + Q_TILE] = kn.T
            nv_out[s_idx, :, pos0:pos0 + Q_TILE] = vt_f32
        if rope:
            kn = rotary(kn, pos0)
        k0 = n_cache + pos0
        k_ref[s_idx, k0:k0 + Q_TILE, :] = kn.astype(BF16)
        vt = vt_f32.astype(BF16)
        for kv in range(N_KV_HEADS):
            vt_ref[s_idx, kv * VT_ROWS:kv * VT_ROWS + HEAD_DIM, k0:k0 + Q_TILE] = (
                vt[kv * HEAD_DIM:(kv + 1) * HEAD_DIM, :])
        for g in range(len(POOL_WINDOWS)):
            c0 = ATTN_W + 2 * KV_W + g * POOL_GW
            p_ref[s_idx, g, POOL_PAD + pos0:POOL_PAD + pos0 + Q_TILE, :] = raw[:, c0:c0 + POOL_GW]

    project(0, 0)
    for tile in range(n_tiles):
        if tile + 1 < n_tiles:
            project(tile + 1, (tile + 1) % 2)
        epilogue(tile, tile % 2)

    n_pairs = N_Q_HEADS // 2
    n_steps = n_tiles * n_pairs

    def scores(step, slot):
        tile, pair = step // n_pairs, step % n_pairs
        q2 = qa_ref[tile, pair * 2 * Q_TILE:(pair + 1) * 2 * Q_TILE, :]
        s_ref[slot][...] = _dot_nt(k_ref[tile // tiles_per_seq], q2)

    def exponent(step, slot):
        s = s_ref[slot][...]
        e_ref[slot][...] = jnp.exp2(s - jnp.max(s, axis=0, keepdims=True)).astype(BF16)

    def values(step, slot):
        tile, pair = step // n_pairs, step % n_pairs
        kv = (2 * pair) // Q_PER_KV
        vt = vt_ref[tile // tiles_per_seq, kv * VT_ROWS:(kv + 1) * VT_ROWS, :]
        ot = _dot(vt, e_ref[slot][...])
        ot = ot[0:HEAD_DIM] * (1.0 / ot[HEAD_DIM:HEAD_DIM + 1])
        both = jnp.concatenate([ot[:, 0:Q_TILE], ot[:, Q_TILE:2 * Q_TILE]], axis=0)
        ao_out[tile * Q_TILE:(tile + 1) * Q_TILE, pair * LANES:(pair + 1) * LANES] = both.T.astype(BF16)

    scores(0, 0)
    for step in range(n_steps + 1):
        if step + 1 < n_steps:
            scores(step + 1, (step + 1) % 2)
        if step < n_steps:
            exponent(step, step % 2)
        if step >= 1:
            values(step - 1, (step - 1) % 2)

    for s_idx in range(n_seq):
        r0 = s_idx * seq_len
        for g, window in enumerate(POOL_WINDOWS):
            cols = slice(g * POOL_GW, (g + 1) * POOL_GW)
            p2_ref = p2_a if window == 8 else p2_b
            d = _pool_group(p_ref.at[s_idx, g], window, seq_len, p2_ref, p4_b)
            yg = _dot(d.astype(BF16), wpool_ref[g]) * pscale_ref[:, cols]
            y_out[r0:r0 + seq_len, cols] = yg.astype(BF16)

    if n_cast:
        @pl.when(grid_step > 0)
        def _():
            for j in range(n_cast):
                copy_out(j, grid_step - 1).wait()

        for j in range(n_cast):
            copy_in(j, grid_step).wait()
            stage_out[j][...] = stage_in[j][...].astype(BF16)
            copy_out(j, grid_step).start()

        @pl.when(grid_step == last_grid_step)
        def _():
            for j in range(n_cast):
                copy_out(j, grid_step).wait()


def _mix_call(l, x, mod_rows, n_pre, w_in_l, q_gain, k_gain, cos_t, sin_t, w_pool, pool_scale,
              cache_k, cache_v, cast_weights_f32, cast_layer, *, seq_len, n_seq, rope):
    n_tok = x.shape[0]
    rows = n_seq * seq_len
    is_lat = cache_k is not None
    n_cache = cache_k.shape[2] if is_lat else 0
    n_keys = n_cache + seq_len
    if is_lat:
        assert n_seq == 1
        mod_idx = lambda i: (l * MOD_ROWS + i, 0, 0)
    else:
        mod_idx = lambda i: (l * MOD_ROWS + CTX_MOD_ROW, 0, 0)
    lay = lambda i: (l, 0, 0)
    row = lambda i: (i, 0)
    in_specs = [
        pl.BlockSpec((rows, D_MODEL), row),
        pl.BlockSpec((None, 1, N_MOD * D_MODEL), mod_idx),
        pl.BlockSpec((None, 1, D_MODEL), lay),
        pl.BlockSpec((D_MODEL, GATE_COL0), lambda i: (0, 0)),
        pl.BlockSpec((None, 1, LANES), lay),
        pl.BlockSpec((None, 1, LANES), lay),
        pl.BlockSpec((seq_len, LANES), lambda i: (0, 0)),
        pl.BlockSpec((seq_len, LANES), lambda i: (0, 0)),
        pl.BlockSpec((None, len(POOL_WINDOWS), POOL_GW, POOL_GW), lambda i: (l, 0, 0, 0)),
        pl.BlockSpec((None, 1, POOL_W), lay),
    ]
    args = [x, mod_rows, n_pre, w_in_l, q_gain, k_gain, cos_t, sin_t, w_pool, pool_scale]
    out_shape = [jax.ShapeDtypeStruct((n_tok, ATTN_W), BF16), jax.ShapeDtypeStruct((n_tok, POOL_W), BF16)]
    out_specs = [pl.BlockSpec((rows, ATTN_W), row), pl.BlockSpec((rows, POOL_W), row)]
    if is_lat:
        cache_spec = pl.BlockSpec((1, 1, n_cache, KV_W), lambda i: (i, l, 0, 0))
        in_specs += [cache_spec, cache_spec]
        args += [cache_k, cache_v]
    else:
        out_shape += [jax.ShapeDtypeStruct((n_tok // seq_len, KV_W, seq_len), F32)] * 2
        out_specs += [pl.BlockSpec((n_seq, KV_W, seq_len), lambda i: (i, 0, 0))] * 2
    n_cast = len(cast_weights_f32)
    n_steps = n_tok // rows
    any_spec = pl.BlockSpec(memory_space=pl.ANY)
    in_specs += [any_spec] * n_cast
    args += [w.reshape(-1, w.shape[2]) for w in cast_weights_f32]
    out_shape += [jax.ShapeDtypeStruct(w.shape[1:], BF16) for w in cast_weights_f32]
    out_specs += [any_spec] * n_cast
    slices = [(w.shape[1] // n_steps, w.shape[2]) for w in cast_weights_f32]
    assert all(w.shape[1] % n_steps == 0 and r % 16 == 0 for w, (r, _) in zip(cast_weights_f32, slices))
    scratch = [
        pltpu.VMEM((rows // Q_TILE, N_Q_HEADS * Q_TILE, LANES), BF16),
        pltpu.VMEM((n_seq, n_keys, KV_W), BF16),
        pltpu.VMEM((n_seq, N_KV_HEADS * VT_ROWS, n_keys), BF16),
        pltpu.VMEM((n_seq, len(POOL_WINDOWS), seq_len + 2 * POOL_PAD, POOL_GW), F32),
    ]
    scratch += [pltpu.VMEM((Q_TILE, GATE_COL0), F32)] * 2
    scratch += [pltpu.VMEM((n_keys, 2 * Q_TILE), F32)] * 2
    scratch += [pltpu.VMEM((n_keys, 2 * Q_TILE), BF16)] * 2
    scratch += [pltpu.VMEM((seq_len + 2 * POOL_PAD - 8, POOL_GW), F32)] * 2
    scratch += [pltpu.VMEM((seq_len + 2 * POOL_PAD - 16, POOL_GW), F32)]
    if n_cast:
        scratch += [pltpu.VMEM(s, F32) for s in slices] + [pltpu.VMEM(s, BF16) for s in slices]
        scratch += [pltpu.SemaphoreType.DMA((2 * n_cast,))]
    return pl.pallas_call(
        functools.partial(_mix_kernel, seq_len=seq_len, n_seq=n_seq, n_cache=n_cache, rope=rope,
                          n_cast=n_cast, cast_layer=cast_layer),
        grid=(n_tok // rows,),
        in_specs=in_specs,
        out_specs=out_specs,
        out_shape=out_shape,
        scratch_shapes=scratch,
        compiler_params=_params(),
        name="mix_lat" if is_lat else "mix_ctx",
    )(*args)


def _tail_kernel(x_ref, ao_ref, y_ref, mod_post_ref, mod_ffn_ref, npre_mix_ref, npost_mix_ref,
                 npre_ffn_ref, npost_ffn_ref, win_ref, wau_ref, wpu_ref, wout_ref, wg_hbm, wu_hbm,
                 wd_hbm, o_ref, x1_new, x1_cur, wg_ref, wu_ref, wd_ref, sems, *, sub):
    grid_step = pl.program_id(0)
    n_tiles = pl.num_programs(0) - 1
    n_sub = x_ref.shape[0] // sub

    def weight_copies():
        pairs = ((wg_hbm, wg_ref), (wu_hbm, wu_ref), (wd_hbm, wd_ref))
        return [pltpu.make_async_copy(src, dst, sems.at[j]) for j, (src, dst) in enumerate(pairs)]

    def post(i):
        mod = mod_post_ref[...]
        sh1, sc1, g1 = [mod[:, k * D_MODEL:(k + 1) * D_MODEL] for k in range(3)]
        r0 = i * sub
        x = x_ref[r0:r0 + sub, :]
        hb = ((_rms(x) * npre_mix_ref[...]) * (1.0 + sc1) + sh1).astype(BF16)
        g_attn = jax.nn.sigmoid(_dot(hb, win_ref[:, GATE_COL0:GATE_COL0 + D_MODEL]))
        g_pool = jax.nn.sigmoid(_dot(hb, win_ref[:, GATE_COL0 + D_MODEL:IN_COLS]))
        attn = _dot(ao_ref[r0:r0 + sub, :], wau_ref[...])
        pool = _dot(y_ref[r0:r0 + sub, :], wpu_ref[...])
        mixed = g_attn * attn + g_pool * pool
        m = _dot(mixed.astype(BF16), wout_ref[...])
        x1_new[r0:r0 + sub, :] = x + g1 * (_rms(m) * npost_mix_ref[...])

    def ffn(i):
        mod = mod_ffn_ref[...]
        sh2, sc2, g2 = [mod[:, k * D_MODEL:(k + 1) * D_MODEL] for k in range(3, N_MOD)]
        r0 = i * sub
        x1 = x1_cur[r0:r0 + sub, :]
        hb = ((_rms(x1) * npre_ffn_ref[...]) * (1.0 + sc2) + sh2).astype(BF16)
        g = _dot(hb, wg_ref[...])
        u = _dot(hb, wu_ref[...])
        a = ((g * jax.nn.sigmoid(g)) * u).astype(BF16)
        f = _dot(a, wd_ref[...])
        o_ref[r0:r0 + sub, :] = x1 + g2 * (_rms(f) * npost_ffn_ref[...])

    def carry():
        x1_cur[...] = x1_new[...]

    @pl.when(grid_step == 0)
    def _():
        for copy in weight_copies():
            copy.start()
        for i in range(n_sub):
            post(i)
        carry()

    @pl.when(grid_step == 1)
    def _():
        for copy in weight_copies():
            copy.wait()

    @pl.when(jnp.logical_and(grid_step > 0, grid_step < n_tiles))
    def _():
        for i in range(n_sub):
            ffn(i)
            post(i)
        carry()

    @pl.when(grid_step == n_tiles)
    def _():
        for i in range(n_sub):
            ffn(i)


def _tail_call(l, x, ao, y, mod_rows, n_pre_mix, n_post_mix, n_pre_ffn, n_post_ffn, w_in,
               w_attn_up, w_pool_up, w_out, w_ffn_gate, w_ffn_up, w_ffn_down,
               *, seq_len, per_seq_mod, tm, sub):
    n_tok = x.shape[0]
    n_tiles = n_tok // tm
    assert n_tiles >= 2
    post_tile = lambda s: jnp.minimum(s, n_tiles - 1)
    ffn_tile = lambda s: jnp.maximum(s - 1, 0)

    def mod_idx(tile_of_step):
        if per_seq_mod:
            return lambda s: (l * MOD_ROWS + tile_of_step(s) // (seq_len // tm), 0, 0)
        return lambda s: (l * MOD_ROWS + CTX_MOD_ROW, 0, 0)

    lay = lambda s: (l, 0, 0)
    post_row = lambda s: (post_tile(s), 0)
    vec_spec = pl.BlockSpec((None, 1, D_MODEL), lay)
    mod_spec = lambda tile_of_step: pl.BlockSpec((None, 1, N_MOD * D_MODEL), mod_idx(tile_of_step))
    layer_resident = lambda r, c: pl.BlockSpec((r, c), lambda s: (0, 0), pipeline_mode=pl.Buffered(1))
    any_spec = pl.BlockSpec(memory_space=pl.ANY)
    return pl.pallas_call(
        functools.partial(_tail_kernel, sub=sub),
        grid=(n_tiles + 1,),
        in_specs=[
            pl.BlockSpec((tm, D_MODEL), post_row),
            pl.BlockSpec((tm, ATTN_W), post_row),
            pl.BlockSpec((tm, POOL_W), post_row),
            mod_spec(post_tile), mod_spec(ffn_tile),
            vec_spec, vec_spec, vec_spec, vec_spec,
            layer_resident(D_MODEL, IN_COLS),
            layer_resident(ATTN_W, D_MODEL),
            layer_resident(POOL_W, D_MODEL),
            layer_resident(D_MODEL, D_MODEL),
            any_spec, any_spec, any_spec,
        ],
        out_specs=pl.BlockSpec((tm, D_MODEL), lambda s: (ffn_tile(s), 0)),
        out_shape=jax.ShapeDtypeStruct((n_tok, D_MODEL), F32),
        scratch_shapes=[
            pltpu.VMEM((tm, D_MODEL), F32), pltpu.VMEM((tm, D_MODEL), F32),
            pltpu.VMEM((D_MODEL, FFN_HIDDEN), BF16), pltpu.VMEM((D_MODEL, FFN_HIDDEN), BF16),
            pltpu.VMEM((FFN_HIDDEN, D_MODEL), BF16), pltpu.SemaphoreType.DMA((3,)),
        ],
        compiler_params=_params(),
        name="tail",
    )(x, ao, y, mod_rows, mod_rows, n_pre_mix, n_post_mix, n_pre_ffn, n_post_ffn, w_in, w_attn_up,
      w_pool_up, w_out, w_ffn_gate, w_ffn_up, w_ffn_down)


def _rope_tables(n):
    rows = n // GRID_W
    row = np.repeat(np.arange(rows), GRID_W).astype(np.float32)
    col = np.tile(np.arange(GRID_W), rows).astype(np.float32)
    n_freq = HEAD_DIM // 4
    inv = jnp.asarray(ROPE_THETA, F32) ** (-jnp.arange(n_freq, dtype=F32) / n_freq)
    ang = jnp.concatenate([jnp.asarray(row)[:, None] * inv[None, :],
                           jnp.asarray(col)[:, None] * inv[None, :]], axis=-1)
    cos, sin = jnp.cos(ang), jnp.sin(ang)
    cos_t = jnp.tile(cos, (1, LANES // (HEAD_DIM // 2)))
    sin_t = jnp.tile(jnp.concatenate([-sin, sin], axis=-1), (1, LANES // HEAD_DIM))
    return cos_t, sin_t


def kernel(x_prompt, x_sample, cache_k, cache_v, c, c_ctx, w_ada, b_ada, w_in, q_norm, k_norm,
           w_attn_up, w_pool, pool_scale, w_pool_up, w_out, n_pre_mix, n_post_mix, n_pre_ffn,
           n_post_ffn, w_ffn_gate, w_ffn_up, w_ffn_down):
    batch, seq, _ = x_prompt.shape
    dec_batch, dec_seq, _ = x_sample.shape
    past_len = cache_k.shape[2]
    assert dec_batch <= CTX_MOD_ROW

    c_rows = jnp.concatenate(
        [c, c_ctx[None, :], jnp.zeros((MOD_ROWS - dec_batch - 1, D_MODEL), F32)], axis=0)
    mod_rows = _ada_call(c_rows, w_ada, b_ada).reshape(DEPTH * MOD_ROWS, 1, N_MOD * D_MODEL)

    cos_t, sin_t = _rope_tables(dec_seq)
    vec = lambda a: a.reshape(DEPTH, 1, a.shape[-1])
    q_gain = vec(jnp.tile(q_norm, (1, LANES // HEAD_DIM)))
    k_gain = vec(jnp.tile(k_norm, (1, LANES // HEAD_DIM)))
    w_in_l = w_in[0].astype(BF16)
    w_pool_b = w_pool.astype(BF16)
    tail_weights_f32 = (w_attn_up, w_pool_up, w_out, w_ffn_gate, w_ffn_up, w_ffn_down)
    n_pre_mix_v, n_post_mix_v = vec(n_pre_mix), vec(n_post_mix)
    n_pre_ffn_v, n_post_ffn_v = vec(n_pre_ffn), vec(n_post_ffn)
    pool_scale_v = vec(pool_scale)
    ck = cache_k.reshape(dec_batch, DEPTH, past_len, KV_W)
    cv = cache_v.reshape(dec_batch, DEPTH, past_len, KV_W)

    tm = 512
    ctx_seqs_per_step = 4

    y = x_prompt.reshape(batch * seq, D_MODEL)
    z = x_sample.reshape(dec_batch * dec_seq, D_MODEL)
    new_ks, new_vs = [], []
    for l in range(DEPTH):
        outs = _mix_call(l, y, mod_rows, n_pre_mix_v, w_in_l, q_gain, k_gain, cos_t, sin_t, w_pool_b,
                         pool_scale_v, None, None, tail_weights_f32, l,
                         seq_len=seq, n_seq=ctx_seqs_per_step, rope=False)
        ao, yp, nk, nv = outs[:4]
        tail_weights_b = outs[4:]
        new_ks.append(nk)
        new_vs.append(nv)
        y = _tail_call(l, y, ao, yp, mod_rows, n_pre_mix_v, n_post_mix_v, n_pre_ffn_v, n_post_ffn_v,
                       w_in_l, *tail_weights_b, seq_len=seq, per_seq_mod=False, tm=tm, sub=256)
        has_next = l + 1 < DEPTH
        outs = _mix_call(l, z, mod_rows, n_pre_mix_v, w_in_l, q_gain, k_gain, cos_t, sin_t, w_pool_b,
                         pool_scale_v, ck, cv, (w_in,) if has_next else (), l + 1,
                         seq_len=dec_seq, n_seq=1, rope=True)
        ao, yp = outs[:2]
        z = _tail_call(l, z, ao, yp, mod_rows, n_pre_mix_v, n_post_mix_v, n_pre_ffn_v, n_post_ffn_v,
                       w_in_l, *tail_weights_b, seq_len=dec_seq, per_seq_mod=True, tm=tm, sub=256)
        if has_next:
            w_in_l = outs[2]

    def kv_layout(parts):
        a = jnp.stack(parts, axis=1).reshape(batch, DEPTH, N_KV_HEADS, HEAD_DIM, seq)
        return jnp.transpose(a, (0, 1, 4, 2, 3))

    return (y.reshape(batch, seq, D_MODEL), z.reshape(dec_batch, dec_seq, D_MODEL),
            kv_layout(new_ks), kv_layout(new_vs))
```

```python
import functools

import jax
import jax.numpy as jnp
import numpy as np
from jax import lax
from jax.experimental import pallas as pl
from jax.experimental.pallas import tpu as pltpu

D_MODEL = 1024
DEPTH = 4
GRID_W = 64
HEAD_DIM = 64
N_Q_HEADS = 8
N_KV_HEADS = 2
Q_PER_KV = N_Q_HEADS // N_KV_HEADS
ATTN_W = N_Q_HEADS * HEAD_DIM
KV_W = N_KV_HEADS * HEAD_DIM
POOL_WINDOWS = (2, 4, 8, 16)
POOL_W = D_MODEL // 2
POOL_GW = POOL_W // len(POOL_WINDOWS)
GATE_COL0 = ATTN_W + 2 * KV_W + POOL_W
IN_COLS = GATE_COL0 + 2 * D_MODEL
FFN_HIDDEN = 2816
N_MOD = 6
ROPE_THETA = 10000.0
EPS = 1e-6

LANES = 128
Q_TILE = 256
VT_ROWS = 80
POOL_PAD = 24
LOG2_E = 1.4426950408889634
MOD_ROWS = 8
CTX_MOD_ROW = 4
VMEM_LIMIT_BYTES = 56 * 1024 * 1024

F32 = jnp.float32
BF16 = jnp.bfloat16


def _dot(a, b):
    return jnp.dot(a, b, preferred_element_type=F32)


def _dot_nt(a, b):
    return lax.dot_general(a, b, (((1,), (1,)), ((), ())), preferred_element_type=F32)


def _rms(x):
    return x * lax.rsqrt(jnp.mean(x * x, axis=-1, keepdims=True) + EPS)


def _params(n_axes=1):
    return pltpu.CompilerParams(dimension_semantics=("arbitrary",) * n_axes,
                                vmem_limit_bytes=VMEM_LIMIT_BYTES)


def _ada_kernel(c_ref, w_ref, b_ref, o_ref):
    c = c_ref[...]
    s = (c * jax.nn.sigmoid(c)).astype(BF16)
    o_ref[...] = _dot(s, w_ref[...].astype(BF16)) + b_ref[...]


def _ada_call(c_rows, w_ada, b_ada):
    tn = 1536
    n_cols = N_MOD * D_MODEL
    return pl.pallas_call(
        _ada_kernel,
        grid=(DEPTH, n_cols // tn),
        in_specs=[
            pl.BlockSpec((MOD_ROWS, D_MODEL), lambda l, j: (0, 0)),
            pl.BlockSpec((None, D_MODEL, tn), lambda l, j: (l, 0, j)),
            pl.BlockSpec((None, 1, tn), lambda l, j: (l, 0, j)),
        ],
        out_specs=pl.BlockSpec((None, MOD_ROWS, tn), lambda l, j: (l, 0, j)),
        out_shape=jax.ShapeDtypeStruct((DEPTH, MOD_ROWS, n_cols), F32),
        compiler_params=_params(2),
        name="ada_rows",
    )(c_rows, w_ada, b_ada.reshape(DEPTH, 1, n_cols))


def _pool_group(src, window, seq_len, p2_ref, p4_ref):
    half = window // 2
    rows = lambda ref, off: ref[POOL_PAD + off:POOL_PAD + off + seq_len, :]
    if window <= 4:
        total = rows(src, -half)
        for off in range(-half + 1, half):
            total = total + rows(src, off)
    else:
        n2 = seq_len + 2 * POOL_PAD - 8
        p2_ref[0:n2, :] = src[0:n2, :] + src[1:n2 + 1, :]
        if window == 8:
            terms = [rows(p2_ref, off) for off in (-4, -2, 0, 2)]
        else:
            assert window == 16
            n4 = n2 - 8
            p4_ref[0:n4, :] = p2_ref[0:n4, :] + p2_ref[2:n4 + 2, :]
            terms = [rows(p4_ref, off) for off in (-8, -4, 0, 4)]
        total = (terms[0] + terms[1]) + (terms[2] + terms[3])
    t = lax.broadcasted_iota(jnp.int32, (seq_len, POOL_GW), 0)
    count = jnp.minimum(t + half, seq_len) - jnp.maximum(t - half, 0)
    return total / count.astype(F32) - rows(src, 0)


def _mix_kernel(x_ref, mod_ref, npre_ref, w_ref, qg_ref, kg_ref, cos_ref, sin_ref, wpool_ref,
                pscale_ref, *rest, seq_len, n_seq, n_cache, rope, n_cast, cast_layer):
    if n_cache:
        (ck_ref, cv_ref), rest = rest[:2], rest[2:]
    w32_refs, rest = rest[:n_cast], rest[n_cast:]
    if n_cache:
        (ao_out, y_out), rest = rest[:2], rest[2:]
    else:
        (ao_out, y_out, nk_out, nv_out), rest = rest[:4], rest[4:]
    wb_outs, rest = rest[:n_cast], rest[n_cast:]
    (qa_ref, k_ref, vt_ref, p_ref, raw_a, raw_b, s_a, s_b, e_a, e_b, p2_a, p2_b, p4_b, m_a, m_b), rest = (
        rest[:15], rest[15:])
    raw_ref, s_ref, e_ref, m_ref = (raw_a, raw_b), (s_a, s_b), (e_a, e_b), (m_a, m_b)

    if n_cast:
        stage_in, stage_out, sems = rest[0:n_cast], rest[n_cast:2 * n_cast], rest[2 * n_cast]
        grid_step = pl.program_id(0)
        last_grid_step = pl.num_programs(0) - 1

        def copy_in(j, at_step):
            n = stage_in[j].shape[0]
            row0 = cast_layer * (w32_refs[j].shape[0] // DEPTH) + at_step * n
            return pltpu.make_async_copy(w32_refs[j].at[pl.ds(row0, n), :], stage_in[j], sems.at[j])

        def copy_out(j, at_step):
            n = stage_out[j].shape[0]
            return pltpu.make_async_copy(stage_out[j], wb_outs[j].at[pl.ds(at_step * n, n), :],
                                         sems.at[n_cast + j])

        for j in range(n_cast):
            copy_in(j, grid_step).start()
    n_keys = n_cache + seq_len
    tiles_per_seq = seq_len // Q_TILE
    n_tiles = n_seq * tiles_per_seq

    mod = mod_ref[...]
    sh1 = mod[:, 0:D_MODEL]
    norm_gain = npre_ref[...] * (1.0 + mod[:, D_MODEL:2 * D_MODEL])

    lane = lax.broadcasted_iota(jnp.int32, (Q_TILE, LANES), 1)
    first_head = lane < HEAD_DIM
    second_half = (lane & (HEAD_DIM // 2)) != 0

    def head_norm(t, gain):
        t2 = t * t
        s_lo = jnp.sum(jnp.where(first_head, t2, 0.0), axis=-1, keepdims=True)
        s_hi = jnp.sum(jnp.where(first_head, 0.0, t2), axis=-1, keepdims=True)
        ms = jnp.where(first_head, s_lo, s_hi) * (1.0 / HEAD_DIM)
        return (t * lax.rsqrt(ms + EPS)) * gain

    def rotary(t, pos0):
        partner = jnp.where(second_half, pltpu.roll(t, HEAD_DIM // 2, 1),
                            pltpu.roll(t, LANES - HEAD_DIM // 2, 1))
        return t * cos_ref[pos0:pos0 + Q_TILE, :] + partner * sin_ref[pos0:pos0 + Q_TILE, :]

    ones_rows = jnp.ones((VT_ROWS - HEAD_DIM, n_keys), BF16)
    pad_rows = jnp.zeros((POOL_PAD, POOL_GW), F32)
    for s_idx in range(n_seq):
        for g in range(len(POOL_WINDOWS)):
            p_ref[s_idx, g, 0:POOL_PAD, :] = pad_rows
            p_ref[s_idx, g, POOL_PAD + seq_len:2 * POOL_PAD + seq_len, :] = pad_rows
        for kv in range(N_KV_HEADS):
            vt_ref[s_idx, kv * VT_ROWS + HEAD_DIM:(kv + 1) * VT_ROWS, :] = ones_rows
        if n_cache:
            k_ref[s_idx, 0:n_cache, :] = ck_ref[0, 0].astype(BF16)
            cvt = cv_ref[0, 0].T.astype(BF16)
            for kv in range(N_KV_HEADS):
                vt_ref[s_idx, kv * VT_ROWS:kv * VT_ROWS + HEAD_DIM, 0:n_cache] = (
                    cvt[kv * HEAD_DIM:(kv + 1) * HEAD_DIM, :])

    def project(tile, slot):
        r0 = tile * Q_TILE
        x = x_ref[r0:r0 + Q_TILE, :]
        hb = (_rms(x) * norm_gain + sh1).astype(BF16)
        raw_ref[slot][...] = _dot(hb, w_ref[...])

    def epilogue(tile, slot):
        s_idx, pos0 = tile // tiles_per_seq, (tile % tiles_per_seq) * Q_TILE
        r0 = tile * Q_TILE
        raw = raw_ref[slot]
        for j in range(ATTN_W // LANES):
            t = head_norm(raw[:, j * LANES:(j + 1) * LANES], qg_ref[...])
            if rope:
                t = rotary(t, pos0)
            t = t * (HEAD_DIM ** -0.5 * LOG2_E)
            swapped = pltpu.roll(t, HEAD_DIM, 1)
            if (2 * j) // Q_PER_KV == 0:
                even = jnp.where(first_head, t, 0.0)
                odd = jnp.where(first_head, swapped, 0.0)
            else:
                even = jnp.where(first_head, 0.0, swapped)
                odd = jnp.where(first_head, 0.0, t)
            qa_ref[tile, (2 * j) * Q_TILE:(2 * j + 1) * Q_TILE, :] = even.astype(BF16)
            qa_ref[tile, (2 * j + 1) * Q_TILE:(2 * j + 2) * Q_TILE, :] = odd.astype(BF16)
        kn = head_norm(raw[:, ATTN_W:ATTN_W + KV_W], kg_ref[...])
        v = raw[:, ATTN_W + KV_W:ATTN_W + 2 * KV_W]
        vt_f32 = v.T
        if not n_cache:
            nk_out[s_idx, :, pos0:pos0 + Q_TILE] = kn.T
            nv_out[s_idx, :, pos0:pos0 + Q_TILE] = vt_f32
        if rope:
            kn = rotary(kn, pos0)
        k0 = n_cache + pos0
        k_ref[s_idx, k0:k0 + Q_TILE, :] = kn.astype(BF16)
        vt = vt_f32.astype(BF16)
        for kv in range(N_KV_HEADS):
            vt_ref[s_idx, kv * VT_ROWS:kv * VT_ROWS + HEAD_DIM, k0:k0 + Q_TILE] = (
                vt[kv * HEAD_DIM:(kv + 1) * HEAD_DIM, :])
        for g in range(len(POOL_WINDOWS)):
            c0 = ATTN_W + 2 * KV_W + g * POOL_GW
            p_ref[s_idx, g, POOL_PAD + pos0:POOL_PAD + pos0 + Q_TILE, :] = raw[:, c0:c0 + POOL_GW]

    project(0, 0)
    for tile in range(n_tiles):
        if tile + 1 < n_tiles:
            project(tile + 1, (tile + 1) % 2)
        epilogue(tile, tile % 2)

    n_pairs = N_Q_HEADS // 2
    n_steps = n_tiles * n_pairs

    def scores(step, slot):
        tile, pair = step // n_pairs, step % n_pairs
        q2 = qa_ref[tile, pair * 2 * Q_TILE:(pair + 1) * 2 * Q_TILE, :]
        s = _dot_nt(k_ref[tile // tiles_per_seq], q2)
        s_ref[slot][...] = s
        m_ref[slot][...] = jnp.max(s, axis=0, keepdims=True)

    def exponent(step, slot):
        e_ref[slot][...] = jnp.exp2(s_ref[slot][...] - m_ref[slot][...]).astype(BF16)

    def values(step, slot):
        tile, pair = step // n_pairs, step % n_pairs
        kv = (2 * pair) // Q_PER_KV
        vt = vt_ref[tile // tiles_per_seq, kv * VT_ROWS:(kv + 1) * VT_ROWS, :]
        ot = _dot(vt, e_ref[slot][...])
        ot = ot[0:HEAD_DIM] * (1.0 / ot[HEAD_DIM:HEAD_DIM + 1])
        both = jnp.concatenate([ot[:, 0:Q_TILE], ot[:, Q_TILE:2 * Q_TILE]], axis=0)
        ao_out[tile * Q_TILE:(tile + 1) * Q_TILE, pair * LANES:(pair + 1) * LANES] = both.T.astype(BF16)

    scores(0, 0)
    for step in range(n_steps + 1):
        if step + 1 < n_steps:
            scores(step + 1, (step + 1) % 2)
        if step < n_steps:
            exponent(step, step % 2)
        if step >= 1:
            values(step - 1, (step - 1) % 2)

    for s_idx in range(n_seq):
        r0 = s_idx * seq_len
        for g, window in enumerate(POOL_WINDOWS):
            cols = slice(g * POOL_GW, (g + 1) * POOL_GW)
            p2_ref = p2_a if window == 8 else p2_b
            d = _pool_group(p_ref.at[s_idx, g], window, seq_len, p2_ref, p4_b)
            yg = _dot(d.astype(BF16), wpool_ref[g]) * pscale_ref[:, cols]
            y_out[r0:r0 + seq_len, cols] = yg.astype(BF16)

    if n_cast:
        @pl.when(grid_step > 0)
        def _():
            for j in range(n_cast):
                copy_out(j, grid_step - 1).wait()

        for j in range(n_cast):
            copy_in(j, grid_step).wait()
            stage_out[j][...] = stage_in[j][...].astype(BF16)
            copy_out(j, grid_step).start()

        @pl.when(grid_step == last_grid_step)
        def _():
            for j in range(n_cast):
                copy_out(j, grid_step).wait()


def _mix_call(l, x, mod_rows, n_pre, w_in_l, q_gain, k_gain, cos_t, sin_t, w_pool, pool_scale,
              cache_k, cache_v, cast_weights_f32, cast_layer, *, seq_len, n_seq, rope):
    n_tok = x.shape[0]
    rows = n_seq * seq_len
    is_lat = cache_k is not None
    n_cache = cache_k.shape[2] if is_lat else 0
    n_keys = n_cache + seq_len
    if is_lat:
        assert n_seq == 1
        mod_idx = lambda i: (l * MOD_ROWS + i, 0, 0)
    else:
        mod_idx = lambda i: (l * MOD_ROWS + CTX_MOD_ROW, 0, 0)
    lay = lambda i: (l, 0, 0)
    row = lambda i: (i, 0)
    in_specs = [
        pl.BlockSpec((rows, D_MODEL), row),
        pl.BlockSpec((None, 1, N_MOD * D_MODEL), mod_idx),
        pl.BlockSpec((None, 1, D_MODEL), lay),
        pl.BlockSpec((D_MODEL, GATE_COL0), lambda i: (0, 0)),
        pl.BlockSpec((None, 1, LANES), lay),
        pl.BlockSpec((None, 1, LANES), lay),
        pl.BlockSpec((seq_len, LANES), lambda i: (0, 0)),
        pl.BlockSpec((seq_len, LANES), lambda i: (0, 0)),
        pl.BlockSpec((None, len(POOL_WINDOWS), POOL_GW, POOL_GW), lambda i: (l, 0, 0, 0)),
        pl.BlockSpec((None, 1, POOL_W), lay),
    ]
    args = [x, mod_rows, n_pre, w_in_l, q_gain, k_gain, cos_t, sin_t, w_pool, pool_scale]
    out_shape = [jax.ShapeDtypeStruct((n_tok, ATTN_W), BF16), jax.ShapeDtypeStruct((n_tok, POOL_W), BF16)]
    out_specs = [pl.BlockSpec((rows, ATTN_W), row), pl.BlockSpec((rows, POOL_W), row)]
    if is_lat:
        cache_spec = pl.BlockSpec((1, 1, n_cache, KV_W), lambda i: (i, l, 0, 0))
        in_specs += [cache_spec, cache_spec]
        args += [cache_k, cache_v]
    else:
        out_shape += [jax.ShapeDtypeStruct((n_tok // seq_len, KV_W, seq_len), F32)] * 2
        out_specs += [pl.BlockSpec((n_seq, KV_W, seq_len), lambda i: (i, 0, 0))] * 2
    n_cast = len(cast_weights_f32)
    n_steps = n_tok // rows
    any_spec = pl.BlockSpec(memory_space=pl.ANY)
    in_specs += [any_spec] * n_cast
    args += [w.reshape(-1, w.shape[2]) for w in cast_weights_f32]
    out_shape += [jax.ShapeDtypeStruct(w.shape[1:], BF16) for w in cast_weights_f32]
    out_specs += [any_spec] * n_cast
    slices = [(w.shape[1] // n_steps, w.shape[2]) for w in cast_weights_f32]
    assert all(w.shape[1] % n_steps == 0 and r % 16 == 0 for w, (r, _) in zip(cast_weights_f32, slices))
    scratch = [
        pltpu.VMEM((rows // Q_TILE, N_Q_HEADS * Q_TILE, LANES), BF16),
        pltpu.VMEM((n_seq, n_keys, KV_W), BF16),
        pltpu.VMEM((n_seq, N_KV_HEADS * VT_ROWS, n_keys), BF16),
        pltpu.VMEM((n_seq, len(POOL_WINDOWS), seq_len + 2 * POOL_PAD, POOL_GW), F32),
    ]
    scratch += [pltpu.VMEM((Q_TILE, GATE_COL0), F32)] * 2
    scratch += [pltpu.VMEM((n_keys, 2 * Q_TILE), F32)] * 2
    scratch += [pltpu.VMEM((n_keys, 2 * Q_TILE), BF16)] * 2
    scratch += [pltpu.VMEM((seq_len + 2 * POOL_PAD - 8, POOL_GW), F32)] * 2
    scratch += [pltpu.VMEM((seq_len + 2 * POOL_PAD - 16, POOL_GW), F32)]
    scratch += [pltpu.VMEM((1, 2 * Q_TILE), F32)] * 2
    if n_cast:
        scratch += [pltpu.VMEM(s, F32) for s in slices] + [pltpu.VMEM(s, BF16) for s in slices]
        scratch += [pltpu.SemaphoreType.DMA((2 * n_cast,))]
    return pl.pallas_call(
        functools.partial(_mix_kernel, seq_len=seq_len, n_seq=n_seq, n_cache=n_cache, rope=rope,
                          n_cast=n_cast, cast_layer=cast_layer),
        grid=(n_tok // rows,),
        in_specs=in_specs,
        out_specs=out_specs,
        out_shape=out_shape,
        scratch_shapes=scratch,
        compiler_params=_params(),
        name="mix_lat" if is_lat else "mix_ctx",
    )(*args)


def _tail_kernel(x_ref, ao_ref, y_ref, mod_ref, npre_mix_ref, npost_mix_ref, npre_ffn_ref,
                 npost_ffn_ref, win_ref, wau_ref, wpu_ref, wout_ref, wg_ref, wu_ref, wd_ref, o_ref,
                 x1_a, x1_b, *, sub):
    mod = mod_ref[...]
    sh1, sc1, g1, sh2, sc2, g2 = [mod[:, i * D_MODEL:(i + 1) * D_MODEL] for i in range(N_MOD)]
    x1_ref = (x1_a, x1_b)
    n_sub = x_ref.shape[0] // sub

    def post(i, slot):
        r0 = i * sub
        x = x_ref[r0:r0 + sub, :]
        hb = ((_rms(x) * npre_mix_ref[...]) * (1.0 + sc1) + sh1).astype(BF16)
        g_attn = jax.nn.sigmoid(_dot(hb, win_ref[:, GATE_COL0:GATE_COL0 + D_MODEL]))
        g_pool = jax.nn.sigmoid(_dot(hb, win_ref[:, GATE_COL0 + D_MODEL:IN_COLS]))
        attn = _dot(ao_ref[r0:r0 + sub, :], wau_ref[...])
        pool = _dot(y_ref[r0:r0 + sub, :], wpu_ref[...])
        mixed = g_attn * attn + g_pool * pool
        m = _dot(mixed.astype(BF16), wout_ref[...])
        x1_ref[slot][...] = x + g1 * (_rms(m) * npost_mix_ref[...])

    def ffn(i, slot):
        r0 = i * sub
        x1 = x1_ref[slot][...]
        hb = ((_rms(x1) * npre_ffn_ref[...]) * (1.0 + sc2) + sh2).astype(BF16)
        g = _dot(hb, wg_ref[...])
        u = _dot(hb, wu_ref[...])
        a = ((g * jax.nn.sigmoid(g)) * u).astype(BF16)
        f = _dot(a, wd_ref[...])
        o_ref[r0:r0 + sub, :] = x1 + g2 * (_rms(f) * npost_ffn_ref[...])

    post(0, 0)
    for i in range(n_sub):
        if i + 1 < n_sub:
            post(i + 1, (i + 1) % 2)
        ffn(i, i % 2)


def _tail_call(l, x, ao, y, mod_rows, n_pre_mix, n_post_mix, n_pre_ffn, n_post_ffn, w_in,
               w_attn_up, w_pool_up, w_out, w_ffn_gate, w_ffn_up, w_ffn_down,
               *, seq_len, per_seq_mod, tm, sub):
    n_tok = x.shape[0]
    if per_seq_mod:
        mod_idx = lambda i: (l * MOD_ROWS + i // (seq_len // tm), 0, 0)
    else:
        mod_idx = lambda i: (l * MOD_ROWS + CTX_MOD_ROW, 0, 0)
    lay = lambda i: (l, 0, 0)
    row = lambda i: (i, 0)
    vec_spec = pl.BlockSpec((None, 1, D_MODEL), lay)
    layer_resident = lambda r, c: pl.BlockSpec((r, c), lambda i: (0, 0), pipeline_mode=pl.Buffered(1))
    return pl.pallas_call(
        functools.partial(_tail_kernel, sub=sub),
        grid=(n_tok // tm,),
        in_specs=[
            pl.BlockSpec((tm, D_MODEL), row),
            pl.BlockSpec((tm, ATTN_W), row),
            pl.BlockSpec((tm, POOL_W), row),
            pl.BlockSpec((None, 1, N_MOD * D_MODEL), mod_idx),
            vec_spec, vec_spec, vec_spec, vec_spec,
            layer_resident(D_MODEL, IN_COLS),
            layer_resident(ATTN_W, D_MODEL),
            layer_resident(POOL_W, D_MODEL),
            layer_resident(D_MODEL, D_MODEL),
            layer_resident(D_MODEL, FFN_HIDDEN),
            layer_resident(D_MODEL, FFN_HIDDEN),
            layer_resident(FFN_HIDDEN, D_MODEL),
        ],
        out_specs=pl.BlockSpec((tm, D_MODEL), row),
        out_shape=jax.ShapeDtypeStruct((n_tok, D_MODEL), F32),
        scratch_shapes=[pltpu.VMEM((sub, D_MODEL), F32)] * 2,
        compiler_params=_params(),
        name="tail",
    )(x, ao, y, mod_rows, n_pre_mix, n_post_mix, n_pre_ffn, n_post_ffn, w_in, w_attn_up,
      w_pool_up, w_out, w_ffn_gate, w_ffn_up, w_ffn_down)


def _rope_tables(n):
    rows = n // GRID_W
    row = np.repeat(np.arange(rows), GRID_W).astype(np.float32)
    col = np.tile(np.arange(GRID_W), rows).astype(np.float32)
    n_freq = HEAD_DIM // 4
    inv = jnp.asarray(ROPE_THETA, F32) ** (-jnp.arange(n_freq, dtype=F32) / n_freq)
    ang = jnp.concatenate([jnp.asarray(row)[:, None] * inv[None, :],
                           jnp.asarray(col)[:, None] * inv[None, :]], axis=-1)
    cos, sin = jnp.cos(ang), jnp.sin(ang)
    cos_t = jnp.tile(cos, (1, LANES // (HEAD_DIM // 2)))
    sin_t = jnp.tile(jnp.concatenate([-sin, sin], axis=-1), (1, LANES // HEAD_DIM))
    return cos_t, sin_t


def kernel(x_prompt, x_sample, cache_k, cache_v, c, c_ctx, w_ada, b_ada, w_in, q_norm, k_norm,
           w_attn_up, w_pool, pool_scale, w_pool_up, w_out, n_pre_mix, n_post_mix, n_pre_ffn,
           n_post_ffn, w_ffn_gate, w_ffn_up, w_ffn_down):
    batch, seq, _ = x_prompt.shape
    dec_batch, dec_seq, _ = x_sample.shape
    past_len = cache_k.shape[2]
    assert dec_batch <= CTX_MOD_ROW

    c_rows = jnp.concatenate(
        [c, c_ctx[None, :], jnp.zeros((MOD_ROWS - dec_batch - 1, D_MODEL), F32)], axis=0)
    mod_rows = _ada_call(c_rows, w_ada, b_ada).reshape(DEPTH * MOD_ROWS, 1, N_MOD * D_MODEL)

    cos_t, sin_t = _rope_tables(dec_seq)
    vec = lambda a: a.reshape(DEPTH, 1, a.shape[-1])
    q_gain = vec(jnp.tile(q_norm, (1, LANES // HEAD_DIM)))
    k_gain = vec(jnp.tile(k_norm, (1, LANES // HEAD_DIM)))
    w_in_l = w_in[0].astype(BF16)
    w_pool_b = w_pool.astype(BF16)
    tail_weights_f32 = (w_attn_up, w_pool_up, w_out, w_ffn_gate, w_ffn_up, w_ffn_down)
    n_pre_mix_v, n_post_mix_v = vec(n_pre_mix), vec(n_post_mix)
    n_pre_ffn_v, n_post_ffn_v = vec(n_pre_ffn), vec(n_post_ffn)
    pool_scale_v = vec(pool_scale)
    ck = cache_k.reshape(dec_batch, DEPTH, past_len, KV_W)
    cv = cache_v.reshape(dec_batch, DEPTH, past_len, KV_W)

    tm = 512
    ctx_seqs_per_step = 4

    y = x_prompt.reshape(batch * seq, D_MODEL)
    z = x_sample.reshape(dec_batch * dec_seq, D_MODEL)
    new_ks, new_vs = [], []
    for l in range(DEPTH):
        outs = _mix_call(l, y, mod_rows, n_pre_mix_v, w_in_l, q_gain, k_gain, cos_t, sin_t, w_pool_b,
                         pool_scale_v, None, None, tail_weights_f32, l,
                         seq_len=seq, n_seq=ctx_seqs_per_step, rope=False)
        ao, yp, nk, nv = outs[:4]
        tail_weights_b = outs[4:]
        new_ks.append(nk)
        new_vs.append(nv)
        y = _tail_call(l, y, ao, yp, mod_rows, n_pre_mix_v, n_post_mix_v, n_pre_ffn_v, n_post_ffn_v,
                       w_in_l, *tail_weights_b, seq_len=seq, per_seq_mod=False, tm=tm, sub=256)
        has_next = l + 1 < DEPTH
        outs = _mix_call(l, z, mod_rows, n_pre_mix_v, w_in_l, q_gain, k_gain, cos_t, sin_t, w_pool_b,
                         pool_scale_v, ck, cv, (w_in,) if has_next else (), l + 1,
                         seq_len=dec_seq, n_seq=1, rope=True)
        ao, yp = outs[:2]
        z = _tail_call(l, z, ao, yp, mod_rows, n_pre_mix_v, n_post_mix_v, n_pre_ffn_v, n_post_ffn_v,
                       w_in_l, *tail_weights_b, seq_len=dec_seq, per_seq_mod=True, tm=tm, sub=256)
        if has_next:
            w_in_l = outs[2]

    def kv_layout(parts):
        a = jnp.stack(parts, axis=1).reshape(batch, DEPTH, N_KV_HEADS, HEAD_DIM, seq)
        return jnp.transpose(a, (0, 1, 4, 2, 3))

    return (y.reshape(batch, seq, D_MODEL), z.reshape(dec_batch, dec_seq, D_MODEL),
            kv_layout(new_ks), kv_layout(new_vs))
```

```python
import functools

import jax
import jax.numpy as jnp
import numpy as np
from jax import lax
from jax.experimental import pallas as pl
from jax.experimental.pallas import tpu as pltpu

D_MODEL = 1024
DEPTH = 4
GRID_W = 64
HEAD_DIM = 64
N_Q_HEADS = 8
N_KV_HEADS = 2
Q_PER_KV = N_Q_HEADS // N_KV_HEADS
ATTN_W = N_Q_HEADS * HEAD_DIM
KV_W = N_KV_HEADS * HEAD_DIM
POOL_WINDOWS = (2, 4, 8, 16)
POOL_W = D_MODEL // 2
POOL_GW = POOL_W // len(POOL_WINDOWS)
GATE_COL0 = ATTN_W + 2 * KV_W + POOL_W
IN_COLS = GATE_COL0 + 2 * D_MODEL
FFN_HIDDEN = 2816
N_MOD = 6
ROPE_THETA = 10000.0
EPS = 1e-6

LANES = 128
Q_TILE = 256
VT_ROWS = 80
POOL_PAD = 24
LOG2_E = 1.4426950408889634
MOD_ROWS = 8
CTX_MOD_ROW = 4
VMEM_LIMIT_BYTES = 56 * 1024 * 1024

F32 = jnp.float32
BF16 = jnp.bfloat16


def _dot(a, b):
    return jnp.dot(a, b, preferred_element_type=F32)


def _dot_nt(a, b):
    return lax.dot_general(a, b, (((1,), (1,)), ((), ())), preferred_element_type=F32)


def _rms(x):
    return x * lax.rsqrt(jnp.mean(x * x, axis=-1, keepdims=True) + EPS)


def _params(n_axes=1):
    return pltpu.CompilerParams(dimension_semantics=("arbitrary",) * n_axes,
                                vmem_limit_bytes=VMEM_LIMIT_BYTES)


def _ada_kernel(c_ref, w_ref, b_ref, o_ref):
    c = c_ref[...]
    s = (c * jax.nn.sigmoid(c)).astype(BF16)
    o_ref[...] = _dot(s, w_ref[...].astype(BF16)) + b_ref[...]


def _ada_call(c_rows, w_ada, b_ada):
    tn = 1536
    n_cols = N_MOD * D_MODEL
    return pl.pallas_call(
        _ada_kernel,
        grid=(DEPTH, n_cols // tn),
        in_specs=[
            pl.BlockSpec((MOD_ROWS, D_MODEL), lambda l, j: (0, 0)),
            pl.BlockSpec((None, D_MODEL, tn), lambda l, j: (l, 0, j)),
            pl.BlockSpec((None, 1, tn), lambda l, j: (l, 0, j)),
        ],
        out_specs=pl.BlockSpec((None, MOD_ROWS, tn), lambda l, j: (l, 0, j)),
        out_shape=jax.ShapeDtypeStruct((DEPTH, MOD_ROWS, n_cols), F32),
        compiler_params=_params(2),
        name="ada_rows",
    )(c_rows, w_ada, b_ada.reshape(DEPTH, 1, n_cols))


def _pool_group(src, window, seq_len, p2_ref, p4_ref):
    half = window // 2
    rows = lambda ref, off: ref[POOL_PAD + off:POOL_PAD + off + seq_len, :]
    if window <= 4:
        total = rows(src, -half)
        for off in range(-half + 1, half):
            total = total + rows(src, off)
    else:
        n2 = seq_len + 2 * POOL_PAD - 8
        p2_ref[0:n2, :] = src[0:n2, :] + src[1:n2 + 1, :]
        if window == 8:
            terms = [rows(p2_ref, off) for off in (-4, -2, 0, 2)]
        else:
            assert window == 16
            n4 = n2 - 8
            p4_ref[0:n4, :] = p2_ref[0:n4, :] + p2_ref[2:n4 + 2, :]
            terms = [rows(p4_ref, off) for off in (-8, -4, 0, 4)]
        total = (terms[0] + terms[1]) + (terms[2] + terms[3])
    t = lax.broadcasted_iota(jnp.int32, (seq_len, POOL_GW), 0)
    count = jnp.minimum(t + half, seq_len) - jnp.maximum(t - half, 0)
    return total / count.astype(F32) - rows(src, 0)


def _mix_kernel(x_ref, mod_ref, npre_ref, w_ref, qg_ref, kg_ref, cos_ref, sin_ref, wpool_ref,
                pscale_ref, *rest, seq_len, n_seq, n_cache, rope, n_cast, cast_layer):
    if n_cache:
        (ck_ref, cv_ref), rest = rest[:2], rest[2:]
    w32_refs, rest = rest[:n_cast], rest[n_cast:]
    if n_cache:
        (ao_out, y_out), rest = rest[:2], rest[2:]
    else:
        (ao_out, y_out, nk_out, nv_out), rest = rest[:4], rest[4:]
    wb_outs, rest = rest[:n_cast], rest[n_cast:]
    (qa_ref, k_ref, vt_ref, p_ref, raw_a, raw_b, s_a, s_b, e_a, e_b, p2_a, p2_b, p4_b), rest = (
        rest[:13], rest[13:])
    raw_ref, s_ref, e_ref = (raw_a, raw_b), (s_a, s_b), (e_a, e_b)

    if n_cast:
        stage_in, stage_out, sems = rest[0:n_cast], rest[n_cast:2 * n_cast], rest[2 * n_cast]
        grid_step = pl.program_id(0)
        last_grid_step = pl.num_programs(0) - 1

        def copy_in(j, at_step):
            n = stage_in[j].shape[0]
            row0 = cast_layer * (w32_refs[j].shape[0] // DEPTH) + at_step * n
            return pltpu.make_async_copy(w32_refs[j].at[pl.ds(row0, n), :], stage_in[j], sems.at[j])

        def copy_out(j, at_step):
            n = stage_out[j].shape[0]
            return pltpu.make_async_copy(stage_out[j], wb_outs[j].at[pl.ds(at_step * n, n), :],
                                         sems.at[n_cast + j])

        for j in range(n_cast):
            copy_in(j, grid_step).start()
    n_keys = n_cache + seq_len
    tiles_per_seq = seq_len // Q_TILE
    n_tiles = n_seq * tiles_per_seq

    mod = mod_ref[...]
    sh1 = mod[:, 0:D_MODEL]
    norm_gain = npre_ref[...] * (1.0 + mod[:, D_MODEL:2 * D_MODEL])

    lane = lax.broadcasted_iota(jnp.int32, (Q_TILE, LANES), 1)
    first_head = lane < HEAD_DIM
    second_half = (lane & (HEAD_DIM // 2)) != 0

    def head_norm(t, gain):
        t2 = t * t
        s_lo = jnp.sum(jnp.where(first_head, t2, 0.0), axis=-1, keepdims=True)
        s_hi = jnp.sum(jnp.where(first_head, 0.0, t2), axis=-1, keepdims=True)
        ms = jnp.where(first_head, s_lo, s_hi) * (1.0 / HEAD_DIM)
        return (t * lax.rsqrt(ms + EPS)) * gain

    def rotary(t, pos0):
        partner = jnp.where(second_half, pltpu.roll(t, HEAD_DIM // 2, 1),
                            pltpu.roll(t, LANES - HEAD_DIM // 2, 1))
        return t * cos_ref[pos0:pos0 + Q_TILE, :] + partner * sin_ref[pos0:pos0 + Q_TILE, :]

    ones_rows = jnp.ones((VT_ROWS - HEAD_DIM, n_keys), BF16)
    pad_rows = jnp.zeros((POOL_PAD, POOL_GW), F32)
    for s_idx in range(n_seq):
        for g in range(len(POOL_WINDOWS)):
            p_ref[s_idx, g, 0:POOL_PAD, :] = pad_rows
            p_ref[s_idx, g, POOL_PAD + seq_len:2 * POOL_PAD + seq_len, :] = pad_rows
        for kv in range(N_KV_HEADS):
            vt_ref[s_idx, kv * VT_ROWS + HEAD_DIM:(kv + 1) * VT_ROWS, :] = ones_rows
        if n_cache:
            k_ref[s_idx, 0:n_cache, :] = ck_ref[0, 0].astype(BF16)
            cvt = cv_ref[0, 0].T.astype(BF16)
            for kv in range(N_KV_HEADS):
                vt_ref[s_idx, kv * VT_ROWS:kv * VT_ROWS + HEAD_DIM, 0:n_cache] = (
                    cvt[kv * HEAD_DIM:(kv + 1) * HEAD_DIM, :])

    def project(tile, slot):
        r0 = tile * Q_TILE
        x = x_ref[r0:r0 + Q_TILE, :]
        hb = (_rms(x) * norm_gain + sh1).astype(BF16)
        raw_ref[slot][...] = _dot(hb, w_ref[...])

    def epilogue(tile, slot):
        s_idx, pos0 = tile // tiles_per_seq, (tile % tiles_per_seq) * Q_TILE
        r0 = tile * Q_TILE
        raw = raw_ref[slot]
        for j in range(ATTN_W // LANES):
            t = head_norm(raw[:, j * LANES:(j + 1) * LANES], qg_ref[...])
            if rope:
                t = rotary(t, pos0)
            t = t * (HEAD_DIM ** -0.5 * LOG2_E)
            swapped = pltpu.roll(t, HEAD_DIM, 1)
            if (2 * j) // Q_PER_KV == 0:
                even = jnp.where(first_head, t, 0.0)
                odd = jnp.where(first_head, swapped, 0.0)
            else:
                even = jnp.where(first_head, 0.0, swapped)
                odd = jnp.where(first_head, 0.0, t)
            qa_ref[tile, (2 * j) * Q_TILE:(2 * j + 1) * Q_TILE, :] = even.astype(BF16)
            qa_ref[tile, (2 * j + 1) * Q_TILE:(2 * j + 2) * Q_TILE, :] = odd.astype(BF16)
        kn = head_norm(raw[:, ATTN_W:ATTN_W + KV_W], kg_ref[...])
        v = raw[:, ATTN_W + KV_W:ATTN_W + 2 * KV_W]
        vt_f32 = v.T
        if not n_cache:
            nk_out[s_idx, :, pos0:pos0 + Q_TILE] = kn.T
            nv_out[s_idx, :, pos0:pos0 + Q_TILE] = vt_f32
        if rope:
            kn = rotary(kn, pos0)
        k0 = n_cache + pos0
        k_ref[s_idx, k0:k0 + Q_TILE, :] = kn.astype(BF16)
        vt = vt_f32.astype(BF16)
        for kv in range(N_KV_HEADS):
            vt_ref[s_idx, kv * VT_ROWS:kv * VT_ROWS + HEAD_DIM, k0:k0 + Q_TILE] = (
                vt[kv * HEAD_DIM:(kv + 1) * HEAD_DIM, :])
        for g in range(len(POOL_WINDOWS)):
            c0 = ATTN_W + 2 * KV_W + g * POOL_GW
            p_ref[s_idx, g, POOL_PAD + pos0:POOL_PAD + pos0 + Q_TILE, :] = raw[:, c0:c0 + POOL_GW]

    project(0, 0)
    for tile in range(n_tiles):
        if tile + 1 < n_tiles:
            project(tile + 1, (tile + 1) % 2)
        epilogue(tile, tile % 2)

    n_pairs = N_Q_HEADS // 2
    n_steps = n_tiles * n_pairs

    def scores(step, slot):
        tile, pair = step // n_pairs, step % n_pairs
        q2 = qa_ref[tile, pair * 2 * Q_TILE:(pair + 1) * 2 * Q_TILE, :]
        s_ref[slot][...] = _dot_nt(k_ref[tile // tiles_per_seq], q2)

    def exponent(step, slot):
        s = s_ref[slot][...]
        e_ref[slot][...] = jnp.exp2(s - jnp.max(s, axis=0, keepdims=True)).astype(BF16)

    def values(step, slot):
        tile, pair = step // n_pairs, step % n_pairs
        kv = (2 * pair) // Q_PER_KV
        vt = vt_ref[tile // tiles_per_seq, kv * VT_ROWS:(kv + 1) * VT_ROWS, :]
        ot = _dot(vt, e_ref[slot][...])
        ot = ot[0:HEAD_DIM] * (1.0 / ot[HEAD_DIM:HEAD_DIM + 1])
        both = jnp.concatenate([ot[:, 0:Q_TILE], ot[:, Q_TILE:2 * Q_TILE]], axis=0)
        ao_out[tile * Q_TILE:(tile + 1) * Q_TILE, pair * LANES:(pair + 1) * LANES] = both.T.astype(BF16)

    scores(0, 0)
    for step in range(n_steps + 1):
        if step + 1 < n_steps:
            scores(step + 1, (step + 1) % 2)
        if step < n_steps:
            exponent(step, step % 2)
        if step >= 1:
            values(step - 1, (step - 1) % 2)

    for s_idx in range(n_seq):
        r0 = s_idx * seq_len
        for g, window in enumerate(POOL_WINDOWS):
            cols = slice(g * POOL_GW, (g + 1) * POOL_GW)
            p2_ref = p2_a if window == 8 else p2_b
            d = _pool_group(p_ref.at[s_idx, g], window, seq_len, p2_ref, p4_b)
            yg = _dot(d.astype(BF16), wpool_ref[g]) * pscale_ref[:, cols]
            y_out[r0:r0 + seq_len, cols] = yg.astype(BF16)

    if n_cast:
        @pl.when(grid_step > 0)
        def _():
            for j in range(n_cast):
                copy_out(j, grid_step - 1).wait()

        for j in range(n_cast):
            copy_in(j, grid_step).wait()
            stage_out[j][...] = stage_in[j][...].astype(BF16)
            copy_out(j, grid_step).start()

        @pl.when(grid_step == last_grid_step)
        def _():
            for j in range(n_cast):
                copy_out(j, grid_step).wait()


def _mix_call(l, x, mod_rows, n_pre, w_in_l, q_gain, k_gain, cos_t, sin_t, w_pool, pool_scale,
              cache_k, cache_v, cast_weights_f32, cast_layer, *, seq_len, n_seq, rope):
    n_tok = x.shape[0]
    rows = n_seq * seq_len
    is_lat = cache_k is not None
    n_cache = cache_k.shape[2] if is_lat else 0
    n_keys = n_cache + seq_len
    if is_lat:
        assert n_seq == 1
        mod_idx = lambda i: (l * MOD_ROWS + i, 0, 0)
    else:
        mod_idx = lambda i: (l * MOD_ROWS + CTX_MOD_ROW, 0, 0)
    lay = lambda i: (l, 0, 0)
    row = lambda i: (i, 0)
    in_specs = [
        pl.BlockSpec((rows, D_MODEL), row),
        pl.BlockSpec((None, 1, N_MOD * D_MODEL), mod_idx),
        pl.BlockSpec((None, 1, D_MODEL), lay),
        pl.BlockSpec((D_MODEL, GATE_COL0), lambda i: (0, 0)),
        pl.BlockSpec((None, 1, LANES), lay),
        pl.BlockSpec((None, 1, LANES), lay),
        pl.BlockSpec((seq_len, LANES), lambda i: (0, 0)),
        pl.BlockSpec((seq_len, LANES), lambda i: (0, 0)),
        pl.BlockSpec((None, len(POOL_WINDOWS), POOL_GW, POOL_GW), lambda i: (l, 0, 0, 0)),
        pl.BlockSpec((None, 1, POOL_W), lay),
    ]
    args = [x, mod_rows, n_pre, w_in_l, q_gain, k_gain, cos_t, sin_t, w_pool, pool_scale]
    out_shape = [jax.ShapeDtypeStruct((n_tok, ATTN_W), BF16), jax.ShapeDtypeStruct((n_tok, POOL_W), BF16)]
    out_specs = [pl.BlockSpec((rows, ATTN_W), row), pl.BlockSpec((rows, POOL_W), row)]
    if is_lat:
        cache_spec = pl.BlockSpec((1, 1, n_cache, KV_W), lambda i: (i, l, 0, 0))
        in_specs += [cache_spec, cache_spec]
        args += [cache_k, cache_v]
    else:
        out_shape += [jax.ShapeDtypeStruct((n_tok // seq_len, KV_W, seq_len), F32)] * 2
        out_specs += [pl.BlockSpec((n_seq, KV_W, seq_len), lambda i: (i, 0, 0))] * 2
    n_cast = len(cast_weights_f32)
    n_steps = n_tok // rows
    any_spec = pl.BlockSpec(memory_space=pl.ANY)
    in_specs += [any_spec] * n_cast
    args += [w.reshape(-1, w.shape[2]) for w in cast_weights_f32]
    out_shape += [jax.ShapeDtypeStruct(w.shape[1:], BF16) for w in cast_weights_f32]
    out_specs += [any_spec] * n_cast
    slices = [(w.shape[1] // n_steps, w.shape[2]) for w in cast_weights_f32]
    assert all(w.shape[1] % n_steps == 0 and r % 16 == 0 for w, (r, _) in zip(cast_weights_f32, slices))
    scratch = [
        pltpu.VMEM((rows // Q_TILE, N_Q_HEADS * Q_TILE, LANES), BF16),
        pltpu.VMEM((n_seq, n_keys, KV_W), BF16),
        pltpu.VMEM((n_seq, N_KV_HEADS * VT_ROWS, n_keys), BF16),
        pltpu.VMEM((n_seq, len(POOL_WINDOWS), seq_len + 2 * POOL_PAD, POOL_GW), F32),
    ]
    scratch += [pltpu.VMEM((Q_TILE, GATE_COL0), F32)] * 2
    scratch += [pltpu.VMEM((n_keys, 2 * Q_TILE), F32)] * 2
    scratch += [pltpu.VMEM((n_keys, 2 * Q_TILE), BF16)] * 2
    scratch += [pltpu.VMEM((seq_len + 2 * POOL_PAD - 8, POOL_GW), F32)] * 2
    scratch += [pltpu.VMEM((seq_len + 2 * POOL_PAD - 16, POOL_GW), F32)]
    if n_cast:
        scratch += [pltpu.VMEM(s, F32) for s in slices] + [pltpu.VMEM(s, BF16) for s in slices]
        scratch += [pltpu.SemaphoreType.DMA((2 * n_cast,))]
    return pl.pallas_call(
        functools.partial(_mix_kernel, seq_len=seq_len, n_seq=n_seq, n_cache=n_cache, rope=rope,
                          n_cast=n_cast, cast_layer=cast_layer),
        grid=(n_tok // rows,),
        in_specs=in_specs,
        out_specs=out_specs,
        out_shape=out_shape,
        scratch_shapes=scratch,
        compiler_params=_params(),
        name="mix_lat" if is_lat else "mix_ctx",
    )(*args)


def _tail_kernel(x_ref, ao_ref, y_ref, mod_ref, npre_mix_ref, npost_mix_ref, npre_ffn_ref,
                 npost_ffn_ref, win_ref, wau_ref, wpu_ref, wout_ref, wg_ref, wu_ref, wd_ref, o_ref,
                 x1_a, x1_b, *, sub):
    mod = mod_ref[...]
    sh1, sc1, g1, sh2, sc2, g2 = [mod[:, i * D_MODEL:(i + 1) * D_MODEL] for i in range(N_MOD)]
    x1_ref = (x1_a, x1_b)
    n_sub = x_ref.shape[0] // sub

    def post(i, slot):
        r0 = i * sub
        x = x_ref[r0:r0 + sub, :]
        hb = ((_rms(x) * npre_mix_ref[...]) * (1.0 + sc1) + sh1).astype(BF16)
        g_attn = jax.nn.sigmoid(_dot(hb, win_ref[:, GATE_COL0:GATE_COL0 + D_MODEL]))
        g_pool = jax.nn.sigmoid(_dot(hb, win_ref[:, GATE_COL0 + D_MODEL:IN_COLS]))
        attn = _dot(ao_ref[r0:r0 + sub, :], wau_ref[...])
        pool = _dot(y_ref[r0:r0 + sub, :], wpu_ref[...])
        mixed = g_attn * attn + g_pool * pool
        m = _dot(mixed.astype(BF16), wout_ref[...])
        x1_ref[slot][...] = x + g1 * (_rms(m) * npost_mix_ref[...])

    def ffn(i, slot):
        r0 = i * sub
        x1 = x1_ref[slot][...]
        hb = ((_rms(x1) * npre_ffn_ref[...]) * (1.0 + sc2) + sh2).astype(BF16)
        g = _dot(hb, wg_ref[...])
        u = _dot(hb, wu_ref[...])
        a = ((g * jax.nn.sigmoid(g)) * u).astype(BF16)
        f = _dot(a, wd_ref[...])
        o_ref[r0:r0 + sub, :] = x1 + g2 * (_rms(f) * npost_ffn_ref[...])

    post(0, 0)
    for i in range(n_sub):
        if i + 1 < n_sub:
            post(i + 1, (i + 1) % 2)
        ffn(i, i % 2)


def _tail_call(l, x, ao, y, mod_rows, n_pre_mix, n_post_mix, n_pre_ffn, n_post_ffn, w_in,
               w_attn_up, w_pool_up, w_out, w_ffn_gate, w_ffn_up, w_ffn_down,
               *, seq_len, per_seq_mod, tm, sub):
    n_tok = x.shape[0]
    if per_seq_mod:
        mod_idx = lambda i: (l * MOD_ROWS + i // (seq_len // tm), 0, 0)
    else:
        mod_idx = lambda i: (l * MOD_ROWS + CTX_MOD_ROW, 0, 0)
    lay = lambda i: (l, 0, 0)
    row = lambda i: (i, 0)
    vec_spec = pl.BlockSpec((None, 1, D_MODEL), lay)
    layer_resident = lambda r, c: pl.BlockSpec((r, c), lambda i: (0, 0), pipeline_mode=pl.Buffered(1))
    return pl.pallas_call(
        functools.partial(_tail_kernel, sub=sub),
        grid=(n_tok // tm,),
        in_specs=[
            pl.BlockSpec((tm, D_MODEL), row),
            pl.BlockSpec((tm, ATTN_W), row),
            pl.BlockSpec((tm, POOL_W), row),
            pl.BlockSpec((None, 1, N_MOD * D_MODEL), mod_idx),
            vec_spec, vec_spec, vec_spec, vec_spec,
            layer_resident(D_MODEL, IN_COLS),
            layer_resident(ATTN_W, D_MODEL),
            layer_resident(POOL_W, D_MODEL),
            layer_resident(D_MODEL, D_MODEL),
            layer_resident(D_MODEL, FFN_HIDDEN),
            layer_resident(D_MODEL, FFN_HIDDEN),
            layer_resident(FFN_HIDDEN, D_MODEL),
        ],
        out_specs=pl.BlockSpec((tm, D_MODEL), row),
        out_shape=jax.ShapeDtypeStruct((n_tok, D_MODEL), F32),
        scratch_shapes=[pltpu.VMEM((sub, D_MODEL), F32)] * 2,
        compiler_params=_params(),
        name="tail",
    )(x, ao, y, mod_rows, n_pre_mix, n_post_mix, n_pre_ffn, n_post_ffn, w_in, w_attn_up,
      w_pool_up, w_out, w_ffn_gate, w_ffn_up, w_ffn_down)


def _rope_tables(n):
    rows = n // GRID_W
    row = np.repeat(np.arange(rows), GRID_W).astype(np.float32)
    col = np.tile(np.arange(GRID_W), rows).astype(np.float32)
    n_freq = HEAD_DIM // 4
    inv = jnp.asarray(ROPE_THETA, F32) ** (-jnp.arange(n_freq, dtype=F32) / n_freq)
    ang = jnp.concatenate([jnp.asarray(row)[:, None] * inv[None, :],
                           jnp.asarray(col)[:, None] * inv[None, :]], axis=-1)
    cos, sin = jnp.cos(ang), jnp.sin(ang)
    cos_t = jnp.tile(cos, (1, LANES // (HEAD_DIM // 2)))
    sin_t = jnp.tile(jnp.concatenate([-sin, sin], axis=-1), (1, LANES // HEAD_DIM))
    return cos_t, sin_t


def kernel(x_prompt, x_sample, cache_k, cache_v, c, c_ctx, w_ada, b_ada, w_in, q_norm, k_norm,
           w_attn_up, w_pool, pool_scale, w_pool_up, w_out, n_pre_mix, n_post_mix, n_pre_ffn,
           n_post_ffn, w_ffn_gate, w_ffn_up, w_ffn_down):
    batch, seq, _ = x_prompt.shape
    dec_batch, dec_seq, _ = x_sample.shape
    past_len = cache_k.shape[2]
    assert dec_batch <= CTX_MOD_ROW

    c_rows = jnp.concatenate(
        [c, c_ctx[None, :], jnp.zeros((MOD_ROWS - dec_batch - 1, D_MODEL), F32)], axis=0)
    mod_rows = _ada_call(c_rows, w_ada, b_ada).reshape(DEPTH * MOD_ROWS, 1, N_MOD * D_MODEL)

    cos_t, sin_t = _rope_tables(dec_seq)
    vec = lambda a: a.reshape(DEPTH, 1, a.shape[-1])
    q_gain = vec(jnp.tile(q_norm, (1, LANES // HEAD_DIM)))
    k_gain = vec(jnp.tile(k_norm, (1, LANES // HEAD_DIM)))
    w_in_l = w_in[0].astype(BF16)
    w_pool_b = w_pool.astype(BF16)
    tail_weights_f32 = (w_attn_up, w_pool_up, w_out, w_ffn_gate, w_ffn_up, w_ffn_down)
    n_pre_mix_v, n_post_mix_v = vec(n_pre_mix), vec(n_post_mix)
    n_pre_ffn_v, n_post_ffn_v = vec(n_pre_ffn), vec(n_post_ffn)
    pool_scale_v = vec(pool_scale)
    ck = cache_k.reshape(dec_batch, DEPTH, past_len, KV_W)
    cv = cache_v.reshape(dec_batch, DEPTH, past_len, KV_W)

    tm = 512
    ctx_seqs_per_step = 2

    y = x_prompt.reshape(batch * seq, D_MODEL)
    z = x_sample.reshape(dec_batch * dec_seq, D_MODEL)
    new_ks, new_vs = [], []
    for l in range(DEPTH):
        outs = _mix_call(l, y, mod_rows, n_pre_mix_v, w_in_l, q_gain, k_gain, cos_t, sin_t, w_pool_b,
                         pool_scale_v, None, None, tail_weights_f32, l,
                         seq_len=seq, n_seq=ctx_seqs_per_step, rope=False)
        ao, yp, nk, nv = outs[:4]
        tail_weights_b = outs[4:]
        new_ks.append(nk)
        new_vs.append(nv)
        y = _tail_call(l, y, ao, yp, mod_rows, n_pre_mix_v, n_post_mix_v, n_pre_ffn_v, n_post_ffn_v,
                       w_in_l, *tail_weights_b, seq_len=seq, per_seq_mod=False, tm=tm, sub=256)
        has_next = l + 1 < DEPTH
        outs = _mix_call(l, z, mod_rows, n_pre_mix_v, w_in_l, q_gain, k_gain, cos_t, sin_t, w_pool_b,
                         pool_scale_v, ck, cv, (w_in,) if has_next else (), l + 1,
                         seq_len=dec_seq, n_seq=1, rope=True)
        ao, yp = outs[:2]
        z = _tail_call(l, z, ao, yp, mod_rows, n_pre_mix_v, n_post_mix_v, n_pre_ffn_v, n_post_ffn_v,
                       w_in_l, *tail_weights_b, seq_len=dec_seq, per_seq_mod=True, tm=tm, sub=256)
        if has_next:
            w_in_l = outs[2]

    def kv_layout(parts):
        a = jnp.stack(parts, axis=1).reshape(batch, DEPTH, N_KV_HEADS, HEAD_DIM, seq)
        return jnp.transpose(a, (0, 1, 4, 2, 3))

    return (y.reshape(batch, seq, D_MODEL), z.reshape(dec_batch, dec_seq, D_MODEL),
            kv_layout(new_ks), kv_layout(new_vs))
```

```python
import functools

import jax
import jax.numpy as jnp
import numpy as np
from jax import lax
from jax.experimental import pallas as pl
from jax.experimental.pallas import tpu as pltpu

D_MODEL = 1024
DEPTH = 4
GRID_W = 64
HEAD_DIM = 64
N_Q_HEADS = 8
N_KV_HEADS = 2
Q_PER_KV = N_Q_HEADS // N_KV_HEADS
ATTN_W = N_Q_HEADS * HEAD_DIM
KV_W = N_KV_HEADS * HEAD_DIM
POOL_WINDOWS = (2, 4, 8, 16)
POOL_W = D_MODEL // 2
POOL_GW = POOL_W // len(POOL_WINDOWS)
GATE_COL0 = ATTN_W + 2 * KV_W + POOL_W
IN_COLS = GATE_COL0 + 2 * D_MODEL
FFN_HIDDEN = 2816
N_MOD = 6
ROPE_THETA = 10000.0
EPS = 1e-6

LANES = 128
Q_TILE = 256
VT_ROWS = 80
POOL_PAD = 24
LOG2_E = 1.4426950408889634
MOD_ROWS = 8
CTX_MOD_ROW = 4
VMEM_LIMIT_BYTES = 56 * 1024 * 1024

F32 = jnp.float32
BF16 = jnp.bfloat16


def _dot(a, b):
    return jnp.dot(a, b, preferred_element_type=F32)


def _dot_nt(a, b):
    return lax.dot_general(a, b, (((1,), (1,)), ((), ())), preferred_element_type=F32)


def _rms(x):
    return x * lax.rsqrt(jnp.mean(x * x, axis=-1, keepdims=True) + EPS)


def _params(n_axes=1):
    return pltpu.CompilerParams(dimension_semantics=("arbitrary",) * n_axes,
                                vmem_limit_bytes=VMEM_LIMIT_BYTES)


def _ada_rows(c_ref, w_ref, b_ref):
    c = c_ref[...]
    s = (c * jax.nn.sigmoid(c)).astype(BF16)
    return _dot(s, w_ref[...].astype(BF16)) + b_ref[...]


def _ada_kernel(c_ref, w_ref, b_ref, o_ref):
    o_ref[...] = _ada_rows(c_ref, w_ref, b_ref)


def _ada_call(c_rows, w_ada, b_ada3):
    tn = 1536
    n_cols = N_MOD * D_MODEL
    return pl.pallas_call(
        _ada_kernel,
        grid=(n_cols // tn,),
        in_specs=[
            pl.BlockSpec((MOD_ROWS, D_MODEL), lambda j: (0, 0)),
            pl.BlockSpec((None, D_MODEL, tn), lambda j: (0, 0, j)),
            pl.BlockSpec((None, 1, tn), lambda j: (0, 0, j)),
        ],
        out_specs=pl.BlockSpec((MOD_ROWS, tn), lambda j: (0, j)),
        out_shape=jax.ShapeDtypeStruct((MOD_ROWS, n_cols), F32),
        compiler_params=_params(),
        name="ada_rows",
    )(c_rows, w_ada, b_ada3)


def _pool_group(src, window, seq_len, p2_ref, p4_ref):
    half = window // 2
    rows = lambda ref, off: ref[POOL_PAD + off:POOL_PAD + off + seq_len, :]
    if window <= 4:
        total = rows(src, -half)
        for off in range(-half + 1, half):
            total = total + rows(src, off)
    else:
        n2 = seq_len + 2 * POOL_PAD - 8
        p2_ref[0:n2, :] = src[0:n2, :] + src[1:n2 + 1, :]
        if window == 8:
            terms = [rows(p2_ref, off) for off in (-4, -2, 0, 2)]
        else:
            assert window == 16
            n4 = n2 - 8
            p4_ref[0:n4, :] = p2_ref[0:n4, :] + p2_ref[2:n4 + 2, :]
            terms = [rows(p4_ref, off) for off in (-8, -4, 0, 4)]
        total = (terms[0] + terms[1]) + (terms[2] + terms[3])
    t = lax.broadcasted_iota(jnp.int32, (seq_len, POOL_GW), 0)
    count = jnp.minimum(t + half, seq_len) - jnp.maximum(t - half, 0)
    return total / count.astype(F32) - rows(src, 0)


def _mix_kernel(x_ref, mod_ref, npre_ref, w_ref, qg_ref, kg_ref, cos_ref, sin_ref, wpool_ref,
                pscale_ref, *rest, seq_len, n_seq, n_cache, rope, n_cast, cast_layer):
    if n_cache:
        (ck_ref, cv_ref), rest = rest[:2], rest[2:]
    w32_refs, rest = rest[:n_cast], rest[n_cast:]
    if n_cache:
        (ao_out, y_out), rest = rest[:2], rest[2:]
    else:
        (ao_out, y_out, nk_out, nv_out), rest = rest[:4], rest[4:]
    wb_outs, rest = rest[:n_cast], rest[n_cast:]
    (qa_ref, k_ref, vt_ref, p_ref, raw_a, raw_b, s_a, s_b, e_a, e_b, p2_a, p2_b, p4_b), rest = (
        rest[:13], rest[13:])
    raw_ref, s_ref, e_ref = (raw_a, raw_b), (s_a, s_b), (e_a, e_b)

    if n_cast:
        stage_in, stage_out, sems = rest[0:n_cast], rest[n_cast:2 * n_cast], rest[2 * n_cast]
        grid_step = pl.program_id(0)
        last_grid_step = pl.num_programs(0) - 1

        def copy_in(j, at_step):
            n = stage_in[j].shape[0]
            row0 = cast_layer * (w32_refs[j].shape[0] // DEPTH) + at_step * n
            return pltpu.make_async_copy(w32_refs[j].at[pl.ds(row0, n), :], stage_in[j], sems.at[j])

        def copy_out(j, at_step):
            n = stage_out[j].shape[0]
            return pltpu.make_async_copy(stage_out[j], wb_outs[j].at[pl.ds(at_step * n, n), :],
                                         sems.at[n_cast + j])

        for j in range(n_cast):
            copy_in(j, grid_step).start()
    n_keys = n_cache + seq_len
    tiles_per_seq = seq_len // Q_TILE
    n_tiles = n_seq * tiles_per_seq

    mod_row = pl.program_id(0) if n_cache else CTX_MOD_ROW
    sh1 = mod_ref[pl.ds(mod_row, 1), 0:D_MODEL]
    norm_gain = npre_ref[...] * (1.0 + mod_ref[pl.ds(mod_row, 1), D_MODEL:2 * D_MODEL])

    lane = lax.broadcasted_iota(jnp.int32, (Q_TILE, LANES), 1)
    first_head = lane < HEAD_DIM
    second_half = (lane & (HEAD_DIM // 2)) != 0

    def head_norm(t, gain):
        t2 = t * t
        s_lo = jnp.sum(jnp.where(first_head, t2, 0.0), axis=-1, keepdims=True)
        s_hi = jnp.sum(jnp.where(first_head, 0.0, t2), axis=-1, keepdims=True)
        ms = jnp.where(first_head, s_lo, s_hi) * (1.0 / HEAD_DIM)
        return (t * lax.rsqrt(ms + EPS)) * gain

    def rotary(t, pos0):
        partner = jnp.where(second_half, pltpu.roll(t, HEAD_DIM // 2, 1),
                            pltpu.roll(t, LANES - HEAD_DIM // 2, 1))
        return t * cos_ref[pos0:pos0 + Q_TILE, :] + partner * sin_ref[pos0:pos0 + Q_TILE, :]

    ones_rows = jnp.ones((VT_ROWS - HEAD_DIM, n_keys), BF16)
    pad_rows = jnp.zeros((POOL_PAD, POOL_GW), F32)
    for s_idx in range(n_seq):
        for g in range(len(POOL_WINDOWS)):
            p_ref[s_idx, g, 0:POOL_PAD, :] = pad_rows
            p_ref[s_idx, g, POOL_PAD + seq_len:2 * POOL_PAD + seq_len, :] = pad_rows
        for kv in range(N_KV_HEADS):
            vt_ref[s_idx, kv * VT_ROWS + HEAD_DIM:(kv + 1) * VT_ROWS, :] = ones_rows
        if n_cache:
            k_ref[s_idx, 0:n_cache, :] = ck_ref[0, 0].astype(BF16)
            cvt = cv_ref[0, 0].T.astype(BF16)
            for kv in range(N_KV_HEADS):
                vt_ref[s_idx, kv * VT_ROWS:kv * VT_ROWS + HEAD_DIM, 0:n_cache] = (
                    cvt[kv * HEAD_DIM:(kv + 1) * HEAD_DIM, :])

    def project(tile, slot):
        r0 = tile * Q_TILE
        x = x_ref[r0:r0 + Q_TILE, :]
        hb = (_rms(x) * norm_gain + sh1).astype(BF16)
        raw_ref[slot][...] = _dot(hb, w_ref[...])

    def epilogue(tile, slot):
        s_idx, pos0 = tile // tiles_per_seq, (tile % tiles_per_seq) * Q_TILE
        r0 = tile * Q_TILE
        raw = raw_ref[slot]
        for j in range(ATTN_W // LANES):
            t = head_norm(raw[:, j * LANES:(j + 1) * LANES], qg_ref[...])
            if rope:
                t = rotary(t, pos0)
            t = t * (HEAD_DIM ** -0.5 * LOG2_E)
            swapped = pltpu.roll(t, HEAD_DIM, 1)
            if (2 * j) // Q_PER_KV == 0:
                even = jnp.where(first_head, t, 0.0)
                odd = jnp.where(first_head, swapped, 0.0)
            else:
                even = jnp.where(first_head, 0.0, swapped)
                odd = jnp.where(first_head, 0.0, t)
            qa_ref[tile, (2 * j) * Q_TILE:(2 * j + 1) * Q_TILE, :] = even.astype(BF16)
            qa_ref[tile, (2 * j + 1) * Q_TILE:(2 * j + 2) * Q_TILE, :] = odd.astype(BF16)
        kn = head_norm(raw[:, ATTN_W:ATTN_W + KV_W], kg_ref[...])
        v = raw[:, ATTN_W + KV_W:ATTN_W + 2 * KV_W]
        vt_f32 = v.T
        if not n_cache:
            nk_out[s_idx, :, pos0:pos0 + Q_TILE] = kn.T
            nv_out[s_idx, :, pos0:pos0 + Q_TILE] = vt_f32
        if rope:
            kn = rotary(kn, pos0)
        k0 = n_cache + pos0
        k_ref[s_idx, k0:k0 + Q_TILE, :] = kn.astype(BF16)
        vt = vt_f32.astype(BF16)
        for kv in range(N_KV_HEADS):
            vt_ref[s_idx, kv * VT_ROWS:kv * VT_ROWS + HEAD_DIM, k0:k0 + Q_TILE] = (
                vt[kv * HEAD_DIM:(kv + 1) * HEAD_DIM, :])
        for g in range(len(POOL_WINDOWS)):
            c0 = ATTN_W + 2 * KV_W + g * POOL_GW
            p_ref[s_idx, g, POOL_PAD + pos0:POOL_PAD + pos0 + Q_TILE, :] = raw[:, c0:c0 + POOL_GW]

    project(0, 0)
    for tile in range(n_tiles):
        if tile + 1 < n_tiles:
            project(tile + 1, (tile + 1) % 2)
        epilogue(tile, tile % 2)

    n_pairs = N_Q_HEADS // 2
    n_steps = n_tiles * n_pairs

    def scores(step, slot):
        tile, pair = step // n_pairs, step % n_pairs
        q2 = qa_ref[tile, pair * 2 * Q_TILE:(pair + 1) * 2 * Q_TILE, :]
        s_ref[slot][...] = _dot_nt(k_ref[tile // tiles_per_seq], q2)

    def exponent(step, slot):
        s = s_ref[slot][...]
        e_ref[slot][...] = jnp.exp2(s - jnp.max(s, axis=0, keepdims=True)).astype(BF16)

    def values(step, slot):
        tile, pair = step // n_pairs, step % n_pairs
        kv = (2 * pair) // Q_PER_KV
        vt = vt_ref[tile // tiles_per_seq, kv * VT_ROWS:(kv + 1) * VT_ROWS, :]
        ot = _dot(vt, e_ref[slot][...])
        ot = ot[0:HEAD_DIM] * (1.0 / ot[HEAD_DIM:HEAD_DIM + 1])
        both = jnp.concatenate([ot[:, 0:Q_TILE], ot[:, Q_TILE:2 * Q_TILE]], axis=0)
        ao_out[tile * Q_TILE:(tile + 1) * Q_TILE, pair * LANES:(pair + 1) * LANES] = both.T.astype(BF16)

    scores(0, 0)
    for step in range(n_steps + 1):
        if step + 1 < n_steps:
            scores(step + 1, (step + 1) % 2)
        if step < n_steps:
            exponent(step, step % 2)
        if step >= 1:
            values(step - 1, (step - 1) % 2)

    for s_idx in range(n_seq):
        r0 = s_idx * seq_len
        for g, window in enumerate(POOL_WINDOWS):
            cols = slice(g * POOL_GW, (g + 1) * POOL_GW)
            p2_ref = p2_a if window == 8 else p2_b
            d = _pool_group(p_ref.at[s_idx, g], window, seq_len, p2_ref, p4_b)
            yg = _dot(d.astype(BF16), wpool_ref[g]) * pscale_ref[:, cols]
            y_out[r0:r0 + seq_len, cols] = yg.astype(BF16)

    if n_cast:
        @pl.when(grid_step > 0)
        def _():
            for j in range(n_cast):
                copy_out(j, grid_step - 1).wait()

        for j in range(n_cast):
            copy_in(j, grid_step).wait()
            stage_out[j][...] = stage_in[j][...].astype(BF16)
            copy_out(j, grid_step).start()

        @pl.when(grid_step == last_grid_step)
        def _():
            for j in range(n_cast):
                copy_out(j, grid_step).wait()


def _mix_call(l, x, mod_rows, n_pre, w_in_l, q_gain, k_gain, cos_t, sin_t, w_pool, pool_scale,
              cache_k, cache_v, cast_weights_f32, cast_layer, *, seq_len, n_seq, rope):
    n_tok = x.shape[0]
    rows = n_seq * seq_len
    is_lat = cache_k is not None
    n_cache = cache_k.shape[2] if is_lat else 0
    n_keys = n_cache + seq_len
    assert n_seq == 1 or not is_lat
    lay = lambda i: (l, 0, 0)
    row = lambda i: (i, 0)
    in_specs = [
        pl.BlockSpec((rows, D_MODEL), row),
        pl.BlockSpec((MOD_ROWS, N_MOD * D_MODEL), lambda i: (0, 0)),
        pl.BlockSpec((None, 1, D_MODEL), lay),
        pl.BlockSpec((D_MODEL, GATE_COL0), lambda i: (0, 0)),
        pl.BlockSpec((None, 1, LANES), lay),
        pl.BlockSpec((None, 1, LANES), lay),
        pl.BlockSpec((seq_len, LANES), lambda i: (0, 0)),
        pl.BlockSpec((seq_len, LANES), lambda i: (0, 0)),
        pl.BlockSpec((None, len(POOL_WINDOWS), POOL_GW, POOL_GW), lambda i: (l, 0, 0, 0)),
        pl.BlockSpec((None, 1, POOL_W), lay),
    ]
    args = [x, mod_rows, n_pre, w_in_l, q_gain, k_gain, cos_t, sin_t, w_pool, pool_scale]
    out_shape = [jax.ShapeDtypeStruct((n_tok, ATTN_W), BF16), jax.ShapeDtypeStruct((n_tok, POOL_W), BF16)]
    out_specs = [pl.BlockSpec((rows, ATTN_W), row), pl.BlockSpec((rows, POOL_W), row)]
    if is_lat:
        cache_spec = pl.BlockSpec((1, 1, n_cache, KV_W), lambda i: (i, l, 0, 0))
        in_specs += [cache_spec, cache_spec]
        args += [cache_k, cache_v]
    else:
        out_shape += [jax.ShapeDtypeStruct((n_tok // seq_len, KV_W, seq_len), F32)] * 2
        out_specs += [pl.BlockSpec((n_seq, KV_W, seq_len), lambda i: (i, 0, 0))] * 2
    n_cast = len(cast_weights_f32)
    n_steps = n_tok // rows
    any_spec = pl.BlockSpec(memory_space=pl.ANY)
    in_specs += [any_spec] * n_cast
    args += [w.reshape(-1, w.shape[2]) for w in cast_weights_f32]
    out_shape += [jax.ShapeDtypeStruct(w.shape[1:], BF16) for w in cast_weights_f32]
    out_specs += [any_spec] * n_cast
    slices = [(w.shape[1] // n_steps, w.shape[2]) for w in cast_weights_f32]
    assert all(w.shape[1] % n_steps == 0 and r % 16 == 0 for w, (r, _) in zip(cast_weights_f32, slices))
    scratch = [
        pltpu.VMEM((rows // Q_TILE, N_Q_HEADS * Q_TILE, LANES), BF16),
        pltpu.VMEM((n_seq, n_keys, KV_W), BF16),
        pltpu.VMEM((n_seq, N_KV_HEADS * VT_ROWS, n_keys), BF16),
        pltpu.VMEM((n_seq, len(POOL_WINDOWS), seq_len + 2 * POOL_PAD, POOL_GW), F32),
    ]
    scratch += [pltpu.VMEM((Q_TILE, GATE_COL0), F32)] * 2
    scratch += [pltpu.VMEM((n_keys, 2 * Q_TILE), F32)] * 2
    scratch += [pltpu.VMEM((n_keys, 2 * Q_TILE), BF16)] * 2
    scratch += [pltpu.VMEM((seq_len + 2 * POOL_PAD - 8, POOL_GW), F32)] * 2
    scratch += [pltpu.VMEM((seq_len + 2 * POOL_PAD - 16, POOL_GW), F32)]
    if n_cast:
        scratch += [pltpu.VMEM(s, F32) for s in slices] + [pltpu.VMEM(s, BF16) for s in slices]
        scratch += [pltpu.SemaphoreType.DMA((2 * n_cast,))]
    return pl.pallas_call(
        functools.partial(_mix_kernel, seq_len=seq_len, n_seq=n_seq, n_cache=n_cache, rope=rope,
                          n_cast=n_cast, cast_layer=cast_layer),
        grid=(n_tok // rows,),
        in_specs=in_specs,
        out_specs=out_specs,
        out_shape=out_shape,
        scratch_shapes=scratch,
        compiler_params=_params(),
        name="mix_lat" if is_lat else "mix_ctx",
    )(*args)


def _tail_kernel(x_ref, ao_ref, y_ref, mod_ref, npre_mix_ref, npost_mix_ref, npre_ffn_ref,
                 npost_ffn_ref, win_ref, wau_ref, wpu_ref, wout_ref, wg_ref, wu_ref, wd_ref, *rest,
                 sub, tiles_per_mod_row, next_rows):
    if next_rows:
        c_ref, wada_ref, bada_ref, o_ref, next_mod_out, x1_a, x1_b = rest
        next_mod_out[...] = _ada_rows(c_ref, wada_ref, bada_ref)
    else:
        o_ref, x1_a, x1_b = rest
    mod_row = pl.program_id(0) // tiles_per_mod_row if tiles_per_mod_row else CTX_MOD_ROW
    sh1, sc1, g1, sh2, sc2, g2 = [mod_ref[pl.ds(mod_row, 1), i * D_MODEL:(i + 1) * D_MODEL]
                                  for i in range(N_MOD)]
    x1_ref = (x1_a, x1_b)
    n_sub = x_ref.shape[0] // sub

    def post(i, slot):
        r0 = i * sub
        x = x_ref[r0:r0 + sub, :]
        hb = ((_rms(x) * npre_mix_ref[...]) * (1.0 + sc1) + sh1).astype(BF16)
        g_attn = jax.nn.sigmoid(_dot(hb, win_ref[:, GATE_COL0:GATE_COL0 + D_MODEL]))
        g_pool = jax.nn.sigmoid(_dot(hb, win_ref[:, GATE_COL0 + D_MODEL:IN_COLS]))
        attn = _dot(ao_ref[r0:r0 + sub, :], wau_ref[...])
        pool = _dot(y_ref[r0:r0 + sub, :], wpu_ref[...])
        mixed = g_attn * attn + g_pool * pool
        m = _dot(mixed.astype(BF16), wout_ref[...])
        x1_ref[slot][...] = x + g1 * (_rms(m) * npost_mix_ref[...])

    def ffn(i, slot):
        r0 = i * sub
        x1 = x1_ref[slot][...]
        hb = ((_rms(x1) * npre_ffn_ref[...]) * (1.0 + sc2) + sh2).astype(BF16)
        g = _dot(hb, wg_ref[...])
        u = _dot(hb, wu_ref[...])
        a = ((g * jax.nn.sigmoid(g)) * u).astype(BF16)
        f = _dot(a, wd_ref[...])
        o_ref[r0:r0 + sub, :] = x1 + g2 * (_rms(f) * npost_ffn_ref[...])

    post(0, 0)
    for i in range(n_sub):
        if i + 1 < n_sub:
            post(i + 1, (i + 1) % 2)
        ffn(i, i % 2)


def _tail_call(l, x, ao, y, mod_l, n_pre_mix, n_post_mix, n_pre_ffn, n_post_ffn, w_in,
               w_attn_up, w_pool_up, w_out, w_ffn_gate, w_ffn_up, w_ffn_down, next_mod_inputs,
               *, seq_len, per_seq_mod, tm, sub):
    n_tok = x.shape[0]
    n_tiles = n_tok // tm
    lay = lambda i: (l, 0, 0)
    row = lambda i: (i, 0)
    vec_spec = pl.BlockSpec((None, 1, D_MODEL), lay)
    layer_resident = lambda r, c: pl.BlockSpec((r, c), lambda i: (0, 0), pipeline_mode=pl.Buffered(1))
    in_specs = [
        pl.BlockSpec((tm, D_MODEL), row),
        pl.BlockSpec((tm, ATTN_W), row),
        pl.BlockSpec((tm, POOL_W), row),
        pl.BlockSpec((MOD_ROWS, N_MOD * D_MODEL), lambda i: (0, 0)),
        vec_spec, vec_spec, vec_spec, vec_spec,
        layer_resident(D_MODEL, IN_COLS),
        layer_resident(ATTN_W, D_MODEL),
        layer_resident(POOL_W, D_MODEL),
        layer_resident(D_MODEL, D_MODEL),
        layer_resident(D_MODEL, FFN_HIDDEN),
        layer_resident(D_MODEL, FFN_HIDDEN),
        layer_resident(FFN_HIDDEN, D_MODEL),
    ]
    args = [x, ao, y, mod_l, n_pre_mix, n_post_mix, n_pre_ffn, n_post_ffn, w_in, w_attn_up,
            w_pool_up, w_out, w_ffn_gate, w_ffn_up, w_ffn_down]
    out_specs = [pl.BlockSpec((tm, D_MODEL), row)]
    out_shape = [jax.ShapeDtypeStruct((n_tok, D_MODEL), F32)]
    if next_mod_inputs is not None:
        n_cols = N_MOD * D_MODEL
        tn = n_cols // n_tiles
        assert n_cols % n_tiles == 0 and tn % LANES == 0
        in_specs += [
            pl.BlockSpec((MOD_ROWS, D_MODEL), lambda i: (0, 0)),
            pl.BlockSpec((None, D_MODEL, tn), lambda i: (l + 1, 0, i)),
            pl.BlockSpec((None, 1, tn), lambda i: (l + 1, 0, i)),
        ]
        args += list(next_mod_inputs)
        out_specs += [pl.BlockSpec((MOD_ROWS, tn), lambda i: (0, i))]
        out_shape += [jax.ShapeDtypeStruct((MOD_ROWS, n_cols), F32)]
    return pl.pallas_call(
        functools.partial(_tail_kernel, sub=sub, next_rows=next_mod_inputs is not None,
                          tiles_per_mod_row=seq_len // tm if per_seq_mod else 0),
        grid=(n_tiles,),
        in_specs=in_specs,
        out_specs=out_specs,
        out_shape=out_shape,
        scratch_shapes=[pltpu.VMEM((sub, D_MODEL), F32)] * 2,
        compiler_params=_params(),
        name="tail",
    )(*args)


def _rope_tables(n):
    rows = n // GRID_W
    row = np.repeat(np.arange(rows), GRID_W).astype(np.float32)
    col = np.tile(np.arange(GRID_W), rows).astype(np.float32)
    n_freq = HEAD_DIM // 4
    inv = jnp.asarray(ROPE_THETA, F32) ** (-jnp.arange(n_freq, dtype=F32) / n_freq)
    ang = jnp.concatenate([jnp.asarray(row)[:, None] * inv[None, :],
                           jnp.asarray(col)[:, None] * inv[None, :]], axis=-1)
    cos, sin = jnp.cos(ang), jnp.sin(ang)
    cos_t = jnp.tile(cos, (1, LANES // (HEAD_DIM // 2)))
    sin_t = jnp.tile(jnp.concatenate([-sin, sin], axis=-1), (1, LANES // HEAD_DIM))
    return cos_t, sin_t


def kernel(x_prompt, x_sample, cache_k, cache_v, c, c_ctx, w_ada, b_ada, w_in, q_norm, k_norm,
           w_attn_up, w_pool, pool_scale, w_pool_up, w_out, n_pre_mix, n_post_mix, n_pre_ffn,
           n_post_ffn, w_ffn_gate, w_ffn_up, w_ffn_down):
    batch, seq, _ = x_prompt.shape
    dec_batch, dec_seq, _ = x_sample.shape
    past_len = cache_k.shape[2]
    assert dec_batch <= CTX_MOD_ROW

    c_rows = jnp.concatenate(
        [c, c_ctx[None, :], jnp.zeros((MOD_ROWS - dec_batch - 1, D_MODEL), F32)], axis=0)
    b_ada3 = b_ada.reshape(DEPTH, 1, N_MOD * D_MODEL)
    mod_l = _ada_call(c_rows, w_ada, b_ada3)

    cos_t, sin_t = _rope_tables(dec_seq)
    vec = lambda a: a.reshape(DEPTH, 1, a.shape[-1])
    q_gain = vec(jnp.tile(q_norm, (1, LANES // HEAD_DIM)))
    k_gain = vec(jnp.tile(k_norm, (1, LANES // HEAD_DIM)))
    w_in_l = w_in[0].astype(BF16)
    w_pool_b = w_pool.astype(BF16)
    tail_weights_f32 = (w_attn_up, w_pool_up, w_out, w_ffn_gate, w_ffn_up, w_ffn_down)
    n_pre_mix_v, n_post_mix_v = vec(n_pre_mix), vec(n_post_mix)
    n_pre_ffn_v, n_post_ffn_v = vec(n_pre_ffn), vec(n_post_ffn)
    pool_scale_v = vec(pool_scale)
    ck = cache_k.reshape(dec_batch, DEPTH, past_len, KV_W)
    cv = cache_v.reshape(dec_batch, DEPTH, past_len, KV_W)

    tm = 512
    ctx_seqs_per_step = 4

    y = x_prompt.reshape(batch * seq, D_MODEL)
    z = x_sample.reshape(dec_batch * dec_seq, D_MODEL)
    new_ks, new_vs = [], []
    for l in range(DEPTH):
        has_next = l + 1 < DEPTH
        outs = _mix_call(l, y, mod_l, n_pre_mix_v, w_in_l, q_gain, k_gain, cos_t, sin_t, w_pool_b,
                         pool_scale_v, None, None, tail_weights_f32, l,
                         seq_len=seq, n_seq=ctx_seqs_per_step, rope=False)
        ao, yp, nk, nv = outs[:4]
        tail_weights_b = outs[4:]
        new_ks.append(nk)
        new_vs.append(nv)
        tail_outs = _tail_call(l, y, ao, yp, mod_l, n_pre_mix_v, n_post_mix_v, n_pre_ffn_v, n_post_ffn_v,
                               w_in_l, *tail_weights_b, (c_rows, w_ada, b_ada3) if has_next else None,
                               seq_len=seq, per_seq_mod=False, tm=tm, sub=256)
        y = tail_outs[0]
        outs = _mix_call(l, z, mod_l, n_pre_mix_v, w_in_l, q_gain, k_gain, cos_t, sin_t, w_pool_b,
                         pool_scale_v, ck, cv, (w_in,) if has_next else (), l + 1,
                         seq_len=dec_seq, n_seq=1, rope=True)
        ao, yp = outs[:2]
        z = _tail_call(l, z, ao, yp, mod_l, n_pre_mix_v, n_post_mix_v, n_pre_ffn_v, n_post_ffn_v,
                       w_in_l, *tail_weights_b, None,
                       seq_len=dec_seq, per_seq_mod=True, tm=tm, sub=256)[0]
        if has_next:
            w_in_l = outs[2]
            mod_l = tail_outs[1]

    def kv_layout(parts):
        a = jnp.stack(parts, axis=1).reshape(batch, DEPTH, N_KV_HEADS, HEAD_DIM, seq)
        return jnp.transpose(a, (0, 1, 4, 2, 3))

    return (y.reshape(batch, seq, D_MODEL), z.reshape(dec_batch, dec_seq, D_MODEL),
            kv_layout(new_ks), kv_layout(new_vs))
```

```python
import functools

import jax
import jax.numpy as jnp
import numpy as np
from jax import lax
from jax.experimental import pallas as pl
from jax.experimental.pallas import tpu as pltpu

D_MODEL = 1024
DEPTH = 4
GRID_W = 64
HEAD_DIM = 64
N_Q_HEADS = 8
N_KV_HEADS = 2
Q_PER_KV = N_Q_HEADS // N_KV_HEADS
ATTN_W = N_Q_HEADS * HEAD_DIM
KV_W = N_KV_HEADS * HEAD_DIM
POOL_WINDOWS = (2, 4, 8, 16)
POOL_W = D_MODEL // 2
POOL_GW = POOL_W // len(POOL_WINDOWS)
GATE_COL0 = ATTN_W + 2 * KV_W + POOL_W
IN_COLS = GATE_COL0 + 2 * D_MODEL
FFN_HIDDEN = 2816
N_MOD = 6
ROPE_THETA = 10000.0
EPS = 1e-6

LANES = 128
Q_TILE = 256
VT_ROWS = 80
POOL_PAD = 24
LOG2_E = 1.4426950408889634
MOD_ROWS = 8
CTX_MOD_ROW = 4
VMEM_LIMIT_BYTES = 56 * 1024 * 1024

F32 = jnp.float32
BF16 = jnp.bfloat16


def _dot(a, b):
    return jnp.dot(a, b, preferred_element_type=F32)


def _dot_nt(a, b):
    return lax.dot_general(a, b, (((1,), (1,)), ((), ())), preferred_element_type=F32)


def _rms(x):
    return x * lax.rsqrt(jnp.mean(x * x, axis=-1, keepdims=True) + EPS)


def _params(n_axes=1):
    return pltpu.CompilerParams(dimension_semantics=("arbitrary",) * n_axes,
                                vmem_limit_bytes=VMEM_LIMIT_BYTES)


def _ada_rows(c_ref, w_ref, b_ref):
    c = c_ref[...]
    s = (c * jax.nn.sigmoid(c)).astype(BF16)
    return _dot(s, w_ref[...].astype(BF16)) + b_ref[...]


def _ada_kernel(c_ref, w_ref, b_ref, o_ref):
    o_ref[...] = _ada_rows(c_ref, w_ref, b_ref)


def _ada_call(c_rows, w_ada, b_ada3):
    tn = 1536
    n_cols = N_MOD * D_MODEL
    return pl.pallas_call(
        _ada_kernel,
        grid=(n_cols // tn,),
        in_specs=[
            pl.BlockSpec((MOD_ROWS, D_MODEL), lambda j: (0, 0)),
            pl.BlockSpec((None, D_MODEL, tn), lambda j: (0, 0, j)),
            pl.BlockSpec((None, 1, tn), lambda j: (0, 0, j)),
        ],
        out_specs=pl.BlockSpec((MOD_ROWS, tn), lambda j: (0, j)),
        out_shape=jax.ShapeDtypeStruct((MOD_ROWS, n_cols), F32),
        compiler_params=_params(),
        name="ada_rows",
    )(c_rows, w_ada, b_ada3)


def _pool_group(src, window, seq_len, p2_ref, p4_ref):
    half = window // 2
    rows = lambda ref, off: ref[POOL_PAD + off:POOL_PAD + off + seq_len, :]
    if window <= 4:
        total = rows(src, -half)
        for off in range(-half + 1, half):
            total = total + rows(src, off)
    else:
        n2 = seq_len + 2 * POOL_PAD - 8
        p2_ref[0:n2, :] = src[0:n2, :] + src[1:n2 + 1, :]
        if window == 8:
            terms = [rows(p2_ref, off) for off in (-4, -2, 0, 2)]
        else:
            assert window == 16
            n4 = n2 - 8
            p4_ref[0:n4, :] = p2_ref[0:n4, :] + p2_ref[2:n4 + 2, :]
            terms = [rows(p4_ref, off) for off in (-8, -4, 0, 4)]
        total = (terms[0] + terms[1]) + (terms[2] + terms[3])
    t = lax.broadcasted_iota(jnp.int32, (seq_len, POOL_GW), 0)
    count = jnp.minimum(t + half, seq_len) - jnp.maximum(t - half, 0)
    return total / count.astype(F32) - rows(src, 0)


def _mix_kernel(x_ref, mod_ref, npre_ref, w_ref, qg_ref, kg_ref, cos_ref, sin_ref, wpool_ref,
                pscale_ref, *rest, seq_len, n_seq, n_cache, rope, n_cast, cast_layer, kv_first_layer,
                n_kv_alias):
    if n_cache:
        (ck_ref, cv_ref), rest = rest[:2], rest[2:]
    w32_refs, rest = rest[:n_cast], rest[n_cast:]
    rest = rest[n_kv_alias:]
    if n_cache:
        (ao_out, y_out), rest = rest[:2], rest[2:]
    else:
        (ao_out, y_out, nk_out, nv_out), rest = rest[:4], rest[4:]
    wb_outs, rest = rest[:n_cast], rest[n_cast:]
    (qa_ref, k_ref, vt_ref, p_ref, raw_a, raw_b, s_a, s_b, e_a, e_b, p2_a, p2_b, p4_b), rest = (
        rest[:13], rest[13:])
    raw_ref, s_ref, e_ref = (raw_a, raw_b), (s_a, s_b), (e_a, e_b)

    if n_cast:
        stage_in, stage_out, sems = rest[0:n_cast], rest[n_cast:2 * n_cast], rest[2 * n_cast]
        grid_step = pl.program_id(0)
        last_grid_step = pl.num_programs(0) - 1

        def copy_in(j, at_step):
            n = stage_in[j].shape[0]
            row0 = cast_layer * (w32_refs[j].shape[0] // DEPTH) + at_step * n
            return pltpu.make_async_copy(w32_refs[j].at[pl.ds(row0, n), :], stage_in[j], sems.at[j])

        def copy_out(j, at_step):
            n = stage_out[j].shape[0]
            return pltpu.make_async_copy(stage_out[j], wb_outs[j].at[pl.ds(at_step * n, n), :],
                                         sems.at[n_cast + j])

        for j in range(n_cast):
            copy_in(j, grid_step).start()
    n_keys = n_cache + seq_len
    tiles_per_seq = seq_len // Q_TILE
    n_tiles = n_seq * tiles_per_seq

    mod_row = pl.program_id(0) if n_cache else CTX_MOD_ROW
    sh1 = mod_ref[pl.ds(mod_row, 1), 0:D_MODEL]
    norm_gain = npre_ref[...] * (1.0 + mod_ref[pl.ds(mod_row, 1), D_MODEL:2 * D_MODEL])

    lane = lax.broadcasted_iota(jnp.int32, (Q_TILE, LANES), 1)
    first_head = lane < HEAD_DIM
    second_half = (lane & (HEAD_DIM // 2)) != 0

    def head_norm(t, gain):
        t2 = t * t
        s_lo = jnp.sum(jnp.where(first_head, t2, 0.0), axis=-1, keepdims=True)
        s_hi = jnp.sum(jnp.where(first_head, 0.0, t2), axis=-1, keepdims=True)
        ms = jnp.where(first_head, s_lo, s_hi) * (1.0 / HEAD_DIM)
        return (t * lax.rsqrt(ms + EPS)) * gain

    def rotary(t, pos0):
        partner = jnp.where(second_half, pltpu.roll(t, HEAD_DIM // 2, 1),
                            pltpu.roll(t, LANES - HEAD_DIM // 2, 1))
        return t * cos_ref[pos0:pos0 + Q_TILE, :] + partner * sin_ref[pos0:pos0 + Q_TILE, :]

    ones_rows = jnp.ones((VT_ROWS - HEAD_DIM, n_keys), BF16)
    pad_rows = jnp.zeros((POOL_PAD, POOL_GW), F32)
    for s_idx in range(n_seq):
        for g in range(len(POOL_WINDOWS)):
            p_ref[s_idx, g, 0:POOL_PAD, :] = pad_rows
            p_ref[s_idx, g, POOL_PAD + seq_len:2 * POOL_PAD + seq_len, :] = pad_rows
        for kv in range(N_KV_HEADS):
            vt_ref[s_idx, kv * VT_ROWS + HEAD_DIM:(kv + 1) * VT_ROWS, :] = ones_rows
        if n_cache:
            k_ref[s_idx, 0:n_cache, :] = ck_ref[0, 0].astype(BF16)
            cvt = cv_ref[0, 0].T.astype(BF16)
            for kv in range(N_KV_HEADS):
                vt_ref[s_idx, kv * VT_ROWS:kv * VT_ROWS + HEAD_DIM, 0:n_cache] = (
                    cvt[kv * HEAD_DIM:(kv + 1) * HEAD_DIM, :])

    def project(tile, slot):
        r0 = tile * Q_TILE
        x = x_ref[r0:r0 + Q_TILE, :]
        hb = (_rms(x) * norm_gain + sh1).astype(BF16)
        raw_ref[slot][...] = _dot(hb, w_ref[...])

    def epilogue(tile, slot):
        s_idx, pos0 = tile // tiles_per_seq, (tile % tiles_per_seq) * Q_TILE
        r0 = tile * Q_TILE
        raw = raw_ref[slot]
        for j in range(ATTN_W // LANES):
            t = head_norm(raw[:, j * LANES:(j + 1) * LANES], qg_ref[...])
            if rope:
                t = rotary(t, pos0)
            t = t * (HEAD_DIM ** -0.5 * LOG2_E)
            swapped = pltpu.roll(t, HEAD_DIM, 1)
            if (2 * j) // Q_PER_KV == 0:
                even = jnp.where(first_head, t, 0.0)
                odd = jnp.where(first_head, swapped, 0.0)
            else:
                even = jnp.where(first_head, 0.0, swapped)
                odd = jnp.where(first_head, 0.0, t)
            qa_ref[tile, (2 * j) * Q_TILE:(2 * j + 1) * Q_TILE, :] = even.astype(BF16)
            qa_ref[tile, (2 * j + 1) * Q_TILE:(2 * j + 2) * Q_TILE, :] = odd.astype(BF16)
        kn = head_norm(raw[:, ATTN_W:ATTN_W + KV_W], kg_ref[...])
        v = raw[:, ATTN_W + KV_W:ATTN_W + 2 * KV_W]
        vt_f32 = v.T
        if not n_cache:
            if kv_first_layer:
                nk_out[s_idx, 0, :, pos0:pos0 + Q_TILE] = kn.T
                nv_out[s_idx, 0, :, pos0:pos0 + Q_TILE] = vt_f32
                later = jnp.zeros((DEPTH - 1, KV_W, Q_TILE), F32)
                nk_out[s_idx, 1:DEPTH, :, pos0:pos0 + Q_TILE] = later
                nv_out[s_idx, 1:DEPTH, :, pos0:pos0 + Q_TILE] = later
            else:
                nk_out[s_idx, :, pos0:pos0 + Q_TILE] = kn.T
                nv_out[s_idx, :, pos0:pos0 + Q_TILE] = vt_f32
        if rope:
            kn = rotary(kn, pos0)
        k0 = n_cache + pos0
        k_ref[s_idx, k0:k0 + Q_TILE, :] = kn.astype(BF16)
        vt = vt_f32.astype(BF16)
        for kv in range(N_KV_HEADS):
            vt_ref[s_idx, kv * VT_ROWS:kv * VT_ROWS + HEAD_DIM, k0:k0 + Q_TILE] = (
                vt[kv * HEAD_DIM:(kv + 1) * HEAD_DIM, :])
        for g in range(len(POOL_WINDOWS)):
            c0 = ATTN_W + 2 * KV_W + g * POOL_GW
            p_ref[s_idx, g, POOL_PAD + pos0:POOL_PAD + pos0 + Q_TILE, :] = raw[:, c0:c0 + POOL_GW]

    project(0, 0)
    for tile in range(n_tiles):
        if tile + 1 < n_tiles:
            project(tile + 1, (tile + 1) % 2)
        epilogue(tile, tile % 2)

    n_pairs = N_Q_HEADS // 2
    n_steps = n_tiles * n_pairs

    def scores(step, slot):
        tile, pair = step // n_pairs, step % n_pairs
        q2 = qa_ref[tile, pair * 2 * Q_TILE:(pair + 1) * 2 * Q_TILE, :]
        s_ref[slot][...] = _dot_nt(k_ref[tile // tiles_per_seq], q2)

    def exponent(step, slot):
        s = s_ref[slot][...]
        e_ref[slot][...] = jnp.exp2(s - jnp.max(s, axis=0, keepdims=True)).astype(BF16)

    def values(step, slot):
        tile, pair = step // n_pairs, step % n_pairs
        kv = (2 * pair) // Q_PER_KV
        vt = vt_ref[tile // tiles_per_seq, kv * VT_ROWS:(kv + 1) * VT_ROWS, :]
        ot = _dot(vt, e_ref[slot][...])
        ot = ot[0:HEAD_DIM] * (1.0 / ot[HEAD_DIM:HEAD_DIM + 1])
        both = jnp.concatenate([ot[:, 0:Q_TILE], ot[:, Q_TILE:2 * Q_TILE]], axis=0)
        ao_out[tile * Q_TILE:(tile + 1) * Q_TILE, pair * LANES:(pair + 1) * LANES] = both.T.astype(BF16)

    scores(0, 0)
    for step in range(n_steps + 1):
        if step + 1 < n_steps:
            scores(step + 1, (step + 1) % 2)
        if step < n_steps:
            exponent(step, step % 2)
        if step >= 1:
            values(step - 1, (step - 1) % 2)

    for s_idx in range(n_seq):
        r0 = s_idx * seq_len
        for g, window in enumerate(POOL_WINDOWS):
            cols = slice(g * POOL_GW, (g + 1) * POOL_GW)
            p2_ref = p2_a if window == 8 else p2_b
            d = _pool_group(p_ref.at[s_idx, g], window, seq_len, p2_ref, p4_b)
            yg = _dot(d.astype(BF16), wpool_ref[g]) * pscale_ref[:, cols]
            y_out[r0:r0 + seq_len, cols] = yg.astype(BF16)

    if n_cast:
        @pl.when(grid_step > 0)
        def _():
            for j in range(n_cast):
                copy_out(j, grid_step - 1).wait()

        for j in range(n_cast):
            copy_in(j, grid_step).wait()
            stage_out[j][...] = stage_in[j][...].astype(BF16)
            copy_out(j, grid_step).start()

        @pl.when(grid_step == last_grid_step)
        def _():
            for j in range(n_cast):
                copy_out(j, grid_step).wait()


def _mix_call(l, x, mod_rows, n_pre, w_in_l, q_gain, k_gain, cos_t, sin_t, w_pool, pool_scale,
              cache_k, cache_v, cast_weights_f32, cast_layer, new_kv=None, *, seq_len, n_seq, rope):
    n_tok = x.shape[0]
    rows = n_seq * seq_len
    is_lat = cache_k is not None
    n_cache = cache_k.shape[2] if is_lat else 0
    n_keys = n_cache + seq_len
    assert n_seq == 1 or not is_lat
    lay = lambda i: (l, 0, 0)
    row = lambda i: (i, 0)
    in_specs = [
        pl.BlockSpec((rows, D_MODEL), row),
        pl.BlockSpec((MOD_ROWS, N_MOD * D_MODEL), lambda i: (0, 0)),
        pl.BlockSpec((None, 1, D_MODEL), lay),
        pl.BlockSpec((D_MODEL, GATE_COL0), lambda i: (0, 0)),
        pl.BlockSpec((None, 1, LANES), lay),
        pl.BlockSpec((None, 1, LANES), lay),
        pl.BlockSpec((seq_len, LANES), lambda i: (0, 0)),
        pl.BlockSpec((seq_len, LANES), lambda i: (0, 0)),
        pl.BlockSpec((None, len(POOL_WINDOWS), POOL_GW, POOL_GW), lambda i: (l, 0, 0, 0)),
        pl.BlockSpec((None, 1, POOL_W), lay),
    ]
    args = [x, mod_rows, n_pre, w_in_l, q_gain, k_gain, cos_t, sin_t, w_pool, pool_scale]
    out_shape = [jax.ShapeDtypeStruct((n_tok, ATTN_W), BF16), jax.ShapeDtypeStruct((n_tok, POOL_W), BF16)]
    out_specs = [pl.BlockSpec((rows, ATTN_W), row), pl.BlockSpec((rows, POOL_W), row)]
    if is_lat:
        cache_spec = pl.BlockSpec((1, 1, n_cache, KV_W), lambda i: (i, l, 0, 0))
        in_specs += [cache_spec, cache_spec]
        args += [cache_k, cache_v]
    else:
        out_shape += [jax.ShapeDtypeStruct((n_tok // seq_len, DEPTH, KV_W, seq_len), F32)] * 2
        if new_kv is None:
            out_specs += [pl.BlockSpec((n_seq, DEPTH, KV_W, seq_len), lambda i: (i, 0, 0, 0))] * 2
        else:
            out_specs += [pl.BlockSpec((n_seq, None, KV_W, seq_len), lambda i: (i, l, 0, 0))] * 2
    n_cast = len(cast_weights_f32)
    n_steps = n_tok // rows
    any_spec = pl.BlockSpec(memory_space=pl.ANY)
    in_specs += [any_spec] * n_cast
    args += [w.reshape(-1, w.shape[2]) for w in cast_weights_f32]
    aliases = {}
    if new_kv is not None:
        aliases = {len(args): 2, len(args) + 1: 3}
        in_specs += [any_spec] * 2
        args += list(new_kv)
    out_shape += [jax.ShapeDtypeStruct(w.shape[1:], BF16) for w in cast_weights_f32]
    out_specs += [any_spec] * n_cast
    slices = [(w.shape[1] // n_steps, w.shape[2]) for w in cast_weights_f32]
    assert all(w.shape[1] % n_steps == 0 and r % 16 == 0 for w, (r, _) in zip(cast_weights_f32, slices))
    scratch = [
        pltpu.VMEM((rows // Q_TILE, N_Q_HEADS * Q_TILE, LANES), BF16),
        pltpu.VMEM((n_seq, n_keys, KV_W), BF16),
        pltpu.VMEM((n_seq, N_KV_HEADS * VT_ROWS, n_keys), BF16),
        pltpu.VMEM((n_seq, len(POOL_WINDOWS), seq_len + 2 * POOL_PAD, POOL_GW), F32),
    ]
    scratch += [pltpu.VMEM((Q_TILE, GATE_COL0), F32)] * 2
    scratch += [pltpu.VMEM((n_keys, 2 * Q_TILE), F32)] * 2
    scratch += [pltpu.VMEM((n_keys, 2 * Q_TILE), BF16)] * 2
    scratch += [pltpu.VMEM((seq_len + 2 * POOL_PAD - 8, POOL_GW), F32)] * 2
    scratch += [pltpu.VMEM((seq_len + 2 * POOL_PAD - 16, POOL_GW), F32)]
    if n_cast:
        scratch += [pltpu.VMEM(s, F32) for s in slices] + [pltpu.VMEM(s, BF16) for s in slices]
        scratch += [pltpu.SemaphoreType.DMA((2 * n_cast,))]
    return pl.pallas_call(
        functools.partial(_mix_kernel, seq_len=seq_len, n_seq=n_seq, n_cache=n_cache, rope=rope,
                          n_cast=n_cast, cast_layer=cast_layer,
                          kv_first_layer=not is_lat and new_kv is None, n_kv_alias=len(aliases)),
        grid=(n_tok // rows,),
        in_specs=in_specs,
        out_specs=out_specs,
        out_shape=out_shape,
        scratch_shapes=scratch,
        input_output_aliases=aliases,
        compiler_params=_params(),
        name="mix_lat" if is_lat else "mix_ctx",
    )(*args)


def _tail_kernel(x_ref, ao_ref, y_ref, mod_ref, npre_mix_ref, npost_mix_ref, npre_ffn_ref,
                 npost_ffn_ref, win_ref, wau_ref, wpu_ref, wout_ref, wg_ref, wu_ref, wd_ref, *rest,
                 sub, tiles_per_mod_row, next_rows):
    if next_rows:
        c_ref, wada_ref, bada_ref, o_ref, next_mod_out, x1_a, x1_b = rest
        next_mod_out[...] = _ada_rows(c_ref, wada_ref, bada_ref)
    else:
        o_ref, x1_a, x1_b = rest
    mod_row = pl.program_id(0) // tiles_per_mod_row if tiles_per_mod_row else CTX_MOD_ROW
    sh1, sc1, g1, sh2, sc2, g2 = [mod_ref[pl.ds(mod_row, 1), i * D_MODEL:(i + 1) * D_MODEL]
                                  for i in range(N_MOD)]
    x1_ref = (x1_a, x1_b)
    n_sub = x_ref.shape[0] // sub

    def post(i, slot):
        r0 = i * sub
        x = x_ref[r0:r0 + sub, :]
        hb = ((_rms(x) * npre_mix_ref[...]) * (1.0 + sc1) + sh1).astype(BF16)
        g_attn = jax.nn.sigmoid(_dot(hb, win_ref[:, GATE_COL0:GATE_COL0 + D_MODEL]))
        g_pool = jax.nn.sigmoid(_dot(hb, win_ref[:, GATE_COL0 + D_MODEL:IN_COLS]))
        attn = _dot(ao_ref[r0:r0 + sub, :], wau_ref[...])
        pool = _dot(y_ref[r0:r0 + sub, :], wpu_ref[...])
        mixed = g_attn * attn + g_pool * pool
        m = _dot(mixed.astype(BF16), wout_ref[...])
        x1_ref[slot][...] = x + g1 * (_rms(m) * npost_mix_ref[...])

    def ffn(i, slot):
        r0 = i * sub
        x1 = x1_ref[slot][...]
        hb = ((_rms(x1) * npre_ffn_ref[...]) * (1.0 + sc2) + sh2).astype(BF16)
        g = _dot(hb, wg_ref[...])
        u = _dot(hb, wu_ref[...])
        a = ((g * jax.nn.sigmoid(g)) * u).astype(BF16)
        f = _dot(a, wd_ref[...])
        o_ref[r0:r0 + sub, :] = x1 + g2 * (_rms(f) * npost_ffn_ref[...])

    post(0, 0)
    for i in range(n_sub):
        if i + 1 < n_sub:
            post(i + 1, (i + 1) % 2)
        ffn(i, i % 2)


def _tail_call(l, x, ao, y, mod_l, n_pre_mix, n_post_mix, n_pre_ffn, n_post_ffn, w_in,
               w_attn_up, w_pool_up, w_out, w_ffn_gate, w_ffn_up, w_ffn_down, next_mod_inputs,
               *, seq_len, per_seq_mod, tm, sub):
    n_tok = x.shape[0]
    n_tiles = n_tok // tm
    lay = lambda i: (l, 0, 0)
    row = lambda i: (i, 0)
    vec_spec = pl.BlockSpec((None, 1, D_MODEL), lay)
    layer_resident = lambda r, c: pl.BlockSpec((r, c), lambda i: (0, 0), pipeline_mode=pl.Buffered(1))
    in_specs = [
        pl.BlockSpec((tm, D_MODEL), row),
        pl.BlockSpec((tm, ATTN_W), row),
        pl.BlockSpec((tm, POOL_W), row),
        pl.BlockSpec((MOD_ROWS, N_MOD * D_MODEL), lambda i: (0, 0)),
        vec_spec, vec_spec, vec_spec, vec_spec,
        layer_resident(D_MODEL, IN_COLS),
        layer_resident(ATTN_W, D_MODEL),
        layer_resident(POOL_W, D_MODEL),
        layer_resident(D_MODEL, D_MODEL),
        layer_resident(D_MODEL, FFN_HIDDEN),
        layer_resident(D_MODEL, FFN_HIDDEN),
        layer_resident(FFN_HIDDEN, D_MODEL),
    ]
    args = [x, ao, y, mod_l, n_pre_mix, n_post_mix, n_pre_ffn, n_post_ffn, w_in, w_attn_up,
            w_pool_up, w_out, w_ffn_gate, w_ffn_up, w_ffn_down]
    out_specs = [pl.BlockSpec((tm, D_MODEL), row)]
    out_shape = [jax.ShapeDtypeStruct((n_tok, D_MODEL), F32)]
    if next_mod_inputs is not None:
        n_cols = N_MOD * D_MODEL
        tn = n_cols // n_tiles
        assert n_cols % n_tiles == 0 and tn % LANES == 0
        in_specs += [
            pl.BlockSpec((MOD_ROWS, D_MODEL), lambda i: (0, 0)),
            pl.BlockSpec((None, D_MODEL, tn), lambda i: (l + 1, 0, i)),
            pl.BlockSpec((None, 1, tn), lambda i: (l + 1, 0, i)),
        ]
        args += list(next_mod_inputs)
        out_specs += [pl.BlockSpec((MOD_ROWS, tn), lambda i: (0, i))]
        out_shape += [jax.ShapeDtypeStruct((MOD_ROWS, n_cols), F32)]
    return pl.pallas_call(
        functools.partial(_tail_kernel, sub=sub, next_rows=next_mod_inputs is not None,
                          tiles_per_mod_row=seq_len // tm if per_seq_mod else 0),
        grid=(n_tiles,),
        in_specs=in_specs,
        out_specs=out_specs,
        out_shape=out_shape,
        scratch_shapes=[pltpu.VMEM((sub, D_MODEL), F32)] * 2,
        compiler_params=_params(),
        name="tail",
    )(*args)


def _rope_tables(n):
    rows = n // GRID_W
    row = np.repeat(np.arange(rows), GRID_W).astype(np.float32)
    col = np.tile(np.arange(GRID_W), rows).astype(np.float32)
    n_freq = HEAD_DIM // 4
    inv = jnp.asarray(ROPE_THETA, F32) ** (-jnp.arange(n_freq, dtype=F32) / n_freq)
    ang = jnp.concatenate([jnp.asarray(row)[:, None] * inv[None, :],
                           jnp.asarray(col)[:, None] * inv[None, :]], axis=-1)
    cos, sin = jnp.cos(ang), jnp.sin(ang)
    cos_t = jnp.tile(cos, (1, LANES // (HEAD_DIM // 2)))
    sin_t = jnp.tile(jnp.concatenate([-sin, sin], axis=-1), (1, LANES // HEAD_DIM))
    return cos_t, sin_t


def kernel(x_prompt, x_sample, cache_k, cache_v, c, c_ctx, w_ada, b_ada, w_in, q_norm, k_norm,
           w_attn_up, w_pool, pool_scale, w_pool_up, w_out, n_pre_mix, n_post_mix, n_pre_ffn,
           n_post_ffn, w_ffn_gate, w_ffn_up, w_ffn_down):
    batch, seq, _ = x_prompt.shape
    dec_batch, dec_seq, _ = x_sample.shape
    past_len = cache_k.shape[2]
    assert dec_batch <= CTX_MOD_ROW

    c_rows = jnp.concatenate(
        [c, c_ctx[None, :], jnp.zeros((MOD_ROWS - dec_batch - 1, D_MODEL), F32)], axis=0)
    b_ada3 = b_ada.reshape(DEPTH, 1, N_MOD * D_MODEL)
    mod_l = _ada_call(c_rows, w_ada, b_ada3)

    cos_t, sin_t = _rope_tables(dec_seq)
    vec = lambda a: a.reshape(DEPTH, 1, a.shape[-1])
    q_gain = vec(jnp.tile(q_norm, (1, LANES // HEAD_DIM)))
    k_gain = vec(jnp.tile(k_norm, (1, LANES // HEAD_DIM)))
    w_in_l = w_in[0].astype(BF16)
    w_pool_b = w_pool.astype(BF16)
    tail_weights_f32 = (w_attn_up, w_pool_up, w_out, w_ffn_gate, w_ffn_up, w_ffn_down)
    n_pre_mix_v, n_post_mix_v = vec(n_pre_mix), vec(n_post_mix)
    n_pre_ffn_v, n_post_ffn_v = vec(n_pre_ffn), vec(n_post_ffn)
    pool_scale_v = vec(pool_scale)
    ck = cache_k.reshape(dec_batch, DEPTH, past_len, KV_W)
    cv = cache_v.reshape(dec_batch, DEPTH, past_len, KV_W)

    tm = 512
    ctx_seqs_per_step = 4

    y = x_prompt.reshape(batch * seq, D_MODEL)
    z = x_sample.reshape(dec_batch * dec_seq, D_MODEL)
    new_kv = None
    for l in range(DEPTH):
        has_next = l + 1 < DEPTH
        outs = _mix_call(l, y, mod_l, n_pre_mix_v, w_in_l, q_gain, k_gain, cos_t, sin_t, w_pool_b,
                         pool_scale_v, None, None, tail_weights_f32, l, new_kv,
                         seq_len=seq, n_seq=ctx_seqs_per_step, rope=False)
        ao, yp = outs[:2]
        new_kv = outs[2:4]
        tail_weights_b = outs[4:]
        tail_outs = _tail_call(l, y, ao, yp, mod_l, n_pre_mix_v, n_post_mix_v, n_pre_ffn_v, n_post_ffn_v,
                               w_in_l, *tail_weights_b, (c_rows, w_ada, b_ada3) if has_next else None,
                               seq_len=seq, per_seq_mod=False, tm=tm, sub=256)
        y = tail_outs[0]
        outs = _mix_call(l, z, mod_l, n_pre_mix_v, w_in_l, q_gain, k_gain, cos_t, sin_t, w_pool_b,
                         pool_scale_v, ck, cv, (w_in,) if has_next else (), l + 1,
                         seq_len=dec_seq, n_seq=1, rope=True)
        ao, yp = outs[:2]
        z = _tail_call(l, z, ao, yp, mod_l, n_pre_mix_v, n_post_mix_v, n_pre_ffn_v, n_post_ffn_v,
                       w_in_l, *tail_weights_b, None,
                       seq_len=dec_seq, per_seq_mod=True, tm=tm, sub=256)[0]
        if has_next:
            w_in_l = outs[2]
            mod_l = tail_outs[1]

    def kv_layout(a):
        a = a.reshape(batch, DEPTH, N_KV_HEADS, HEAD_DIM, seq)
        return jnp.transpose(a, (0, 1, 4, 2, 3))

    return (y.reshape(batch, seq, D_MODEL), z.reshape(dec_batch, dec_seq, D_MODEL),
            kv_layout(new_kv[0]), kv_layout(new_kv[1]))
```

```python
import functools

import jax
import jax.numpy as jnp
import numpy as np
from jax import lax
from jax.experimental import pallas as pl
from jax.experimental.pallas import tpu as pltpu

D_MODEL = 1024
DEPTH = 4
GRID_W = 64
HEAD_DIM = 64
N_Q_HEADS = 8
N_KV_HEADS = 2
Q_PER_KV = N_Q_HEADS // N_KV_HEADS
ATTN_W = N_Q_HEADS * HEAD_DIM
KV_W = N_KV_HEADS * HEAD_DIM
POOL_WINDOWS = (2, 4, 8, 16)
POOL_W = D_MODEL // 2
POOL_GW = POOL_W // len(POOL_WINDOWS)
GATE_COL0 = ATTN_W + 2 * KV_W + POOL_W
IN_COLS = GATE_COL0 + 2 * D_MODEL
FFN_HIDDEN = 2816
N_MOD = 6
ROPE_THETA = 10000.0
EPS = 1e-6

LANES = 128
Q_TILE = 256
VT_ROWS = 80
POOL_PAD = 24
LOG2_E = 1.4426950408889634
MOD_ROWS = 8
CTX_MOD_ROW = 4
VMEM_LIMIT_BYTES = 56 * 1024 * 1024

F32 = jnp.float32
BF16 = jnp.bfloat16


def _dot(a, b):
    return jnp.dot(a, b, preferred_element_type=F32)


def _dot_nt(a, b):
    return lax.dot_general(a, b, (((1,), (1,)), ((), ())), preferred_element_type=F32)


def _rms(x):
    return x * lax.rsqrt(jnp.mean(x * x, axis=-1, keepdims=True) + EPS)


def _params(n_axes=1):
    return pltpu.CompilerParams(dimension_semantics=("arbitrary",) * n_axes,
                                vmem_limit_bytes=VMEM_LIMIT_BYTES)


def _ada_rows(c_ref, w_ref, b_ref):
    c = c_ref[...]
    s = (c * jax.nn.sigmoid(c)).astype(BF16)
    return _dot(s, w_ref[...].astype(BF16)) + b_ref[...]


def _ada_kernel(c_ref, w_ref, b_ref, o_ref):
    o_ref[...] = _ada_rows(c_ref, w_ref, b_ref)


def _ada_call(c_rows, w_ada, b_ada3):
    tn = 1536
    n_cols = N_MOD * D_MODEL
    return pl.pallas_call(
        _ada_kernel,
        grid=(n_cols // tn,),
        in_specs=[
            pl.BlockSpec((MOD_ROWS, D_MODEL), lambda j: (0, 0)),
            pl.BlockSpec((None, D_MODEL, tn), lambda j: (0, 0, j)),
            pl.BlockSpec((None, 1, tn), lambda j: (0, 0, j)),
        ],
        out_specs=pl.BlockSpec((MOD_ROWS, tn), lambda j: (0, j)),
        out_shape=jax.ShapeDtypeStruct((MOD_ROWS, n_cols), F32),
        compiler_params=_params(),
        name="ada_rows",
    )(c_rows, w_ada, b_ada3)


def _pool_group(src, window, seq_len, p2_ref, p4_ref):
    half = window // 2
    rows = lambda ref, off: ref[POOL_PAD + off:POOL_PAD + off + seq_len, :]
    if window <= 4:
        total = rows(src, -half)
        for off in range(-half + 1, half):
            total = total + rows(src, off)
    else:
        n2 = seq_len + 2 * POOL_PAD - 8
        p2_ref[0:n2, :] = src[0:n2, :] + src[1:n2 + 1, :]
        if window == 8:
            terms = [rows(p2_ref, off) for off in (-4, -2, 0, 2)]
        else:
            assert window == 16
            n4 = n2 - 8
            p4_ref[0:n4, :] = p2_ref[0:n4, :] + p2_ref[2:n4 + 2, :]
            terms = [rows(p4_ref, off) for off in (-8, -4, 0, 4)]
        total = (terms[0] + terms[1]) + (terms[2] + terms[3])
    t = lax.broadcasted_iota(jnp.int32, (seq_len, POOL_GW), 0)
    count = jnp.minimum(t + half, seq_len) - jnp.maximum(t - half, 0)
    return total / count.astype(F32) - rows(src, 0)


def _mix_kernel(x_ref, mod_ref, npre_ref, w_ref, qg_ref, kg_ref, cos_ref, sin_ref, wpool_ref,
                pscale_ref, *rest, seq_len, n_seq, n_cache, rope, n_cast, cast_layer, kv_first_layer,
                n_kv_alias, heads_per_step):
    if n_cache:
        (ck_ref, cv_ref), rest = rest[:2], rest[2:]
    w32_refs, rest = rest[:n_cast], rest[n_cast:]
    rest = rest[n_kv_alias:]
    if n_cache:
        (ao_out, y_out), rest = rest[:2], rest[2:]
    else:
        (ao_out, y_out, nk_out, nv_out), rest = rest[:4], rest[4:]
    wb_outs, rest = rest[:n_cast], rest[n_cast:]
    (qa_ref, k_ref, vt_ref, p_ref, raw_a, raw_b, s_a, s_b, e_a, e_b, p2_a, p2_b, p4_b), rest = (
        rest[:13], rest[13:])
    raw_ref, s_ref, e_ref = (raw_a, raw_b), (s_a, s_b), (e_a, e_b)

    if n_cast:
        stage_in, stage_out, sems = rest[0:n_cast], rest[n_cast:2 * n_cast], rest[2 * n_cast]
        grid_step = pl.program_id(0)
        last_grid_step = pl.num_programs(0) - 1

        def copy_in(j, at_step):
            n = stage_in[j].shape[0]
            row0 = cast_layer * (w32_refs[j].shape[0] // DEPTH) + at_step * n
            return pltpu.make_async_copy(w32_refs[j].at[pl.ds(row0, n), :], stage_in[j], sems.at[j])

        def copy_out(j, at_step):
            n = stage_out[j].shape[0]
            return pltpu.make_async_copy(stage_out[j], wb_outs[j].at[pl.ds(at_step * n, n), :],
                                         sems.at[n_cast + j])

        for j in range(n_cast):
            copy_in(j, grid_step).start()
    n_keys = n_cache + seq_len
    tiles_per_seq = seq_len // Q_TILE
    n_tiles = n_seq * tiles_per_seq

    mod_row = pl.program_id(0) if n_cache else CTX_MOD_ROW
    sh1 = mod_ref[pl.ds(mod_row, 1), 0:D_MODEL]
    norm_gain = npre_ref[...] * (1.0 + mod_ref[pl.ds(mod_row, 1), D_MODEL:2 * D_MODEL])

    lane = lax.broadcasted_iota(jnp.int32, (Q_TILE, LANES), 1)
    first_head = lane < HEAD_DIM
    second_half = (lane & (HEAD_DIM // 2)) != 0

    def head_norm(t, gain):
        t2 = t * t
        s_lo = jnp.sum(jnp.where(first_head, t2, 0.0), axis=-1, keepdims=True)
        s_hi = jnp.sum(jnp.where(first_head, 0.0, t2), axis=-1, keepdims=True)
        ms = jnp.where(first_head, s_lo, s_hi) * (1.0 / HEAD_DIM)
        return (t * lax.rsqrt(ms + EPS)) * gain

    def rotary(t, pos0):
        partner = jnp.where(second_half, pltpu.roll(t, HEAD_DIM // 2, 1),
                            pltpu.roll(t, LANES - HEAD_DIM // 2, 1))
        return t * cos_ref[pos0:pos0 + Q_TILE, :] + partner * sin_ref[pos0:pos0 + Q_TILE, :]

    ones_rows = jnp.ones((VT_ROWS - HEAD_DIM, n_keys), BF16)
    pad_rows = jnp.zeros((POOL_PAD, POOL_GW), F32)
    for s_idx in range(n_seq):
        for g in range(len(POOL_WINDOWS)):
            p_ref[s_idx, g, 0:POOL_PAD, :] = pad_rows
            p_ref[s_idx, g, POOL_PAD + seq_len:2 * POOL_PAD + seq_len, :] = pad_rows
        for kv in range(N_KV_HEADS):
            vt_ref[s_idx, kv * VT_ROWS + HEAD_DIM:(kv + 1) * VT_ROWS, :] = ones_rows
        if n_cache:
            k_ref[s_idx, 0:n_cache, :] = ck_ref[0, 0].astype(BF16)
            cvt = cv_ref[0, 0].T.astype(BF16)
            for kv in range(N_KV_HEADS):
                vt_ref[s_idx, kv * VT_ROWS:kv * VT_ROWS + HEAD_DIM, 0:n_cache] = (
                    cvt[kv * HEAD_DIM:(kv + 1) * HEAD_DIM, :])

    def project(tile, slot):
        r0 = tile * Q_TILE
        x = x_ref[r0:r0 + Q_TILE, :]
        hb = (_rms(x) * norm_gain + sh1).astype(BF16)
        raw_ref[slot][...] = _dot(hb, w_ref[...])

    def epilogue(tile, slot):
        s_idx, pos0 = tile // tiles_per_seq, (tile % tiles_per_seq) * Q_TILE
        r0 = tile * Q_TILE
        raw = raw_ref[slot]
        for j in range(ATTN_W // LANES):
            t = head_norm(raw[:, j * LANES:(j + 1) * LANES], qg_ref[...])
            if rope:
                t = rotary(t, pos0)
            t = t * (HEAD_DIM ** -0.5 * LOG2_E)
            swapped = pltpu.roll(t, HEAD_DIM, 1)
            if (2 * j) // Q_PER_KV == 0:
                even = jnp.where(first_head, t, 0.0)
                odd = jnp.where(first_head, swapped, 0.0)
            else:
                even = jnp.where(first_head, 0.0, swapped)
                odd = jnp.where(first_head, 0.0, t)
            qa_ref[tile, (2 * j) * Q_TILE:(2 * j + 1) * Q_TILE, :] = even.astype(BF16)
            qa_ref[tile, (2 * j + 1) * Q_TILE:(2 * j + 2) * Q_TILE, :] = odd.astype(BF16)
        kn = head_norm(raw[:, ATTN_W:ATTN_W + KV_W], kg_ref[...])
        v = raw[:, ATTN_W + KV_W:ATTN_W + 2 * KV_W]
        vt_f32 = v.T
        if not n_cache:
            if kv_first_layer:
                nk_out[s_idx, 0, :, pos0:pos0 + Q_TILE] = kn.T
                nv_out[s_idx, 0, :, pos0:pos0 + Q_TILE] = vt_f32
                later = jnp.zeros((DEPTH - 1, KV_W, Q_TILE), F32)
                nk_out[s_idx, 1:DEPTH, :, pos0:pos0 + Q_TILE] = later
                nv_out[s_idx, 1:DEPTH, :, pos0:pos0 + Q_TILE] = later
            else:
                nk_out[s_idx, :, pos0:pos0 + Q_TILE] = kn.T
                nv_out[s_idx, :, pos0:pos0 + Q_TILE] = vt_f32
        if rope:
            kn = rotary(kn, pos0)
        k0 = n_cache + pos0
        k_ref[s_idx, k0:k0 + Q_TILE, :] = kn.astype(BF16)
        vt = vt_f32.astype(BF16)
        for kv in range(N_KV_HEADS):
            vt_ref[s_idx, kv * VT_ROWS:kv * VT_ROWS + HEAD_DIM, k0:k0 + Q_TILE] = (
                vt[kv * HEAD_DIM:(kv + 1) * HEAD_DIM, :])
        for g in range(len(POOL_WINDOWS)):
            c0 = ATTN_W + 2 * KV_W + g * POOL_GW
            p_ref[s_idx, g, POOL_PAD + pos0:POOL_PAD + pos0 + Q_TILE, :] = raw[:, c0:c0 + POOL_GW]

    project(0, 0)
    for tile in range(n_tiles):
        if tile + 1 < n_tiles:
            project(tile + 1, (tile + 1) % 2)
        epilogue(tile, tile % 2)

    n_groups = N_Q_HEADS // heads_per_step
    n_steps = n_tiles * n_groups

    def scores(step, slot):
        tile, grp = step // n_groups, step % n_groups
        qs = qa_ref[tile, grp * heads_per_step * Q_TILE:(grp + 1) * heads_per_step * Q_TILE, :]
        s_ref[slot][...] = _dot_nt(k_ref[tile // tiles_per_seq], qs)

    def exponent(step, slot):
        s = s_ref[slot][...]
        e_ref[slot][...] = jnp.exp2(s - jnp.max(s, axis=0, keepdims=True)).astype(BF16)

    def values(step, slot):
        tile, grp = step // n_groups, step % n_groups
        head0 = grp * heads_per_step
        for kv in range(head0 // Q_PER_KV, (head0 + heads_per_step - 1) // Q_PER_KV + 1):
            lo = max(kv * Q_PER_KV, head0) - head0
            hi = min((kv + 1) * Q_PER_KV, head0 + heads_per_step) - head0
            vt = vt_ref[tile // tiles_per_seq, kv * VT_ROWS:(kv + 1) * VT_ROWS, :]
            ot = _dot(vt, e_ref[slot][:, lo * Q_TILE:hi * Q_TILE])
            ot = ot[0:HEAD_DIM] * (1.0 / ot[HEAD_DIM:HEAD_DIM + 1])
            for j in range(0, hi - lo, 2):
                pair = (head0 + lo + j) // 2
                both = jnp.concatenate([ot[:, j * Q_TILE:(j + 1) * Q_TILE],
                                        ot[:, (j + 1) * Q_TILE:(j + 2) * Q_TILE]], axis=0)
                ao_out[tile * Q_TILE:(tile + 1) * Q_TILE, pair * LANES:(pair + 1) * LANES] = (
                    both.T.astype(BF16))

    scores(0, 0)
    for step in range(n_steps + 1):
        if step + 1 < n_steps:
            scores(step + 1, (step + 1) % 2)
        if step < n_steps:
            exponent(step, step % 2)
        if step >= 1:
            values(step - 1, (step - 1) % 2)

    for s_idx in range(n_seq):
        r0 = s_idx * seq_len
        for g, window in enumerate(POOL_WINDOWS):
            cols = slice(g * POOL_GW, (g + 1) * POOL_GW)
            p2_ref = p2_a if window == 8 else p2_b
            d = _pool_group(p_ref.at[s_idx, g], window, seq_len, p2_ref, p4_b)
            yg = _dot(d.astype(BF16), wpool_ref[g]) * pscale_ref[:, cols]
            y_out[r0:r0 + seq_len, cols] = yg.astype(BF16)

    if n_cast:
        @pl.when(grid_step > 0)
        def _():
            for j in range(n_cast):
                copy_out(j, grid_step - 1).wait()

        for j in range(n_cast):
            copy_in(j, grid_step).wait()
            stage_out[j][...] = stage_in[j][...].astype(BF16)
            copy_out(j, grid_step).start()

        @pl.when(grid_step == last_grid_step)
        def _():
            for j in range(n_cast):
                copy_out(j, grid_step).wait()


def _mix_call(l, x, mod_rows, n_pre, w_in_l, q_gain, k_gain, cos_t, sin_t, w_pool, pool_scale,
              cache_k, cache_v, cast_weights_f32, cast_layer, new_kv=None, *, seq_len, n_seq, rope):
    n_tok = x.shape[0]
    rows = n_seq * seq_len
    is_lat = cache_k is not None
    n_cache = cache_k.shape[2] if is_lat else 0
    n_keys = n_cache + seq_len
    assert n_seq == 1 or not is_lat
    lay = lambda i: (l, 0, 0)
    row = lambda i: (i, 0)
    in_specs = [
        pl.BlockSpec((rows, D_MODEL), row),
        pl.BlockSpec((MOD_ROWS, N_MOD * D_MODEL), lambda i: (0, 0)),
        pl.BlockSpec((None, 1, D_MODEL), lay),
        pl.BlockSpec((D_MODEL, GATE_COL0), lambda i: (0, 0)),
        pl.BlockSpec((None, 1, LANES), lay),
        pl.BlockSpec((None, 1, LANES), lay),
        pl.BlockSpec((seq_len, LANES), lambda i: (0, 0)),
        pl.BlockSpec((seq_len, LANES), lambda i: (0, 0)),
        pl.BlockSpec((None, len(POOL_WINDOWS), POOL_GW, POOL_GW), lambda i: (l, 0, 0, 0)),
        pl.BlockSpec((None, 1, POOL_W), lay),
    ]
    args = [x, mod_rows, n_pre, w_in_l, q_gain, k_gain, cos_t, sin_t, w_pool, pool_scale]
    out_shape = [jax.ShapeDtypeStruct((n_tok, ATTN_W), BF16), jax.ShapeDtypeStruct((n_tok, POOL_W), BF16)]
    out_specs = [pl.BlockSpec((rows, ATTN_W), row), pl.BlockSpec((rows, POOL_W), row)]
    if is_lat:
        cache_spec = pl.BlockSpec((1, 1, n_cache, KV_W), lambda i: (i, l, 0, 0))
        in_specs += [cache_spec, cache_spec]
        args += [cache_k, cache_v]
    else:
        out_shape += [jax.ShapeDtypeStruct((n_tok // seq_len, DEPTH, KV_W, seq_len), F32)] * 2
        if new_kv is None:
            out_specs += [pl.BlockSpec((n_seq, DEPTH, KV_W, seq_len), lambda i: (i, 0, 0, 0))] * 2
        else:
            out_specs += [pl.BlockSpec((n_seq, None, KV_W, seq_len), lambda i: (i, l, 0, 0))] * 2
    n_cast = len(cast_weights_f32)
    n_steps = n_tok // rows
    any_spec = pl.BlockSpec(memory_space=pl.ANY)
    in_specs += [any_spec] * n_cast
    args += [w.reshape(-1, w.shape[2]) for w in cast_weights_f32]
    aliases = {}
    if new_kv is not None:
        aliases = {len(args): 2, len(args) + 1: 3}
        in_specs += [any_spec] * 2
        args += list(new_kv)
    out_shape += [jax.ShapeDtypeStruct(w.shape[1:], BF16) for w in cast_weights_f32]
    out_specs += [any_spec] * n_cast
    slices = [(w.shape[1] // n_steps, w.shape[2]) for w in cast_weights_f32]
    assert all(w.shape[1] % n_steps == 0 and r % 16 == 0 for w, (r, _) in zip(cast_weights_f32, slices))
    scratch = [
        pltpu.VMEM((rows // Q_TILE, N_Q_HEADS * Q_TILE, LANES), BF16),
        pltpu.VMEM((n_seq, n_keys, KV_W), BF16),
        pltpu.VMEM((n_seq, N_KV_HEADS * VT_ROWS, n_keys), BF16),
        pltpu.VMEM((n_seq, len(POOL_WINDOWS), seq_len + 2 * POOL_PAD, POOL_GW), F32),
    ]
    scratch += [pltpu.VMEM((Q_TILE, GATE_COL0), F32)] * 2
    heads_per_step = N_Q_HEADS if n_keys <= Q_TILE else 2
    scratch += [pltpu.VMEM((n_keys, heads_per_step * Q_TILE), F32)] * 2
    scratch += [pltpu.VMEM((n_keys, heads_per_step * Q_TILE), BF16)] * 2
    scratch += [pltpu.VMEM((seq_len + 2 * POOL_PAD - 8, POOL_GW), F32)] * 2
    scratch += [pltpu.VMEM((seq_len + 2 * POOL_PAD - 16, POOL_GW), F32)]
    if n_cast:
        scratch += [pltpu.VMEM(s, F32) for s in slices] + [pltpu.VMEM(s, BF16) for s in slices]
        scratch += [pltpu.SemaphoreType.DMA((2 * n_cast,))]
    return pl.pallas_call(
        functools.partial(_mix_kernel, seq_len=seq_len, n_seq=n_seq, n_cache=n_cache, rope=rope,
                          n_cast=n_cast, cast_layer=cast_layer,
                          kv_first_layer=not is_lat and new_kv is None, n_kv_alias=len(aliases),
                          heads_per_step=heads_per_step),
        grid=(n_tok // rows,),
        in_specs=in_specs,
        out_specs=out_specs,
        out_shape=out_shape,
        scratch_shapes=scratch,
        input_output_aliases=aliases,
        compiler_params=_params(),
        name="mix_lat" if is_lat else "mix_ctx",
    )(*args)


def _tail_kernel(x_ref, ao_ref, y_ref, mod_ref, npre_mix_ref, npost_mix_ref, npre_ffn_ref,
                 npost_ffn_ref, win_ref, wau_ref, wpu_ref, wout_ref, wg_ref, wu_ref, wd_ref, *rest,
                 sub, tiles_per_mod_row, next_rows):
    if next_rows:
        c_ref, wada_ref, bada_ref, o_ref, next_mod_out, x1_a, x1_b = rest
        next_mod_out[...] = _ada_rows(c_ref, wada_ref, bada_ref)
    else:
        o_ref, x1_a, x1_b = rest
    mod_row = pl.program_id(0) // tiles_per_mod_row if tiles_per_mod_row else CTX_MOD_ROW
    sh1, sc1, g1, sh2, sc2, g2 = [mod_ref[pl.ds(mod_row, 1), i * D_MODEL:(i + 1) * D_MODEL]
                                  for i in range(N_MOD)]
    x1_ref = (x1_a, x1_b)
    n_sub = x_ref.shape[0] // sub

    def post(i, slot):
        r0 = i * sub
        x = x_ref[r0:r0 + sub, :]
        hb = ((_rms(x) * npre_mix_ref[...]) * (1.0 + sc1) + sh1).astype(BF16)
        g_attn = jax.nn.sigmoid(_dot(hb, win_ref[:, GATE_COL0:GATE_COL0 + D_MODEL]))
        g_pool = jax.nn.sigmoid(_dot(hb, win_ref[:, GATE_COL0 + D_MODEL:IN_COLS]))
        attn = _dot(ao_ref[r0:r0 + sub, :], wau_ref[...])
        pool = _dot(y_ref[r0:r0 + sub, :], wpu_ref[...])
        mixed = g_attn * attn + g_pool * pool
        m = _dot(mixed.astype(BF16), wout_ref[...])
        x1_ref[slot][...] = x + g1 * (_rms(m) * npost_mix_ref[...])

    def ffn(i, slot):
        r0 = i * sub
        x1 = x1_ref[slot][...]
        hb = ((_rms(x1) * npre_ffn_ref[...]) * (1.0 + sc2) + sh2).astype(BF16)
        g = _dot(hb, wg_ref[...])
        u = _dot(hb, wu_ref[...])
        a = ((g * jax.nn.sigmoid(g)) * u).astype(BF16)
        f = _dot(a, wd_ref[...])
        o_ref[r0:r0 + sub, :] = x1 + g2 * (_rms(f) * npost_ffn_ref[...])

    post(0, 0)
    for i in range(n_sub):
        if i + 1 < n_sub:
            post(i + 1, (i + 1) % 2)
        ffn(i, i % 2)


def _tail_call(l, x, ao, y, mod_l, n_pre_mix, n_post_mix, n_pre_ffn, n_post_ffn, w_in,
               w_attn_up, w_pool_up, w_out, w_ffn_gate, w_ffn_up, w_ffn_down, next_mod_inputs,
               *, seq_len, per_seq_mod, tm, sub):
    n_tok = x.shape[0]
    n_tiles = n_tok // tm
    lay = lambda i: (l, 0, 0)
    row = lambda i: (i, 0)
    vec_spec = pl.BlockSpec((None, 1, D_MODEL), lay)
    layer_resident = lambda r, c: pl.BlockSpec((r, c), lambda i: (0, 0), pipeline_mode=pl.Buffered(1))
    in_specs = [
        pl.BlockSpec((tm, D_MODEL), row),
        pl.BlockSpec((tm, ATTN_W), row),
        pl.BlockSpec((tm, POOL_W), row),
        pl.BlockSpec((MOD_ROWS, N_MOD * D_MODEL), lambda i: (0, 0)),
        vec_spec, vec_spec, vec_spec, vec_spec,
        layer_resident(D_MODEL, IN_COLS),
        layer_resident(ATTN_W, D_MODEL),
        layer_resident(POOL_W, D_MODEL),
        layer_resident(D_MODEL, D_MODEL),
        layer_resident(D_MODEL, FFN_HIDDEN),
        layer_resident(D_MODEL, FFN_HIDDEN),
        layer_resident(FFN_HIDDEN, D_MODEL),
    ]
    args = [x, ao, y, mod_l, n_pre_mix, n_post_mix, n_pre_ffn, n_post_ffn, w_in, w_attn_up,
            w_pool_up, w_out, w_ffn_gate, w_ffn_up, w_ffn_down]
    out_specs = [pl.BlockSpec((tm, D_MODEL), row)]
    out_shape = [jax.ShapeDtypeStruct((n_tok, D_MODEL), F32)]
    if next_mod_inputs is not None:
        n_cols = N_MOD * D_MODEL
        tn = n_cols // n_tiles
        assert n_cols % n_tiles == 0 and tn % LANES == 0
        in_specs += [
            pl.BlockSpec((MOD_ROWS, D_MODEL), lambda i: (0, 0)),
            pl.BlockSpec((None, D_MODEL, tn), lambda i: (l + 1, 0, i)),
            pl.BlockSpec((None, 1, tn), lambda i: (l + 1, 0, i)),
        ]
        args += list(next_mod_inputs)
        out_specs += [pl.BlockSpec((MOD_ROWS, tn), lambda i: (0, i))]
        out_shape += [jax.ShapeDtypeStruct((MOD_ROWS, n_cols), F32)]
    return pl.pallas_call(
        functools.partial(_tail_kernel, sub=sub, next_rows=next_mod_inputs is not None,
                          tiles_per_mod_row=seq_len // tm if per_seq_mod else 0),
        grid=(n_tiles,),
        in_specs=in_specs,
        out_specs=out_specs,
        out_shape=out_shape,
        scratch_shapes=[pltpu.VMEM((sub, D_MODEL), F32)] * 2,
        compiler_params=_params(),
        name="tail",
    )(*args)


def _rope_tables(n):
    rows = n // GRID_W
    row = np.repeat(np.arange(rows), GRID_W).astype(np.float32)
    col = np.tile(np.arange(GRID_W), rows).astype(np.float32)
    n_freq = HEAD_DIM // 4
    inv = jnp.asarray(ROPE_THETA, F32) ** (-jnp.arange(n_freq, dtype=F32) / n_freq)
    ang = jnp.concatenate([jnp.asarray(row)[:, None] * inv[None, :],
                           jnp.asarray(col)[:, None] * inv[None, :]], axis=-1)
    cos, sin = jnp.cos(ang), jnp.sin(ang)
    cos_t = jnp.tile(cos, (1, LANES // (HEAD_DIM // 2)))
    sin_t = jnp.tile(jnp.concatenate([-sin, sin], axis=-1), (1, LANES // HEAD_DIM))
    return cos_t, sin_t


def kernel(x_prompt, x_sample, cache_k, cache_v, c, c_ctx, w_ada, b_ada, w_in, q_norm, k_norm,
           w_attn_up, w_pool, pool_scale, w_pool_up, w_out, n_pre_mix, n_post_mix, n_pre_ffn,
           n_post_ffn, w_ffn_gate, w_ffn_up, w_ffn_down):
    batch, seq, _ = x_prompt.shape
    dec_batch, dec_seq, _ = x_sample.shape
    past_len = cache_k.shape[2]
    assert dec_batch <= CTX_MOD_ROW

    c_rows = jnp.concatenate(
        [c, c_ctx[None, :], jnp.zeros((MOD_ROWS - dec_batch - 1, D_MODEL), F32)], axis=0)
    b_ada3 = b_ada.reshape(DEPTH, 1, N_MOD * D_MODEL)
    mod_l = _ada_call(c_rows, w_ada, b_ada3)

    cos_t, sin_t = _rope_tables(dec_seq)
    vec = lambda a: a.reshape(DEPTH, 1, a.shape[-1])
    q_gain = vec(jnp.tile(q_norm, (1, LANES // HEAD_DIM)))
    k_gain = vec(jnp.tile(k_norm, (1, LANES // HEAD_DIM)))
    w_in_l = w_in[0].astype(BF16)
    w_pool_b = w_pool.astype(BF16)
    tail_weights_f32 = (w_attn_up, w_pool_up, w_out, w_ffn_gate, w_ffn_up, w_ffn_down)
    n_pre_mix_v, n_post_mix_v = vec(n_pre_mix), vec(n_post_mix)
    n_pre_ffn_v, n_post_ffn_v = vec(n_pre_ffn), vec(n_post_ffn)
    pool_scale_v = vec(pool_scale)
    ck = cache_k.reshape(dec_batch, DEPTH, past_len, KV_W)
    cv = cache_v.reshape(dec_batch, DEPTH, past_len, KV_W)

    tm = 512
    ctx_seqs_per_step = 4

    y = x_prompt.reshape(batch * seq, D_MODEL)
    z = x_sample.reshape(dec_batch * dec_seq, D_MODEL)
    new_kv = None
    for l in range(DEPTH):
        has_next = l + 1 < DEPTH
        outs = _mix_call(l, y, mod_l, n_pre_mix_v, w_in_l, q_gain, k_gain, cos_t, sin_t, w_pool_b,
                         pool_scale_v, None, None, tail_weights_f32, l, new_kv,
                         seq_len=seq, n_seq=ctx_seqs_per_step, rope=False)
        ao, yp = outs[:2]
        new_kv = outs[2:4]
        tail_weights_b = outs[4:]
        tail_outs = _tail_call(l, y, ao, yp, mod_l, n_pre_mix_v, n_post_mix_v, n_pre_ffn_v, n_post_ffn_v,
                               w_in_l, *tail_weights_b, (c_rows, w_ada, b_ada3) if has_next else None,
                               seq_len=seq, per_seq_mod=False, tm=tm, sub=256)
        y = tail_outs[0]
        outs = _mix_call(l, z, mod_l, n_pre_mix_v, w_in_l, q_gain, k_gain, cos_t, sin_t, w_pool_b,
                         pool_scale_v, ck, cv, (w_in,) if has_next else (), l + 1,
                         seq_len=dec_seq, n_seq=1, rope=True)
        ao, yp = outs[:2]
        z = _tail_call(l, z, ao, yp, mod_l, n_pre_mix_v, n_post_mix_v, n_pre_ffn_v, n_post_ffn_v,
                       w_in_l, *tail_weights_b, None,
                       seq_len=dec_seq, per_seq_mod=True, tm=tm, sub=256)[0]
        if has_next:
            w_in_l = outs[2]
            mod_l = tail_outs[1]

    def kv_layout(a):
        a = a.reshape(batch, DEPTH, N_KV_HEADS, HEAD_DIM, seq)
        return jnp.transpose(a, (0, 1, 4, 2, 3))

    return (y.reshape(batch, seq, D_MODEL), z.reshape(dec_batch, dec_seq, D_MODEL),
            kv_layout(new_kv[0]), kv_layout(new_kv[1]))
```

```python
import functools

import jax
import jax.numpy as jnp
import numpy as np
from jax import lax
from jax.experimental import pallas as pl
from jax.experimental.pallas import tpu as pltpu

D_MODEL = 1024
DEPTH = 4
GRID_W = 64
HEAD_DIM = 64
N_Q_HEADS = 8
N_KV_HEADS = 2
Q_PER_KV = N_Q_HEADS // N_KV_HEADS
ATTN_W = N_Q_HEADS * HEAD_DIM
KV_W = N_KV_HEADS * HEAD_DIM
POOL_WINDOWS = (2, 4, 8, 16)
POOL_W = D_MODEL // 2
POOL_GW = POOL_W // len(POOL_WINDOWS)
GATE_COL0 = ATTN_W + 2 * KV_W + POOL_W
IN_COLS = GATE_COL0 + 2 * D_MODEL
FFN_HIDDEN = 2816
N_MOD = 6
ROPE_THETA = 10000.0
EPS = 1e-6

LANES = 128
Q_TILE = 256
VT_ROWS = 80
POOL_PAD = 24
LOG2_E = 1.4426950408889634
MOD_ROWS = 8
CTX_MOD_ROW = 4
VMEM_LIMIT_BYTES = 56 * 1024 * 1024

F32 = jnp.float32
BF16 = jnp.bfloat16


def _dot(a, b):
    return jnp.dot(a, b, preferred_element_type=F32)


def _dot_nt(a, b):
    return lax.dot_general(a, b, (((1,), (1,)), ((), ())), preferred_element_type=F32)


def _rms(x):
    return x * lax.rsqrt(jnp.mean(x * x, axis=-1, keepdims=True) + EPS)


def _params(n_axes=1):
    return pltpu.CompilerParams(dimension_semantics=("arbitrary",) * n_axes,
                                vmem_limit_bytes=VMEM_LIMIT_BYTES)


def _ada_rows(c_ref, w_ref, b_ref):
    c = c_ref[...]
    s = (c * jax.nn.sigmoid(c)).astype(BF16)
    return _dot(s, w_ref[...].astype(BF16)) + b_ref[...]


def _ada_kernel(c_ref, w_ref, b_ref, o_ref):
    o_ref[...] = _ada_rows(c_ref, w_ref, b_ref)


def _ada_call(c_rows, w_ada, b_ada3):
    tn = 1536
    n_cols = N_MOD * D_MODEL
    return pl.pallas_call(
        _ada_kernel,
        grid=(n_cols // tn,),
        in_specs=[
            pl.BlockSpec((MOD_ROWS, D_MODEL), lambda j: (0, 0)),
            pl.BlockSpec((None, D_MODEL, tn), lambda j: (0, 0, j)),
            pl.BlockSpec((None, 1, tn), lambda j: (0, 0, j)),
        ],
        out_specs=pl.BlockSpec((MOD_ROWS, tn), lambda j: (0, j)),
        out_shape=jax.ShapeDtypeStruct((MOD_ROWS, n_cols), F32),
        compiler_params=_params(),
        name="ada_rows",
    )(c_rows, w_ada, b_ada3)


def _pool_group(src, window, seq_len, p2_ref, p4_ref):
    half = window // 2
    rows = lambda ref, off: ref[POOL_PAD + off:POOL_PAD + off + seq_len, :]
    if window <= 4:
        total = rows(src, -half)
        for off in range(-half + 1, half):
            total = total + rows(src, off)
    else:
        n2 = seq_len + 2 * POOL_PAD - 8
        p2_ref[0:n2, :] = src[0:n2, :] + src[1:n2 + 1, :]
        if window == 8:
            terms = [rows(p2_ref, off) for off in (-4, -2, 0, 2)]
        else:
            assert window == 16
            n4 = n2 - 8
            p4_ref[0:n4, :] = p2_ref[0:n4, :] + p2_ref[2:n4 + 2, :]
            terms = [rows(p4_ref, off) for off in (-8, -4, 0, 4)]
        total = (terms[0] + terms[1]) + (terms[2] + terms[3])
    t = lax.broadcasted_iota(jnp.int32, (seq_len, POOL_GW), 0)
    count = jnp.minimum(t + half, seq_len) - jnp.maximum(t - half, 0)
    return total / count.astype(F32) - rows(src, 0)


def _mix_kernel(x_ref, mod_ref, npre_ref, w_ref, qg_ref, kg_ref, cos_ref, sin_ref, wpool_ref,
                pscale_ref, *rest, seq_len, n_seq, n_cache, rope, n_cast, cast_layer, kv_first_layer,
                n_kv_alias, heads_per_step):
    if n_cache:
        (ck_ref, cv_ref), rest = rest[:2], rest[2:]
    w32_refs, rest = rest[:n_cast], rest[n_cast:]
    rest = rest[n_kv_alias:]
    if n_cache:
        (ao_out, y_out), rest = rest[:2], rest[2:]
    else:
        (ao_out, y_out, nk_out, nv_out), rest = rest[:4], rest[4:]
    wb_outs, rest = rest[:n_cast], rest[n_cast:]
    (qa_ref, k_ref, vt_ref, p_ref, raw_a, raw_b, s_a, s_b, e_a, e_b, p2_a, p2_b, p4_b), rest = (
        rest[:13], rest[13:])
    raw_ref, s_ref, e_ref = (raw_a, raw_b), (s_a, s_b), (e_a, e_b)

    if n_cast:
        stage_in, stage_out, sems = rest[0:n_cast], rest[n_cast:2 * n_cast], rest[2 * n_cast]
        grid_step = pl.program_id(0)
        last_grid_step = pl.num_programs(0) - 1

        def copy_in(j, at_step):
            n = stage_in[j].shape[0]
            row0 = cast_layer * (w32_refs[j].shape[0] // DEPTH) + at_step * n
            return pltpu.make_async_copy(w32_refs[j].at[pl.ds(row0, n), :], stage_in[j], sems.at[j])

        def copy_out(j, at_step):
            n = stage_out[j].shape[0]
            return pltpu.make_async_copy(stage_out[j], wb_outs[j].at[pl.ds(at_step * n, n), :],
                                         sems.at[n_cast + j])

        for j in range(n_cast):
            copy_in(j, grid_step).start()
    n_keys = n_cache + seq_len
    tiles_per_seq = seq_len // Q_TILE
    n_tiles = n_seq * tiles_per_seq

    mod_row = pl.program_id(0) if n_cache else CTX_MOD_ROW
    sh1 = mod_ref[pl.ds(mod_row, 1), 0:D_MODEL]
    norm_gain = npre_ref[...] * (1.0 + mod_ref[pl.ds(mod_row, 1), D_MODEL:2 * D_MODEL])

    lane = lax.broadcasted_iota(jnp.int32, (Q_TILE, LANES), 1)
    first_head = lane < HEAD_DIM
    second_half = (lane & (HEAD_DIM // 2)) != 0

    def head_norm(t, gain):
        t2 = t * t
        s_lo = jnp.sum(jnp.where(first_head, t2, 0.0), axis=-1, keepdims=True)
        s_hi = jnp.sum(jnp.where(first_head, 0.0, t2), axis=-1, keepdims=True)
        ms = jnp.where(first_head, s_lo, s_hi) * (1.0 / HEAD_DIM)
        return (t * lax.rsqrt(ms + EPS)) * gain

    def rotary(t, pos0):
        partner = jnp.where(second_half, pltpu.roll(t, HEAD_DIM // 2, 1),
                            pltpu.roll(t, LANES - HEAD_DIM // 2, 1))
        return t * cos_ref[pos0:pos0 + Q_TILE, :] + partner * sin_ref[pos0:pos0 + Q_TILE, :]

    ones_rows = jnp.ones((VT_ROWS - HEAD_DIM, n_keys), BF16)
    pad_rows = jnp.zeros((POOL_PAD, POOL_GW), F32)
    for s_idx in range(n_seq):
        for g in range(len(POOL_WINDOWS)):
            p_ref[s_idx, g, 0:POOL_PAD, :] = pad_rows
            p_ref[s_idx, g, POOL_PAD + seq_len:2 * POOL_PAD + seq_len, :] = pad_rows
        for kv in range(N_KV_HEADS):
            vt_ref[s_idx, kv * VT_ROWS + HEAD_DIM:(kv + 1) * VT_ROWS, :] = ones_rows
        if n_cache:
            k_ref[s_idx, 0:n_cache, :] = ck_ref[0, 0].astype(BF16)
            cvt = cv_ref[0, 0].T.astype(BF16)
            for kv in range(N_KV_HEADS):
                vt_ref[s_idx, kv * VT_ROWS:kv * VT_ROWS + HEAD_DIM, 0:n_cache] = (
                    cvt[kv * HEAD_DIM:(kv + 1) * HEAD_DIM, :])

    def project(tile, slot):
        r0 = tile * Q_TILE
        x = x_ref[r0:r0 + Q_TILE, :]
        hb = (_rms(x) * norm_gain + sh1).astype(BF16)
        raw_ref[slot][...] = _dot(hb, w_ref[...])

    def epilogue(tile, slot):
        s_idx, pos0 = tile // tiles_per_seq, (tile % tiles_per_seq) * Q_TILE
        r0 = tile * Q_TILE
        raw = raw_ref[slot]
        for j in range(ATTN_W // LANES):
            t = head_norm(raw[:, j * LANES:(j + 1) * LANES], qg_ref[...])
            if rope:
                t = rotary(t, pos0)
            t = t * (HEAD_DIM ** -0.5 * LOG2_E)
            swapped = pltpu.roll(t, HEAD_DIM, 1)
            if (2 * j) // Q_PER_KV == 0:
                even = jnp.where(first_head, t, 0.0)
                odd = jnp.where(first_head, swapped, 0.0)
            else:
                even = jnp.where(first_head, 0.0, swapped)
                odd = jnp.where(first_head, 0.0, t)
            qa_ref[tile, (2 * j) * Q_TILE:(2 * j + 1) * Q_TILE, :] = even.astype(BF16)
            qa_ref[tile, (2 * j + 1) * Q_TILE:(2 * j + 2) * Q_TILE, :] = odd.astype(BF16)
        kn = head_norm(raw[:, ATTN_W:ATTN_W + KV_W], kg_ref[...])
        v = raw[:, ATTN_W + KV_W:ATTN_W + 2 * KV_W]
        vt_f32 = v.T
        if not n_cache:
            if kv_first_layer:
                nk_out[s_idx, 0, :, pos0:pos0 + Q_TILE] = kn.T
                nv_out[s_idx, 0, :, pos0:pos0 + Q_TILE] = vt_f32
                later = jnp.zeros((DEPTH - 1, KV_W, Q_TILE), F32)
                nk_out[s_idx, 1:DEPTH, :, pos0:pos0 + Q_TILE] = later
                nv_out[s_idx, 1:DEPTH, :, pos0:pos0 + Q_TILE] = later
            else:
                nk_out[s_idx, :, pos0:pos0 + Q_TILE] = kn.T
                nv_out[s_idx, :, pos0:pos0 + Q_TILE] = vt_f32
        if rope:
            kn = rotary(kn, pos0)
        k0 = n_cache + pos0
        k_ref[s_idx, k0:k0 + Q_TILE, :] = kn.astype(BF16)
        vt = vt_f32.astype(BF16)
        for kv in range(N_KV_HEADS):
            vt_ref[s_idx, kv * VT_ROWS:kv * VT_ROWS + HEAD_DIM, k0:k0 + Q_TILE] = (
                vt[kv * HEAD_DIM:(kv + 1) * HEAD_DIM, :])
        for g in range(len(POOL_WINDOWS)):
            c0 = ATTN_W + 2 * KV_W + g * POOL_GW
            p_ref[s_idx, g, POOL_PAD + pos0:POOL_PAD + pos0 + Q_TILE, :] = raw[:, c0:c0 + POOL_GW]

    project(0, 0)
    for tile in range(n_tiles):
        if tile + 1 < n_tiles:
            project(tile + 1, (tile + 1) % 2)
        epilogue(tile, tile % 2)

    n_groups = N_Q_HEADS // heads_per_step
    n_steps = n_tiles * n_groups

    def scores(step, slot):
        tile, grp = step // n_groups, step % n_groups
        qs = qa_ref[tile, grp * heads_per_step * Q_TILE:(grp + 1) * heads_per_step * Q_TILE, :]
        s_ref[slot][...] = _dot_nt(k_ref[tile // tiles_per_seq], qs)

    def exponent(step, slot):
        s = s_ref[slot][...]
        e_ref[slot][...] = jnp.exp2(s - jnp.max(s, axis=0, keepdims=True)).astype(BF16)

    def values(step, slot):
        tile, grp = step // n_groups, step % n_groups
        head0 = grp * heads_per_step
        for kv in range(head0 // Q_PER_KV, (head0 + heads_per_step - 1) // Q_PER_KV + 1):
            lo = max(kv * Q_PER_KV, head0) - head0
            hi = min((kv + 1) * Q_PER_KV, head0 + heads_per_step) - head0
            vt = vt_ref[tile // tiles_per_seq, kv * VT_ROWS:(kv + 1) * VT_ROWS, :]
            ot = _dot(vt, e_ref[slot][:, lo * Q_TILE:hi * Q_TILE])
            ot = ot[0:HEAD_DIM] * (1.0 / ot[HEAD_DIM:HEAD_DIM + 1])
            for j in range(0, hi - lo, 2):
                pair = (head0 + lo + j) // 2
                both = jnp.concatenate([ot[:, j * Q_TILE:(j + 1) * Q_TILE],
                                        ot[:, (j + 1) * Q_TILE:(j + 2) * Q_TILE]], axis=0)
                ao_out[tile * Q_TILE:(tile + 1) * Q_TILE, pair * LANES:(pair + 1) * LANES] = (
                    both.T.astype(BF16))

    scores(0, 0)
    for step in range(n_steps + 1):
        if step + 1 < n_steps:
            scores(step + 1, (step + 1) % 2)
        if step < n_steps:
            exponent(step, step % 2)
        if step >= 1:
            values(step - 1, (step - 1) % 2)

    for s_idx in range(n_seq):
        r0 = s_idx * seq_len
        for g, window in enumerate(POOL_WINDOWS):
            cols = slice(g * POOL_GW, (g + 1) * POOL_GW)
            p2_ref = p2_a if window == 8 else p2_b
            d = _pool_group(p_ref.at[s_idx, g], window, seq_len, p2_ref, p4_b)
            yg = _dot(d.astype(BF16), wpool_ref[g]) * pscale_ref[:, cols]
            y_out[r0:r0 + seq_len, cols] = yg.astype(BF16)

    if n_cast:
        @pl.when(grid_step > 0)
        def _():
            for j in range(n_cast):
                copy_out(j, grid_step - 1).wait()

        for j in range(n_cast):
            copy_in(j, grid_step).wait()
            stage_out[j][...] = stage_in[j][...].astype(BF16)
            copy_out(j, grid_step).start()

        @pl.when(grid_step == last_grid_step)
        def _():
            for j in range(n_cast):
                copy_out(j, grid_step).wait()


def _mix_call(l, x, mod_rows, n_pre, w_in_l, q_gain, k_gain, cos_t, sin_t, w_pool, pool_scale,
              cache_k, cache_v, cast_weights_f32, cast_layer, new_kv=None, *, seq_len, n_seq, rope):
    n_tok = x.shape[0]
    rows = n_seq * seq_len
    is_lat = cache_k is not None
    n_cache = cache_k.shape[2] if is_lat else 0
    n_keys = n_cache + seq_len
    assert n_seq == 1 or not is_lat
    lay = lambda i: (l, 0, 0)
    row = lambda i: (i, 0)
    in_specs = [
        pl.BlockSpec((rows, D_MODEL), row),
        pl.BlockSpec((MOD_ROWS, N_MOD * D_MODEL), lambda i: (0, 0)),
        pl.BlockSpec((None, 1, D_MODEL), lay),
        pl.BlockSpec((D_MODEL, GATE_COL0), lambda i: (0, 0)),
        pl.BlockSpec((None, 1, LANES), lay),
        pl.BlockSpec((None, 1, LANES), lay),
        pl.BlockSpec((seq_len, LANES), lambda i: (0, 0)),
        pl.BlockSpec((seq_len, LANES), lambda i: (0, 0)),
        pl.BlockSpec((None, len(POOL_WINDOWS), POOL_GW, POOL_GW), lambda i: (l, 0, 0, 0)),
        pl.BlockSpec((None, 1, POOL_W), lay),
    ]
    args = [x, mod_rows, n_pre, w_in_l, q_gain, k_gain, cos_t, sin_t, w_pool, pool_scale]
    out_shape = [jax.ShapeDtypeStruct((n_tok, ATTN_W), BF16), jax.ShapeDtypeStruct((n_tok, POOL_W), BF16)]
    out_specs = [pl.BlockSpec((rows, ATTN_W), row), pl.BlockSpec((rows, POOL_W), row)]
    if is_lat:
        cache_spec = pl.BlockSpec((1, 1, n_cache, KV_W), lambda i: (i, l, 0, 0))
        in_specs += [cache_spec, cache_spec]
        args += [cache_k, cache_v]
    else:
        out_shape += [jax.ShapeDtypeStruct((n_tok // seq_len, DEPTH, KV_W, seq_len), F32)] * 2
        if new_kv is None:
            out_specs += [pl.BlockSpec((n_seq, DEPTH, KV_W, seq_len), lambda i: (i, 0, 0, 0))] * 2
        else:
            out_specs += [pl.BlockSpec((n_seq, None, KV_W, seq_len), lambda i: (i, l, 0, 0))] * 2
    n_cast = len(cast_weights_f32)
    n_steps = n_tok // rows
    any_spec = pl.BlockSpec(memory_space=pl.ANY)
    in_specs += [any_spec] * n_cast
    args += [w.reshape(-1, w.shape[2]) for w in cast_weights_f32]
    aliases = {}
    if new_kv is not None:
        aliases = {len(args): 2, len(args) + 1: 3}
        in_specs += [any_spec] * 2
        args += list(new_kv)
    out_shape += [jax.ShapeDtypeStruct(w.shape[1:], BF16) for w in cast_weights_f32]
    out_specs += [any_spec] * n_cast
    slices = [(w.shape[1] // n_steps, w.shape[2]) for w in cast_weights_f32]
    assert all(w.shape[1] % n_steps == 0 and r % 16 == 0 for w, (r, _) in zip(cast_weights_f32, slices))
    scratch = [
        pltpu.VMEM((rows // Q_TILE, N_Q_HEADS * Q_TILE, LANES), BF16),
        pltpu.VMEM((n_seq, n_keys, KV_W), BF16),
        pltpu.VMEM((n_seq, N_KV_HEADS * VT_ROWS, n_keys), BF16),
        pltpu.VMEM((n_seq, len(POOL_WINDOWS), seq_len + 2 * POOL_PAD, POOL_GW), F32),
    ]
    scratch += [pltpu.VMEM((Q_TILE, GATE_COL0), F32)] * 2
    heads_per_step = N_Q_HEADS if n_keys <= Q_TILE else 2
    scratch += [pltpu.VMEM((n_keys, heads_per_step * Q_TILE), F32)] * 2
    scratch += [pltpu.VMEM((n_keys, heads_per_step * Q_TILE), BF16)] * 2
    scratch += [pltpu.VMEM((seq_len + 2 * POOL_PAD - 8, POOL_GW), F32)] * 2
    scratch += [pltpu.VMEM((seq_len + 2 * POOL_PAD - 16, POOL_GW), F32)]
    if n_cast:
        scratch += [pltpu.VMEM(s, F32) for s in slices] + [pltpu.VMEM(s, BF16) for s in slices]
        scratch += [pltpu.SemaphoreType.DMA((2 * n_cast,))]
    return pl.pallas_call(
        functools.partial(_mix_kernel, seq_len=seq_len, n_seq=n_seq, n_cache=n_cache, rope=rope,
                          n_cast=n_cast, cast_layer=cast_layer,
                          kv_first_layer=not is_lat and new_kv is None, n_kv_alias=len(aliases),
                          heads_per_step=heads_per_step),
        grid=(n_tok // rows,),
        in_specs=in_specs,
        out_specs=out_specs,
        out_shape=out_shape,
        scratch_shapes=scratch,
        input_output_aliases=aliases,
        compiler_params=_params(),
        name="mix_lat" if is_lat else "mix_ctx",
    )(*args)


def _tail_kernel(x_ref, ao_ref, y_ref, mod_ref, npre_mix_ref, npost_mix_ref, npre_ffn_ref,
                 npost_ffn_ref, win_ref, wau_ref, wpu_ref, wout_ref, wg_ref, wu_ref, wd_ref, *rest,
                 sub, tiles_per_mod_row, next_rows):
    if next_rows:
        c_ref, wada_ref, bada_ref, o_ref, next_mod_out, x1_a, x1_b, act_a, act_b = rest
        next_mod_out[...] = _ada_rows(c_ref, wada_ref, bada_ref)
    else:
        o_ref, x1_a, x1_b, act_a, act_b = rest
    act_ref = (act_a, act_b)
    mod_row = pl.program_id(0) // tiles_per_mod_row if tiles_per_mod_row else CTX_MOD_ROW
    sh1, sc1, g1, sh2, sc2, g2 = [mod_ref[pl.ds(mod_row, 1), i * D_MODEL:(i + 1) * D_MODEL]
                                  for i in range(N_MOD)]
    x1_ref = (x1_a, x1_b)
    n_sub = x_ref.shape[0] // sub

    def post(i, slot):
        r0 = i * sub
        x = x_ref[r0:r0 + sub, :]
        hb = ((_rms(x) * npre_mix_ref[...]) * (1.0 + sc1) + sh1).astype(BF16)
        g_attn = jax.nn.sigmoid(_dot(hb, win_ref[:, GATE_COL0:GATE_COL0 + D_MODEL]))
        g_pool = jax.nn.sigmoid(_dot(hb, win_ref[:, GATE_COL0 + D_MODEL:IN_COLS]))
        attn = _dot(ao_ref[r0:r0 + sub, :], wau_ref[...])
        pool = _dot(y_ref[r0:r0 + sub, :], wpu_ref[...])
        mixed = g_attn * attn + g_pool * pool
        m = _dot(mixed.astype(BF16), wout_ref[...])
        x1_ref[slot][...] = x + g1 * (_rms(m) * npost_mix_ref[...])

    def ffn_up(i, slot):
        x1 = x1_ref[slot][...]
        hb = ((_rms(x1) * npre_ffn_ref[...]) * (1.0 + sc2) + sh2).astype(BF16)
        g = _dot(hb, wg_ref[...])
        u = _dot(hb, wu_ref[...])
        act_ref[slot][...] = ((g * jax.nn.sigmoid(g)) * u).astype(BF16)

    def ffn_down(i, slot):
        r0 = i * sub
        f = _dot(act_ref[slot][...], wd_ref[...])
        o_ref[r0:r0 + sub, :] = x1_ref[slot][...] + g2 * (_rms(f) * npost_ffn_ref[...])

    post(0, 0)
    for i in range(n_sub):
        if i + 1 < n_sub:
            post(i + 1, (i + 1) % 2)
        ffn_up(i, i % 2)
        if i >= 1:
            ffn_down(i - 1, (i - 1) % 2)
    ffn_down(n_sub - 1, (n_sub - 1) % 2)


def _tail_call(l, x, ao, y, mod_l, n_pre_mix, n_post_mix, n_pre_ffn, n_post_ffn, w_in,
               w_attn_up, w_pool_up, w_out, w_ffn_gate, w_ffn_up, w_ffn_down, next_mod_inputs,
               *, seq_len, per_seq_mod, tm, sub):
    n_tok = x.shape[0]
    n_tiles = n_tok // tm
    lay = lambda i: (l, 0, 0)
    row = lambda i: (i, 0)
    vec_spec = pl.BlockSpec((None, 1, D_MODEL), lay)
    layer_resident = lambda r, c: pl.BlockSpec((r, c), lambda i: (0, 0), pipeline_mode=pl.Buffered(1))
    in_specs = [
        pl.BlockSpec((tm, D_MODEL), row),
        pl.BlockSpec((tm, ATTN_W), row),
        pl.BlockSpec((tm, POOL_W), row),
        pl.BlockSpec((MOD_ROWS, N_MOD * D_MODEL), lambda i: (0, 0)),
        vec_spec, vec_spec, vec_spec, vec_spec,
        layer_resident(D_MODEL, IN_COLS),
        layer_resident(ATTN_W, D_MODEL),
        layer_resident(POOL_W, D_MODEL),
        layer_resident(D_MODEL, D_MODEL),
        layer_resident(D_MODEL, FFN_HIDDEN),
        layer_resident(D_MODEL, FFN_HIDDEN),
        layer_resident(FFN_HIDDEN, D_MODEL),
    ]
    args = [x, ao, y, mod_l, n_pre_mix, n_post_mix, n_pre_ffn, n_post_ffn, w_in, w_attn_up,
            w_pool_up, w_out, w_ffn_gate, w_ffn_up, w_ffn_down]
    out_specs = [pl.BlockSpec((tm, D_MODEL), row)]
    out_shape = [jax.ShapeDtypeStruct((n_tok, D_MODEL), F32)]
    if next_mod_inputs is not None:
        n_cols = N_MOD * D_MODEL
        tn = n_cols // n_tiles
        assert n_cols % n_tiles == 0 and tn % LANES == 0
        in_specs += [
            pl.BlockSpec((MOD_ROWS, D_MODEL), lambda i: (0, 0)),
            pl.BlockSpec((None, D_MODEL, tn), lambda i: (l + 1, 0, i)),
            pl.BlockSpec((None, 1, tn), lambda i: (l + 1, 0, i)),
        ]
        args += list(next_mod_inputs)
        out_specs += [pl.BlockSpec((MOD_ROWS, tn), lambda i: (0, i))]
        out_shape += [jax.ShapeDtypeStruct((MOD_ROWS, n_cols), F32)]
    return pl.pallas_call(
        functools.partial(_tail_kernel, sub=sub, next_rows=next_mod_inputs is not None,
                          tiles_per_mod_row=seq_len // tm if per_seq_mod else 0),
        grid=(n_tiles,),
        in_specs=in_specs,
        out_specs=out_specs,
        out_shape=out_shape,
        scratch_shapes=([pltpu.VMEM((sub, D_MODEL), F32)] * 2
                        + [pltpu.VMEM((sub, FFN_HIDDEN), BF16)] * 2),
        compiler_params=_params(),
        name="tail",
    )(*args)


def _rope_tables(n):
    rows = n // GRID_W
    row = np.repeat(np.arange(rows), GRID_W).astype(np.float32)
    col = np.tile(np.arange(GRID_W), rows).astype(np.float32)
    n_freq = HEAD_DIM // 4
    inv = jnp.asarray(ROPE_THETA, F32) ** (-jnp.arange(n_freq, dtype=F32) / n_freq)
    ang = jnp.concatenate([jnp.asarray(row)[:, None] * inv[None, :],
                           jnp.asarray(col)[:, None] * inv[None, :]], axis=-1)
    cos, sin = jnp.cos(ang), jnp.sin(ang)
    cos_t = jnp.tile(cos, (1, LANES // (HEAD_DIM // 2)))
    sin_t = jnp.tile(jnp.concatenate([-sin, sin], axis=-1), (1, LANES // HEAD_DIM))
    return cos_t, sin_t


def kernel(x_prompt, x_sample, cache_k, cache_v, c, c_ctx, w_ada, b_ada, w_in, q_norm, k_norm,
           w_attn_up, w_pool, pool_scale, w_pool_up, w_out, n_pre_mix, n_post_mix, n_pre_ffn,
           n_post_ffn, w_ffn_gate, w_ffn_up, w_ffn_down):
    batch, seq, _ = x_prompt.shape
    dec_batch, dec_seq, _ = x_sample.shape
    past_len = cache_k.shape[2]
    assert dec_batch <= CTX_MOD_ROW

    c_rows = jnp.concatenate(
        [c, c_ctx[None, :], jnp.zeros((MOD_ROWS - dec_batch - 1, D_MODEL), F32)], axis=0)
    b_ada3 = b_ada.reshape(DEPTH, 1, N_MOD * D_MODEL)
    mod_l = _ada_call(c_rows, w_ada, b_ada3)

    cos_t, sin_t = _rope_tables(dec_seq)
    vec = lambda a: a.reshape(DEPTH, 1, a.shape[-1])
    q_gain = vec(jnp.tile(q_norm, (1, LANES // HEAD_DIM)))
    k_gain = vec(jnp.tile(k_norm, (1, LANES // HEAD_DIM)))
    w_in_l = w_in[0].astype(BF16)
    w_pool_b = w_pool.astype(BF16)
    tail_weights_f32 = (w_attn_up, w_pool_up, w_out, w_ffn_gate, w_ffn_up, w_ffn_down)
    n_pre_mix_v, n_post_mix_v = vec(n_pre_mix), vec(n_post_mix)
    n_pre_ffn_v, n_post_ffn_v = vec(n_pre_ffn), vec(n_post_ffn)
    pool_scale_v = vec(pool_scale)
    ck = cache_k.reshape(dec_batch, DEPTH, past_len, KV_W)
    cv = cache_v.reshape(dec_batch, DEPTH, past_len, KV_W)

    tm = 512
    ctx_seqs_per_step = 4

    y = x_prompt.reshape(batch * seq, D_MODEL)
    z = x_sample.reshape(dec_batch * dec_seq, D_MODEL)
    new_kv = None
    for l in range(DEPTH):
        has_next = l + 1 < DEPTH
        outs = _mix_call(l, y, mod_l, n_pre_mix_v, w_in_l, q_gain, k_gain, cos_t, sin_t, w_pool_b,
                         pool_scale_v, None, None, tail_weights_f32, l, new_kv,
                         seq_len=seq, n_seq=ctx_seqs_per_step, rope=False)
        ao, yp = outs[:2]
        new_kv = outs[2:4]
        tail_weights_b = outs[4:]
        tail_outs = _tail_call(l, y, ao, yp, mod_l, n_pre_mix_v, n_post_mix_v, n_pre_ffn_v, n_post_ffn_v,
                               w_in_l, *tail_weights_b, (c_rows, w_ada, b_ada3) if has_next else None,
                               seq_len=seq, per_seq_mod=False, tm=tm, sub=256)
        y = tail_outs[0]
        outs = _mix_call(l, z, mod_l, n_pre_mix_v, w_in_l, q_gain, k_gain, cos_t, sin_t, w_pool_b,
                         pool_scale_v, ck, cv, (w_in,) if has_next else (), l + 1,
                         seq_len=dec_seq, n_seq=1, rope=True)
        ao, yp = outs[:2]
        z = _tail_call(l, z, ao, yp, mod_l, n_pre_mix_v, n_post_mix_v, n_pre_ffn_v, n_post_ffn_v,
                       w_in_l, *tail_weights_b, None,
                       seq_len=dec_seq, per_seq_mod=True, tm=tm, sub=256)[0]
        if has_next:
            w_in_l = outs[2]
            mod_l = tail_outs[1]

    def kv_layout(a):
        a = a.reshape(batch, DEPTH, N_KV_HEADS, HEAD_DIM, seq)
        return jnp.transpose(a, (0, 1, 4, 2, 3))

    return (y.reshape(batch, seq, D_MODEL), z.reshape(dec_batch, dec_seq, D_MODEL),
            kv_layout(new_kv[0]), kv_layout(new_kv[1]))
```

```python
import functools

import jax
import jax.numpy as jnp
import numpy as np
from jax import lax
from jax.experimental import pallas as pl
from jax.experimental.pallas import tpu as pltpu

D_MODEL = 1024
DEPTH = 4
GRID_W = 64
HEAD_DIM = 64
N_Q_HEADS = 8
N_KV_HEADS = 2
Q_PER_KV = N_Q_HEADS // N_KV_HEADS
ATTN_W = N_Q_HEADS * HEAD_DIM
KV_W = N_KV_HEADS * HEAD_DIM
POOL_WINDOWS = (2, 4, 8, 16)
POOL_W = D_MODEL // 2
POOL_GW = POOL_W // len(POOL_WINDOWS)
GATE_COL0 = ATTN_W + 2 * KV_W + POOL_W
IN_COLS = GATE_COL0 + 2 * D_MODEL
FFN_HIDDEN = 2816
N_MOD = 6
ROPE_THETA = 10000.0
EPS = 1e-6

LANES = 128
Q_TILE = 256
VT_ROWS = 80
POOL_PAD = 24
LOG2_E = 1.4426950408889634
MOD_ROWS = 8
CTX_MOD_ROW = 4
VMEM_LIMIT_BYTES = 56 * 1024 * 1024

F32 = jnp.float32
BF16 = jnp.bfloat16


def _dot(a, b):
    return jnp.dot(a, b, preferred_element_type=F32)


def _dot_nt(a, b):
    return lax.dot_general(a, b, (((1,), (1,)), ((), ())), preferred_element_type=F32)


def _rms(x):
    return x * lax.rsqrt(jnp.mean(x * x, axis=-1, keepdims=True) + EPS)


def _params(n_axes=1):
    return pltpu.CompilerParams(dimension_semantics=("arbitrary",) * n_axes,
                                vmem_limit_bytes=VMEM_LIMIT_BYTES)


def _ada_rows(c_ref, w_ref, b_ref):
    c = c_ref[...]
    s = (c * jax.nn.sigmoid(c)).astype(BF16)
    return _dot(s, w_ref[...].astype(BF16)) + b_ref[...]


def _ada_kernel(c_ref, w_ref, b_ref, o_ref):
    o_ref[...] = _ada_rows(c_ref, w_ref, b_ref)


def _ada_call(c_rows, w_ada, b_ada3):
    tn = 1536
    n_cols = N_MOD * D_MODEL
    return pl.pallas_call(
        _ada_kernel,
        grid=(n_cols // tn,),
        in_specs=[
            pl.BlockSpec((MOD_ROWS, D_MODEL), lambda j: (0, 0)),
            pl.BlockSpec((None, D_MODEL, tn), lambda j: (0, 0, j)),
            pl.BlockSpec((None, 1, tn), lambda j: (0, 0, j)),
        ],
        out_specs=pl.BlockSpec((MOD_ROWS, tn), lambda j: (0, j)),
        out_shape=jax.ShapeDtypeStruct((MOD_ROWS, n_cols), F32),
        compiler_params=_params(),
        name="ada_rows",
    )(c_rows, w_ada, b_ada3)


def _pool_group(src, window, seq_len, p2_ref, p4_ref):
    half = window // 2
    rows = lambda ref, off: ref[POOL_PAD + off:POOL_PAD + off + seq_len, :]
    if window <= 4:
        total = rows(src, -half)
        for off in range(-half + 1, half):
            total = total + rows(src, off)
    else:
        n2 = seq_len + 2 * POOL_PAD - 8
        p2_ref[0:n2, :] = src[0:n2, :] + src[1:n2 + 1, :]
        if window == 8:
            terms = [rows(p2_ref, off) for off in (-4, -2, 0, 2)]
        else:
            assert window == 16
            n4 = n2 - 8
            p4_ref[0:n4, :] = p2_ref[0:n4, :] + p2_ref[2:n4 + 2, :]
            terms = [rows(p4_ref, off) for off in (-8, -4, 0, 4)]
        total = (terms[0] + terms[1]) + (terms[2] + terms[3])
    t = lax.broadcasted_iota(jnp.int32, (seq_len, POOL_GW), 0)
    count = jnp.minimum(t + half, seq_len) - jnp.maximum(t - half, 0)
    return total / count.astype(F32) - rows(src, 0)


def _mix_kernel(x_ref, mod_ref, npre_ref, w_ref, qg_ref, kg_ref, cos_ref, sin_ref, wpool_ref,
                pscale_ref, *rest, seq_len, n_seq, n_cache, rope, n_cast, cast_layer, kv_first_layer,
                n_kv_alias, heads_per_step):
    if n_cache:
        (ck_ref, cv_ref), rest = rest[:2], rest[2:]
    w32_refs, rest = rest[:n_cast], rest[n_cast:]
    rest = rest[n_kv_alias:]
    if n_cache:
        (ao_out, y_out), rest = rest[:2], rest[2:]
    else:
        (ao_out, y_out, nk_out, nv_out), rest = rest[:4], rest[4:]
    wb_outs, rest = rest[:n_cast], rest[n_cast:]
    (qa_ref, k_ref, vt_ref, p_ref, raw_a, raw_b, s_a, s_b, e_a, e_b, p2_a, p2_b, p4_b), rest = (
        rest[:13], rest[13:])
    raw_ref, s_ref, e_ref = (raw_a, raw_b), (s_a, s_b), (e_a, e_b)

    if n_cast:
        stage_in, stage_out, sems = rest[0:n_cast], rest[n_cast:2 * n_cast], rest[2 * n_cast]
        grid_step = pl.program_id(0)
        last_grid_step = pl.num_programs(0) - 1

        def copy_in(j, at_step):
            n = stage_in[j].shape[0]
            row0 = cast_layer * (w32_refs[j].shape[0] // DEPTH) + at_step * n
            return pltpu.make_async_copy(w32_refs[j].at[pl.ds(row0, n), :], stage_in[j], sems.at[j])

        def copy_out(j, at_step):
            n = stage_out[j].shape[0]
            return pltpu.make_async_copy(stage_out[j], wb_outs[j].at[pl.ds(at_step * n, n), :],
                                         sems.at[n_cast + j])

        for j in range(n_cast):
            copy_in(j, grid_step).start()
    n_keys = n_cache + seq_len
    tiles_per_seq = seq_len // Q_TILE
    n_tiles = n_seq * tiles_per_seq

    mod_row = pl.program_id(0) if n_cache else CTX_MOD_ROW
    sh1 = mod_ref[pl.ds(mod_row, 1), 0:D_MODEL]
    norm_gain = npre_ref[...] * (1.0 + mod_ref[pl.ds(mod_row, 1), D_MODEL:2 * D_MODEL])

    lane = lax.broadcasted_iota(jnp.int32, (Q_TILE, LANES), 1)
    first_head = lane < HEAD_DIM
    second_half = (lane & (HEAD_DIM // 2)) != 0

    def head_norm(t, gain):
        t2 = t * t
        s_lo = jnp.sum(jnp.where(first_head, t2, 0.0), axis=-1, keepdims=True)
        s_hi = jnp.sum(jnp.where(first_head, 0.0, t2), axis=-1, keepdims=True)
        ms = jnp.where(first_head, s_lo, s_hi) * (1.0 / HEAD_DIM)
        return (t * lax.rsqrt(ms + EPS)) * gain

    def rotary(t, pos0):
        partner = jnp.where(second_half, pltpu.roll(t, HEAD_DIM // 2, 1),
                            pltpu.roll(t, LANES - HEAD_DIM // 2, 1))
        return t * cos_ref[pos0:pos0 + Q_TILE, :] + partner * sin_ref[pos0:pos0 + Q_TILE, :]

    ones_rows = jnp.ones((VT_ROWS - HEAD_DIM, n_keys), BF16)
    pad_rows = jnp.zeros((POOL_PAD, POOL_GW), F32)
    for s_idx in range(n_seq):
        for g in range(len(POOL_WINDOWS)):
            p_ref[s_idx, g, 0:POOL_PAD, :] = pad_rows
            p_ref[s_idx, g, POOL_PAD + seq_len:2 * POOL_PAD + seq_len, :] = pad_rows
        for kv in range(N_KV_HEADS):
            vt_ref[s_idx, kv * VT_ROWS + HEAD_DIM:(kv + 1) * VT_ROWS, :] = ones_rows
        if n_cache:
            k_ref[s_idx, 0:n_cache, :] = ck_ref[0, 0].astype(BF16)
            cvt = cv_ref[0, 0].T.astype(BF16)
            for kv in range(N_KV_HEADS):
                vt_ref[s_idx, kv * VT_ROWS:kv * VT_ROWS + HEAD_DIM, 0:n_cache] = (
                    cvt[kv * HEAD_DIM:(kv + 1) * HEAD_DIM, :])

    def project(tile, slot):
        r0 = tile * Q_TILE
        x = x_ref[r0:r0 + Q_TILE, :]
        hb = (_rms(x) * norm_gain + sh1).astype(BF16)
        raw_ref[slot][...] = _dot(hb, w_ref[...])

    def epilogue(tile, slot):
        s_idx, pos0 = tile // tiles_per_seq, (tile % tiles_per_seq) * Q_TILE
        r0 = tile * Q_TILE
        raw = raw_ref[slot]
        for j in range(ATTN_W // LANES):
            t = head_norm(raw[:, j * LANES:(j + 1) * LANES], qg_ref[...])
            if rope:
                t = rotary(t, pos0)
            t = t * (HEAD_DIM ** -0.5 * LOG2_E)
            swapped = pltpu.roll(t, HEAD_DIM, 1)
            if (2 * j) // Q_PER_KV == 0:
                even = jnp.where(first_head, t, 0.0)
                odd = jnp.where(first_head, swapped, 0.0)
            else:
                even = jnp.where(first_head, 0.0, swapped)
                odd = jnp.where(first_head, 0.0, t)
            qa_ref[tile, (2 * j) * Q_TILE:(2 * j + 1) * Q_TILE, :] = even.astype(BF16)
            qa_ref[tile, (2 * j + 1) * Q_TILE:(2 * j + 2) * Q_TILE, :] = odd.astype(BF16)
        kn = head_norm(raw[:, ATTN_W:ATTN_W + KV_W], kg_ref[...])
        v = raw[:, ATTN_W + KV_W:ATTN_W + 2 * KV_W]
        vt_f32 = v.T
        if not n_cache:
            if kv_first_layer:
                nk_out[s_idx, 0, :, pos0:pos0 + Q_TILE] = kn.T
                nv_out[s_idx, 0, :, pos0:pos0 + Q_TILE] = vt_f32
                later = jnp.zeros((DEPTH - 1, KV_W, Q_TILE), F32)
                nk_out[s_idx, 1:DEPTH, :, pos0:pos0 + Q_TILE] = later
                nv_out[s_idx, 1:DEPTH, :, pos0:pos0 + Q_TILE] = later
            else:
                nk_out[s_idx, :, pos0:pos0 + Q_TILE] = kn.T
                nv_out[s_idx, :, pos0:pos0 + Q_TILE] = vt_f32
        if rope:
            kn = rotary(kn, pos0)
        k0 = n_cache + pos0
        k_ref[s_idx, k0:k0 + Q_TILE, :] = kn.astype(BF16)
        vt = vt_f32.astype(BF16)
        for kv in range(N_KV_HEADS):
            vt_ref[s_idx, kv * VT_ROWS:kv * VT_ROWS + HEAD_DIM, k0:k0 + Q_TILE] = (
                vt[kv * HEAD_DIM:(kv + 1) * HEAD_DIM, :])
        for g in range(len(POOL_WINDOWS)):
            c0 = ATTN_W + 2 * KV_W + g * POOL_GW
            p_ref[s_idx, g, POOL_PAD + pos0:POOL_PAD + pos0 + Q_TILE, :] = raw[:, c0:c0 + POOL_GW]

    project(0, 0)
    for tile in range(n_tiles):
        if tile + 1 < n_tiles:
            project(tile + 1, (tile + 1) % 2)
        epilogue(tile, tile % 2)

    n_groups = N_Q_HEADS // heads_per_step
    n_steps = n_tiles * n_groups

    def scores(step, slot):
        tile, grp = step // n_groups, step % n_groups
        qs = qa_ref[tile, grp * heads_per_step * Q_TILE:(grp + 1) * heads_per_step * Q_TILE, :]
        s_ref[slot][...] = _dot_nt(k_ref[tile // tiles_per_seq], qs)

    def exponent(step, slot):
        s = s_ref[slot][...]
        e_ref[slot][...] = jnp.exp2(s - jnp.max(s, axis=0, keepdims=True)).astype(BF16)

    def values(step, slot):
        tile, grp = step // n_groups, step % n_groups
        head0 = grp * heads_per_step
        for kv in range(head0 // Q_PER_KV, (head0 + heads_per_step - 1) // Q_PER_KV + 1):
            lo = max(kv * Q_PER_KV, head0) - head0
            hi = min((kv + 1) * Q_PER_KV, head0 + heads_per_step) - head0
            vt = vt_ref[tile // tiles_per_seq, kv * VT_ROWS:(kv + 1) * VT_ROWS, :]
            ot = _dot(vt, e_ref[slot][:, lo * Q_TILE:hi * Q_TILE])
            ot = ot[0:HEAD_DIM] * (1.0 / ot[HEAD_DIM:HEAD_DIM + 1])
            for j in range(0, hi - lo, 2):
                pair = (head0 + lo + j) // 2
                both = jnp.concatenate([ot[:, j * Q_TILE:(j + 1) * Q_TILE],
                                        ot[:, (j + 1) * Q_TILE:(j + 2) * Q_TILE]], axis=0)
                ao_out[tile * Q_TILE:(tile + 1) * Q_TILE, pair * LANES:(pair + 1) * LANES] = (
                    both.T.astype(BF16))

    scores(0, 0)
    for step in range(n_steps + 1):
        if step + 1 < n_steps:
            scores(step + 1, (step + 1) % 2)
        if step < n_steps:
            exponent(step, step % 2)
        if step >= 1:
            values(step - 1, (step - 1) % 2)

    for s_idx in range(n_seq):
        r0 = s_idx * seq_len
        for g, window in enumerate(POOL_WINDOWS):
            cols = slice(g * POOL_GW, (g + 1) * POOL_GW)
            p2_ref = p2_a if window == 8 else p2_b
            d = _pool_group(p_ref.at[s_idx, g], window, seq_len, p2_ref, p4_b)
            yg = _dot(d.astype(BF16), wpool_ref[g]) * pscale_ref[:, cols]
            y_out[r0:r0 + seq_len, cols] = yg.astype(BF16)

    if n_cast:
        @pl.when(grid_step > 0)
        def _():
            for j in range(n_cast):
                copy_out(j, grid_step - 1).wait()

        for j in range(n_cast):
            copy_in(j, grid_step).wait()
            stage_out[j][...] = stage_in[j][...].astype(BF16)
            copy_out(j, grid_step).start()

        @pl.when(grid_step == last_grid_step)
        def _():
            for j in range(n_cast):
                copy_out(j, grid_step).wait()


def _mix_call(l, x, mod_rows, n_pre, w_in_l, q_gain, k_gain, cos_t, sin_t, w_pool, pool_scale,
              cache_k, cache_v, cast_weights_f32, cast_layer, new_kv=None, *, seq_len, n_seq, rope):
    n_tok = x.shape[0]
    rows = n_seq * seq_len
    is_lat = cache_k is not None
    n_cache = cache_k.shape[2] if is_lat else 0
    n_keys = n_cache + seq_len
    assert n_seq == 1 or not is_lat
    lay = lambda i: (l, 0, 0)
    row = lambda i: (i, 0)
    in_specs = [
        pl.BlockSpec((rows, D_MODEL), row),
        pl.BlockSpec((MOD_ROWS, N_MOD * D_MODEL), lambda i: (0, 0)),
        pl.BlockSpec((None, 1, D_MODEL), lay),
        pl.BlockSpec((D_MODEL, GATE_COL0), lambda i: (0, 0)),
        pl.BlockSpec((None, 1, LANES), lay),
        pl.BlockSpec((None, 1, LANES), lay),
        pl.BlockSpec((seq_len, LANES), lambda i: (0, 0)),
        pl.BlockSpec((seq_len, LANES), lambda i: (0, 0)),
        pl.BlockSpec((None, len(POOL_WINDOWS), POOL_GW, POOL_GW), lambda i: (l, 0, 0, 0)),
        pl.BlockSpec((None, 1, POOL_W), lay),
    ]
    args = [x, mod_rows, n_pre, w_in_l, q_gain, k_gain, cos_t, sin_t, w_pool, pool_scale]
    out_shape = [jax.ShapeDtypeStruct((n_tok, ATTN_W), BF16), jax.ShapeDtypeStruct((n_tok, POOL_W), BF16)]
    out_specs = [pl.BlockSpec((rows, ATTN_W), row), pl.BlockSpec((rows, POOL_W), row)]
    if is_lat:
        cache_spec = pl.BlockSpec((1, 1, n_cache, KV_W), lambda i: (i, l, 0, 0))
        in_specs += [cache_spec, cache_spec]
        args += [cache_k, cache_v]
    else:
        out_shape += [jax.ShapeDtypeStruct((n_tok // seq_len, DEPTH, KV_W, seq_len), F32)] * 2
        if new_kv is None:
            out_specs += [pl.BlockSpec((n_seq, DEPTH, KV_W, seq_len), lambda i: (i, 0, 0, 0))] * 2
        else:
            out_specs += [pl.BlockSpec((n_seq, None, KV_W, seq_len), lambda i: (i, l, 0, 0))] * 2
    n_cast = len(cast_weights_f32)
    n_steps = n_tok // rows
    any_spec = pl.BlockSpec(memory_space=pl.ANY)
    in_specs += [any_spec] * n_cast
    args += [w.reshape(-1, w.shape[2]) for w in cast_weights_f32]
    aliases = {}
    if new_kv is not None:
        aliases = {len(args): 2, len(args) + 1: 3}
        in_specs += [any_spec] * 2
        args += list(new_kv)
    out_shape += [jax.ShapeDtypeStruct(w.shape[1:], BF16) for w in cast_weights_f32]
    out_specs += [any_spec] * n_cast
    slices = [(w.shape[1] // n_steps, w.shape[2]) for w in cast_weights_f32]
    assert all(w.shape[1] % n_steps == 0 and r % 16 == 0 for w, (r, _) in zip(cast_weights_f32, slices))
    scratch = [
        pltpu.VMEM((rows // Q_TILE, N_Q_HEADS * Q_TILE, LANES), BF16),
        pltpu.VMEM((n_seq, n_keys, KV_W), BF16),
        pltpu.VMEM((n_seq, N_KV_HEADS * VT_ROWS, n_keys), BF16),
        pltpu.VMEM((n_seq, len(POOL_WINDOWS), seq_len + 2 * POOL_PAD, POOL_GW), F32),
    ]
    scratch += [pltpu.VMEM((Q_TILE, GATE_COL0), F32)] * 2
    heads_per_step = N_Q_HEADS if n_keys <= Q_TILE else Q_PER_KV
    scratch += [pltpu.VMEM((n_keys, heads_per_step * Q_TILE), F32)] * 2
    scratch += [pltpu.VMEM((n_keys, heads_per_step * Q_TILE), BF16)] * 2
    scratch += [pltpu.VMEM((seq_len + 2 * POOL_PAD - 8, POOL_GW), F32)] * 2
    scratch += [pltpu.VMEM((seq_len + 2 * POOL_PAD - 16, POOL_GW), F32)]
    if n_cast:
        scratch += [pltpu.VMEM(s, F32) for s in slices] + [pltpu.VMEM(s, BF16) for s in slices]
        scratch += [pltpu.SemaphoreType.DMA((2 * n_cast,))]
    return pl.pallas_call(
        functools.partial(_mix_kernel, seq_len=seq_len, n_seq=n_seq, n_cache=n_cache, rope=rope,
                          n_cast=n_cast, cast_layer=cast_layer,
                          kv_first_layer=not is_lat and new_kv is None, n_kv_alias=len(aliases),
                          heads_per_step=heads_per_step),
        grid=(n_tok // rows,),
        in_specs=in_specs,
        out_specs=out_specs,
        out_shape=out_shape,
        scratch_shapes=scratch,
        input_output_aliases=aliases,
        compiler_params=_params(),
        name="mix_lat" if is_lat else "mix_ctx",
    )(*args)


def _tail_kernel(x_ref, ao_ref, y_ref, mod_ref, npre_mix_ref, npost_mix_ref, npre_ffn_ref,
                 npost_ffn_ref, win_ref, wau_ref, wpu_ref, wout_ref, wg_ref, wu_ref, wd_ref, *rest,
                 sub, tiles_per_mod_row, next_rows):
    if next_rows:
        c_ref, wada_ref, bada_ref, o_ref, next_mod_out, x1_a, x1_b = rest
        next_mod_out[...] = _ada_rows(c_ref, wada_ref, bada_ref)
    else:
        o_ref, x1_a, x1_b = rest
    mod_row = pl.program_id(0) // tiles_per_mod_row if tiles_per_mod_row else CTX_MOD_ROW
    sh1, sc1, g1, sh2, sc2, g2 = [mod_ref[pl.ds(mod_row, 1), i * D_MODEL:(i + 1) * D_MODEL]
                                  for i in range(N_MOD)]
    x1_ref = (x1_a, x1_b)
    n_sub = x_ref.shape[0] // sub

    def post(i, slot):
        r0 = i * sub
        x = x_ref[r0:r0 + sub, :]
        hb = ((_rms(x) * npre_mix_ref[...]) * (1.0 + sc1) + sh1).astype(BF16)
        g_attn = jax.nn.sigmoid(_dot(hb, win_ref[:, GATE_COL0:GATE_COL0 + D_MODEL]))
        g_pool = jax.nn.sigmoid(_dot(hb, win_ref[:, GATE_COL0 + D_MODEL:IN_COLS]))
        attn = _dot(ao_ref[r0:r0 + sub, :], wau_ref[...])
        pool = _dot(y_ref[r0:r0 + sub, :], wpu_ref[...])
        mixed = g_attn * attn + g_pool * pool
        m = _dot(mixed.astype(BF16), wout_ref[...])
        x1_ref[slot][...] = x + g1 * (_rms(m) * npost_mix_ref[...])

    def ffn(i, slot):
        r0 = i * sub
        x1 = x1_ref[slot][...]
        hb = ((_rms(x1) * npre_ffn_ref[...]) * (1.0 + sc2) + sh2).astype(BF16)
        g = _dot(hb, wg_ref[...])
        u = _dot(hb, wu_ref[...])
        a = ((g * jax.nn.sigmoid(g)) * u).astype(BF16)
        f = _dot(a, wd_ref[...])
        o_ref[r0:r0 + sub, :] = x1 + g2 * (_rms(f) * npost_ffn_ref[...])

    post(0, 0)
    for i in range(n_sub):
        if i + 1 < n_sub:
            post(i + 1, (i + 1) % 2)
        ffn(i, i % 2)


def _tail_call(l, x, ao, y, mod_l, n_pre_mix, n_post_mix, n_pre_ffn, n_post_ffn, w_in,
               w_attn_up, w_pool_up, w_out, w_ffn_gate, w_ffn_up, w_ffn_down, next_mod_inputs,
               *, seq_len, per_seq_mod, tm, sub):
    n_tok = x.shape[0]
    n_tiles = n_tok // tm
    lay = lambda i: (l, 0, 0)
    row = lambda i: (i, 0)
    vec_spec = pl.BlockSpec((None, 1, D_MODEL), lay)
    layer_resident = lambda r, c: pl.BlockSpec((r, c), lambda i: (0, 0), pipeline_mode=pl.Buffered(1))
    in_specs = [
        pl.BlockSpec((tm, D_MODEL), row),
        pl.BlockSpec((tm, ATTN_W), row),
        pl.BlockSpec((tm, POOL_W), row),
        pl.BlockSpec((MOD_ROWS, N_MOD * D_MODEL), lambda i: (0, 0)),
        vec_spec, vec_spec, vec_spec, vec_spec,
        layer_resident(D_MODEL, IN_COLS),
        layer_resident(ATTN_W, D_MODEL),
        layer_resident(POOL_W, D_MODEL),
        layer_resident(D_MODEL, D_MODEL),
        layer_resident(D_MODEL, FFN_HIDDEN),
        layer_resident(D_MODEL, FFN_HIDDEN),
        layer_resident(FFN_HIDDEN, D_MODEL),
    ]
    args = [x, ao, y, mod_l, n_pre_mix, n_post_mix, n_pre_ffn, n_post_ffn, w_in, w_attn_up,
            w_pool_up, w_out, w_ffn_gate, w_ffn_up, w_ffn_down]
    out_specs = [pl.BlockSpec((tm, D_MODEL), row)]
    out_shape = [jax.ShapeDtypeStruct((n_tok, D_MODEL), F32)]
    if next_mod_inputs is not None:
        n_cols = N_MOD * D_MODEL
        tn = n_cols // n_tiles
        assert n_cols % n_tiles == 0 and tn % LANES == 0
        in_specs += [
            pl.BlockSpec((MOD_ROWS, D_MODEL), lambda i: (0, 0)),
            pl.BlockSpec((None, D_MODEL, tn), lambda i: (l + 1, 0, i)),
            pl.BlockSpec((None, 1, tn), lambda i: (l + 1, 0, i)),
        ]
        args += list(next_mod_inputs)
        out_specs += [pl.BlockSpec((MOD_ROWS, tn), lambda i: (0, i))]
        out_shape += [jax.ShapeDtypeStruct((MOD_ROWS, n_cols), F32)]
    return pl.pallas_call(
        functools.partial(_tail_kernel, sub=sub, next_rows=next_mod_inputs is not None,
                          tiles_per_mod_row=seq_len // tm if per_seq_mod else 0),
        grid=(n_tiles,),
        in_specs=in_specs,
        out_specs=out_specs,
        out_shape=out_shape,
        scratch_shapes=[pltpu.VMEM((sub, D_MODEL), F32)] * 2,
        compiler_params=_params(),
        name="tail",
    )(*args)


def _rope_tables(n):
    rows = n // GRID_W
    row = np.repeat(np.arange(rows), GRID_W).astype(np.float32)
    col = np.tile(np.arange(GRID_W), rows).astype(np.float32)
    n_freq = HEAD_DIM // 4
    inv = jnp.asarray(ROPE_THETA, F32) ** (-jnp.arange(n_freq, dtype=F32) / n_freq)
    ang = jnp.concatenate([jnp.asarray(row)[:, None] * inv[None, :],
                           jnp.asarray(col)[:, None] * inv[None, :]], axis=-1)
    cos, sin = jnp.cos(ang), jnp.sin(ang)
    cos_t = jnp.tile(cos, (1, LANES // (HEAD_DIM // 2)))
    sin_t = jnp.tile(jnp.concatenate([-sin, sin], axis=-1), (1, LANES // HEAD_DIM))
    return cos_t, sin_t


def kernel(x_prompt, x_sample, cache_k, cache_v, c, c_ctx, w_ada, b_ada, w_in, q_norm, k_norm,
           w_attn_up, w_pool, pool_scale, w_pool_up, w_out, n_pre_mix, n_post_mix, n_pre_ffn,
           n_post_ffn, w_ffn_gate, w_ffn_up, w_ffn_down):
    batch, seq, _ = x_prompt.shape
    dec_batch, dec_seq, _ = x_sample.shape
    past_len = cache_k.shape[2]
    assert dec_batch <= CTX_MOD_ROW

    c_rows = jnp.concatenate(
        [c, c_ctx[None, :], jnp.zeros((MOD_ROWS - dec_batch - 1, D_MODEL), F32)], axis=0)
    b_ada3 = b_ada.reshape(DEPTH, 1, N_MOD * D_MODEL)
    mod_l = _ada_call(c_rows, w_ada, b_ada3)

    cos_t, sin_t = _rope_tables(dec_seq)
    vec = lambda a: a.reshape(DEPTH, 1, a.shape[-1])
    q_gain = vec(jnp.tile(q_norm, (1, LANES // HEAD_DIM)))
    k_gain = vec(jnp.tile(k_norm, (1, LANES // HEAD_DIM)))
    w_in_l = w_in[0].astype(BF16)
    w_pool_b = w_pool.astype(BF16)
    tail_weights_f32 = (w_attn_up, w_pool_up, w_out, w_ffn_gate, w_ffn_up, w_ffn_down)
    n_pre_mix_v, n_post_mix_v = vec(n_pre_mix), vec(n_post_mix)
    n_pre_ffn_v, n_post_ffn_v = vec(n_pre_ffn), vec(n_post_ffn)
    pool_scale_v = vec(pool_scale)
    ck = cache_k.reshape(dec_batch, DEPTH, past_len, KV_W)
    cv = cache_v.reshape(dec_batch, DEPTH, past_len, KV_W)

    tm = 512
    ctx_seqs_per_step = 4

    y = x_prompt.reshape(batch * seq, D_MODEL)
    z = x_sample.reshape(dec_batch * dec_seq, D_MODEL)
    new_kv = None
    for l in range(DEPTH):
        has_next = l + 1 < DEPTH
        outs = _mix_call(l, y, mod_l, n_pre_mix_v, w_in_l, q_gain, k_gain, cos_t, sin_t, w_pool_b,
                         pool_scale_v, None, None, tail_weights_f32, l, new_kv,
                         seq_len=seq, n_seq=ctx_seqs_per_step, rope=False)
        ao, yp = outs[:2]
        new_kv = outs[2:4]
        tail_weights_b = outs[4:]
        tail_outs = _tail_call(l, y, ao, yp, mod_l, n_pre_mix_v, n_post_mix_v, n_pre_ffn_v, n_post_ffn_v,
                               w_in_l, *tail_weights_b, (c_rows, w_ada, b_ada3) if has_next else None,
                               seq_len=seq, per_seq_mod=False, tm=tm, sub=256)
        y = tail_outs[0]
        outs = _mix_call(l, z, mod_l, n_pre_mix_v, w_in_l, q_gain, k_gain, cos_t, sin_t, w_pool_b,
                         pool_scale_v, ck, cv, (w_in,) if has_next else (), l + 1,
                         seq_len=dec_seq, n_seq=1, rope=True)
        ao, yp = outs[:2]
        z = _tail_call(l, z, ao, yp, mod_l, n_pre_mix_v, n_post_mix_v, n_pre_ffn_v, n_post_ffn_v,
                       w_in_l, *tail_weights_b, None,
                       seq_len=dec_seq, per_seq_mod=True, tm=tm, sub=256)[0]
        if has_next:
            w_in_l = outs[2]
            mod_l = tail_outs[1]

    def kv_layout(a):
        a = a.reshape(batch, DEPTH, N_KV_HEADS, HEAD_DIM, seq)
        return jnp.transpose(a, (0, 1, 4, 2, 3))

    return (y.reshape(batch, seq, D_MODEL), z.reshape(dec_batch, dec_seq, D_MODEL),
            kv_layout(new_kv[0]), kv_layout(new_kv[1]))
```

```python
import functools

import jax
import jax.numpy as jnp
import numpy as np
from jax import lax
from jax.experimental import pallas as pl
from jax.experimental.pallas import tpu as pltpu

D_MODEL = 1024
DEPTH = 4
GRID_W = 64
HEAD_DIM = 64
N_Q_HEADS = 8
N_KV_HEADS = 2
Q_PER_KV = N_Q_HEADS // N_KV_HEADS
ATTN_W = N_Q_HEADS * HEAD_DIM
KV_W = N_KV_HEADS * HEAD_DIM
POOL_WINDOWS = (2, 4, 8, 16)
POOL_W = D_MODEL // 2
POOL_GW = POOL_W // len(POOL_WINDOWS)
GATE_COL0 = ATTN_W + 2 * KV_W + POOL_W
IN_COLS = GATE_COL0 + 2 * D_MODEL
FFN_HIDDEN = 2816
N_MOD = 6
ROPE_THETA = 10000.0
EPS = 1e-6

LANES = 128
Q_TILE = 256
PROJ_TILES = 2
VT_ROWS = 80
POOL_PAD = 24
LOG2_E = 1.4426950408889634
MOD_ROWS = 8
CTX_MOD_ROW = 4
VMEM_LIMIT_BYTES = 56 * 1024 * 1024

F32 = jnp.float32
BF16 = jnp.bfloat16


def _dot(a, b):
    return jnp.dot(a, b, preferred_element_type=F32)


def _dot_nt(a, b):
    return lax.dot_general(a, b, (((1,), (1,)), ((), ())), preferred_element_type=F32)


def _rms(x):
    return x * lax.rsqrt(jnp.mean(x * x, axis=-1, keepdims=True) + EPS)


def _params(n_axes=1):
    return pltpu.CompilerParams(dimension_semantics=("arbitrary",) * n_axes,
                                vmem_limit_bytes=VMEM_LIMIT_BYTES)


def _ada_rows(c_ref, w_ref, b_ref):
    c = c_ref[...]
    s = (c * jax.nn.sigmoid(c)).astype(BF16)
    return _dot(s, w_ref[...].astype(BF16)) + b_ref[...]


def _ada_kernel(c_ref, w_ref, b_ref, o_ref):
    o_ref[...] = _ada_rows(c_ref, w_ref, b_ref)


def _ada_call(c_rows, w_ada, b_ada3):
    tn = 1536
    n_cols = N_MOD * D_MODEL
    return pl.pallas_call(
        _ada_kernel,
        grid=(n_cols // tn,),
        in_specs=[
            pl.BlockSpec((MOD_ROWS, D_MODEL), lambda j: (0, 0)),
            pl.BlockSpec((None, D_MODEL, tn), lambda j: (0, 0, j)),
            pl.BlockSpec((None, 1, tn), lambda j: (0, 0, j)),
        ],
        out_specs=pl.BlockSpec((MOD_ROWS, tn), lambda j: (0, j)),
        out_shape=jax.ShapeDtypeStruct((MOD_ROWS, n_cols), F32),
        compiler_params=_params(),
        name="ada_rows",
    )(c_rows, w_ada, b_ada3)


def _pool_group(src, window, seq_len, p2_ref, p4_ref):
    half = window // 2
    rows = lambda ref, off: ref[POOL_PAD + off:POOL_PAD + off + seq_len, :]
    if window <= 4:
        total = rows(src, -half)
        for off in range(-half + 1, half):
            total = total + rows(src, off)
    else:
        n2 = seq_len + 2 * POOL_PAD - 8
        p2_ref[0:n2, :] = src[0:n2, :] + src[1:n2 + 1, :]
        if window == 8:
            terms = [rows(p2_ref, off) for off in (-4, -2, 0, 2)]
        else:
            assert window == 16
            n4 = n2 - 8
            p4_ref[0:n4, :] = p2_ref[0:n4, :] + p2_ref[2:n4 + 2, :]
            terms = [rows(p4_ref, off) for off in (-8, -4, 0, 4)]
        total = (terms[0] + terms[1]) + (terms[2] + terms[3])
    t = lax.broadcasted_iota(jnp.int32, (seq_len, POOL_GW), 0)
    count = jnp.minimum(t + half, seq_len) - jnp.maximum(t - half, 0)
    return total / count.astype(F32) - rows(src, 0)


def _mix_kernel(x_ref, mod_ref, npre_ref, w_ref, qg_ref, kg_ref, cos_ref, sin_ref, wpool_ref,
                pscale_ref, *rest, seq_len, n_seq, n_cache, rope, n_cast, cast_layer, kv_first_layer,
                n_kv_alias, heads_per_step):
    if n_cache:
        (ck_ref, cv_ref), rest = rest[:2], rest[2:]
    w32_refs, rest = rest[:n_cast], rest[n_cast:]
    rest = rest[n_kv_alias:]
    if n_cache:
        (ao_out, y_out), rest = rest[:2], rest[2:]
    else:
        (ao_out, y_out, nk_out, nv_out), rest = rest[:4], rest[4:]
    wb_outs, rest = rest[:n_cast], rest[n_cast:]
    (qa_ref, k_ref, vt_ref, p_ref, raw_a, raw_b, s_a, s_b, e_a, e_b, p2_a, p2_b, p4_b), rest = (
        rest[:13], rest[13:])
    raw_ref, s_ref, e_ref = (raw_a, raw_b), (s_a, s_b), (e_a, e_b)

    if n_cast:
        stage_in, stage_out, sems = rest[0:n_cast], rest[n_cast:2 * n_cast], rest[2 * n_cast]
        grid_step = pl.program_id(0)
        last_grid_step = pl.num_programs(0) - 1

        def copy_in(j, at_step):
            n = stage_in[j].shape[0]
            row0 = cast_layer * (w32_refs[j].shape[0] // DEPTH) + at_step * n
            return pltpu.make_async_copy(w32_refs[j].at[pl.ds(row0, n), :], stage_in[j], sems.at[j])

        def copy_out(j, at_step):
            n = stage_out[j].shape[0]
            return pltpu.make_async_copy(stage_out[j], wb_outs[j].at[pl.ds(at_step * n, n), :],
                                         sems.at[n_cast + j])

        for j in range(n_cast):
            copy_in(j, grid_step).start()
    n_keys = n_cache + seq_len
    tiles_per_seq = seq_len // Q_TILE
    n_tiles = n_seq * tiles_per_seq

    mod_row = pl.program_id(0) if n_cache else CTX_MOD_ROW
    sh1 = mod_ref[pl.ds(mod_row, 1), 0:D_MODEL]
    norm_gain = npre_ref[...] * (1.0 + mod_ref[pl.ds(mod_row, 1), D_MODEL:2 * D_MODEL])

    lane = lax.broadcasted_iota(jnp.int32, (Q_TILE, LANES), 1)
    first_head = lane < HEAD_DIM
    second_half = (lane & (HEAD_DIM // 2)) != 0

    def head_norm(t, gain):
        t2 = t * t
        s_lo = jnp.sum(jnp.where(first_head, t2, 0.0), axis=-1, keepdims=True)
        s_hi = jnp.sum(jnp.where(first_head, 0.0, t2), axis=-1, keepdims=True)
        ms = jnp.where(first_head, s_lo, s_hi) * (1.0 / HEAD_DIM)
        return (t * lax.rsqrt(ms + EPS)) * gain

    def rotary(t, pos0):
        partner = jnp.where(second_half, pltpu.roll(t, HEAD_DIM // 2, 1),
                            pltpu.roll(t, LANES - HEAD_DIM // 2, 1))
        return t * cos_ref[pos0:pos0 + Q_TILE, :] + partner * sin_ref[pos0:pos0 + Q_TILE, :]

    ones_rows = jnp.ones((VT_ROWS - HEAD_DIM, n_keys), BF16)
    pad_rows = jnp.zeros((POOL_PAD, POOL_GW), F32)
    for s_idx in range(n_seq):
        for g in range(len(POOL_WINDOWS)):
            p_ref[s_idx, g, 0:POOL_PAD, :] = pad_rows
            p_ref[s_idx, g, POOL_PAD + seq_len:2 * POOL_PAD + seq_len, :] = pad_rows
        for kv in range(N_KV_HEADS):
            vt_ref[s_idx, kv * VT_ROWS + HEAD_DIM:(kv + 1) * VT_ROWS, :] = ones_rows
        if n_cache:
            k_ref[s_idx, 0:n_cache, :] = ck_ref[0, 0].astype(BF16)
            cvt = cv_ref[0, 0].T.astype(BF16)
            for kv in range(N_KV_HEADS):
                vt_ref[s_idx, kv * VT_ROWS:kv * VT_ROWS + HEAD_DIM, 0:n_cache] = (
                    cvt[kv * HEAD_DIM:(kv + 1) * HEAD_DIM, :])

    def project(pair_idx, slot):
        r0 = pair_idx * PROJ_TILES * Q_TILE
        x = x_ref[r0:r0 + PROJ_TILES * Q_TILE, :]
        hb = (_rms(x) * norm_gain + sh1).astype(BF16)
        raw_ref[slot][...] = _dot(hb, w_ref[...])

    def epilogue(tile, slot):
        s_idx, pos0 = tile // tiles_per_seq, (tile % tiles_per_seq) * Q_TILE
        r0 = tile * Q_TILE
        h0 = (tile % PROJ_TILES) * Q_TILE
        raw = raw_ref[slot].at[h0:h0 + Q_TILE, :]
        for j in range(ATTN_W // LANES):
            t = head_norm(raw[:, j * LANES:(j + 1) * LANES], qg_ref[...])
            if rope:
                t = rotary(t, pos0)
            t = t * (HEAD_DIM ** -0.5 * LOG2_E)
            swapped = pltpu.roll(t, HEAD_DIM, 1)
            if (2 * j) // Q_PER_KV == 0:
                even = jnp.where(first_head, t, 0.0)
                odd = jnp.where(first_head, swapped, 0.0)
            else:
                even = jnp.where(first_head, 0.0, swapped)
                odd = jnp.where(first_head, 0.0, t)
            qa_ref[tile, (2 * j) * Q_TILE:(2 * j + 1) * Q_TILE, :] = even.astype(BF16)
            qa_ref[tile, (2 * j + 1) * Q_TILE:(2 * j + 2) * Q_TILE, :] = odd.astype(BF16)
        kn = head_norm(raw[:, ATTN_W:ATTN_W + KV_W], kg_ref[...])
        v = raw[:, ATTN_W + KV_W:ATTN_W + 2 * KV_W]
        vt_f32 = v.T
        if not n_cache:
            if kv_first_layer:
                nk_out[s_idx, 0, :, pos0:pos0 + Q_TILE] = kn.T
                nv_out[s_idx, 0, :, pos0:pos0 + Q_TILE] = vt_f32
                later = jnp.zeros((DEPTH - 1, KV_W, Q_TILE), F32)
                nk_out[s_idx, 1:DEPTH, :, pos0:pos0 + Q_TILE] = later
                nv_out[s_idx, 1:DEPTH, :, pos0:pos0 + Q_TILE] = later
            else:
                nk_out[s_idx, :, pos0:pos0 + Q_TILE] = kn.T
                nv_out[s_idx, :, pos0:pos0 + Q_TILE] = vt_f32
        if rope:
            kn = rotary(kn, pos0)
        k0 = n_cache + pos0
        k_ref[s_idx, k0:k0 + Q_TILE, :] = kn.astype(BF16)
        vt = vt_f32.astype(BF16)
        for kv in range(N_KV_HEADS):
            vt_ref[s_idx, kv * VT_ROWS:kv * VT_ROWS + HEAD_DIM, k0:k0 + Q_TILE] = (
                vt[kv * HEAD_DIM:(kv + 1) * HEAD_DIM, :])
        for g in range(len(POOL_WINDOWS)):
            c0 = ATTN_W + 2 * KV_W + g * POOL_GW
            p_ref[s_idx, g, POOL_PAD + pos0:POOL_PAD + pos0 + Q_TILE, :] = raw[:, c0:c0 + POOL_GW]

    n_proj = n_tiles // PROJ_TILES
    project(0, 0)
    for pair_idx in range(n_proj):
        if pair_idx + 1 < n_proj:
            project(pair_idx + 1, (pair_idx + 1) % 2)
        for half in range(PROJ_TILES):
            epilogue(pair_idx * PROJ_TILES + half, pair_idx % 2)

    n_groups = N_Q_HEADS // heads_per_step
    n_steps = n_tiles * n_groups

    def scores(step, slot):
        tile, grp = step // n_groups, step % n_groups
        qs = qa_ref[tile, grp * heads_per_step * Q_TILE:(grp + 1) * heads_per_step * Q_TILE, :]
        s_ref[slot][...] = _dot_nt(k_ref[tile // tiles_per_seq], qs)

    def exponent(step, slot):
        s = s_ref[slot][...]
        e_ref[slot][...] = jnp.exp2(s - jnp.max(s, axis=0, keepdims=True)).astype(BF16)

    def values(step, slot):
        tile, grp = step // n_groups, step % n_groups
        head0 = grp * heads_per_step
        for kv in range(head0 // Q_PER_KV, (head0 + heads_per_step - 1) // Q_PER_KV + 1):
            lo = max(kv * Q_PER_KV, head0) - head0
            hi = min((kv + 1) * Q_PER_KV, head0 + heads_per_step) - head0
            vt = vt_ref[tile // tiles_per_seq, kv * VT_ROWS:(kv + 1) * VT_ROWS, :]
            ot = _dot(vt, e_ref[slot][:, lo * Q_TILE:hi * Q_TILE])
            ot = ot[0:HEAD_DIM] * (1.0 / ot[HEAD_DIM:HEAD_DIM + 1])
            for j in range(0, hi - lo, 2):
                pair = (head0 + lo + j) // 2
                both = jnp.concatenate([ot[:, j * Q_TILE:(j + 1) * Q_TILE],
                                        ot[:, (j + 1) * Q_TILE:(j + 2) * Q_TILE]], axis=0)
                ao_out[tile * Q_TILE:(tile + 1) * Q_TILE, pair * LANES:(pair + 1) * LANES] = (
                    both.T.astype(BF16))

    scores(0, 0)
    for step in range(n_steps + 1):
        if step + 1 < n_steps:
            scores(step + 1, (step + 1) % 2)
        if step < n_steps:
            exponent(step, step % 2)
        if step >= 1:
            values(step - 1, (step - 1) % 2)

    for s_idx in range(n_seq):
        r0 = s_idx * seq_len
        for g, window in enumerate(POOL_WINDOWS):
            cols = slice(g * POOL_GW, (g + 1) * POOL_GW)
            p2_ref = p2_a if window == 8 else p2_b
            d = _pool_group(p_ref.at[s_idx, g], window, seq_len, p2_ref, p4_b)
            yg = _dot(d.astype(BF16), wpool_ref[g]) * pscale_ref[:, cols]
            y_out[r0:r0 + seq_len, cols] = yg.astype(BF16)

    if n_cast:
        @pl.when(grid_step > 0)
        def _():
            for j in range(n_cast):
                copy_out(j, grid_step - 1).wait()

        for j in range(n_cast):
            copy_in(j, grid_step).wait()
            stage_out[j][...] = stage_in[j][...].astype(BF16)
            copy_out(j, grid_step).start()

        @pl.when(grid_step == last_grid_step)
        def _():
            for j in range(n_cast):
                copy_out(j, grid_step).wait()


def _mix_call(l, x, mod_rows, n_pre, w_in_l, q_gain, k_gain, cos_t, sin_t, w_pool, pool_scale,
              cache_k, cache_v, cast_weights_f32, cast_layer, new_kv=None, *, seq_len, n_seq, rope):
    n_tok = x.shape[0]
    rows = n_seq * seq_len
    is_lat = cache_k is not None
    n_cache = cache_k.shape[2] if is_lat else 0
    n_keys = n_cache + seq_len
    assert n_seq == 1 or not is_lat
    lay = lambda i: (l, 0, 0)
    row = lambda i: (i, 0)
    in_specs = [
        pl.BlockSpec((rows, D_MODEL), row),
        pl.BlockSpec((MOD_ROWS, N_MOD * D_MODEL), lambda i: (0, 0)),
        pl.BlockSpec((None, 1, D_MODEL), lay),
        pl.BlockSpec((D_MODEL, GATE_COL0), lambda i: (0, 0)),
        pl.BlockSpec((None, 1, LANES), lay),
        pl.BlockSpec((None, 1, LANES), lay),
        pl.BlockSpec((seq_len, LANES), lambda i: (0, 0)),
        pl.BlockSpec((seq_len, LANES), lambda i: (0, 0)),
        pl.BlockSpec((None, len(POOL_WINDOWS), POOL_GW, POOL_GW), lambda i: (l, 0, 0, 0)),
        pl.BlockSpec((None, 1, POOL_W), lay),
    ]
    args = [x, mod_rows, n_pre, w_in_l, q_gain, k_gain, cos_t, sin_t, w_pool, pool_scale]
    out_shape = [jax.ShapeDtypeStruct((n_tok, ATTN_W), BF16), jax.ShapeDtypeStruct((n_tok, POOL_W), BF16)]
    out_specs = [pl.BlockSpec((rows, ATTN_W), row), pl.BlockSpec((rows, POOL_W), row)]
    if is_lat:
        cache_spec = pl.BlockSpec((1, 1, n_cache, KV_W), lambda i: (i, l, 0, 0))
        in_specs += [cache_spec, cache_spec]
        args += [cache_k, cache_v]
    else:
        out_shape += [jax.ShapeDtypeStruct((n_tok // seq_len, DEPTH, KV_W, seq_len), F32)] * 2
        if new_kv is None:
            out_specs += [pl.BlockSpec((n_seq, DEPTH, KV_W, seq_len), lambda i: (i, 0, 0, 0))] * 2
        else:
            out_specs += [pl.BlockSpec((n_seq, None, KV_W, seq_len), lambda i: (i, l, 0, 0))] * 2
    n_cast = len(cast_weights_f32)
    n_steps = n_tok // rows
    any_spec = pl.BlockSpec(memory_space=pl.ANY)
    in_specs += [any_spec] * n_cast
    args += [w.reshape(-1, w.shape[2]) for w in cast_weights_f32]
    aliases = {}
    if new_kv is not None:
        aliases = {len(args): 2, len(args) + 1: 3}
        in_specs += [any_spec] * 2
        args += list(new_kv)
    out_shape += [jax.ShapeDtypeStruct(w.shape[1:], BF16) for w in cast_weights_f32]
    out_specs += [any_spec] * n_cast
    slices = [(w.shape[1] // n_steps, w.shape[2]) for w in cast_weights_f32]
    assert all(w.shape[1] % n_steps == 0 and r % 16 == 0 for w, (r, _) in zip(cast_weights_f32, slices))
    scratch = [
        pltpu.VMEM((rows // Q_TILE, N_Q_HEADS * Q_TILE, LANES), BF16),
        pltpu.VMEM((n_seq, n_keys, KV_W), BF16),
        pltpu.VMEM((n_seq, N_KV_HEADS * VT_ROWS, n_keys), BF16),
        pltpu.VMEM((n_seq, len(POOL_WINDOWS), seq_len + 2 * POOL_PAD, POOL_GW), F32),
    ]
    assert (rows // Q_TILE) % PROJ_TILES == 0
    scratch += [pltpu.VMEM((PROJ_TILES * Q_TILE, GATE_COL0), F32)] * 2
    heads_per_step = N_Q_HEADS if n_keys <= Q_TILE else 2
    scratch += [pltpu.VMEM((n_keys, heads_per_step * Q_TILE), F32)] * 2
    scratch += [pltpu.VMEM((n_keys, heads_per_step * Q_TILE), BF16)] * 2
    scratch += [pltpu.VMEM((seq_len + 2 * POOL_PAD - 8, POOL_GW), F32)] * 2
    scratch += [pltpu.VMEM((seq_len + 2 * POOL_PAD - 16, POOL_GW), F32)]
    if n_cast:
        scratch += [pltpu.VMEM(s, F32) for s in slices] + [pltpu.VMEM(s, BF16) for s in slices]
        scratch += [pltpu.SemaphoreType.DMA((2 * n_cast,))]
    return pl.pallas_call(
        functools.partial(_mix_kernel, seq_len=seq_len, n_seq=n_seq, n_cache=n_cache, rope=rope,
                          n_cast=n_cast, cast_layer=cast_layer,
                          kv_first_layer=not is_lat and new_kv is None, n_kv_alias=len(aliases),
                          heads_per_step=heads_per_step),
        grid=(n_tok // rows,),
        in_specs=in_specs,
        out_specs=out_specs,
        out_shape=out_shape,
        scratch_shapes=scratch,
        input_output_aliases=aliases,
        compiler_params=_params(),
        name="mix_lat" if is_lat else "mix_ctx",
    )(*args)


def _tail_kernel(x_ref, ao_ref, y_ref, mod_ref, npre_mix_ref, npost_mix_ref, npre_ffn_ref,
                 npost_ffn_ref, win_ref, wau_ref, wpu_ref, wout_ref, wg_ref, wu_ref, wd_ref, *rest,
                 sub, tiles_per_mod_row, next_rows):
    if next_rows:
        c_ref, wada_ref, bada_ref, o_ref, next_mod_out, x1_a, x1_b = rest
        next_mod_out[...] = _ada_rows(c_ref, wada_ref, bada_ref)
    else:
        o_ref, x1_a, x1_b = rest
    mod_row = pl.program_id(0) // tiles_per_mod_row if tiles_per_mod_row else CTX_MOD_ROW
    sh1, sc1, g1, sh2, sc2, g2 = [mod_ref[pl.ds(mod_row, 1), i * D_MODEL:(i + 1) * D_MODEL]
                                  for i in range(N_MOD)]
    x1_ref = (x1_a, x1_b)
    n_sub = x_ref.shape[0] // sub

    def post(i, slot):
        r0 = i * sub
        x = x_ref[r0:r0 + sub, :]
        hb = ((_rms(x) * npre_mix_ref[...]) * (1.0 + sc1) + sh1).astype(BF16)
        g_attn = jax.nn.sigmoid(_dot(hb, win_ref[:, GATE_COL0:GATE_COL0 + D_MODEL]))
        g_pool = jax.nn.sigmoid(_dot(hb, win_ref[:, GATE_COL0 + D_MODEL:IN_COLS]))
        attn = _dot(ao_ref[r0:r0 + sub, :], wau_ref[...])
        pool = _dot(y_ref[r0:r0 + sub, :], wpu_ref[...])
        mixed = g_attn * attn + g_pool * pool
        m = _dot(mixed.astype(BF16), wout_ref[...])
        x1_ref[slot][...] = x + g1 * (_rms(m) * npost_mix_ref[...])

    def ffn(i, slot):
        r0 = i * sub
        x1 = x1_ref[slot][...]
        hb = ((_rms(x1) * npre_ffn_ref[...]) * (1.0 + sc2) + sh2).astype(BF16)
        g = _dot(hb, wg_ref[...])
        u = _dot(hb, wu_ref[...])
        a = ((g * jax.nn.sigmoid(g)) * u).astype(BF16)
        f = _dot(a, wd_ref[...])
        o_ref[r0:r0 + sub, :] = x1 + g2 * (_rms(f) * npost_ffn_ref[...])

    post(0, 0)
    for i in range(n_sub):
        if i + 1 < n_sub:
            post(i + 1, (i + 1) % 2)
        ffn(i, i % 2)


def _tail_call(l, x, ao, y, mod_l, n_pre_mix, n_post_mix, n_pre_ffn, n_post_ffn, w_in,
               w_attn_up, w_pool_up, w_out, w_ffn_gate, w_ffn_up, w_ffn_down, next_mod_inputs,
               *, seq_len, per_seq_mod, tm, sub):
    n_tok = x.shape[0]
    n_tiles = n_tok // tm
    lay = lambda i: (l, 0, 0)
    row = lambda i: (i, 0)
    vec_spec = pl.BlockSpec((None, 1, D_MODEL), lay)
    layer_resident = lambda r, c: pl.BlockSpec((r, c), lambda i: (0, 0), pipeline_mode=pl.Buffered(1))
    in_specs = [
        pl.BlockSpec((tm, D_MODEL), row),
        pl.BlockSpec((tm, ATTN_W), row),
        pl.BlockSpec((tm, POOL_W), row),
        pl.BlockSpec((MOD_ROWS, N_MOD * D_MODEL), lambda i: (0, 0)),
        vec_spec, vec_spec, vec_spec, vec_spec,
        layer_resident(D_MODEL, IN_COLS),
        layer_resident(ATTN_W, D_MODEL),
        layer_resident(POOL_W, D_MODEL),
        layer_resident(D_MODEL, D_MODEL),
        layer_resident(D_MODEL, FFN_HIDDEN),
        layer_resident(D_MODEL, FFN_HIDDEN),
        layer_resident(FFN_HIDDEN, D_MODEL),
    ]
    args = [x, ao, y, mod_l, n_pre_mix, n_post_mix, n_pre_ffn, n_post_ffn, w_in, w_attn_up,
            w_pool_up, w_out, w_ffn_gate, w_ffn_up, w_ffn_down]
    out_specs = [pl.BlockSpec((tm, D_MODEL), row)]
    out_shape = [jax.ShapeDtypeStruct((n_tok, D_MODEL), F32)]
    if next_mod_inputs is not None:
        n_cols = N_MOD * D_MODEL
        tn = n_cols // n_tiles
        assert n_cols % n_tiles == 0 and tn % LANES == 0
        in_specs += [
            pl.BlockSpec((MOD_ROWS, D_MODEL), lambda i: (0, 0)),
            pl.BlockSpec((None, D_MODEL, tn), lambda i: (l + 1, 0, i)),
            pl.BlockSpec((None, 1, tn), lambda i: (l + 1, 0, i)),
        ]
        args += list(next_mod_inputs)
        out_specs += [pl.BlockSpec((MOD_ROWS, tn), lambda i: (0, i))]
        out_shape += [jax.ShapeDtypeStruct((MOD_ROWS, n_cols), F32)]
    return pl.pallas_call(
        functools.partial(_tail_kernel, sub=sub, next_rows=next_mod_inputs is not None,
                          tiles_per_mod_row=seq_len // tm if per_seq_mod else 0),
        grid=(n_tiles,),
        in_specs=in_specs,
        out_specs=out_specs,
        out_shape=out_shape,
        scratch_shapes=[pltpu.VMEM((sub, D_MODEL), F32)] * 2,
        compiler_params=_params(),
        name="tail",
    )(*args)


def _rope_tables(n):
    rows = n // GRID_W
    row = np.repeat(np.arange(rows), GRID_W).astype(np.float32)
    col = np.tile(np.arange(GRID_W), rows).astype(np.float32)
    n_freq = HEAD_DIM // 4
    inv = jnp.asarray(ROPE_THETA, F32) ** (-jnp.arange(n_freq, dtype=F32) / n_freq)
    ang = jnp.concatenate([jnp.asarray(row)[:, None] * inv[None, :],
                           jnp.asarray(col)[:, None] * inv[None, :]], axis=-1)
    cos, sin = jnp.cos(ang), jnp.sin(ang)
    cos_t = jnp.tile(cos, (1, LANES // (HEAD_DIM // 2)))
    sin_t = jnp.tile(jnp.concatenate([-sin, sin], axis=-1), (1, LANES // HEAD_DIM))
    return cos_t, sin_t


def kernel(x_prompt, x_sample, cache_k, cache_v, c, c_ctx, w_ada, b_ada, w_in, q_norm, k_norm,
           w_attn_up, w_pool, pool_scale, w_pool_up, w_out, n_pre_mix, n_post_mix, n_pre_ffn,
           n_post_ffn, w_ffn_gate, w_ffn_up, w_ffn_down):
    batch, seq, _ = x_prompt.shape
    dec_batch, dec_seq, _ = x_sample.shape
    past_len = cache_k.shape[2]
    assert dec_batch <= CTX_MOD_ROW

    c_rows = jnp.concatenate(
        [c, c_ctx[None, :], jnp.zeros((MOD_ROWS - dec_batch - 1, D_MODEL), F32)], axis=0)
    b_ada3 = b_ada.reshape(DEPTH, 1, N_MOD * D_MODEL)
    mod_l = _ada_call(c_rows, w_ada, b_ada3)

    cos_t, sin_t = _rope_tables(dec_seq)
    vec = lambda a: a.reshape(DEPTH, 1, a.shape[-1])
    q_gain = vec(jnp.tile(q_norm, (1, LANES // HEAD_DIM)))
    k_gain = vec(jnp.tile(k_norm, (1, LANES // HEAD_DIM)))
    w_in_l = w_in[0].astype(BF16)
    w_pool_b = w_pool.astype(BF16)
    tail_weights_f32 = (w_attn_up, w_pool_up, w_out, w_ffn_gate, w_ffn_up, w_ffn_down)
    n_pre_mix_v, n_post_mix_v = vec(n_pre_mix), vec(n_post_mix)
    n_pre_ffn_v, n_post_ffn_v = vec(n_pre_ffn), vec(n_post_ffn)
    pool_scale_v = vec(pool_scale)
    ck = cache_k.reshape(dec_batch, DEPTH, past_len, KV_W)
    cv = cache_v.reshape(dec_batch, DEPTH, past_len, KV_W)

    tm = 512
    ctx_seqs_per_step = 4

    y = x_prompt.reshape(batch * seq, D_MODEL)
    z = x_sample.reshape(dec_batch * dec_seq, D_MODEL)
    new_kv = None
    for l in range(DEPTH):
        has_next = l + 1 < DEPTH
        outs = _mix_call(l, y, mod_l, n_pre_mix_v, w_in_l, q_gain, k_gain, cos_t, sin_t, w_pool_b,
                         pool_scale_v, None, None, tail_weights_f32, l, new_kv,
                         seq_len=seq, n_seq=ctx_seqs_per_step, rope=False)
        ao, yp = outs[:2]
        new_kv = outs[2:4]
        tail_weights_b = outs[4:]
        tail_outs = _tail_call(l, y, ao, yp, mod_l, n_pre_mix_v, n_post_mix_v, n_pre_ffn_v, n_post_ffn_v,
                               w_in_l, *tail_weights_b, (c_rows, w_ada, b_ada3) if has_next else None,
                               seq_len=seq, per_seq_mod=False, tm=tm, sub=256)
        y = tail_outs[0]
        outs = _mix_call(l, z, mod_l, n_pre_mix_v, w_in_l, q_gain, k_gain, cos_t, sin_t, w_pool_b,
                         pool_scale_v, ck, cv, (w_in,) if has_next else (), l + 1,
                         seq_len=dec_seq, n_seq=1, rope=True)
        ao, yp = outs[:2]
        z = _tail_call(l, z, ao, yp, mod_l, n_pre_mix_v, n_post_mix_v, n_pre_ffn_v, n_post_ffn_v,
                       w_in_l, *tail_weights_b, None,
                       seq_len=dec_seq, per_seq_mod=True, tm=tm, sub=256)[0]
        if has_next:
            w_in_l = outs[2]
            mod_l = tail_outs[1]

    def kv_layout(a):
        a = a.reshape(batch, DEPTH, N_KV_HEADS, HEAD_DIM, seq)
        return jnp.transpose(a, (0, 1, 4, 2, 3))

    return (y.reshape(batch, seq, D_MODEL), z.reshape(dec_batch, dec_seq, D_MODEL),
            kv_layout(new_kv[0]), kv_layout(new_kv[1]))
```

```python
import functools

import jax
import jax.numpy as jnp
import numpy as np
from jax import lax
from jax.experimental import pallas as pl
from jax.experimental.pallas import tpu as pltpu

D_MODEL = 1024
DEPTH = 4
GRID_W = 64
HEAD_DIM = 64
N_Q_HEADS = 8
N_KV_HEADS = 2
Q_PER_KV = N_Q_HEADS // N_KV_HEADS
ATTN_W = N_Q_HEADS * HEAD_DIM
KV_W = N_KV_HEADS * HEAD_DIM
POOL_WINDOWS = (2, 4, 8, 16)
POOL_W = D_MODEL // 2
POOL_GW = POOL_W // len(POOL_WINDOWS)
GATE_COL0 = ATTN_W + 2 * KV_W + POOL_W
IN_COLS = GATE_COL0 + 2 * D_MODEL
FFN_HIDDEN = 2816
N_MOD = 6
ROPE_THETA = 10000.0
EPS = 1e-6

LANES = 128
Q_TILE = 256
VT_ROWS = 80
POOL_PAD = 24
LOG2_E = 1.4426950408889634
MOD_ROWS = 8
CTX_MOD_ROW = 4
VMEM_LIMIT_BYTES = 56 * 1024 * 1024

F32 = jnp.float32
BF16 = jnp.bfloat16


def _dot(a, b):
    return jnp.dot(a, b, preferred_element_type=F32)


def _dot_nt(a, b):
    return lax.dot_general(a, b, (((1,), (1,)), ((), ())), preferred_element_type=F32)


def _rms(x):
    return x * lax.rsqrt(jnp.mean(x * x, axis=-1, keepdims=True) + EPS)


def _params(n_axes=1):
    return pltpu.CompilerParams(dimension_semantics=("arbitrary",) * n_axes,
                                vmem_limit_bytes=VMEM_LIMIT_BYTES)


def _ada_rows(c_ref, w_ref, b_ref):
    c = c_ref[...]
    s = (c * jax.nn.sigmoid(c)).astype(BF16)
    return _dot(s, w_ref[...].astype(BF16)) + b_ref[...]


def _ada_kernel(c_ref, w_ref, b_ref, o_ref):
    o_ref[...] = _ada_rows(c_ref, w_ref, b_ref)


def _ada_call(c_rows, w_ada, b_ada3):
    tn = 1536
    n_cols = N_MOD * D_MODEL
    return pl.pallas_call(
        _ada_kernel,
        grid=(n_cols // tn,),
        in_specs=[
            pl.BlockSpec((MOD_ROWS, D_MODEL), lambda j: (0, 0)),
            pl.BlockSpec((None, D_MODEL, tn), lambda j: (0, 0, j)),
            pl.BlockSpec((None, 1, tn), lambda j: (0, 0, j)),
        ],
        out_specs=pl.BlockSpec((MOD_ROWS, tn), lambda j: (0, j)),
        out_shape=jax.ShapeDtypeStruct((MOD_ROWS, n_cols), F32),
        compiler_params=_params(),
        name="ada_rows",
    )(c_rows, w_ada, b_ada3)


def _pool_group(src, window, seq_len, p2_ref, p4_ref):
    half = window // 2
    rows = lambda ref, off: ref[POOL_PAD + off:POOL_PAD + off + seq_len, :]
    if window <= 4:
        total = rows(src, -half)
        for off in range(-half + 1, half):
            total = total + rows(src, off)
    else:
        n2 = seq_len + 2 * POOL_PAD - 8
        p2_ref[0:n2, :] = src[0:n2, :] + src[1:n2 + 1, :]
        if window == 8:
            terms = [rows(p2_ref, off) for off in (-4, -2, 0, 2)]
        else:
            assert window == 16
            n4 = n2 - 8
            p4_ref[0:n4, :] = p2_ref[0:n4, :] + p2_ref[2:n4 + 2, :]
            terms = [rows(p4_ref, off) for off in (-8, -4, 0, 4)]
        total = (terms[0] + terms[1]) + (terms[2] + terms[3])
    t = lax.broadcasted_iota(jnp.int32, (seq_len, POOL_GW), 0)
    count = jnp.minimum(t + half, seq_len) - jnp.maximum(t - half, 0)
    return total / count.astype(F32) - rows(src, 0)


def _mix_kernel(x_ref, mod_ref, npre_ref, w_ref, qg_ref, kg_ref, cos_ref, sin_ref, wpool_ref,
                pscale_ref, *rest, layer, seq_len, n_seq, n_cache, rope, n_cast, cast_layer,
                kv_first_layer, n_kv_alias, heads_per_step):
    if n_cache:
        (ck_ref, cv_ref), rest = rest[:2], rest[2:]
    w32_refs, rest = rest[:n_cast], rest[n_cast:]
    rest = rest[n_kv_alias:]
    if n_cache:
        (ao_out, y_out), rest = rest[:2], rest[2:]
    else:
        (ao_out, y_out, nk_out, nv_out), rest = rest[:4], rest[4:]
    wb_outs, rest = rest[:n_cast], rest[n_cast:]
    (qa_ref, k_ref, vt_ref, p_ref, raw_a, raw_b, s_a, s_b, e_a, e_b, p2_a, p2_b, p4_b), rest = (
        rest[:13], rest[13:])
    raw_ref, s_ref, e_ref = (raw_a, raw_b), (s_a, s_b), (e_a, e_b)

    if n_cast:
        stage_in, stage_out, sems = rest[0:n_cast], rest[n_cast:2 * n_cast], rest[2 * n_cast]
        grid_step = pl.program_id(0)
        last_grid_step = pl.num_programs(0) - 1

        def copy_in(j, at_step):
            n = stage_in[j].shape[0]
            row0 = cast_layer * (w32_refs[j].shape[0] // DEPTH) + at_step * n
            return pltpu.make_async_copy(w32_refs[j].at[pl.ds(row0, n), :], stage_in[j], sems.at[j])

        def copy_out(j, at_step):
            n = stage_out[j].shape[0]
            return pltpu.make_async_copy(stage_out[j], wb_outs[j].at[pl.ds(at_step * n, n), :],
                                         sems.at[n_cast + j])

        for j in range(n_cast):
            copy_in(j, grid_step).start()
    n_keys = n_cache + seq_len
    tiles_per_seq = seq_len // Q_TILE
    n_tiles = n_seq * tiles_per_seq

    mod_row = pl.program_id(0) if n_cache else CTX_MOD_ROW
    sh1 = mod_ref[pl.ds(mod_row, 1), 0:D_MODEL]
    this_layer = slice(layer, layer + 1)
    norm_gain = npre_ref[this_layer, :] * (1.0 + mod_ref[pl.ds(mod_row, 1), D_MODEL:2 * D_MODEL])

    lane = lax.broadcasted_iota(jnp.int32, (Q_TILE, LANES), 1)
    first_head = lane < HEAD_DIM
    second_half = (lane & (HEAD_DIM // 2)) != 0

    def head_norm(t, gain):
        t2 = t * t
        s_lo = jnp.sum(jnp.where(first_head, t2, 0.0), axis=-1, keepdims=True)
        s_hi = jnp.sum(jnp.where(first_head, 0.0, t2), axis=-1, keepdims=True)
        ms = jnp.where(first_head, s_lo, s_hi) * (1.0 / HEAD_DIM)
        return (t * lax.rsqrt(ms + EPS)) * gain

    def rotary(t, pos0):
        partner = jnp.where(second_half, pltpu.roll(t, HEAD_DIM // 2, 1),
                            pltpu.roll(t, LANES - HEAD_DIM // 2, 1))
        return t * cos_ref[pos0:pos0 + Q_TILE, :] + partner * sin_ref[pos0:pos0 + Q_TILE, :]

    ones_rows = jnp.ones((VT_ROWS - HEAD_DIM, n_keys), BF16)
    pad_rows = jnp.zeros((POOL_PAD, POOL_GW), F32)
    for s_idx in range(n_seq):
        for g in range(len(POOL_WINDOWS)):
            p_ref[s_idx, g, 0:POOL_PAD, :] = pad_rows
            p_ref[s_idx, g, POOL_PAD + seq_len:2 * POOL_PAD + seq_len, :] = pad_rows
        for kv in range(N_KV_HEADS):
            vt_ref[s_idx, kv * VT_ROWS + HEAD_DIM:(kv + 1) * VT_ROWS, :] = ones_rows
        if n_cache:
            k_ref[s_idx, 0:n_cache, :] = ck_ref[0, 0].astype(BF16)
            cvt = cv_ref[0, 0].T.astype(BF16)
            for kv in range(N_KV_HEADS):
                vt_ref[s_idx, kv * VT_ROWS:kv * VT_ROWS + HEAD_DIM, 0:n_cache] = (
                    cvt[kv * HEAD_DIM:(kv + 1) * HEAD_DIM, :])

    def project(tile, slot):
        r0 = tile * Q_TILE
        x = x_ref[r0:r0 + Q_TILE, :]
        hb = (_rms(x) * norm_gain + sh1).astype(BF16)
        raw_ref[slot][...] = _dot(hb, w_ref[...])

    def epilogue(tile, slot):
        s_idx, pos0 = tile // tiles_per_seq, (tile % tiles_per_seq) * Q_TILE
        r0 = tile * Q_TILE
        raw = raw_ref[slot]
        for j in range(ATTN_W // LANES):
            t = head_norm(raw[:, j * LANES:(j + 1) * LANES], qg_ref[this_layer, :])
            if rope:
                t = rotary(t, pos0)
            t = t * (HEAD_DIM ** -0.5 * LOG2_E)
            swapped = pltpu.roll(t, HEAD_DIM, 1)
            if (2 * j) // Q_PER_KV == 0:
                even = jnp.where(first_head, t, 0.0)
                odd = jnp.where(first_head, swapped, 0.0)
            else:
                even = jnp.where(first_head, 0.0, swapped)
                odd = jnp.where(first_head, 0.0, t)
            qa_ref[tile, (2 * j) * Q_TILE:(2 * j + 1) * Q_TILE, :] = even.astype(BF16)
            qa_ref[tile, (2 * j + 1) * Q_TILE:(2 * j + 2) * Q_TILE, :] = odd.astype(BF16)
        kn = head_norm(raw[:, ATTN_W:ATTN_W + KV_W], kg_ref[this_layer, :])
        v = raw[:, ATTN_W + KV_W:ATTN_W + 2 * KV_W]
        vt_f32 = v.T
        if not n_cache:
            if kv_first_layer:
                nk_out[s_idx, 0, :, pos0:pos0 + Q_TILE] = kn.T
                nv_out[s_idx, 0, :, pos0:pos0 + Q_TILE] = vt_f32
                later = jnp.zeros((DEPTH - 1, KV_W, Q_TILE), F32)
                nk_out[s_idx, 1:DEPTH, :, pos0:pos0 + Q_TILE] = later
                nv_out[s_idx, 1:DEPTH, :, pos0:pos0 + Q_TILE] = later
            else:
                nk_out[s_idx, :, pos0:pos0 + Q_TILE] = kn.T
                nv_out[s_idx, :, pos0:pos0 + Q_TILE] = vt_f32
        if rope:
            kn = rotary(kn, pos0)
        k0 = n_cache + pos0
        k_ref[s_idx, k0:k0 + Q_TILE, :] = kn.astype(BF16)
        vt = vt_f32.astype(BF16)
        for kv in range(N_KV_HEADS):
            vt_ref[s_idx, kv * VT_ROWS:kv * VT_ROWS + HEAD_DIM, k0:k0 + Q_TILE] = (
                vt[kv * HEAD_DIM:(kv + 1) * HEAD_DIM, :])
        for g in range(len(POOL_WINDOWS)):
            c0 = ATTN_W + 2 * KV_W + g * POOL_GW
            p_ref[s_idx, g, POOL_PAD + pos0:POOL_PAD + pos0 + Q_TILE, :] = raw[:, c0:c0 + POOL_GW]

    project(0, 0)
    for tile in range(n_tiles):
        if tile + 1 < n_tiles:
            project(tile + 1, (tile + 1) % 2)
        epilogue(tile, tile % 2)

    n_groups = N_Q_HEADS // heads_per_step
    n_steps = n_tiles * n_groups

    def scores(step, slot):
        tile, grp = step // n_groups, step % n_groups
        qs = qa_ref[tile, grp * heads_per_step * Q_TILE:(grp + 1) * heads_per_step * Q_TILE, :]
        s_ref[slot][...] = _dot_nt(k_ref[tile // tiles_per_seq], qs)

    def exponent(step, slot):
        s = s_ref[slot][...]
        e_ref[slot][...] = jnp.exp2(s - jnp.max(s, axis=0, keepdims=True)).astype(BF16)

    def values(step, slot):
        tile, grp = step // n_groups, step % n_groups
        head0 = grp * heads_per_step
        for kv in range(head0 // Q_PER_KV, (head0 + heads_per_step - 1) // Q_PER_KV + 1):
            lo = max(kv * Q_PER_KV, head0) - head0
            hi = min((kv + 1) * Q_PER_KV, head0 + heads_per_step) - head0
            vt = vt_ref[tile // tiles_per_seq, kv * VT_ROWS:(kv + 1) * VT_ROWS, :]
            ot = _dot(vt, e_ref[slot][:, lo * Q_TILE:hi * Q_TILE])
            ot = ot[0:HEAD_DIM] * (1.0 / ot[HEAD_DIM:HEAD_DIM + 1])
            for j in range(0, hi - lo, 2):
                pair = (head0 + lo + j) // 2
                both = jnp.concatenate([ot[:, j * Q_TILE:(j + 1) * Q_TILE],
                                        ot[:, (j + 1) * Q_TILE:(j + 2) * Q_TILE]], axis=0)
                ao_out[tile * Q_TILE:(tile + 1) * Q_TILE, pair * LANES:(pair + 1) * LANES] = (
                    both.T.astype(BF16))

    scores(0, 0)
    for step in range(n_steps + 1):
        if step + 1 < n_steps:
            scores(step + 1, (step + 1) % 2)
        if step < n_steps:
            exponent(step, step % 2)
        if step >= 1:
            values(step - 1, (step - 1) % 2)

    for s_idx in range(n_seq):
        r0 = s_idx * seq_len
        for g, window in enumerate(POOL_WINDOWS):
            cols = slice(g * POOL_GW, (g + 1) * POOL_GW)
            p2_ref = p2_a if window == 8 else p2_b
            d = _pool_group(p_ref.at[s_idx, g], window, seq_len, p2_ref, p4_b)
            yg = _dot(d.astype(BF16), wpool_ref[g]) * pscale_ref[this_layer, cols]
            y_out[r0:r0 + seq_len, cols] = yg.astype(BF16)

    if n_cast:
        @pl.when(grid_step > 0)
        def _():
            for j in range(n_cast):
                copy_out(j, grid_step - 1).wait()

        for j in range(n_cast):
            copy_in(j, grid_step).wait()
            stage_out[j][...] = stage_in[j][...].astype(BF16)
            copy_out(j, grid_step).start()

        @pl.when(grid_step == last_grid_step)
        def _():
            for j in range(n_cast):
                copy_out(j, grid_step).wait()


def _mix_call(l, x, mod_rows, n_pre, w_in_l, q_gain, k_gain, cos_t, sin_t, w_pool, pool_scale,
              cache_k, cache_v, cast_weights_f32, cast_layer, new_kv=None, *, seq_len, n_seq, rope):
    n_tok = x.shape[0]
    rows = n_seq * seq_len
    is_lat = cache_k is not None
    n_cache = cache_k.shape[2] if is_lat else 0
    n_keys = n_cache + seq_len
    assert n_seq == 1 or not is_lat
    row = lambda i: (i, 0)
    stacked = lambda width: pl.BlockSpec((DEPTH, width), lambda i: (0, 0))
    in_specs = [
        pl.BlockSpec((rows, D_MODEL), row),
        pl.BlockSpec((MOD_ROWS, N_MOD * D_MODEL), lambda i: (0, 0)),
        stacked(D_MODEL),
        pl.BlockSpec((D_MODEL, GATE_COL0), lambda i: (0, 0)),
        stacked(LANES),
        stacked(LANES),
        pl.BlockSpec((seq_len, LANES), lambda i: (0, 0)),
        pl.BlockSpec((seq_len, LANES), lambda i: (0, 0)),
        pl.BlockSpec((None, len(POOL_WINDOWS), POOL_GW, POOL_GW), lambda i: (l, 0, 0, 0)),
        stacked(POOL_W),
    ]
    args = [x, mod_rows, n_pre, w_in_l, q_gain, k_gain, cos_t, sin_t, w_pool, pool_scale]
    out_shape = [jax.ShapeDtypeStruct((n_tok, ATTN_W), BF16), jax.ShapeDtypeStruct((n_tok, POOL_W), BF16)]
    out_specs = [pl.BlockSpec((rows, ATTN_W), row), pl.BlockSpec((rows, POOL_W), row)]
    if is_lat:
        cache_spec = pl.BlockSpec((1, 1, n_cache, KV_W), lambda i: (i, l, 0, 0))
        in_specs += [cache_spec, cache_spec]
        args += [cache_k, cache_v]
    else:
        out_shape += [jax.ShapeDtypeStruct((n_tok // seq_len, DEPTH, KV_W, seq_len), F32)] * 2
        if new_kv is None:
            out_specs += [pl.BlockSpec((n_seq, DEPTH, KV_W, seq_len), lambda i: (i, 0, 0, 0))] * 2
        else:
            out_specs += [pl.BlockSpec((n_seq, None, KV_W, seq_len), lambda i: (i, l, 0, 0))] * 2
    n_cast = len(cast_weights_f32)
    n_steps = n_tok // rows
    any_spec = pl.BlockSpec(memory_space=pl.ANY)
    in_specs += [any_spec] * n_cast
    args += [w.reshape(-1, w.shape[2]) for w in cast_weights_f32]
    aliases = {}
    if new_kv is not None:
        aliases = {len(args): 2, len(args) + 1: 3}
        in_specs += [any_spec] * 2
        args += list(new_kv)
    out_shape += [jax.ShapeDtypeStruct(w.shape[1:], BF16) for w in cast_weights_f32]
    out_specs += [any_spec] * n_cast
    slices = [(w.shape[1] // n_steps, w.shape[2]) for w in cast_weights_f32]
    assert all(w.shape[1] % n_steps == 0 and r % 16 == 0 for w, (r, _) in zip(cast_weights_f32, slices))
    scratch = [
        pltpu.VMEM((rows // Q_TILE, N_Q_HEADS * Q_TILE, LANES), BF16),
        pltpu.VMEM((n_seq, n_keys, KV_W), BF16),
        pltpu.VMEM((n_seq, N_KV_HEADS * VT_ROWS, n_keys), BF16),
        pltpu.VMEM((n_seq, len(POOL_WINDOWS), seq_len + 2 * POOL_PAD, POOL_GW), F32),
    ]
    scratch += [pltpu.VMEM((Q_TILE, GATE_COL0), F32)] * 2
    heads_per_step = N_Q_HEADS if n_keys <= Q_TILE else 2
    scratch += [pltpu.VMEM((n_keys, heads_per_step * Q_TILE), F32)] * 2
    scratch += [pltpu.VMEM((n_keys, heads_per_step * Q_TILE), BF16)] * 2
    scratch += [pltpu.VMEM((seq_len + 2 * POOL_PAD - 8, POOL_GW), F32)] * 2
    scratch += [pltpu.VMEM((seq_len + 2 * POOL_PAD - 16, POOL_GW), F32)]
    if n_cast:
        scratch += [pltpu.VMEM(s, F32) for s in slices] + [pltpu.VMEM(s, BF16) for s in slices]
        scratch += [pltpu.SemaphoreType.DMA((2 * n_cast,))]
    return pl.pallas_call(
        functools.partial(_mix_kernel, layer=l, seq_len=seq_len, n_seq=n_seq, n_cache=n_cache, rope=rope,
                          n_cast=n_cast, cast_layer=cast_layer,
                          kv_first_layer=not is_lat and new_kv is None, n_kv_alias=len(aliases),
                          heads_per_step=heads_per_step),
        grid=(n_tok // rows,),
        in_specs=in_specs,
        out_specs=out_specs,
        out_shape=out_shape,
        scratch_shapes=scratch,
        input_output_aliases=aliases,
        compiler_params=_params(),
        name="mix_lat" if is_lat else "mix_ctx",
    )(*args)


def _tail_kernel(x_ref, ao_ref, y_ref, mod_ref, npre_mix_ref, npost_mix_ref, npre_ffn_ref,
                 npost_ffn_ref, win_ref, wau_ref, wpu_ref, wout_ref, wg_ref, wu_ref, wd_ref, *rest,
                 layer, sub, tiles_per_mod_row, next_rows):
    this_layer = slice(layer, layer + 1)
    if next_rows:
        c_ref, wada_ref, bada_ref, o_ref, next_mod_out, x1_a, x1_b = rest
        next_mod_out[...] = _ada_rows(c_ref, wada_ref, bada_ref)
    else:
        o_ref, x1_a, x1_b = rest
    mod_row = pl.program_id(0) // tiles_per_mod_row if tiles_per_mod_row else CTX_MOD_ROW
    sh1, sc1, g1, sh2, sc2, g2 = [mod_ref[pl.ds(mod_row, 1), i * D_MODEL:(i + 1) * D_MODEL]
                                  for i in range(N_MOD)]
    x1_ref = (x1_a, x1_b)
    n_sub = x_ref.shape[0] // sub

    def post(i, slot):
        r0 = i * sub
        x = x_ref[r0:r0 + sub, :]
        hb = ((_rms(x) * npre_mix_ref[this_layer, :]) * (1.0 + sc1) + sh1).astype(BF16)
        g_attn = jax.nn.sigmoid(_dot(hb, win_ref[:, GATE_COL0:GATE_COL0 + D_MODEL]))
        g_pool = jax.nn.sigmoid(_dot(hb, win_ref[:, GATE_COL0 + D_MODEL:IN_COLS]))
        attn = _dot(ao_ref[r0:r0 + sub, :], wau_ref[...])
        pool = _dot(y_ref[r0:r0 + sub, :], wpu_ref[...])
        mixed = g_attn * attn + g_pool * pool
        m = _dot(mixed.astype(BF16), wout_ref[...])
        x1_ref[slot][...] = x + g1 * (_rms(m) * npost_mix_ref[this_layer, :])

    def ffn(i, slot):
        r0 = i * sub
        x1 = x1_ref[slot][...]
        hb = ((_rms(x1) * npre_ffn_ref[this_layer, :]) * (1.0 + sc2) + sh2).astype(BF16)
        g = _dot(hb, wg_ref[...])
        u = _dot(hb, wu_ref[...])
        a = ((g * jax.nn.sigmoid(g)) * u).astype(BF16)
        f = _dot(a, wd_ref[...])
        o_ref[r0:r0 + sub, :] = x1 + g2 * (_rms(f) * npost_ffn_ref[this_layer, :])

    post(0, 0)
    for i in range(n_sub):
        if i + 1 < n_sub:
            post(i + 1, (i + 1) % 2)
        ffn(i, i % 2)


def _tail_call(l, x, ao, y, mod_l, n_pre_mix, n_post_mix, n_pre_ffn, n_post_ffn, w_in,
               w_attn_up, w_pool_up, w_out, w_ffn_gate, w_ffn_up, w_ffn_down, next_mod_inputs,
               *, seq_len, per_seq_mod, tm, sub):
    n_tok = x.shape[0]
    n_tiles = n_tok // tm
    row = lambda i: (i, 0)
    vec_spec = pl.BlockSpec((DEPTH, D_MODEL), lambda i: (0, 0))
    layer_resident = lambda r, c: pl.BlockSpec((r, c), lambda i: (0, 0), pipeline_mode=pl.Buffered(1))
    in_specs = [
        pl.BlockSpec((tm, D_MODEL), row),
        pl.BlockSpec((tm, ATTN_W), row),
        pl.BlockSpec((tm, POOL_W), row),
        pl.BlockSpec((MOD_ROWS, N_MOD * D_MODEL), lambda i: (0, 0)),
        vec_spec, vec_spec, vec_spec, vec_spec,
        layer_resident(D_MODEL, IN_COLS),
        layer_resident(ATTN_W, D_MODEL),
        layer_resident(POOL_W, D_MODEL),
        layer_resident(D_MODEL, D_MODEL),
        layer_resident(D_MODEL, FFN_HIDDEN),
        layer_resident(D_MODEL, FFN_HIDDEN),
        layer_resident(FFN_HIDDEN, D_MODEL),
    ]
    args = [x, ao, y, mod_l, n_pre_mix, n_post_mix, n_pre_ffn, n_post_ffn, w_in, w_attn_up,
            w_pool_up, w_out, w_ffn_gate, w_ffn_up, w_ffn_down]
    out_specs = [pl.BlockSpec((tm, D_MODEL), row)]
    out_shape = [jax.ShapeDtypeStruct((n_tok, D_MODEL), F32)]
    if next_mod_inputs is not None:
        n_cols = N_MOD * D_MODEL
        tn = n_cols // n_tiles
        assert n_cols % n_tiles == 0 and tn % LANES == 0
        in_specs += [
            pl.BlockSpec((MOD_ROWS, D_MODEL), lambda i: (0, 0)),
            pl.BlockSpec((None, D_MODEL, tn), lambda i: (l + 1, 0, i)),
            pl.BlockSpec((None, 1, tn), lambda i: (l + 1, 0, i)),
        ]
        args += list(next_mod_inputs)
        out_specs += [pl.BlockSpec((MOD_ROWS, tn), lambda i: (0, i))]
        out_shape += [jax.ShapeDtypeStruct((MOD_ROWS, n_cols), F32)]
    return pl.pallas_call(
        functools.partial(_tail_kernel, layer=l, sub=sub, next_rows=next_mod_inputs is not None,
                          tiles_per_mod_row=seq_len // tm if per_seq_mod else 0),
        grid=(n_tiles,),
        in_specs=in_specs,
        out_specs=out_specs,
        out_shape=out_shape,
        scratch_shapes=[pltpu.VMEM((sub, D_MODEL), F32)] * 2,
        compiler_params=_params(),
        name="tail",
    )(*args)


def _rope_tables(n):
    rows = n // GRID_W
    row = np.repeat(np.arange(rows), GRID_W).astype(np.float32)
    col = np.tile(np.arange(GRID_W), rows).astype(np.float32)
    n_freq = HEAD_DIM // 4
    inv = jnp.asarray(ROPE_THETA, F32) ** (-jnp.arange(n_freq, dtype=F32) / n_freq)
    ang = jnp.concatenate([jnp.asarray(row)[:, None] * inv[None, :],
                           jnp.asarray(col)[:, None] * inv[None, :]], axis=-1)
    cos, sin = jnp.cos(ang), jnp.sin(ang)
    cos_t = jnp.tile(cos, (1, LANES // (HEAD_DIM // 2)))
    sin_t = jnp.tile(jnp.concatenate([-sin, sin], axis=-1), (1, LANES // HEAD_DIM))
    return cos_t, sin_t


def kernel(x_prompt, x_sample, cache_k, cache_v, c, c_ctx, w_ada, b_ada, w_in, q_norm, k_norm,
           w_attn_up, w_pool, pool_scale, w_pool_up, w_out, n_pre_mix, n_post_mix, n_pre_ffn,
           n_post_ffn, w_ffn_gate, w_ffn_up, w_ffn_down):
    batch, seq, _ = x_prompt.shape
    dec_batch, dec_seq, _ = x_sample.shape
    past_len = cache_k.shape[2]
    assert dec_batch <= CTX_MOD_ROW

    c_rows = jnp.concatenate(
        [c, c_ctx[None, :], jnp.zeros((MOD_ROWS - dec_batch - 1, D_MODEL), F32)], axis=0)
    b_ada3 = b_ada.reshape(DEPTH, 1, N_MOD * D_MODEL)
    mod_l = _ada_call(c_rows, w_ada, b_ada3)

    cos_t, sin_t = _rope_tables(dec_seq)
    q_gain = jnp.tile(q_norm, (1, LANES // HEAD_DIM))
    k_gain = jnp.tile(k_norm, (1, LANES // HEAD_DIM))
    w_in_l = w_in[0].astype(BF16)
    w_pool_b = w_pool.astype(BF16)
    tail_weights_f32 = (w_attn_up, w_pool_up, w_out, w_ffn_gate, w_ffn_up, w_ffn_down)
    n_pre_mix_v, n_post_mix_v = n_pre_mix, n_post_mix
    n_pre_ffn_v, n_post_ffn_v = n_pre_ffn, n_post_ffn
    pool_scale_v = pool_scale
    ck = cache_k.reshape(dec_batch, DEPTH, past_len, KV_W)
    cv = cache_v.reshape(dec_batch, DEPTH, past_len, KV_W)

    tm = 512
    ctx_seqs_per_step = 4

    y = x_prompt.reshape(batch * seq, D_MODEL)
    z = x_sample.reshape(dec_batch * dec_seq, D_MODEL)
    new_kv = None
    for l in range(DEPTH):
        has_next = l + 1 < DEPTH
        outs = _mix_call(l, y, mod_l, n_pre_mix_v, w_in_l, q_gain, k_gain, cos_t, sin_t, w_pool_b,
                         pool_scale_v, None, None, tail_weights_f32, l, new_kv,
                         seq_len=seq, n_seq=ctx_seqs_per_step, rope=False)
        ao, yp = outs[:2]
        new_kv = outs[2:4]
        tail_weights_b = outs[4:]
        tail_outs = _tail_call(l, y, ao, yp, mod_l, n_pre_mix_v, n_post_mix_v, n_pre_ffn_v, n_post_ffn_v,
                               w_in_l, *tail_weights_b, (c_rows, w_ada, b_ada3) if has_next else None,
                               seq_len=seq, per_seq_mod=False, tm=tm, sub=256)
        y = tail_outs[0]
        outs = _mix_call(l, z, mod_l, n_pre_mix_v, w_in_l, q_gain, k_gain, cos_t, sin_t, w_pool_b,
                         pool_scale_v, ck, cv, (w_in,) if has_next else (), l + 1,
                         seq_len=dec_seq, n_seq=1, rope=True)
        ao, yp = outs[:2]
        z = _tail_call(l, z, ao, yp, mod_l, n_pre_mix_v, n_post_mix_v, n_pre_ffn_v, n_post_ffn_v,
                       w_in_l, *tail_weights_b, None,
                       seq_len=dec_seq, per_seq_mod=True, tm=tm, sub=256)[0]
        if has_next:
            w_in_l = outs[2]
            mod_l = tail_outs[1]

    def kv_layout(a):
        a = a.reshape(batch, DEPTH, N_KV_HEADS, HEAD_DIM, seq)
        return jnp.transpose(a, (0, 1, 4, 2, 3))

    return (y.reshape(batch, seq, D_MODEL), z.reshape(dec_batch, dec_seq, D_MODEL),
            kv_layout(new_kv[0]), kv_layout(new_kv[1]))
```

```python
import functools

import jax
import jax.numpy as jnp
import numpy as np
from jax import lax
from jax.experimental import pallas as pl
from jax.experimental.pallas import tpu as pltpu

D_MODEL = 1024
DEPTH = 4
GRID_W = 64
HEAD_DIM = 64
N_Q_HEADS = 8
N_KV_HEADS = 2
Q_PER_KV = N_Q_HEADS // N_KV_HEADS
ATTN_W = N_Q_HEADS * HEAD_DIM
KV_W = N_KV_HEADS * HEAD_DIM
POOL_WINDOWS = (2, 4, 8, 16)
POOL_W = D_MODEL // 2
POOL_GW = POOL_W // len(POOL_WINDOWS)
GATE_COL0 = ATTN_W + 2 * KV_W + POOL_W
IN_COLS = GATE_COL0 + 2 * D_MODEL
FFN_HIDDEN = 2816
N_MOD = 6
ROPE_THETA = 10000.0
EPS = 1e-6

LANES = 128
Q_TILE = 256
VT_ROWS = 80
POOL_PAD = 24
LOG2_E = 1.4426950408889634
MOD_ROWS = 8
CTX_MOD_ROW = 4
VMEM_LIMIT_BYTES = 56 * 1024 * 1024

F32 = jnp.float32
BF16 = jnp.bfloat16


def _dot(a, b):
    return jnp.dot(a, b, preferred_element_type=F32)


def _dot_nt(a, b):
    return lax.dot_general(a, b, (((1,), (1,)), ((), ())), preferred_element_type=F32)


def _rms(x):
    return x * lax.rsqrt(jnp.mean(x * x, axis=-1, keepdims=True) + EPS)


def _params(n_axes=1):
    return pltpu.CompilerParams(dimension_semantics=("arbitrary",) * n_axes,
                                vmem_limit_bytes=VMEM_LIMIT_BYTES)


def _ada_rows(c_ref, w_ref, b_ref):
    c = c_ref[...]
    s = (c * jax.nn.sigmoid(c)).astype(BF16)
    return _dot(s, w_ref[...].astype(BF16)) + b_ref[...]


def _ada_kernel(c_ref, w_ref, b_ref, o_ref):
    o_ref[...] = _ada_rows(c_ref, w_ref, b_ref)


def _ada_call(c_rows, w_ada, b_ada3):
    tn = 768
    n_cols = N_MOD * D_MODEL
    return pl.pallas_call(
        _ada_kernel,
        grid=(n_cols // tn,),
        in_specs=[
            pl.BlockSpec((MOD_ROWS, D_MODEL), lambda j: (0, 0)),
            pl.BlockSpec((None, D_MODEL, tn), lambda j: (0, 0, j)),
            pl.BlockSpec((None, 1, tn), lambda j: (0, 0, j)),
        ],
        out_specs=pl.BlockSpec((MOD_ROWS, tn), lambda j: (0, j)),
        out_shape=jax.ShapeDtypeStruct((MOD_ROWS, n_cols), F32),
        compiler_params=_params(),
        name="ada_rows",
    )(c_rows, w_ada, b_ada3)


def _pool_group(src, window, seq_len, p2_ref, p4_ref):
    half = window // 2
    rows = lambda ref, off: ref[POOL_PAD + off:POOL_PAD + off + seq_len, :]
    if window <= 4:
        total = rows(src, -half)
        for off in range(-half + 1, half):
            total = total + rows(src, off)
    else:
        n2 = seq_len + 2 * POOL_PAD - 8
        p2_ref[0:n2, :] = src[0:n2, :] + src[1:n2 + 1, :]
        if window == 8:
            terms = [rows(p2_ref, off) for off in (-4, -2, 0, 2)]
        else:
            assert window == 16
            n4 = n2 - 8
            p4_ref[0:n4, :] = p2_ref[0:n4, :] + p2_ref[2:n4 + 2, :]
            terms = [rows(p4_ref, off) for off in (-8, -4, 0, 4)]
        total = (terms[0] + terms[1]) + (terms[2] + terms[3])
    t = lax.broadcasted_iota(jnp.int32, (seq_len, POOL_GW), 0)
    count = jnp.minimum(t + half, seq_len) - jnp.maximum(t - half, 0)
    return total / count.astype(F32) - rows(src, 0)


def _mix_kernel(x_ref, mod_ref, npre_ref, w_ref, qg_ref, kg_ref, cos_ref, sin_ref, wpool_ref,
                pscale_ref, *rest, layer, seq_len, n_seq, n_cache, rope, n_cast, cast_layer,
                kv_first_layer, n_kv_alias, heads_per_step):
    if n_cache:
        (ck_ref, cv_ref), rest = rest[:2], rest[2:]
    w32_refs, rest = rest[:n_cast], rest[n_cast:]
    rest = rest[n_kv_alias:]
    if n_cache:
        (ao_out, y_out), rest = rest[:2], rest[2:]
    else:
        (ao_out, y_out, nk_out, nv_out), rest = rest[:4], rest[4:]
    wb_outs, rest = rest[:n_cast], rest[n_cast:]
    (qa_ref, k_ref, vt_ref, p_ref, raw_a, raw_b, s_a, s_b, e_a, e_b, p2_a, p2_b, p4_b), rest = (
        rest[:13], rest[13:])
    raw_ref, s_ref, e_ref = (raw_a, raw_b), (s_a, s_b), (e_a, e_b)

    if n_cast:
        stage_in, stage_out, sems = rest[0:n_cast], rest[n_cast:2 * n_cast], rest[2 * n_cast]
        grid_step = pl.program_id(0)
        last_grid_step = pl.num_programs(0) - 1

        def copy_in(j, at_step):
            n = stage_in[j].shape[0]
            row0 = cast_layer * (w32_refs[j].shape[0] // DEPTH) + at_step * n
            return pltpu.make_async_copy(w32_refs[j].at[pl.ds(row0, n), :], stage_in[j], sems.at[j])

        def copy_out(j, at_step):
            n = stage_out[j].shape[0]
            return pltpu.make_async_copy(stage_out[j], wb_outs[j].at[pl.ds(at_step * n, n), :],
                                         sems.at[n_cast + j])

        for j in range(n_cast):
            copy_in(j, grid_step).start()
    n_keys = n_cache + seq_len
    tiles_per_seq = seq_len // Q_TILE
    n_tiles = n_seq * tiles_per_seq

    mod_row = pl.program_id(0) if n_cache else CTX_MOD_ROW
    sh1 = mod_ref[pl.ds(mod_row, 1), 0:D_MODEL]
    this_layer = slice(layer, layer + 1)
    norm_gain = npre_ref[this_layer, :] * (1.0 + mod_ref[pl.ds(mod_row, 1), D_MODEL:2 * D_MODEL])

    lane = lax.broadcasted_iota(jnp.int32, (Q_TILE, LANES), 1)
    first_head = lane < HEAD_DIM
    second_half = (lane & (HEAD_DIM // 2)) != 0

    def head_norm(t, gain):
        t2 = t * t
        s_lo = jnp.sum(jnp.where(first_head, t2, 0.0), axis=-1, keepdims=True)
        s_hi = jnp.sum(jnp.where(first_head, 0.0, t2), axis=-1, keepdims=True)
        ms = jnp.where(first_head, s_lo, s_hi) * (1.0 / HEAD_DIM)
        return (t * lax.rsqrt(ms + EPS)) * gain

    def rotary(t, pos0):
        partner = jnp.where(second_half, pltpu.roll(t, HEAD_DIM // 2, 1),
                            pltpu.roll(t, LANES - HEAD_DIM // 2, 1))
        return t * cos_ref[pos0:pos0 + Q_TILE, :] + partner * sin_ref[pos0:pos0 + Q_TILE, :]

    ones_rows = jnp.ones((VT_ROWS - HEAD_DIM, n_keys), BF16)
    pad_rows = jnp.zeros((POOL_PAD, POOL_GW), F32)
    for s_idx in range(n_seq):
        for g in range(len(POOL_WINDOWS)):
            p_ref[s_idx, g, 0:POOL_PAD, :] = pad_rows
            p_ref[s_idx, g, POOL_PAD + seq_len:2 * POOL_PAD + seq_len, :] = pad_rows
        for kv in range(N_KV_HEADS):
            vt_ref[s_idx, kv * VT_ROWS + HEAD_DIM:(kv + 1) * VT_ROWS, :] = ones_rows
        if n_cache:
            k_ref[s_idx, 0:n_cache, :] = ck_ref[0, 0].astype(BF16)
            cvt = cv_ref[0, 0].T.astype(BF16)
            for kv in range(N_KV_HEADS):
                vt_ref[s_idx, kv * VT_ROWS:kv * VT_ROWS + HEAD_DIM, 0:n_cache] = (
                    cvt[kv * HEAD_DIM:(kv + 1) * HEAD_DIM, :])

    def project(tile, slot):
        r0 = tile * Q_TILE
        x = x_ref[r0:r0 + Q_TILE, :]
        hb = (_rms(x) * norm_gain + sh1).astype(BF16)
        raw_ref[slot][...] = _dot(hb, w_ref[...])

    def epilogue(tile, slot):
        s_idx, pos0 = tile // tiles_per_seq, (tile % tiles_per_seq) * Q_TILE
        r0 = tile * Q_TILE
        raw = raw_ref[slot]
        for j in range(ATTN_W // LANES):
            t = head_norm(raw[:, j * LANES:(j + 1) * LANES], qg_ref[this_layer, :])
            if rope:
                t = rotary(t, pos0)
            t = t * (HEAD_DIM ** -0.5 * LOG2_E)
            swapped = pltpu.roll(t, HEAD_DIM, 1)
            if (2 * j) // Q_PER_KV == 0:
                even = jnp.where(first_head, t, 0.0)
                odd = jnp.where(first_head, swapped, 0.0)
            else:
                even = jnp.where(first_head, 0.0, swapped)
                odd = jnp.where(first_head, 0.0, t)
            qa_ref[tile, (2 * j) * Q_TILE:(2 * j + 1) * Q_TILE, :] = even.astype(BF16)
            qa_ref[tile, (2 * j + 1) * Q_TILE:(2 * j + 2) * Q_TILE, :] = odd.astype(BF16)
        kn = head_norm(raw[:, ATTN_W:ATTN_W + KV_W], kg_ref[this_layer, :])
        v = raw[:, ATTN_W + KV_W:ATTN_W + 2 * KV_W]
        vt_f32 = v.T
        if not n_cache:
            if kv_first_layer:
                nk_out[s_idx, 0, :, pos0:pos0 + Q_TILE] = kn.T
                nv_out[s_idx, 0, :, pos0:pos0 + Q_TILE] = vt_f32
                later = jnp.zeros((DEPTH - 1, KV_W, Q_TILE), F32)
                nk_out[s_idx, 1:DEPTH, :, pos0:pos0 + Q_TILE] = later
                nv_out[s_idx, 1:DEPTH, :, pos0:pos0 + Q_TILE] = later
            else:
                nk_out[s_idx, :, pos0:pos0 + Q_TILE] = kn.T
                nv_out[s_idx, :, pos0:pos0 + Q_TILE] = vt_f32
        if rope:
            kn = rotary(kn, pos0)
        k0 = n_cache + pos0
        k_ref[s_idx, k0:k0 + Q_TILE, :] = kn.astype(BF16)
        vt = vt_f32.astype(BF16)
        for kv in range(N_KV_HEADS):
            vt_ref[s_idx, kv * VT_ROWS:kv * VT_ROWS + HEAD_DIM, k0:k0 + Q_TILE] = (
                vt[kv * HEAD_DIM:(kv + 1) * HEAD_DIM, :])
        for g in range(len(POOL_WINDOWS)):
            c0 = ATTN_W + 2 * KV_W + g * POOL_GW
            p_ref[s_idx, g, POOL_PAD + pos0:POOL_PAD + pos0 + Q_TILE, :] = raw[:, c0:c0 + POOL_GW]

    project(0, 0)
    for tile in range(n_tiles):
        if tile + 1 < n_tiles:
            project(tile + 1, (tile + 1) % 2)
        epilogue(tile, tile % 2)

    n_groups = N_Q_HEADS // heads_per_step
    n_steps = n_tiles * n_groups

    def scores(step, slot):
        tile, grp = step // n_groups, step % n_groups
        qs = qa_ref[tile, grp * heads_per_step * Q_TILE:(grp + 1) * heads_per_step * Q_TILE, :]
        s_ref[slot][...] = _dot_nt(k_ref[tile // tiles_per_seq], qs)

    def exponent(step, slot):
        s = s_ref[slot][...]
        e_ref[slot][...] = jnp.exp2(s - jnp.max(s, axis=0, keepdims=True)).astype(BF16)

    def values(step, slot):
        tile, grp = step // n_groups, step % n_groups
        head0 = grp * heads_per_step
        for kv in range(head0 // Q_PER_KV, (head0 + heads_per_step - 1) // Q_PER_KV + 1):
            lo = max(kv * Q_PER_KV, head0) - head0
            hi = min((kv + 1) * Q_PER_KV, head0 + heads_per_step) - head0
            vt = vt_ref[tile // tiles_per_seq, kv * VT_ROWS:(kv + 1) * VT_ROWS, :]
            ot = _dot(vt, e_ref[slot][:, lo * Q_TILE:hi * Q_TILE])
            ot = ot[0:HEAD_DIM] * (1.0 / ot[HEAD_DIM:HEAD_DIM + 1])
            for j in range(0, hi - lo, 2):
                pair = (head0 + lo + j) // 2
                both = jnp.concatenate([ot[:, j * Q_TILE:(j + 1) * Q_TILE],
                                        ot[:, (j + 1) * Q_TILE:(j + 2) * Q_TILE]], axis=0)
                ao_out[tile * Q_TILE:(tile + 1) * Q_TILE, pair * LANES:(pair + 1) * LANES] = (
                    both.T.astype(BF16))

    scores(0, 0)
    for step in range(n_steps + 1):
        if step + 1 < n_steps:
            scores(step + 1, (step + 1) % 2)
        if step < n_steps:
            exponent(step, step % 2)
        if step >= 1:
            values(step - 1, (step - 1) % 2)

    for s_idx in range(n_seq):
        r0 = s_idx * seq_len
        for g, window in enumerate(POOL_WINDOWS):
            cols = slice(g * POOL_GW, (g + 1) * POOL_GW)
            p2_ref = p2_a if window == 8 else p2_b
            d = _pool_group(p_ref.at[s_idx, g], window, seq_len, p2_ref, p4_b)
            yg = _dot(d.astype(BF16), wpool_ref[g]) * pscale_ref[this_layer, cols]
            y_out[r0:r0 + seq_len, cols] = yg.astype(BF16)

    if n_cast:
        @pl.when(grid_step > 0)
        def _():
            for j in range(n_cast):
                copy_out(j, grid_step - 1).wait()

        for j in range(n_cast):
            copy_in(j, grid_step).wait()
            stage_out[j][...] = stage_in[j][...].astype(BF16)
            copy_out(j, grid_step).start()

        @pl.when(grid_step == last_grid_step)
        def _():
            for j in range(n_cast):
                copy_out(j, grid_step).wait()


def _mix_call(l, x, mod_rows, n_pre, w_in_l, q_gain, k_gain, cos_t, sin_t, w_pool, pool_scale,
              cache_k, cache_v, cast_weights_f32, cast_layer, new_kv=None, *, seq_len, n_seq, rope):
    n_tok = x.shape[0]
    rows = n_seq * seq_len
    is_lat = cache_k is not None
    n_cache = cache_k.shape[2] if is_lat else 0
    n_keys = n_cache + seq_len
    assert n_seq == 1 or not is_lat
    row = lambda i: (i, 0)
    stacked = lambda width: pl.BlockSpec((DEPTH, width), lambda i: (0, 0))
    in_specs = [
        pl.BlockSpec((rows, D_MODEL), row),
        pl.BlockSpec((MOD_ROWS, N_MOD * D_MODEL), lambda i: (0, 0)),
        stacked(D_MODEL),
        pl.BlockSpec((D_MODEL, GATE_COL0), lambda i: (0, 0)),
        stacked(LANES),
        stacked(LANES),
        pl.BlockSpec((seq_len, LANES), lambda i: (0, 0)),
        pl.BlockSpec((seq_len, LANES), lambda i: (0, 0)),
        pl.BlockSpec((None, len(POOL_WINDOWS), POOL_GW, POOL_GW), lambda i: (l, 0, 0, 0)),
        stacked(POOL_W),
    ]
    args = [x, mod_rows, n_pre, w_in_l, q_gain, k_gain, cos_t, sin_t, w_pool, pool_scale]
    out_shape = [jax.ShapeDtypeStruct((n_tok, ATTN_W), BF16), jax.ShapeDtypeStruct((n_tok, POOL_W), BF16)]
    out_specs = [pl.BlockSpec((rows, ATTN_W), row), pl.BlockSpec((rows, POOL_W), row)]
    if is_lat:
        cache_spec = pl.BlockSpec((1, 1, n_cache, KV_W), lambda i: (i, l, 0, 0))
        in_specs += [cache_spec, cache_spec]
        args += [cache_k, cache_v]
    else:
        out_shape += [jax.ShapeDtypeStruct((n_tok // seq_len, DEPTH, KV_W, seq_len), F32)] * 2
        if new_kv is None:
            out_specs += [pl.BlockSpec((n_seq, DEPTH, KV_W, seq_len), lambda i: (i, 0, 0, 0))] * 2
        else:
            out_specs += [pl.BlockSpec((n_seq, None, KV_W, seq_len), lambda i: (i, l, 0, 0))] * 2
    n_cast = len(cast_weights_f32)
    n_steps = n_tok // rows
    any_spec = pl.BlockSpec(memory_space=pl.ANY)
    in_specs += [any_spec] * n_cast
    args += [w.reshape(-1, w.shape[2]) for w in cast_weights_f32]
    aliases = {}
    if new_kv is not None:
        aliases = {len(args): 2, len(args) + 1: 3}
        in_specs += [any_spec] * 2
        args += list(new_kv)
    out_shape += [jax.ShapeDtypeStruct(w.shape[1:], BF16) for w in cast_weights_f32]
    out_specs += [any_spec] * n_cast
    slices = [(w.shape[1] // n_steps, w.shape[2]) for w in cast_weights_f32]
    assert all(w.shape[1] % n_steps == 0 and r % 16 == 0 for w, (r, _) in zip(cast_weights_f32, slices))
    scratch = [
        pltpu.VMEM((rows // Q_TILE, N_Q_HEADS * Q_TILE, LANES), BF16),
        pltpu.VMEM((n_seq, n_keys, KV_W), BF16),
        pltpu.VMEM((n_seq, N_KV_HEADS * VT_ROWS, n_keys), BF16),
        pltpu.VMEM((n_seq, len(POOL_WINDOWS), seq_len + 2 * POOL_PAD, POOL_GW), F32),
    ]
    scratch += [pltpu.VMEM((Q_TILE, GATE_COL0), F32)] * 2
    heads_per_step = N_Q_HEADS if n_keys <= Q_TILE else 2
    scratch += [pltpu.VMEM((n_keys, heads_per_step * Q_TILE), F32)] * 2
    scratch += [pltpu.VMEM((n_keys, heads_per_step * Q_TILE), BF16)] * 2
    scratch += [pltpu.VMEM((seq_len + 2 * POOL_PAD - 8, POOL_GW), F32)] * 2
    scratch += [pltpu.VMEM((seq_len + 2 * POOL_PAD - 16, POOL_GW), F32)]
    if n_cast:
        scratch += [pltpu.VMEM(s, F32) for s in slices] + [pltpu.VMEM(s, BF16) for s in slices]
        scratch += [pltpu.SemaphoreType.DMA((2 * n_cast,))]
    return pl.pallas_call(
        functools.partial(_mix_kernel, layer=l, seq_len=seq_len, n_seq=n_seq, n_cache=n_cache, rope=rope,
                          n_cast=n_cast, cast_layer=cast_layer,
                          kv_first_layer=not is_lat and new_kv is None, n_kv_alias=len(aliases),
                          heads_per_step=heads_per_step),
        grid=(n_tok // rows,),
        in_specs=in_specs,
        out_specs=out_specs,
        out_shape=out_shape,
        scratch_shapes=scratch,
        input_output_aliases=aliases,
        compiler_params=_params(),
        name="mix_lat" if is_lat else "mix_ctx",
    )(*args)


def _tail_kernel(x_ref, ao_ref, y_ref, mod_ref, npre_mix_ref, npost_mix_ref, npre_ffn_ref,
                 npost_ffn_ref, win_ref, wau_ref, wpu_ref, wout_ref, wg_ref, wu_ref, wd_ref, *rest,
                 layer, sub, tiles_per_mod_row, next_rows):
    this_layer = slice(layer, layer + 1)
    if next_rows:
        c_ref, wada_ref, bada_ref, o_ref, next_mod_out, x1_a, x1_b = rest
        next_mod_out[...] = _ada_rows(c_ref, wada_ref, bada_ref)
    else:
        o_ref, x1_a, x1_b = rest
    mod_row = pl.program_id(0) // tiles_per_mod_row if tiles_per_mod_row else CTX_MOD_ROW
    sh1, sc1, g1, sh2, sc2, g2 = [mod_ref[pl.ds(mod_row, 1), i * D_MODEL:(i + 1) * D_MODEL]
                                  for i in range(N_MOD)]
    x1_ref = (x1_a, x1_b)
    n_sub = x_ref.shape[0] // sub

    def post(i, slot):
        r0 = i * sub
        x = x_ref[r0:r0 + sub, :]
        hb = ((_rms(x) * npre_mix_ref[this_layer, :]) * (1.0 + sc1) + sh1).astype(BF16)
        g_attn = jax.nn.sigmoid(_dot(hb, win_ref[:, GATE_COL0:GATE_COL0 + D_MODEL]))
        g_pool = jax.nn.sigmoid(_dot(hb, win_ref[:, GATE_COL0 + D_MODEL:IN_COLS]))
        attn = _dot(ao_ref[r0:r0 + sub, :], wau_ref[...])
        pool = _dot(y_ref[r0:r0 + sub, :], wpu_ref[...])
        mixed = g_attn * attn + g_pool * pool
        m = _dot(mixed.astype(BF16), wout_ref[...])
        x1_ref[slot][...] = x + g1 * (_rms(m) * npost_mix_ref[this_layer, :])

    def ffn(i, slot):
        r0 = i * sub
        x1 = x1_ref[slot][...]
        hb = ((_rms(x1) * npre_ffn_ref[this_layer, :]) * (1.0 + sc2) + sh2).astype(BF16)
        g = _dot(hb, wg_ref[...])
        u = _dot(hb, wu_ref[...])
        a = ((g * jax.nn.sigmoid(g)) * u).astype(BF16)
        f = _dot(a, wd_ref[...])
        o_ref[r0:r0 + sub, :] = x1 + g2 * (_rms(f) * npost_ffn_ref[this_layer, :])

    post(0, 0)
    for i in range(n_sub):
        if i + 1 < n_sub:
            post(i + 1, (i + 1) % 2)
        ffn(i, i % 2)


def _tail_call(l, x, ao, y, mod_l, n_pre_mix, n_post_mix, n_pre_ffn, n_post_ffn, w_in,
               w_attn_up, w_pool_up, w_out, w_ffn_gate, w_ffn_up, w_ffn_down, next_mod_inputs,
               *, seq_len, per_seq_mod, tm, sub):
    n_tok = x.shape[0]
    n_tiles = n_tok // tm
    row = lambda i: (i, 0)
    vec_spec = pl.BlockSpec((DEPTH, D_MODEL), lambda i: (0, 0))
    layer_resident = lambda r, c: pl.BlockSpec((r, c), lambda i: (0, 0), pipeline_mode=pl.Buffered(1))
    in_specs = [
        pl.BlockSpec((tm, D_MODEL), row),
        pl.BlockSpec((tm, ATTN_W), row),
        pl.BlockSpec((tm, POOL_W), row),
        pl.BlockSpec((MOD_ROWS, N_MOD * D_MODEL), lambda i: (0, 0)),
        vec_spec, vec_spec, vec_spec, vec_spec,
        layer_resident(D_MODEL, IN_COLS),
        layer_resident(ATTN_W, D_MODEL),
        layer_resident(POOL_W, D_MODEL),
        layer_resident(D_MODEL, D_MODEL),
        layer_resident(D_MODEL, FFN_HIDDEN),
        layer_resident(D_MODEL, FFN_HIDDEN),
        layer_resident(FFN_HIDDEN, D_MODEL),
    ]
    args = [x, ao, y, mod_l, n_pre_mix, n_post_mix, n_pre_ffn, n_post_ffn, w_in, w_attn_up,
            w_pool_up, w_out, w_ffn_gate, w_ffn_up, w_ffn_down]
    out_specs = [pl.BlockSpec((tm, D_MODEL), row)]
    out_shape = [jax.ShapeDtypeStruct((n_tok, D_MODEL), F32)]
    if next_mod_inputs is not None:
        n_cols = N_MOD * D_MODEL
        tn = n_cols // n_tiles
        assert n_cols % n_tiles == 0 and tn % LANES == 0
        in_specs += [
            pl.BlockSpec((MOD_ROWS, D_MODEL), lambda i: (0, 0)),
            pl.BlockSpec((None, D_MODEL, tn), lambda i: (l + 1, 0, i)),
            pl.BlockSpec((None, 1, tn), lambda i: (l + 1, 0, i)),
        ]
        args += list(next_mod_inputs)
        out_specs += [pl.BlockSpec((MOD_ROWS, tn), lambda i: (0, i))]
        out_shape += [jax.ShapeDtypeStruct((MOD_ROWS, n_cols), F32)]
    return pl.pallas_call(
        functools.partial(_tail_kernel, layer=l, sub=sub, next_rows=next_mod_inputs is not None,
                          tiles_per_mod_row=seq_len // tm if per_seq_mod else 0),
        grid=(n_tiles,),
        in_specs=in_specs,
        out_specs=out_specs,
        out_shape=out_shape,
        scratch_shapes=[pltpu.VMEM((sub, D_MODEL), F32)] * 2,
        compiler_params=_params(),
        name="tail",
    )(*args)


def _rope_tables(n):
    rows = n // GRID_W
    row = np.repeat(np.arange(rows), GRID_W).astype(np.float32)
    col = np.tile(np.arange(GRID_W), rows).astype(np.float32)
    n_freq = HEAD_DIM // 4
    inv = jnp.asarray(ROPE_THETA, F32) ** (-jnp.arange(n_freq, dtype=F32) / n_freq)
    ang = jnp.concatenate([jnp.asarray(row)[:, None] * inv[None, :],
                           jnp.asarray(col)[:, None] * inv[None, :]], axis=-1)
    cos, sin = jnp.cos(ang), jnp.sin(ang)
    cos_t = jnp.tile(cos, (1, LANES // (HEAD_DIM // 2)))
    sin_t = jnp.tile(jnp.concatenate([-sin, sin], axis=-1), (1, LANES // HEAD_DIM))
    return cos_t, sin_t


def kernel(x_prompt, x_sample, cache_k, cache_v, c, c_ctx, w_ada, b_ada, w_in, q_norm, k_norm,
           w_attn_up, w_pool, pool_scale, w_pool_up, w_out, n_pre_mix, n_post_mix, n_pre_ffn,
           n_post_ffn, w_ffn_gate, w_ffn_up, w_ffn_down):
    batch, seq, _ = x_prompt.shape
    dec_batch, dec_seq, _ = x_sample.shape
    past_len = cache_k.shape[2]
    assert dec_batch <= CTX_MOD_ROW

    c_rows = jnp.concatenate(
        [c, c_ctx[None, :], jnp.zeros((MOD_ROWS - dec_batch - 1, D_MODEL), F32)], axis=0)
    b_ada3 = b_ada.reshape(DEPTH, 1, N_MOD * D_MODEL)
    mod_l = _ada_call(c_rows, w_ada, b_ada3)

    cos_t, sin_t = _rope_tables(dec_seq)
    q_gain = jnp.tile(q_norm, (1, LANES // HEAD_DIM))
    k_gain = jnp.tile(k_norm, (1, LANES // HEAD_DIM))
    w_in_l = w_in[0].astype(BF16)
    w_pool_b = w_pool.astype(BF16)
    tail_weights_f32 = (w_attn_up, w_pool_up, w_out, w_ffn_gate, w_ffn_up, w_ffn_down)
    n_pre_mix_v, n_post_mix_v = n_pre_mix, n_post_mix
    n_pre_ffn_v, n_post_ffn_v = n_pre_ffn, n_post_ffn
    pool_scale_v = pool_scale
    ck = cache_k.reshape(dec_batch, DEPTH, past_len, KV_W)
    cv = cache_v.reshape(dec_batch, DEPTH, past_len, KV_W)

    tm = 512
    ctx_seqs_per_step = 4

    y = x_prompt.reshape(batch * seq, D_MODEL)
    z = x_sample.reshape(dec_batch * dec_seq, D_MODEL)
    new_kv = None
    for l in range(DEPTH):
        has_next = l + 1 < DEPTH
        outs = _mix_call(l, y, mod_l, n_pre_mix_v, w_in_l, q_gain, k_gain, cos_t, sin_t, w_pool_b,
                         pool_scale_v, None, None, tail_weights_f32, l, new_kv,
                         seq_len=seq, n_seq=ctx_seqs_per_step, rope=False)
        ao, yp = outs[:2]
        new_kv = outs[2:4]
        tail_weights_b = outs[4:]
        tail_outs = _tail_call(l, y, ao, yp, mod_l, n_pre_mix_v, n_post_mix_v, n_pre_ffn_v, n_post_ffn_v,
                               w_in_l, *tail_weights_b, (c_rows, w_ada, b_ada3) if has_next else None,
                               seq_len=seq, per_seq_mod=False, tm=tm, sub=256)
        y = tail_outs[0]
        outs = _mix_call(l, z, mod_l, n_pre_mix_v, w_in_l, q_gain, k_gain, cos_t, sin_t, w_pool_b,
                         pool_scale_v, ck, cv, (w_in,) if has_next else (), l + 1,
                         seq_len=dec_seq, n_seq=1, rope=True)
        ao, yp = outs[:2]
        z = _tail_call(l, z, ao, yp, mod_l, n_pre_mix_v, n_post_mix_v, n_pre_ffn_v, n_post_ffn_v,
                       w_in_l, *tail_weights_b, None,
                       seq_len=dec_seq, per_seq_mod=True, tm=tm, sub=256)[0]
        if has_next:
            w_in_l = outs[2]
            mod_l = tail_outs[1]

    def kv_layout(a):
        a = a.reshape(batch, DEPTH, N_KV_HEADS, HEAD_DIM, seq)
        return jnp.transpose(a, (0, 1, 4, 2, 3))

    return (y.reshape(batch, seq, D_MODEL), z.reshape(dec_batch, dec_seq, D_MODEL),
            kv_layout(new_kv[0]), kv_layout(new_kv[1]))
```

```python
import functools

import jax
import jax.numpy as jnp
import numpy as np
from jax import lax
from jax.experimental import pallas as pl
from jax.experimental.pallas import tpu as pltpu

D_MODEL = 1024
DEPTH = 4
GRID_W = 64
HEAD_DIM = 64
N_Q_HEADS = 8
N_KV_HEADS = 2
Q_PER_KV = N_Q_HEADS // N_KV_HEADS
ATTN_W = N_Q_HEADS * HEAD_DIM
KV_W = N_KV_HEADS * HEAD_DIM
POOL_WINDOWS = (2, 4, 8, 16)
POOL_W = D_MODEL // 2
POOL_GW = POOL_W // len(POOL_WINDOWS)
GATE_COL0 = ATTN_W + 2 * KV_W + POOL_W
IN_COLS = GATE_COL0 + 2 * D_MODEL
FFN_HIDDEN = 2816
N_MOD = 6
ROPE_THETA = 10000.0
EPS = 1e-6

LANES = 128
Q_TILE = 256
VT_ROWS = 80
POOL_PAD = 24
LOG2_E = 1.4426950408889634
MOD_ROWS = 8
CTX_MOD_ROW = 4
VMEM_LIMIT_BYTES = 56 * 1024 * 1024

F32 = jnp.float32
BF16 = jnp.bfloat16


def _dot(a, b):
    return jnp.dot(a, b, preferred_element_type=F32)


def _dot_nt(a, b):
    return lax.dot_general(a, b, (((1,), (1,)), ((), ())), preferred_element_type=F32)


def _rms(x):
    return x * lax.rsqrt(jnp.mean(x * x, axis=-1, keepdims=True) + EPS)


def _params(n_axes=1):
    return pltpu.CompilerParams(dimension_semantics=("arbitrary",) * n_axes,
                                vmem_limit_bytes=VMEM_LIMIT_BYTES)


def _ada_rows(c_ref, w_ref, b_ref):
    c = c_ref[...]
    s = (c * jax.nn.sigmoid(c)).astype(BF16)
    return _dot(s, w_ref[...].astype(BF16)) + b_ref[...]


def _ada_kernel(c_ref, w_ref, b_ref, o_ref):
    o_ref[...] = _ada_rows(c_ref, w_ref, b_ref)


def _ada_call(c_rows, w_ada, b_ada3):
    tn = 1536
    n_cols = N_MOD * D_MODEL
    return pl.pallas_call(
        _ada_kernel,
        grid=(n_cols // tn,),
        in_specs=[
            pl.BlockSpec((MOD_ROWS, D_MODEL), lambda j: (0, 0)),
            pl.BlockSpec((None, D_MODEL, tn), lambda j: (0, 0, j)),
            pl.BlockSpec((None, 1, tn), lambda j: (0, 0, j)),
        ],
        out_specs=pl.BlockSpec((MOD_ROWS, tn), lambda j: (0, j)),
        out_shape=jax.ShapeDtypeStruct((MOD_ROWS, n_cols), F32),
        compiler_params=_params(),
        name="ada_rows",
    )(c_rows, w_ada, b_ada3)


def _pool_group(src, window, seq_len, p2_ref, p4_ref):
    half = window // 2
    rows = lambda ref, off: ref[POOL_PAD + off:POOL_PAD + off + seq_len, :]
    if window <= 4:
        total = rows(src, -half)
        for off in range(-half + 1, half):
            total = total + rows(src, off)
    else:
        n2 = seq_len + 2 * POOL_PAD - 8
        p2_ref[0:n2, :] = src[0:n2, :] + src[1:n2 + 1, :]
        if window == 8:
            terms = [rows(p2_ref, off) for off in (-4, -2, 0, 2)]
        else:
            assert window == 16
            n4 = n2 - 8
            p4_ref[0:n4, :] = p2_ref[0:n4, :] + p2_ref[2:n4 + 2, :]
            terms = [rows(p4_ref, off) for off in (-8, -4, 0, 4)]
        total = (terms[0] + terms[1]) + (terms[2] + terms[3])
    t = lax.broadcasted_iota(jnp.int32, (seq_len, POOL_GW), 0)
    count = jnp.minimum(t + half, seq_len) - jnp.maximum(t - half, 0)
    return total / count.astype(F32) - rows(src, 0)


def _mix_kernel(x_ref, mod_ref, npre_ref, w_ref, qg_ref, kg_ref, cos_ref, sin_ref, wpool_ref,
                pscale_ref, *rest, layer, seq_len, n_seq, n_cache, rope, n_cast, cast_layer,
                kv_first_layer, n_kv_alias, heads_per_step):
    if n_cache:
        (ck_ref, cv_ref), rest = rest[:2], rest[2:]
    w32_refs, rest = rest[:n_cast], rest[n_cast:]
    rest = rest[n_kv_alias:]
    if n_cache:
        (ao_out, y_out), rest = rest[:2], rest[2:]
    else:
        (ao_out, y_out, nk_out, nv_out), rest = rest[:4], rest[4:]
    wb_outs, rest = rest[:n_cast], rest[n_cast:]
    (qa_ref, k_ref, vt_ref, p_ref, raw_a, raw_b, s_a, s_b, e_a, e_b, p2_a, p2_b, p4_b), rest = (
        rest[:13], rest[13:])
    raw_ref, s_ref, e_ref = (raw_a, raw_b), (s_a, s_b), (e_a, e_b)

    if n_cast:
        stage_in, stage_out, sems = rest[0:n_cast], rest[n_cast:2 * n_cast], rest[2 * n_cast]
        grid_step = pl.program_id(0)
        last_grid_step = pl.num_programs(0) - 1

        def copy_in(j, at_step):
            n = stage_in[j].shape[0]
            row0 = cast_layer * (w32_refs[j].shape[0] // DEPTH) + at_step * n
            return pltpu.make_async_copy(w32_refs[j].at[pl.ds(row0, n), :], stage_in[j], sems.at[j])

        def copy_out(j, at_step):
            n = stage_out[j].shape[0]
            return pltpu.make_async_copy(stage_out[j], wb_outs[j].at[pl.ds(at_step * n, n), :],
                                         sems.at[n_cast + j])

        for j in range(n_cast):
            copy_in(j, grid_step).start()
    n_keys = n_cache + seq_len
    tiles_per_seq = seq_len // Q_TILE
    n_tiles = n_seq * tiles_per_seq

    mod_row = pl.program_id(0) if n_cache else CTX_MOD_ROW
    sh1 = mod_ref[pl.ds(mod_row, 1), 0:D_MODEL]
    this_layer = slice(layer, layer + 1)
    norm_gain = npre_ref[this_layer, :] * (1.0 + mod_ref[pl.ds(mod_row, 1), D_MODEL:2 * D_MODEL])

    lane = lax.broadcasted_iota(jnp.int32, (Q_TILE, LANES), 1)
    first_head = lane < HEAD_DIM
    second_half = (lane & (HEAD_DIM // 2)) != 0

    def head_norm(t, gain):
        t2 = t * t
        s_lo = jnp.sum(jnp.where(first_head, t2, 0.0), axis=-1, keepdims=True)
        s_hi = jnp.sum(jnp.where(first_head, 0.0, t2), axis=-1, keepdims=True)
        ms = jnp.where(first_head, s_lo, s_hi) * (1.0 / HEAD_DIM)
        return (t * lax.rsqrt(ms + EPS)) * gain

    def rotary(t, pos0):
        partner = jnp.where(second_half, pltpu.roll(t, HEAD_DIM // 2, 1),
                            pltpu.roll(t, LANES - HEAD_DIM // 2, 1))
        return t * cos_ref[pos0:pos0 + Q_TILE, :] + partner * sin_ref[pos0:pos0 + Q_TILE, :]

    ones_rows = jnp.ones((VT_ROWS - HEAD_DIM, n_keys), BF16)
    pad_rows = jnp.zeros((POOL_PAD, POOL_GW), F32)
    for s_idx in range(n_seq):
        for g in range(len(POOL_WINDOWS)):
            p_ref[s_idx, g, 0:POOL_PAD, :] = pad_rows
            p_ref[s_idx, g, POOL_PAD + seq_len:2 * POOL_PAD + seq_len, :] = pad_rows
        for kv in range(N_KV_HEADS):
            vt_ref[s_idx, kv * VT_ROWS + HEAD_DIM:(kv + 1) * VT_ROWS, :] = ones_rows
        if n_cache:
            k_ref[s_idx, 0:n_cache, :] = ck_ref[0, 0].astype(BF16)
            cvt = cv_ref[0, 0].T.astype(BF16)
            for kv in range(N_KV_HEADS):
                vt_ref[s_idx, kv * VT_ROWS:kv * VT_ROWS + HEAD_DIM, 0:n_cache] = (
                    cvt[kv * HEAD_DIM:(kv + 1) * HEAD_DIM, :])

    def project(tile, slot):
        r0 = tile * Q_TILE
        x = x_ref[r0:r0 + Q_TILE, :]
        hb = (_rms(x) * norm_gain + sh1).astype(BF16)
        raw_ref[slot][...] = _dot(hb, w_ref[...])

    def epilogue(tile, slot):
        s_idx, pos0 = tile // tiles_per_seq, (tile % tiles_per_seq) * Q_TILE
        r0 = tile * Q_TILE
        raw = raw_ref[slot]
        for j in range(ATTN_W // LANES):
            t = head_norm(raw[:, j * LANES:(j + 1) * LANES], qg_ref[this_layer, :])
            if rope:
                t = rotary(t, pos0)
            t = t * (HEAD_DIM ** -0.5 * LOG2_E)
            swapped = pltpu.roll(t, HEAD_DIM, 1)
            if (2 * j) // Q_PER_KV == 0:
                even = jnp.where(first_head, t, 0.0)
                odd = jnp.where(first_head, swapped, 0.0)
            else:
                even = jnp.where(first_head, 0.0, swapped)
                odd = jnp.where(first_head, 0.0, t)
            qa_ref[tile, (2 * j) * Q_TILE:(2 * j + 1) * Q_TILE, :] = even.astype(BF16)
            qa_ref[tile, (2 * j + 1) * Q_TILE:(2 * j + 2) * Q_TILE, :] = odd.astype(BF16)
        kn = head_norm(raw[:, ATTN_W:ATTN_W + KV_W], kg_ref[this_layer, :])
        v = raw[:, ATTN_W + KV_W:ATTN_W + 2 * KV_W]
        vt_f32 = v.T
        if not n_cache:
            if kv_first_layer:
                nk_out[s_idx, 0, :, pos0:pos0 + Q_TILE] = kn.T
                nv_out[s_idx, 0, :, pos0:pos0 + Q_TILE] = vt_f32
                later = jnp.zeros((DEPTH - 1, KV_W, Q_TILE), F32)
                nk_out[s_idx, 1:DEPTH, :, pos0:pos0 + Q_TILE] = later
                nv_out[s_idx, 1:DEPTH, :, pos0:pos0 + Q_TILE] = later
            else:
                nk_out[s_idx, :, pos0:pos0 + Q_TILE] = kn.T
                nv_out[s_idx, :, pos0:pos0 + Q_TILE] = vt_f32
        if rope:
            kn = rotary(kn, pos0)
        k0 = n_cache + pos0
        k_ref[s_idx, k0:k0 + Q_TILE, :] = kn.astype(BF16)
        vt = vt_f32.astype(BF16)
        for kv in range(N_KV_HEADS):
            vt_ref[s_idx, kv * VT_ROWS:kv * VT_ROWS + HEAD_DIM, k0:k0 + Q_TILE] = (
                vt[kv * HEAD_DIM:(kv + 1) * HEAD_DIM, :])
        for g in range(len(POOL_WINDOWS)):
            c0 = ATTN_W + 2 * KV_W + g * POOL_GW
            p_ref[s_idx, g, POOL_PAD + pos0:POOL_PAD + pos0 + Q_TILE, :] = raw[:, c0:c0 + POOL_GW]

    project(0, 0)
    for tile in range(n_tiles):
        if tile + 1 < n_tiles:
            project(tile + 1, (tile + 1) % 2)
        epilogue(tile, tile % 2)

    n_groups = N_Q_HEADS // heads_per_step
    n_steps = n_tiles * n_groups
    stage_w = heads_per_step * Q_TILE

    def scores(step, slot):
        tile, grp = step // n_groups, step % n_groups
        qs = qa_ref[tile, grp * heads_per_step * Q_TILE:(grp + 1) * heads_per_step * Q_TILE, :]
        s_ref[slot][:, 0:stage_w] = _dot_nt(k_ref[tile // tiles_per_seq], qs)

    def exponent(step, slot):
        s = s_ref[slot][:, 0:stage_w]
        e_ref[slot][:, 0:stage_w] = jnp.exp2(s - jnp.max(s, axis=0, keepdims=True)).astype(BF16)

    def values(step, slot):
        tile, grp = step // n_groups, step % n_groups
        head0 = grp * heads_per_step
        for kv in range(head0 // Q_PER_KV, (head0 + heads_per_step - 1) // Q_PER_KV + 1):
            lo = max(kv * Q_PER_KV, head0) - head0
            hi = min((kv + 1) * Q_PER_KV, head0 + heads_per_step) - head0
            vt = vt_ref[tile // tiles_per_seq, kv * VT_ROWS:(kv + 1) * VT_ROWS, :]
            ot = _dot(vt, e_ref[slot][:, lo * Q_TILE:hi * Q_TILE])
            ot = ot[0:HEAD_DIM] * (1.0 / ot[HEAD_DIM:HEAD_DIM + 1])
            for j in range(0, hi - lo, 2):
                pair = (head0 + lo + j) // 2
                both = jnp.concatenate([ot[:, j * Q_TILE:(j + 1) * Q_TILE],
                                        ot[:, (j + 1) * Q_TILE:(j + 2) * Q_TILE]], axis=0)
                ao_out[tile * Q_TILE:(tile + 1) * Q_TILE, pair * LANES:(pair + 1) * LANES] = (
                    both.T.astype(BF16))

    scores(0, 0)
    for step in range(n_steps + 1):
        if step + 1 < n_steps:
            scores(step + 1, (step + 1) % 2)
        if step < n_steps:
            exponent(step, step % 2)
        if step >= 1:
            values(step - 1, (step - 1) % 2)

    for s_idx in range(n_seq):
        r0 = s_idx * seq_len
        for g, window in enumerate(POOL_WINDOWS):
            cols = slice(g * POOL_GW, (g + 1) * POOL_GW)
            p2_ref = p2_a if window == 8 else p2_b
            d = _pool_group(p_ref.at[s_idx, g], window, seq_len, p2_ref, p4_b)
            yg = _dot(d.astype(BF16), wpool_ref[g]) * pscale_ref[this_layer, cols]
            y_out[r0:r0 + seq_len, cols] = yg.astype(BF16)

    if n_cast:
        @pl.when(grid_step > 0)
        def _():
            for j in range(n_cast):
                copy_out(j, grid_step - 1).wait()

        for j in range(n_cast):
            copy_in(j, grid_step).wait()
            stage_out[j][...] = stage_in[j][...].astype(BF16)
            copy_out(j, grid_step).start()

        @pl.when(grid_step == last_grid_step)
        def _():
            for j in range(n_cast):
                copy_out(j, grid_step).wait()


def _mix_call(l, x, mod_rows, n_pre, w_in_l, q_gain, k_gain, cos_t, sin_t, w_pool, pool_scale,
              cache_k, cache_v, cast_weights_f32, cast_layer, new_kv=None, *, seq_len, n_seq, rope):
    n_tok = x.shape[0]
    rows = n_seq * seq_len
    is_lat = cache_k is not None
    n_cache = cache_k.shape[2] if is_lat else 0
    n_keys = n_cache + seq_len
    assert n_seq == 1 or not is_lat
    row = lambda i: (i, 0)
    stacked = lambda width: pl.BlockSpec((DEPTH, width), lambda i: (0, 0))
    in_specs = [
        pl.BlockSpec((rows, D_MODEL), row),
        pl.BlockSpec((MOD_ROWS, N_MOD * D_MODEL), lambda i: (0, 0)),
        stacked(D_MODEL),
        pl.BlockSpec((D_MODEL, GATE_COL0), lambda i: (0, 0)),
        stacked(LANES),
        stacked(LANES),
        pl.BlockSpec((seq_len, LANES), lambda i: (0, 0)),
        pl.BlockSpec((seq_len, LANES), lambda i: (0, 0)),
        pl.BlockSpec((None, len(POOL_WINDOWS), POOL_GW, POOL_GW), lambda i: (l, 0, 0, 0)),
        stacked(POOL_W),
    ]
    args = [x, mod_rows, n_pre, w_in_l, q_gain, k_gain, cos_t, sin_t, w_pool, pool_scale]
    out_shape = [jax.ShapeDtypeStruct((n_tok, ATTN_W), BF16), jax.ShapeDtypeStruct((n_tok, POOL_W), BF16)]
    out_specs = [pl.BlockSpec((rows, ATTN_W), row), pl.BlockSpec((rows, POOL_W), row)]
    if is_lat:
        cache_spec = pl.BlockSpec((1, 1, n_cache, KV_W), lambda i: (i, l, 0, 0))
        in_specs += [cache_spec, cache_spec]
        args += [cache_k, cache_v]
    else:
        out_shape += [jax.ShapeDtypeStruct((n_tok // seq_len, DEPTH, KV_W, seq_len), F32)] * 2
        if new_kv is None:
            out_specs += [pl.BlockSpec((n_seq, DEPTH, KV_W, seq_len), lambda i: (i, 0, 0, 0))] * 2
        else:
            out_specs += [pl.BlockSpec((n_seq, None, KV_W, seq_len), lambda i: (i, l, 0, 0))] * 2
    n_cast = len(cast_weights_f32)
    n_steps = n_tok // rows
    any_spec = pl.BlockSpec(memory_space=pl.ANY)
    in_specs += [any_spec] * n_cast
    args += [w.reshape(-1, w.shape[2]) for w in cast_weights_f32]
    aliases = {}
    if new_kv is not None:
        aliases = {len(args): 2, len(args) + 1: 3}
        in_specs += [any_spec] * 2
        args += list(new_kv)
    out_shape += [jax.ShapeDtypeStruct(w.shape[1:], BF16) for w in cast_weights_f32]
    out_specs += [any_spec] * n_cast
    slices = [(w.shape[1] // n_steps, w.shape[2]) for w in cast_weights_f32]
    assert all(w.shape[1] % n_steps == 0 and r % 16 == 0 for w, (r, _) in zip(cast_weights_f32, slices))
    scratch = [
        pltpu.VMEM((rows // Q_TILE, N_Q_HEADS * Q_TILE, LANES), BF16),
        pltpu.VMEM((n_seq, n_keys, KV_W), BF16),
        pltpu.VMEM((n_seq, N_KV_HEADS * VT_ROWS, n_keys), BF16),
        pltpu.VMEM((n_seq, len(POOL_WINDOWS), seq_len + 2 * POOL_PAD, POOL_GW), F32),
    ]
    scratch += [pltpu.VMEM((Q_TILE, GATE_COL0), F32)] * 2
    heads_per_step = N_Q_HEADS if n_keys <= Q_TILE else 2
    scratch += [pltpu.VMEM((n_keys, heads_per_step * Q_TILE + LANES), F32)] * 2
    scratch += [pltpu.VMEM((n_keys, heads_per_step * Q_TILE + LANES), BF16)] * 2
    scratch += [pltpu.VMEM((seq_len + 2 * POOL_PAD - 8, POOL_GW), F32)] * 2
    scratch += [pltpu.VMEM((seq_len + 2 * POOL_PAD - 16, POOL_GW), F32)]
    if n_cast:
        scratch += [pltpu.VMEM(s, F32) for s in slices] + [pltpu.VMEM(s, BF16) for s in slices]
        scratch += [pltpu.SemaphoreType.DMA((2 * n_cast,))]
    return pl.pallas_call(
        functools.partial(_mix_kernel, layer=l, seq_len=seq_len, n_seq=n_seq, n_cache=n_cache, rope=rope,
                          n_cast=n_cast, cast_layer=cast_layer,
                          kv_first_layer=not is_lat and new_kv is None, n_kv_alias=len(aliases),
                          heads_per_step=heads_per_step),
        grid=(n_tok // rows,),
        in_specs=in_specs,
        out_specs=out_specs,
        out_shape=out_shape,
        scratch_shapes=scratch,
        input_output_aliases=aliases,
        compiler_params=_params(),
        name="mix_lat" if is_lat else "mix_ctx",
    )(*args)


def _tail_kernel(x_ref, ao_ref, y_ref, mod_ref, npre_mix_ref, npost_mix_ref, npre_ffn_ref,
                 npost_ffn_ref, win_ref, wau_ref, wpu_ref, wout_ref, wg_ref, wu_ref, wd_ref, *rest,
                 layer, sub, tiles_per_mod_row, next_rows):
    this_layer = slice(layer, layer + 1)
    if next_rows:
        c_ref, wada_ref, bada_ref, o_ref, next_mod_out, x1_a, x1_b = rest
        next_mod_out[...] = _ada_rows(c_ref, wada_ref, bada_ref)
    else:
        o_ref, x1_a, x1_b = rest
    mod_row = pl.program_id(0) // tiles_per_mod_row if tiles_per_mod_row else CTX_MOD_ROW
    sh1, sc1, g1, sh2, sc2, g2 = [mod_ref[pl.ds(mod_row, 1), i * D_MODEL:(i + 1) * D_MODEL]
                                  for i in range(N_MOD)]
    x1_ref = (x1_a, x1_b)
    n_sub = x_ref.shape[0] // sub

    def post(i, slot):
        r0 = i * sub
        x = x_ref[r0:r0 + sub, :]
        hb = ((_rms(x) * npre_mix_ref[this_layer, :]) * (1.0 + sc1) + sh1).astype(BF16)
        g_attn = jax.nn.sigmoid(_dot(hb, win_ref[:, GATE_COL0:GATE_COL0 + D_MODEL]))
        g_pool = jax.nn.sigmoid(_dot(hb, win_ref[:, GATE_COL0 + D_MODEL:IN_COLS]))
        attn = _dot(ao_ref[r0:r0 + sub, :], wau_ref[...])
        pool = _dot(y_ref[r0:r0 + sub, :], wpu_ref[...])
        mixed = g_attn * attn + g_pool * pool
        m = _dot(mixed.astype(BF16), wout_ref[...])
        x1_ref[slot][...] = x + g1 * (_rms(m) * npost_mix_ref[this_layer, :])

    def ffn(i, slot):
        r0 = i * sub
        x1 = x1_ref[slot][...]
        hb = ((_rms(x1) * npre_ffn_ref[this_layer, :]) * (1.0 + sc2) + sh2).astype(BF16)
        g = _dot(hb, wg_ref[...])
        u = _dot(hb, wu_ref[...])
        a = ((g * jax.nn.sigmoid(g)) * u).astype(BF16)
        f = _dot(a, wd_ref[...])
        o_ref[r0:r0 + sub, :] = x1 + g2 * (_rms(f) * npost_ffn_ref[this_layer, :])

    post(0, 0)
    for i in range(n_sub):
        if i + 1 < n_sub:
            post(i + 1, (i + 1) % 2)
        ffn(i, i % 2)


def _tail_call(l, x, ao, y, mod_l, n_pre_mix, n_post_mix, n_pre_ffn, n_post_ffn, w_in,
               w_attn_up, w_pool_up, w_out, w_ffn_gate, w_ffn_up, w_ffn_down, next_mod_inputs,
               *, seq_len, per_seq_mod, tm, sub):
    n_tok = x.shape[0]
    n_tiles = n_tok // tm
    row = lambda i: (i, 0)
    vec_spec = pl.BlockSpec((DEPTH, D_MODEL), lambda i: (0, 0))
    layer_resident = lambda r, c: pl.BlockSpec((r, c), lambda i: (0, 0), pipeline_mode=pl.Buffered(1))
    in_specs = [
        pl.BlockSpec((tm, D_MODEL), row),
        pl.BlockSpec((tm, ATTN_W), row),
        pl.BlockSpec((tm, POOL_W), row),
        pl.BlockSpec((MOD_ROWS, N_MOD * D_MODEL), lambda i: (0, 0)),
        vec_spec, vec_spec, vec_spec, vec_spec,
        layer_resident(D_MODEL, IN_COLS),
        layer_resident(ATTN_W, D_MODEL),
        layer_resident(POOL_W, D_MODEL),
        layer_resident(D_MODEL, D_MODEL),
        layer_resident(D_MODEL, FFN_HIDDEN),
        layer_resident(D_MODEL, FFN_HIDDEN),
        layer_resident(FFN_HIDDEN, D_MODEL),
    ]
    args = [x, ao, y, mod_l, n_pre_mix, n_post_mix, n_pre_ffn, n_post_ffn, w_in, w_attn_up,
            w_pool_up, w_out, w_ffn_gate, w_ffn_up, w_ffn_down]
    out_specs = [pl.BlockSpec((tm, D_MODEL), row)]
    out_shape = [jax.ShapeDtypeStruct((n_tok, D_MODEL), F32)]
    if next_mod_inputs is not None:
        n_cols = N_MOD * D_MODEL
        tn = n_cols // n_tiles
        assert n_cols % n_tiles == 0 and tn % LANES == 0
        in_specs += [
            pl.BlockSpec((MOD_ROWS, D_MODEL), lambda i: (0, 0)),
            pl.BlockSpec((None, D_MODEL, tn), lambda i: (l + 1, 0, i)),
            pl.BlockSpec((None, 1, tn), lambda i: (l + 1, 0, i)),
        ]
        args += list(next_mod_inputs)
        out_specs += [pl.BlockSpec((MOD_ROWS, tn), lambda i: (0, i))]
        out_shape += [jax.ShapeDtypeStruct((MOD_ROWS, n_cols), F32)]
    return pl.pallas_call(
        functools.partial(_tail_kernel, layer=l, sub=sub, next_rows=next_mod_inputs is not None,
                          tiles_per_mod_row=seq_len // tm if per_seq_mod else 0),
        grid=(n_tiles,),
        in_specs=in_specs,
        out_specs=out_specs,
        out_shape=out_shape,
        scratch_shapes=[pltpu.VMEM((sub, D_MODEL), F32)] * 2,
        compiler_params=_params(),
        name="tail",
    )(*args)


def _rope_tables(n):
    rows = n // GRID_W
    row = np.repeat(np.arange(rows), GRID_W).astype(np.float32)
    col = np.tile(np.arange(GRID_W), rows).astype(np.float32)
    n_freq = HEAD_DIM // 4
    inv = jnp.asarray(ROPE_THETA, F32) ** (-jnp.arange(n_freq, dtype=F32) / n_freq)
    ang = jnp.concatenate([jnp.asarray(row)[:, None] * inv[None, :],
                           jnp.asarray(col)[:, None] * inv[None, :]], axis=-1)
    cos, sin = jnp.cos(ang), jnp.sin(ang)
    cos_t = jnp.tile(cos, (1, LANES // (HEAD_DIM // 2)))
    sin_t = jnp.tile(jnp.concatenate([-sin, sin], axis=-1), (1, LANES // HEAD_DIM))
    return cos_t, sin_t


def kernel(x_prompt, x_sample, cache_k, cache_v, c, c_ctx, w_ada, b_ada, w_in, q_norm, k_norm,
           w_attn_up, w_pool, pool_scale, w_pool_up, w_out, n_pre_mix, n_post_mix, n_pre_ffn,
           n_post_ffn, w_ffn_gate, w_ffn_up, w_ffn_down):
    batch, seq, _ = x_prompt.shape
    dec_batch, dec_seq, _ = x_sample.shape
    past_len = cache_k.shape[2]
    assert dec_batch <= CTX_MOD_ROW

    c_rows = jnp.concatenate(
        [c, c_ctx[None, :], jnp.zeros((MOD_ROWS - dec_batch - 1, D_MODEL), F32)], axis=0)
    b_ada3 = b_ada.reshape(DEPTH, 1, N_MOD * D_MODEL)
    mod_l = _ada_call(c_rows, w_ada, b_ada3)

    cos_t, sin_t = _rope_tables(dec_seq)
    q_gain = jnp.tile(q_norm, (1, LANES // HEAD_DIM))
    k_gain = jnp.tile(k_norm, (1, LANES // HEAD_DIM))
    w_in_l = w_in[0].astype(BF16)
    w_pool_b = w_pool.astype(BF16)
    tail_weights_f32 = (w_attn_up, w_pool_up, w_out, w_ffn_gate, w_ffn_up, w_ffn_down)
    n_pre_mix_v, n_post_mix_v = n_pre_mix, n_post_mix
    n_pre_ffn_v, n_post_ffn_v = n_pre_ffn, n_post_ffn
    pool_scale_v = pool_scale
    ck = cache_k.reshape(dec_batch, DEPTH, past_len, KV_W)
    cv = cache_v.reshape(dec_batch, DEPTH, past_len, KV_W)

    tm = 512
    ctx_seqs_per_step = 4

    y = x_prompt.reshape(batch * seq, D_MODEL)
    z = x_sample.reshape(dec_batch * dec_seq, D_MODEL)
    new_kv = None
    for l in range(DEPTH):
        has_next = l + 1 < DEPTH
        outs = _mix_call(l, y, mod_l, n_pre_mix_v, w_in_l, q_gain, k_gain, cos_t, sin_t, w_pool_b,
                         pool_scale_v, None, None, tail_weights_f32, l, new_kv,
                         seq_len=seq, n_seq=ctx_seqs_per_step, rope=False)
        ao, yp = outs[:2]
        new_kv = outs[2:4]
        tail_weights_b = outs[4:]
        tail_outs = _tail_call(l, y, ao, yp, mod_l, n_pre_mix_v, n_post_mix_v, n_pre_ffn_v, n_post_ffn_v,
                               w_in_l, *tail_weights_b, (c_rows, w_ada, b_ada3) if has_next else None,
                               seq_len=seq, per_seq_mod=False, tm=tm, sub=256)
        y = tail_outs[0]
        outs = _mix_call(l, z, mod_l, n_pre_mix_v, w_in_l, q_gain, k_gain, cos_t, sin_t, w_pool_b,
                         pool_scale_v, ck, cv, (w_in,) if has_next else (), l + 1,
                         seq_len=dec_seq, n_seq=1, rope=True)
        ao, yp = outs[:2]
        z = _tail_call(l, z, ao, yp, mod_l, n_pre_mix_v, n_post_mix_v, n_pre_ffn_v, n_post_ffn_v,
                       w_in_l, *tail_weights_b, None,
                       seq_len=dec_seq, per_seq_mod=True, tm=tm, sub=256)[0]
        if has_next:
            w_in_l = outs[2]
            mod_l = tail_outs[1]

    def kv_layout(a):
        a = a.reshape(batch, DEPTH, N_KV_HEADS, HEAD_DIM, seq)
        return jnp.transpose(a, (0, 1, 4, 2, 3))

    return (y.reshape(batch, seq, D_MODEL), z.reshape(dec_batch, dec_seq, D_MODEL),
            kv_layout(new_kv[0]), kv_layout(new_kv[1]))
```
